```python
import jax, jax.numpy as jnp
from jax import lax
import numpy as np

D_MODEL = 1024
BATCH = 2
SEQ = 8192
DEPTH = 2

GRID_W = 64
CTX_LEN = 256
EPS = 1e-6
ROPE_THETA = 10000.0
Q_BLOCK = 128
N_MOD = 6
FNET_GROUPS = 4
FNET_GROUP_DIM = 128
FNET_WIDTH = FNET_GROUPS * FNET_GROUP_DIM
MLA_HEADS = 8
MLA_Q_LORA = 256
MLA_KV_LORA = 128
MLA_NOPE = 64
MLA_ROPE = 32
MLA_V = 64
MLA_QK = MLA_NOPE + MLA_ROPE
AB_IN = FNET_WIDTH + MLA_Q_LORA + MLA_KV_LORA + MLA_ROPE
AB_OUT = FNET_WIDTH + MLA_HEADS * MLA_V
GQA_HEADS = 8
GQA_KV_HEADS = 2
GQA_GROUP = GQA_HEADS // GQA_KV_HEADS
GQA_HEAD_DIM = 128
C_IN = (GQA_HEADS + 2 * GQA_KV_HEADS) * GQA_HEAD_DIM
C_OUT = GQA_HEADS * GQA_HEAD_DIM
N_EXPERTS = 16
EXPERT_FF = 2048
CAPACITY_FACTOR = 2
N_EVEN = (DEPTH + 1) // 2
N_ODD = DEPTH // 2

kernel_name = "hybrid_fnet_mla_gqa_ecmoe_dit"


def rms_norm(x, g):
    xf = x.astype(jnp.float32)
    y = xf * lax.rsqrt(jnp.mean(xf * xf, axis=-1, keepdims=True) + EPS)
    return (y * g.astype(jnp.float32)).astype(x.dtype)


def modulate(x, g, shift, scale):
    return rms_norm(x, g) * (1 + scale) + shift


def rope_1d(x, pos):
    half = x.shape[-1] // 2
    inv = ROPE_THETA ** (-jnp.arange(half, dtype=jnp.float32) / half)
    ang = pos[:, None] * inv[None, :]
    cos = jnp.cos(ang)[None, :, None, :]
    sin = jnp.sin(ang)[None, :, None, :]
    xf = x.astype(jnp.float32)
    x1, x2 = xf[..., :half], xf[..., half:]
    return jnp.concatenate([x1 * cos - x2 * sin, x1 * sin + x2 * cos], axis=-1).astype(x.dtype)


def axial_rope(x, row, col):
    half = x.shape[-1] // 2
    return jnp.concatenate([rope_1d(x[..., :half], row), rope_1d(x[..., half:], col)], axis=-1)


def block_attention(q, k, v, scale):
    B, S, Hk, G, dk = q.shape
    dv = v.shape[-1]
    nb = S // Q_BLOCK
    qb = q.reshape(B, nb, Q_BLOCK, Hk, G, dk).transpose(1, 0, 2, 3, 4, 5)

    def one_block(qi):
        s = jnp.einsum('bqkgd,btkd->bkgqt', qi, k).astype(jnp.float32) * scale
        p = jax.nn.softmax(s, axis=-1).astype(v.dtype)
        return jnp.einsum('bkgqt,btkd->bqkgd', p, v)

    o = lax.map(one_block, qb)
    return o.transpose(1, 0, 2, 3, 4, 5).reshape(B, S, Hk, G, dv)


def fourier_mix(f):
    B, S, _ = f.shape
    ff = f.reshape(B, S, FNET_GROUPS, FNET_GROUP_DIM).astype(jnp.float32)
    y = jnp.fft.fft2(ff, axes=(1, 3), norm='ortho').real
    return y.reshape(B, S, FNET_WIDTH).astype(f.dtype)


def mla_project(proj, q_norm, w_uq, kv_norm, w_ukv, row, col):
    B, S, _ = proj.shape
    o = FNET_WIDTH
    f = proj[..., :o]
    cq = proj[..., o:o + MLA_Q_LORA]
    o += MLA_Q_LORA
    ckv = proj[..., o:o + MLA_KV_LORA]
    o += MLA_KV_LORA
    kr = proj[..., o:o + MLA_ROPE][:, :, None, :]
    q = (rms_norm(cq, q_norm) @ w_uq).reshape(B, S, MLA_HEADS, MLA_QK)
    kv = (rms_norm(ckv, kv_norm) @ w_ukv).reshape(B, S, MLA_HEADS, MLA_NOPE + MLA_V)
    q_nope, q_rope = q[..., :MLA_NOPE], q[..., MLA_NOPE:]
    k_nope, v = kv[..., :MLA_NOPE], kv[..., MLA_NOPE:]
    if row is not None:
        q_rope = axial_rope(q_rope, row, col)
        kr = axial_rope(kr, row, col)
    q = jnp.concatenate([q_nope, q_rope], axis=-1)
    k = jnp.concatenate([k_nope, jnp.broadcast_to(kr, (B, S, MLA_HEADS, MLA_ROPE))], axis=-1)
    return f, q, k, v


def mixer_ab(h_lat, h_ctx, w_in, q_norm, w_uq, kv_norm, w_ukv, w_o, row, col, ctx_out):
    B, S, _ = h_lat.shape
    scale = MLA_QK ** -0.5
    f_l, q_l, k_l, v_l = mla_project(h_lat @ w_in, q_norm, w_uq, kv_norm, w_ukv, row, col)
    f_c, q_c, k_c, v_c = mla_project(h_ctx @ w_in, q_norm, w_uq, kv_norm, w_ukv, None, None)
    k_all = jnp.concatenate([k_c, k_l], axis=1)
    v_all = jnp.concatenate([v_c, v_l], axis=1)
    a_l = block_attention(q_l[:, :, :, None, :], k_all, v_all, scale).reshape(B, S, MLA_HEADS * MLA_V)
    y_l = jnp.concatenate([fourier_mix(f_l), a_l], axis=-1) @ w_o
    y_c = None
    if ctx_out:
        L = h_ctx.shape[1]
        a_c = block_attention(q_c[:, :, :, None, :], k_c, v_c, scale).reshape(B, L, MLA_HEADS * MLA_V)
        y_c = jnp.concatenate([fourier_mix(f_c), a_c], axis=-1) @ w_o
    return y_l, y_c


def gqa_project(h, w_in, q_gain, k_gain, row, col):
    B, S, _ = h.shape
    p = h @ w_in
    nq = GQA_HEADS * GQA_HEAD_DIM
    nk = GQA_KV_HEADS * GQA_HEAD_DIM
    q = rms_norm(p[..., :nq].reshape(B, S, GQA_HEADS, GQA_HEAD_DIM), q_gain)
    k = rms_norm(p[..., nq:nq + nk].reshape(B, S, GQA_KV_HEADS, GQA_HEAD_DIM), k_gain)
    v = p[..., nq + nk:].reshape(B, S, GQA_KV_HEADS, GQA_HEAD_DIM)
    if row is not None:
        q = axial_rope(q, row, col)
        k = axial_rope(k, row, col)
    return q.reshape(B, S, GQA_KV_HEADS, GQA_GROUP, GQA_HEAD_DIM), k, v


def mixer_c(h_lat, h_ctx, w_in, q_gain, k_gain, w_o, row, col, ctx_out):
    B, S, _ = h_lat.shape
    scale = GQA_HEAD_DIM ** -0.5
    q_l, k_l, v_l = gqa_project(h_lat, w_in, q_gain, k_gain, row, col)
    q_c, k_c, v_c = gqa_project(h_ctx, w_in, q_gain, k_gain, None, None)
    k_all = jnp.concatenate([k_c, k_l], axis=1)
    v_all = jnp.concatenate([v_c, v_l], axis=1)
    y_l = block_attention(q_l, k_all, v_all, scale).reshape(B, S, C_OUT) @ w_o
    y_c = None
    if ctx_out:
        L = h_ctx.shape[1]
        y_c = block_attention(q_c, k_c, v_c, scale).reshape(B, L, C_OUT) @ w_o
    return y_l, y_c


def expert_choice_ffn(h, w_router, w_gate, w_up, w_down):
    B, N, D = h.shape
    cap = CAPACITY_FACTOR * N // N_EXPERTS
    logits = jnp.einsum('bnd,de->bne', h, w_router).astype(jnp.float32)
    aff = jax.nn.softmax(logits, axis=-1)
    g, idx = lax.top_k(jnp.swapaxes(aff, 1, 2), cap)
    xs = jax.vmap(lambda hb, ib: hb[ib])(h, idx)
    a = jnp.einsum('becd,edf->becf', xs, w_gate)
    u = jnp.einsum('becd,edf->becf', xs, w_up)
    y = jnp.einsum('becf,efd->becd', jax.nn.silu(a) * u, w_down)
    y = y * g[..., None].astype(y.dtype)
    return jax.vmap(lambda yb, ib: jnp.zeros((N, D), yb.dtype).at[ib.reshape(-1)].add(yb.reshape(-1, D)))(y, idx)


def setup_inputs(seed: int = 0) -> dict:
    key = jax.random.key(seed)
    ks = jax.random.split(key, 24)
    D = D_MODEL
    nrm = lambda k, shape, s: jax.random.normal(k, shape, jnp.float32) * s
    gain = lambda k, shape: 1.0 + 0.05 * jax.random.normal(k, shape, jnp.float32)
    return {
        "x": nrm(ks[0], (BATCH, SEQ, D), 1.0),
        "c": nrm(ks[1], (BATCH, D), 1.0),
        "ctx": nrm(ks[2], (BATCH, CTX_LEN, D), 1.0),
        "c_ctx": nrm(ks[3], (D,), 1.0),
        "ada_w": nrm(ks[4], (DEPTH, D, N_MOD * D), 0.5 * D ** -0.5),
        "ada_b": nrm(ks[5], (DEPTH, N_MOD * D), 0.02),
        "norm1": gain(ks[6], (DEPTH, D)),
        "norm2": gain(ks[7], (DEPTH, D)),
        "ab_w_in": nrm(ks[8], (N_EVEN, D, AB_IN), D ** -0.5),
        "ab_q_norm": gain(ks[9], (N_EVEN, MLA_Q_LORA)),
        "ab_w_uq": nrm(ks[10], (N_EVEN, MLA_Q_LORA, MLA_HEADS * MLA_QK), MLA_Q_LORA ** -0.5),
        "ab_kv_norm": gain(ks[11], (N_EVEN, MLA_KV_LORA)),
        "ab_w_ukv": nrm(ks[12], (N_EVEN, MLA_KV_LORA, MLA_HEADS * (MLA_NOPE + MLA_V)), MLA_KV_LORA ** -0.5),
        "ab_w_o": nrm(ks[13], (N_EVEN, AB_OUT, D), AB_OUT ** -0.5),
        "c_w_in": nrm(ks[14], (N_ODD, D, C_IN), D ** -0.5),
        "c_q_gain": gain(ks[15], (N_ODD, GQA_HEAD_DIM)),
        "c_k_gain": gain(ks[16], (N_ODD, GQA_HEAD_DIM)),
        "c_w_o": nrm(ks[17], (N_ODD, C_OUT, D), C_OUT ** -0.5),
        "moe_router": nrm(ks[18], (DEPTH, D, N_EXPERTS), D ** -0.5),
        "moe_w_gate": nrm(ks[19], (DEPTH, N_EXPERTS, D, EXPERT_FF), D ** -0.5),
        "moe_w_up": nrm(ks[20], (DEPTH, N_EXPERTS, D, EXPERT_FF), D ** -0.5),
        "moe_w_down": nrm(ks[21], (DEPTH, N_EXPERTS, EXPERT_FF, D), EXPERT_FF ** -0.5),
        "final_norm": gain(ks[22], (D,)),
    }


def reference(x, c, ctx, c_ctx, ada_w, ada_b, norm1, norm2, ab_w_in, ab_q_norm, ab_w_uq, ab_kv_norm,
              ab_w_ukv, ab_w_o, c_w_in, c_q_gain, c_k_gain, c_w_o, moe_router, moe_w_gate, moe_w_up,
              moe_w_down, final_norm):
    B, S, D = x.shape
    rows = S // GRID_W
    row = jnp.repeat(jnp.arange(rows, dtype=jnp.float32), GRID_W)
    col = jnp.tile(jnp.arange(GRID_W, dtype=jnp.float32), rows)
    h_lat, h_ctx = x, ctx
    for i in range(DEPTH):
        last = i == DEPTH - 1
        mod_l = (jax.nn.silu(c) @ ada_w[i] + ada_b[i])[:, None, :]
        mod_c = jax.nn.silu(c_ctx) @ ada_w[i] + ada_b[i]
        sh1_l, sc1_l, g1_l, sh2_l, sc2_l, g2_l = jnp.split(mod_l, N_MOD, axis=-1)
        sh1_c, sc1_c, g1_c, sh2_c, sc2_c, g2_c = jnp.split(mod_c, N_MOD, axis=-1)
        a_l = modulate(h_lat, norm1[i], sh1_l, sc1_l)
        a_c = modulate(h_ctx, norm1[i], sh1_c, sc1_c)
        j = i // 2
        if i % 2 == 0:
            y_l, y_c = mixer_ab(a_l, a_c, ab_w_in[j], ab_q_norm[j], ab_w_uq[j], ab_kv_norm[j],
                                ab_w_ukv[j], ab_w_o[j], row, col, not last)
        else:
            y_l, y_c = mixer_c(a_l, a_c, c_w_in[j], c_q_gain[j], c_k_gain[j], c_w_o[j], row, col, not last)
        h_lat = h_lat + g1_l * y_l
        m_l = modulate(h_lat, norm2[i], sh2_l, sc2_l)
        h_lat = h_lat + g2_l * expert_choice_ffn(m_l, moe_router[i], moe_w_gate[i], moe_w_up[i], moe_w_down[i])
        if not last:
            h_ctx = h_ctx + g1_c * y_c
            m_c = modulate(h_ctx, norm2[i], sh2_c, sc2_c)
            h_ctx = h_ctx + g2_c * expert_choice_ffn(m_c, moe_router[i], moe_w_gate[i], moe_w_up[i], moe_w_down[i])
    return rms_norm(h_lat, final_norm)
```

```python
import functools
import math

import jax
import jax.numpy as jnp
import numpy as np
from jax import lax
from jax.experimental import pallas as pl
from jax.experimental.pallas import tpu as pltpu

F32 = jnp.float32
BF16 = jnp.bfloat16
I32 = jnp.int32
HIGHEST = lax.Precision.HIGHEST

D_MODEL = 1024
GRID_W = 64
EPS = 1e-6
ROPE_THETA = 10000.0
N_MOD = 6
FNET_GROUPS = 4
FNET_GROUP_DIM = 128
FNET_WIDTH = FNET_GROUPS * FNET_GROUP_DIM
MLA_HEADS = 8
MLA_Q_LORA = 256
MLA_KV_LORA = 128
MLA_NOPE = 64
MLA_ROPE = 32
MLA_V = 64
MLA_QK = MLA_NOPE + MLA_ROPE
GQA_HEADS = 8
GQA_KV_HEADS = 2
GQA_GROUP = GQA_HEADS // GQA_KV_HEADS
GQA_HEAD_DIM = 128
N_EXPERTS = 16
EXPERT_FF = 2048
CAPACITY_FACTOR = 2

LANES = 128
VMEM_LIMIT = 56 * 1024 * 1024


def _params(sem, vmem=None):
    return pltpu.CompilerParams(dimension_semantics=sem, vmem_limit_bytes=vmem)


def _rms(x, g):
    return x * lax.rsqrt(jnp.mean(x * x, axis=-1, keepdims=True) + EPS) * g


def _modulate(h, g, shift, scale):
    return _rms(h, g) * (1.0 + scale) + shift


def _rope(x, cos, sin_fwd, sin_bwd, shift):
    return (x * cos + pltpu.roll(x, LANES - shift, 1) * sin_fwd
            + pltpu.roll(x, shift, 1) * sin_bwd)


def _ada_kernel(c_ref, w_ref, b_ref, o_ref):
    c = c_ref[...]
    x = c * jax.nn.sigmoid(c)
    o_ref[0] = jnp.dot(x, w_ref[0], precision=HIGHEST, preferred_element_type=F32) + b_ref[0]


def _ada(c8, ada_w, ada_b):
    depth, d, n = ada_w.shape
    tn = 1536
    return pl.pallas_call(
        _ada_kernel,
        out_shape=jax.ShapeDtypeStruct((depth, 8, n), F32),
        grid=(depth, n // tn),
        in_specs=[pl.BlockSpec((8, d), lambda l, j: (0, 0)),
                  pl.BlockSpec((1, d, tn), lambda l, j: (l, 0, j)),
                  pl.BlockSpec((1, 1, tn), lambda l, j: (l, 0, j))],
        out_specs=pl.BlockSpec((1, 8, tn), lambda l, j: (l, 0, j)),
        compiler_params=_params(("arbitrary", "arbitrary"), VMEM_LIMIT),
        name="ada",
    )(c8, ada_w, ada_b.reshape(depth, 1, n))


def _in_ab_kernel(scale, *refs):
    (h_ref, mod_ref, n1_ref, win_ref, qn_ref, wuq_ref, kvn_ref, wuk_ref, wuv_ref, ekr_ref,
     cos_ref, sf_ref, sb_ref, f_ref, q_ref, k_ref, v_ref) = refs
    D = D_MODEL
    h = h_ref[0]
    mod = mod_ref[0]
    a = _modulate(h, n1_ref[...], mod[:, 0:D], mod[:, D:2 * D]).astype(BF16)
    p = jnp.dot(a, win_ref[...], preferred_element_type=F32)
    o = FNET_WIDTH
    f_ref[0] = p[:, :o]
    cq = p[:, o:o + MLA_Q_LORA]
    o += MLA_Q_LORA
    ckv = p[:, o:o + MLA_KV_LORA]
    o += MLA_KV_LORA
    kr = p[:, o:o + MLA_ROPE]
    cqn = _rms(cq, qn_ref[...]).astype(BF16)
    ckvn = _rms(ckv, kvn_ref[...]).astype(BF16)
    q = jnp.dot(cqn, wuq_ref[...], preferred_element_type=F32)
    k = (jnp.dot(ckvn, wuk_ref[...], preferred_element_type=F32)
         + jnp.dot(kr.astype(BF16), ekr_ref[...], preferred_element_type=F32))
    v = jnp.dot(ckvn, wuv_ref[...], preferred_element_type=F32)
    cos, sf, sb = cos_ref[...], sf_ref[...], sb_ref[...]
    shift = MLA_ROPE // 4
    for hd in range(MLA_HEADS):
        sl = slice(hd * LANES, (hd + 1) * LANES)
        q_ref[0, hd] = (_rope(q[:, sl], cos, sf, sb, shift) * scale).astype(BF16)
        k_ref[0, hd] = _rope(k[:, sl], cos, sf, sb, shift).astype(BF16)
        v_ref[0, hd] = v[:, sl].astype(BF16)


def _in_ab(h, mod, n1, w, tables, tm):
    B, R, D = h.shape
    H = MLA_HEADS
    full = lambda a: pl.BlockSpec(a.shape, lambda b, i: (0,) * a.ndim)
    row = lambda w_: pl.BlockSpec((1, tm, w_), lambda b, i: (b, i, 0))
    tab = pl.BlockSpec((tm, LANES), lambda b, i: (i, 0))
    hd = pl.BlockSpec((1, H, tm, LANES), lambda b, i: (b, 0, i, 0))
    weights = [n1, w["win"], w["qn"], w["wuq"], w["kvn"], w["wuk"], w["wuv"], w["ekr"]]
    return pl.pallas_call(
        functools.partial(_in_ab_kernel, MLA_QK ** -0.5),
        out_shape=(jax.ShapeDtypeStruct((B, R, FNET_WIDTH), F32),
                   jax.ShapeDtypeStruct((B, H, R, LANES), BF16),
                   jax.ShapeDtypeStruct((B, H, R, LANES), BF16),
                   jax.ShapeDtypeStruct((B, H, R, LANES), BF16)),
        grid=(B, R // tm),
        in_specs=[row(D), pl.BlockSpec((1, 1, N_MOD * D), lambda b, i: (b, 0, 0))]
        + [full(a) for a in weights] + [tab, tab, tab],
        out_specs=(row(FNET_WIDTH), hd, hd, hd),
        compiler_params=_params(("arbitrary", "arbitrary"), VMEM_LIMIT),
        name="in_ab",
    )(h, mod, *weights, *tables)


def _in_c_kernel(scale, *refs):
    (h_ref, moe_ref, modp_ref, mod_ref, n1_ref, win_ref, qg_ref, kg_ref,
     cos_ref, sf_ref, sb_ref, h2_ref, q_ref, k_ref, v_ref) = refs
    D = D_MODEL
    h = h_ref[0] + modp_ref[0][:, 5 * D:6 * D] * moe_ref[0]
    h2_ref[0] = h
    mod = mod_ref[0]
    a = _modulate(h, n1_ref[...], mod[:, 0:D], mod[:, D:2 * D]).astype(BF16)
    p = jnp.dot(a, win_ref[...], preferred_element_type=F32)
    cos, sf, sb = cos_ref[...], sf_ref[...], sb_ref[...]
    shift = GQA_HEAD_DIM // 4
    nq = GQA_HEADS * GQA_HEAD_DIM
    nk = GQA_KV_HEADS * GQA_HEAD_DIM
    for hd in range(GQA_HEADS):
        x = _rms(p[:, hd * LANES:(hd + 1) * LANES], qg_ref[...])
        q_ref[0, hd] = (_rope(x, cos, sf, sb, shift) * scale).astype(BF16)
    for hd in range(GQA_KV_HEADS):
        x = _rms(p[:, nq + hd * LANES:nq + (hd + 1) * LANES], kg_ref[...])
        k_ref[0, hd] = _rope(x, cos, sf, sb, shift).astype(BF16)
        v_ref[0, hd] = p[:, nq + nk + hd * LANES:nq + nk + (hd + 1) * LANES].astype(BF16)


def _in_c(h, moe, mod_prev, mod, n1, w, tables, tm):
    B, R, D = h.shape
    full = lambda a: pl.BlockSpec(a.shape, lambda b, i: (0,) * a.ndim)
    row = lambda w_: pl.BlockSpec((1, tm, w_), lambda b, i: (b, i, 0))
    modspec = pl.BlockSpec((1, 1, N_MOD * D), lambda b, i: (b, 0, 0))
    tab = pl.BlockSpec((tm, LANES), lambda b, i: (i, 0))
    hd = lambda n: pl.BlockSpec((1, n, tm, LANES), lambda b, i: (b, 0, i, 0))
    weights = [n1, w["win"], w["qg"], w["kg"]]
    return pl.pallas_call(
        functools.partial(_in_c_kernel, GQA_HEAD_DIM ** -0.5),
        out_shape=(jax.ShapeDtypeStruct((B, R, D), F32),
                   jax.ShapeDtypeStruct((B, GQA_HEADS, R, LANES), BF16),
                   jax.ShapeDtypeStruct((B, GQA_KV_HEADS, R, LANES), BF16),
                   jax.ShapeDtypeStruct((B, GQA_KV_HEADS, R, LANES), BF16)),
        grid=(B, R // tm),
        in_specs=[row(D), row(D), modspec, modspec] + [full(a) for a in weights] + [tab, tab, tab],
        out_specs=(row(D), hd(GQA_HEADS), hd(GQA_KV_HEADS), hd(GQA_KV_HEADS)),
        compiler_params=_params(("arbitrary", "arbitrary"), VMEM_LIMIT),
        name="in_c",
    )(h, moe, mod_prev, mod, *weights, *tables)


def _attn_kernel(n_lat, tk, *refs):
    if n_lat:
        q_ref, kc_ref, vc_ref, kl_ref, vl_ref, o_ref, acc_ref = refs
    else:
        q_ref, kc_ref, vc_ref, o_ref, acc_ref = refs
    G, tq = q_ref.shape[2], q_ref.shape[3]
    q = q_ref[0, 0].reshape(G * tq, LANES)

    def scores(k):
        return lax.dot_general(q, k, (((1,), (1,)), ((), ())), preferred_element_type=F32)

    s = scores(kc_ref[0, 0])
    m = jnp.max(s, axis=1, keepdims=True)
    p = jnp.exp(s - m)
    l = jnp.sum(p, axis=1, keepdims=True)
    acc_ref[...] = jnp.dot(p.astype(BF16), vc_ref[0, 0], preferred_element_type=F32)

    if n_lat:
        def body(c, carry):
            m, l = carry
            start = pl.multiple_of(c * tk, tk)
            s = scores(kl_ref[0, 0, pl.ds(start, tk), :])
            m_new = jnp.maximum(m, jnp.max(s, axis=1, keepdims=True))
            p = jnp.exp(s - m_new)
            alpha = jnp.exp(m - m_new)
            l = alpha * l + jnp.sum(p, axis=1, keepdims=True)
            pv = jnp.dot(p.astype(BF16), vl_ref[0, 0, pl.ds(start, tk), :],
                         preferred_element_type=F32)
            acc_ref[...] = alpha * acc_ref[...] + pv
            return m_new, l

        m, l = lax.fori_loop(0, n_lat, body, (m, l))
    o_ref[0, 0] = (acc_ref[...] / l).reshape(G, tq, LANES).astype(BF16)


def _attention(q, kc, vc, kl, vl, tq, tk):
    B, Hk, G, R, _ = q.shape
    Lc = kc.shape[2]
    n_lat = 0 if kl is None else kl.shape[2] // tk
    qspec = pl.BlockSpec((1, 1, G, tq, LANES), lambda b, h, i: (b, h, 0, i, 0))
    kv = lambda n: pl.BlockSpec((1, 1, n, LANES), lambda b, h, i: (b, h, 0, 0))
    ins = [q, kc, vc] + ([kl, vl] if n_lat else [])
    specs = [qspec, kv(Lc), kv(Lc)] + ([kv(kl.shape[2])] * 2 if n_lat else [])
    return pl.pallas_call(
        functools.partial(_attn_kernel, n_lat, tk),
        out_shape=jax.ShapeDtypeStruct(q.shape, BF16),
        grid=(B, Hk, R // tq),
        in_specs=specs,
        out_specs=qspec,
        scratch_shapes=[pltpu.VMEM((G * tq, LANES), F32)],
        compiler_params=_params(("arbitrary", "arbitrary", "arbitrary"), VMEM_LIMIT),
        name="attn",
    )(*ins)


def _dft_mats(n):
    k = np.arange(n, dtype=np.float64)
    ang = 2.0 * np.pi * np.outer(k, k) / n
    return np.cos(ang), np.sin(ang)


def _dft1_kernel(ns2, x_ref, m1_ref, tw_ref, o_ref):
    s1 = x_ref.shape[1]
    a = jnp.dot(m1_ref[...], x_ref[0], precision=HIGHEST, preferred_element_type=F32)
    are, aim = a[:s1], a[s1:]
    for i in range(ns2):
        sl = slice(i * FNET_WIDTH, (i + 1) * FNET_WIDTH)
        tre = jnp.tile(tw_ref[0, :, i * LANES:(i + 1) * LANES], (1, FNET_GROUPS))
        tim = jnp.tile(tw_ref[1, :, i * LANES:(i + 1) * LANES], (1, FNET_GROUPS))
        o_ref[0, 0, :, sl] = are[:, sl] * tre - aim[:, sl] * tim
        o_ref[0, 1, :, sl] = are[:, sl] * tim + aim[:, sl] * tre


def _dft2_kernel(kb, a_ref, m2_ref, m3_ref, o_ref):
    n2 = a_ref.shape[3]
    for j in range(kb):
        rhs = jnp.concatenate([a_ref[0, 0, j], a_ref[0, 1, j]], axis=0)
        y = jnp.dot(m2_ref[...], rhs, precision=HIGHEST, preferred_element_type=F32)
        for g in range(FNET_GROUPS):
            sl = slice(g * LANES, (g + 1) * LANES)
            lhs = jnp.concatenate([y[:n2, sl], y[n2:, sl]], axis=1)
            o_ref[0, :, j * FNET_WIDTH + g * LANES:j * FNET_WIDTH + (g + 1) * LANES] = jnp.dot(
                lhs, m3_ref[...], precision=HIGHEST, preferred_element_type=F32)


def _fourier_lat(f):
    B, S, W = f.shape
    n2 = LANES
    s1 = S // n2
    c1, sn1 = _dft_mats(s1)
    c2, sn2 = _dft_mats(n2)
    cc, sc = _dft_mats(FNET_GROUP_DIM)
    m1 = jnp.asarray(np.concatenate([c1, -sn1], axis=0), F32)
    m2 = jnp.asarray(np.block([[c2, sn2], [-sn2, c2]]), F32)
    norm = 1.0 / math.sqrt(S * FNET_GROUP_DIM)
    m3 = jnp.asarray(np.concatenate([cc, sc], axis=0) * norm, F32)
    ang = 2.0 * np.pi * np.outer(np.arange(s1), np.arange(n2)) / S
    tw = np.stack([np.cos(ang), -np.sin(ang)])
    tw = jnp.asarray(np.repeat(tw[:, :, :, None], LANES, axis=3).reshape(2, s1, n2 * LANES), F32)

    ns2 = 8
    x = f.reshape(B, s1, n2 * W)
    a = pl.pallas_call(
        functools.partial(_dft1_kernel, ns2),
        out_shape=jax.ShapeDtypeStruct((B, 2, s1, n2 * W), F32),
        grid=(B, n2 // ns2),
        in_specs=[pl.BlockSpec((1, s1, ns2 * W), lambda b, j: (b, 0, j)),
                  pl.BlockSpec(m1.shape, lambda b, j: (0, 0)),
                  pl.BlockSpec((2, s1, ns2 * LANES), lambda b, j: (0, 0, j))],
        out_specs=pl.BlockSpec((1, 2, s1, ns2 * W), lambda b, j: (b, 0, 0, j)),
        compiler_params=_params(("arbitrary", "arbitrary"), VMEM_LIMIT),
        name="dft1",
    )(x, m1, tw)
    a = a.reshape(B, 2, s1, n2, W)
    kb = min(8, s1)
    y = pl.pallas_call(
        functools.partial(_dft2_kernel, kb),
        out_shape=jax.ShapeDtypeStruct((B, n2, s1 * W), F32),
        grid=(B, s1 // kb),
        in_specs=[pl.BlockSpec((1, 2, kb, n2, W), lambda b, j: (b, 0, j, 0, 0)),
                  pl.BlockSpec(m2.shape, lambda b, j: (0, 0)),
                  pl.BlockSpec(m3.shape, lambda b, j: (0, 0))],
        out_specs=pl.BlockSpec((1, n2, kb * W), lambda b, j: (b, 0, j)),
        compiler_params=_params(("arbitrary", "arbitrary"), VMEM_LIMIT),
        name="dft2",
    )(a, m2, m3)
    return y.reshape(B, S, W)


def _dftc_kernel(f_ref, mc_ref, m3_ref, o_ref):
    n = f_ref.shape[1]
    a = jnp.dot(mc_ref[...], f_ref[0], precision=HIGHEST, preferred_element_type=F32)
    for g in range(FNET_GROUPS):
        sl = slice(g * LANES, (g + 1) * LANES)
        lhs = jnp.concatenate([a[:n, sl], a[n:, sl]], axis=1)
        o_ref[0, :, sl] = jnp.dot(lhs, m3_ref[...], precision=HIGHEST, preferred_element_type=F32)


def _fourier_ctx(f):
    B, L, W = f.shape
    c, s = _dft_mats(L)
    cc, sc = _dft_mats(FNET_GROUP_DIM)
    mc = jnp.asarray(np.concatenate([c, -s], axis=0), F32)
    m3 = jnp.asarray(np.concatenate([cc, sc], axis=0) / math.sqrt(L * FNET_GROUP_DIM), F32)
    return pl.pallas_call(
        _dftc_kernel,
        out_shape=jax.ShapeDtypeStruct((B, L, W), F32),
        grid=(B,),
        in_specs=[pl.BlockSpec((1, L, W), lambda b: (b, 0, 0)),
                  pl.BlockSpec(mc.shape, lambda b: (0, 0)),
                  pl.BlockSpec(m3.shape, lambda b: (0, 0))],
        out_specs=pl.BlockSpec((1, L, W), lambda b: (b, 0, 0)),
        compiler_params=_params(("arbitrary",), VMEM_LIMIT),
        name="dftc",
    )(f, mc, m3)


def _out_kernel(has_f, *refs):
    if has_f:
        (h_ref, mod_ref, yf_ref, wof_ref, o_ref, woa_ref, n2_ref, wr_ref,
         h1_ref, m_ref, aff_ref) = refs
    else:
        (h_ref, mod_ref, o_ref, woa_ref, n2_ref, wr_ref, h1_ref, m_ref, aff_ref) = refs
    D = D_MODEL
    Hk, G = o_ref.shape[1], o_ref.shape[2]
    ocat = jnp.concatenate([o_ref[0, hk, g] for hk in range(Hk) for g in range(G)], axis=1)
    y = jnp.dot(ocat, woa_ref[...], preferred_element_type=F32)
    if has_f:
        y = y + jnp.dot(yf_ref[0].astype(BF16), wof_ref[...], preferred_element_type=F32)
    mod = mod_ref[0]
    h1 = h_ref[0] + mod[:, 2 * D:3 * D] * y
    h1_ref[0] = h1
    m = _modulate(h1, n2_ref[...], mod[:, 3 * D:4 * D], mod[:, 4 * D:5 * D])
    m_ref[0] = m
    logit = lax.dot_general(wr_ref[...], m, (((1,), (1,)), ((), ())), precision=HIGHEST,
                            preferred_element_type=F32)
    e = jnp.exp(logit - jnp.max(logit, axis=0, keepdims=True))
    aff_ref[0] = e / jnp.sum(e, axis=0, keepdims=True)


def _out_proj(h, mod, yf, wof, o, woa, n2, wr_t, tm):
    B, R, D = h.shape
    _, Hk, G, _, _ = o.shape
    full = lambda a: pl.BlockSpec(a.shape, lambda b, i: (0,) * a.ndim)
    row = lambda w_: pl.BlockSpec((1, tm, w_), lambda b, i: (b, i, 0))
    modspec = pl.BlockSpec((1, 1, N_MOD * D), lambda b, i: (b, 0, 0))
    ospec = pl.BlockSpec((1, Hk, G, tm, LANES), lambda b, i: (b, 0, 0, i, 0))
    has_f = yf is not None
    ins = [h, mod] + ([yf, wof] if has_f else []) + [o, woa, n2, wr_t]
    specs = ([row(D), modspec] + ([row(FNET_WIDTH), full(wof)] if has_f else [])
             + [ospec, full(woa), full(n2), full(wr_t)])
    return pl.pallas_call(
        functools.partial(_out_kernel, has_f),
        out_shape=(jax.ShapeDtypeStruct((B, R, D), F32), jax.ShapeDtypeStruct((B, R, D), F32),
                   jax.ShapeDtypeStruct((B, N_EXPERTS, R), F32)),
        grid=(B, R // tm),
        in_specs=specs,
        out_specs=(row(D), row(D), pl.BlockSpec((1, N_EXPERTS, tm), lambda b, i: (b, 0, i))),
        compiler_params=_params(("arbitrary", "arbitrary"), VMEM_LIMIT),
        name="out_proj",
    )(*ins)


def _topk_kernel(cap, aff_ref, u_ref, ones_ref, lmat_ref, lc_ref, lg_ref, cnt_ref, off_ref,
                 pos_ref, ac_ref):
    a = aff_ref[0]
    E, N = a.shape
    NC = N // LANES
    R = NC * E
    keys = pltpu.bitcast(a, I32)

    def bit_step(i, tau):
        cand = tau | jnp.left_shift(jnp.int32(1), 30 - i)
        cnt = jnp.sum((keys >= cand).astype(I32), axis=1, keepdims=True)
        return jnp.where(cnt >= cap, cand, tau)

    tau = lax.fori_loop(0, 31, bit_step, jnp.zeros((E, 1), I32))
    gt = (keys > tau).astype(F32)
    eq = (keys == tau).astype(F32)
    need = (cap - jnp.sum(gt, axis=1, keepdims=True))

    def chunked(x):
        return jnp.concatenate([x[:, c * LANES:(c + 1) * LANES] for c in range(NC)], axis=0)

    a_c, gt_c, eq_c = chunked(a), chunked(gt), chunked(eq)
    need_c = jnp.tile(need, (NC, 1))

    def prefix(x):
        xb = x.astype(BF16)
        loc = jnp.dot(xb, u_ref[...], preferred_element_type=F32)
        tot = jnp.dot(xb, ones_ref[...], preferred_element_type=F32)
        offs = jnp.dot(lmat_ref[...], tot.astype(BF16), preferred_element_type=F32)
        return loc, tot, offs

    loc, tot, offs = prefix(eq_c)
    sel = jnp.maximum(gt_c, jnp.where(loc + offs < need_c, eq_c, 0.0))
    loc, tot, offs = prefix(sel)
    pos_ref[...] = jnp.where(sel > 0.0, loc, -1.0)
    ac_ref[...] = a_c
    cnt_ref[0] = tot.astype(I32)
    off_ref[0] = offs.astype(I32)
    rb = 16
    lane = lax.broadcasted_iota(I32, (rb, LANES), 1).astype(F32)

    def compact(i, carry):
        rows = pl.ds(pl.multiple_of(i * rb, rb), rb)
        selpos = pos_ref[rows, :]
        aff = ac_ref[rows, :]
        lc = jnp.zeros((rb, LANES), F32)
        lg = jnp.zeros((rb, LANES), F32)
        for t in range(LANES):
            hit = selpos[:, t:t + 1] == lane
            lc = jnp.where(hit, float(t), lc)
            lg = jnp.where(hit, aff[:, t:t + 1], lg)
        lc_ref[0, rows, :] = lc.astype(I32)
        lg_ref[0, rows, :] = lg
        return carry

    lax.fori_loop(0, R // rb, compact, 0)


def _topk(aff, cap):
    B, E, N = aff.shape
    NC = N // LANES
    R = NC * E
    i = np.arange(LANES)
    u = jnp.asarray(i[:, None] < i[None, :], BF16)
    ones = jnp.ones((LANES, LANES), BF16)
    r = np.arange(R)
    lmat = jnp.asarray((r[:, None] % E == r[None, :] % E) & (r[None, :] // E < r[:, None] // E), BF16)
    full = lambda a: pl.BlockSpec(a.shape, lambda b: (0,) * a.ndim)
    ospec = pl.BlockSpec((1, R, LANES), lambda b: (b, 0, 0))
    sds = lambda dt: jax.ShapeDtypeStruct((B, R, LANES), dt)
    lc, lg, cnt, off = pl.pallas_call(
        functools.partial(_topk_kernel, cap),
        out_shape=(sds(I32), sds(F32), sds(I32), sds(I32)),
        grid=(B,),
        in_specs=[pl.BlockSpec((1, E, N), lambda b: (b, 0, 0)), full(u), full(ones), full(lmat)],
        out_specs=(ospec, ospec, ospec, ospec),
        scratch_shapes=[pltpu.VMEM((R, LANES), F32), pltpu.VMEM((R, LANES), F32)],
        compiler_params=_params(("arbitrary",), VMEM_LIMIT),
        name="topk",
    )(aff, u, ones, lmat)
    by_expert = lambda x: x.reshape(B, NC, E, LANES).transpose(0, 2, 1, 3)
    return (by_expert(lc), by_expert(lg), by_expert(cnt)[..., 0], by_expert(off)[..., 0])


def _gather_kernel(cap, cnt_ref, off_ref, lc_ref, m_hbm, xs_hbm, sem):
    b = pl.program_id(0)
    e = pl.program_id(1)
    NC = lc_ref.shape[0]

    def chunk_body(c, carry):
        base = off_ref[b, e, c] + b * cap

        def row_body(r, carry):
            t = c * LANES + lc_ref[c, r]
            pltpu.make_async_copy(m_hbm.at[b, pl.ds(t, 1), :],
                                  xs_hbm.at[e, pl.ds(base + r, 1), :], sem).start()
            return carry

        return lax.fori_loop(0, cnt_ref[b, e, c], row_body, carry)

    lax.fori_loop(0, NC, chunk_body, 0)
    done = xs_hbm.at[e, pl.ds(b * cap, cap), :]
    pltpu.make_async_copy(done, done, sem).wait()


def _gather(m, lc, cnt, off, cap):
    B, N, D = m.shape
    E, NC = lc.shape[1], lc.shape[2]
    return pl.pallas_call(
        functools.partial(_gather_kernel, cap),
        out_shape=jax.ShapeDtypeStruct((E, B * cap, D), m.dtype),
        grid_spec=pltpu.PrefetchScalarGridSpec(
            num_scalar_prefetch=2,
            grid=(B, E),
            in_specs=[pl.BlockSpec((None, None, NC, LANES), lambda b, e, *_: (b, e, 0, 0),
                                   memory_space=pltpu.SMEM),
                      pl.BlockSpec(memory_space=pl.ANY)],
            out_specs=pl.BlockSpec(memory_space=pl.ANY),
            scratch_shapes=[pltpu.SemaphoreType.DMA(())]),
        compiler_params=_params(("arbitrary", "arbitrary")),
        name="gather",
    )(cnt, off, lc, m)


def _ffn_kernel(n_parts, chunk_rows, *refs):
    wg_ref, wu_ref, wd_ref = refs[:3]
    x_refs = refs[3:3 + n_parts]
    o_refs = refs[3 + n_parts:3 + 2 * n_parts]
    j = pl.program_id(1)
    wg = wg_ref[0].astype(BF16)
    wu = wu_ref[0].astype(BF16)
    wd = wd_ref[0].astype(BF16)
    for x_ref, o_ref, rows in zip(x_refs, o_refs, chunk_rows):
        for r0 in range(0, x_ref.shape[1], rows):
            x = x_ref[0, r0:r0 + rows, :]
            a = jnp.dot(x, wg, preferred_element_type=F32)
            u = jnp.dot(x, wu, preferred_element_type=F32)
            hh = (a * jax.nn.sigmoid(a) * u).astype(BF16)
            y = jnp.dot(hh, wd, preferred_element_type=F32)

            @pl.when(j == 0)
            def _():
                o_ref[0, r0:r0 + rows, :] = y

            @pl.when(j > 0)
            def _():
                o_ref[0, r0:r0 + rows, :] += y


def _ffn(xs, w_gate, w_up, w_down, chunk_rows):
    E, D, F = w_gate.shape
    tf = 512
    xspec = lambda x: pl.BlockSpec((1, x.shape[1], D), lambda e, j: (e, 0, 0))
    outs = pl.pallas_call(
        functools.partial(_ffn_kernel, len(xs), chunk_rows),
        out_shape=tuple(jax.ShapeDtypeStruct(x.shape, F32) for x in xs),
        grid=(E, F // tf),
        in_specs=[pl.BlockSpec((1, D, tf), lambda e, j: (e, 0, j)),
                  pl.BlockSpec((1, D, tf), lambda e, j: (e, 0, j)),
                  pl.BlockSpec((1, tf, D), lambda e, j: (e, j, 0))] + [xspec(x) for x in xs],
        out_specs=tuple(xspec(x) for x in xs),
        compiler_params=_params(("arbitrary", "arbitrary"), VMEM_LIMIT),
        name="ffn",
    )(w_gate, w_up, w_down, *xs)
    return outs


def _combine_kernel(cnt_ref, off_ref, lc_ref, lg_ref, y_ref, o_ref):
    b = pl.program_id(0)
    hf = pl.program_id(1)
    e = pl.program_id(2)
    nch = o_ref.shape[1] // LANES

    @pl.when(e == 0)
    def _():
        o_ref[...] = jnp.zeros_like(o_ref)

    def chunk_body(ci, carry):
        c = hf * nch + ci
        base = off_ref[b, e, c]

        def row_body(r, carry):
            t = ci * LANES + lc_ref[c, r]
            o_ref[0, t] = o_ref[0, t] + lg_ref[c, r] * y_ref[0, base + r]
            return carry

        return lax.fori_loop(0, cnt_ref[b, e, c], row_body, carry)

    lax.fori_loop(0, nch, chunk_body, 0)


def _combine(y, lc, lg, cnt, off, n_tokens, cap, n_split):
    E, _, D = y.shape
    B, _, NC, _ = lc.shape
    sub = D // LANES
    y3 = y.reshape(E, y.shape[1], sub, LANES)
    nh = n_tokens // n_split
    smem = lambda: pl.BlockSpec((None, None, NC, LANES), lambda b, h, e, *_: (b, e, 0, 0),
                                memory_space=pltpu.SMEM)
    out = pl.pallas_call(
        _combine_kernel,
        out_shape=jax.ShapeDtypeStruct((B, n_tokens, sub, LANES), F32),
        grid_spec=pltpu.PrefetchScalarGridSpec(
            num_scalar_prefetch=2,
            grid=(B, n_split, E),
            in_specs=[smem(), smem(),
                      pl.BlockSpec((1, cap, sub, LANES), lambda b, h, e, *_: (e, b, 0, 0))],
            out_specs=pl.BlockSpec((1, nh, sub, LANES), lambda b, h, e, *_: (b, h, 0, 0))),
        compiler_params=_params(("arbitrary", "arbitrary", "arbitrary"), VMEM_LIMIT),
        name="combine",
    )(cnt, off, lc, lg, y3)
    return out.reshape(B, n_tokens, D)


def _final_kernel(h_ref, moe_ref, mod_ref, g_ref, o_ref):
    D = D_MODEL
    h = h_ref[0] + mod_ref[0][:, 5 * D:6 * D] * moe_ref[0]
    o_ref[0] = _rms(h, g_ref[...])


def _final(h, moe, mod, g, tm):
    B, R, D = h.shape
    row = pl.BlockSpec((1, tm, D), lambda b, i: (b, i, 0))
    return pl.pallas_call(
        _final_kernel,
        out_shape=jax.ShapeDtypeStruct((B, R, D), F32),
        grid=(B, R // tm),
        in_specs=[row, row, pl.BlockSpec((1, 1, N_MOD * D), lambda b, i: (b, 0, 0)),
                  pl.BlockSpec((1, D), lambda b, i: (0, 0))],
        out_specs=row,
        compiler_params=_params(("arbitrary", "arbitrary"), VMEM_LIMIT),
        name="final",
    )(h, moe, mod, g)


def _rope_tables(n_lat, n_ctx, segments):
    t = jnp.arange(n_lat)
    pos = {"row": (t // GRID_W).astype(F32), "col": (t % GRID_W).astype(F32)}
    cos = jnp.ones((n_lat, LANES), F32)
    sf = jnp.zeros((n_lat, LANES), F32)
    sb = jnp.zeros((n_lat, LANES), F32)
    for lane0, width, which in segments:
        half = width // 2
        inv = ROPE_THETA ** (-jnp.arange(half, dtype=F32) / half)
        ang = pos[which][:, None] * inv[None, :]
        c, s = jnp.cos(ang), jnp.sin(ang)
        cos = cos.at[:, lane0:lane0 + half].set(c).at[:, lane0 + half:lane0 + width].set(c)
        sf = sf.at[:, lane0:lane0 + half].set(-s)
        sb = sb.at[:, lane0 + half:lane0 + width].set(s)
    lat = (cos, sf, sb)
    ctx = (jnp.ones((n_ctx, LANES), F32), jnp.zeros((n_ctx, LANES), F32), jnp.zeros((n_ctx, LANES), F32))
    return lat, ctx


def _head_slots(w, n_heads, width, lo, hi):
    k = w.shape[0]
    w3 = w.reshape(k, n_heads, width)[:, :, lo:hi]
    return jnp.pad(w3, ((0, 0), (0, 0), (0, LANES - (hi - lo)))).reshape(k, n_heads * LANES)


def _moe(m_l, aff_l, m_c, aff_c, w_gate, w_up, w_down):
    B, S, D = m_l.shape
    cap_l = CAPACITY_FACTOR * S // N_EXPERTS
    lc, lg, cnt, off = _topk(aff_l, cap_l)
    xs = [_gather(m_l, lc, cnt, off, cap_l).astype(BF16)]
    chunk_rows = [cap_l]
    if m_c is not None:
        L = m_c.shape[1]
        cap_c = CAPACITY_FACTOR * L // N_EXPERTS
        lcc, lgc, cntc, offc = _topk(aff_c, cap_c)
        xs.append(_gather(m_c, lcc, cntc, offc, cap_c).astype(BF16))
        chunk_rows.append(B * cap_c)
    ys = _ffn(xs, w_gate, w_up, w_down, tuple(chunk_rows))
    out_l = _combine(ys[0], lc, lg, cnt, off, S, cap_l, 2)
    out_c = None
    if m_c is not None:
        out_c = _combine(ys[1], lcc, lgc, cntc, offc, L, cap_c, 1)
    return out_l, out_c


def kernel(x, c, ctx, c_ctx, ada_w, ada_b, norm1, norm2, ab_w_in, ab_q_norm, ab_w_uq, ab_kv_norm,
           ab_w_ukv, ab_w_o, c_w_in, c_q_gain, c_k_gain, c_w_o, moe_router, moe_w_gate, moe_w_up,
           moe_w_down, final_norm):
    B, S, D = x.shape
    L = ctx.shape[1]
    depth = ada_w.shape[0]
    tm_l, tm_c = 512, L
    row2 = lambda v: v.reshape(1, -1)

    c8 = jnp.zeros((8, D), F32).at[:B].set(c).at[B].set(c_ctx)
    mod = _ada(c8, ada_w, ada_b)
    mod_l = [mod[i, :B][:, None, :] for i in range(depth)]
    mod_c = [jnp.broadcast_to(mod[i, B][None, None, :], (B, 1, N_MOD * D)) for i in range(depth)]

    h_l, h_c = x, ctx
    moe_l = moe_c = None
    for i in range(depth):
        last = i == depth - 1
        j = i // 2
        wr_t = moe_router[i].T
        n2 = row2(norm2[i])
        if i % 2 == 0:
            assert moe_l is None
            tabs_l, tabs_c = _rope_tables(S, L, ((MLA_NOPE, MLA_ROPE // 2, "row"),
                                                 (MLA_NOPE + MLA_ROPE // 2, MLA_ROPE // 2, "col")))
            w_in = ab_w_in[j]
            w_ukv = ab_w_ukv[j]
            ekr = np.zeros((MLA_ROPE, MLA_HEADS * LANES), np.float32)
            for hd in range(MLA_HEADS):
                ekr[np.arange(MLA_ROPE), hd * LANES + MLA_NOPE + np.arange(MLA_ROPE)] = 1.0
            w = {
                "win": w_in.astype(BF16),
                "qn": row2(ab_q_norm[j]),
                "wuq": _head_slots(ab_w_uq[j], MLA_HEADS, MLA_QK, 0, MLA_QK).astype(BF16),
                "kvn": row2(ab_kv_norm[j]),
                "wuk": _head_slots(w_ukv, MLA_HEADS, MLA_NOPE + MLA_V, 0, MLA_NOPE).astype(BF16),
                "wuv": _head_slots(w_ukv, MLA_HEADS, MLA_NOPE + MLA_V, MLA_NOPE,
                                   MLA_NOPE + MLA_V).astype(BF16),
                "ekr": jnp.asarray(ekr, BF16),
            }
            n1 = row2(norm1[i])
            f_l, q_l, k_l, v_l = _in_ab(h_l, mod_l[i], n1, w, tabs_l, tm_l)
            f_c, q_c, k_c, v_c = _in_ab(h_c, mod_c[i], n1, w, tabs_c, tm_c)
            o_l = _attention(q_l[:, :, None], k_c, v_c, k_l, v_l, 512, 512)
            yf_l = _fourier_lat(f_l)
            wof = ab_w_o[j][:FNET_WIDTH].astype(BF16)
            woa = jnp.pad(ab_w_o[j][FNET_WIDTH:].reshape(MLA_HEADS, MLA_V, D),
                          ((0, 0), (0, LANES - MLA_V), (0, 0))).reshape(MLA_HEADS * LANES, D).astype(BF16)
            h_l, m_l, aff_l = _out_proj(h_l, mod_l[i], yf_l, wof, o_l, woa, n2, wr_t, tm_l)
            m_c = aff_c = None
            if not last:
                o_c = _attention(q_c[:, :, None], k_c, v_c, None, None, L, L)
                yf_c = _fourier_ctx(f_c)
                h_c, m_c, aff_c = _out_proj(h_c, mod_c[i], yf_c, wof, o_c, woa, n2, wr_t, tm_c)
        else:
            tabs_l, tabs_c = _rope_tables(S, L, ((0, GQA_HEAD_DIM // 2, "row"),
                                                 (GQA_HEAD_DIM // 2, GQA_HEAD_DIM // 2, "col")))
            w = {"win": c_w_in[j].astype(BF16), "qg": row2(c_q_gain[j]), "kg": row2(c_k_gain[j])}
            n1 = row2(norm1[i])
            h_l, q_l, k_l, v_l = _in_c(h_l, moe_l, mod_l[i - 1], mod_l[i], n1, w, tabs_l, tm_l)
            h_c, q_c, k_c, v_c = _in_c(h_c, moe_c, mod_c[i - 1], mod_c[i], n1, w, tabs_c, tm_c)
            grp = lambda q: q.reshape(B, GQA_KV_HEADS, GQA_GROUP, q.shape[2], LANES)
            o_l = _attention(grp(q_l), k_c, v_c, k_l, v_l, 256, 512)
            woa = c_w_o[j].astype(BF16)
            h_l, m_l, aff_l = _out_proj(h_l, mod_l[i], None, None, o_l, woa, n2, wr_t, tm_l)
            m_c = aff_c = None
            if not last:
                o_c = _attention(grp(q_c), k_c, v_c, None, None, L, L)
                h_c, m_c, aff_c = _out_proj(h_c, mod_c[i], None, None, o_c, woa, n2, wr_t, tm_c)
        moe_l, moe_c = _moe(m_l, aff_l, m_c, aff_c, moe_w_gate[i], moe_w_up[i], moe_w_down[i])
    return _final(h_l, moe_l, mod_l[depth - 1], row2(final_norm), tm_l)
```

```python
import functools
import math

import jax
import jax.numpy as jnp
import numpy as np
from jax import lax
from jax.experimental import pallas as pl
from jax.experimental.pallas import tpu as pltpu

F32 = jnp.float32
BF16 = jnp.bfloat16
I32 = jnp.int32
HIGHEST = lax.Precision.HIGHEST

D_MODEL = 1024
GRID_W = 64
EPS = 1e-6
ROPE_THETA = 10000.0
N_MOD = 6
FNET_GROUPS = 4
FNET_GROUP_DIM = 128
FNET_WIDTH = FNET_GROUPS * FNET_GROUP_DIM
MLA_HEADS = 8
MLA_Q_LORA = 256
MLA_KV_LORA = 128
MLA_NOPE = 64
MLA_ROPE = 32
MLA_V = 64
MLA_QK = MLA_NOPE + MLA_ROPE
GQA_HEADS = 8
GQA_KV_HEADS = 2
GQA_GROUP = GQA_HEADS // GQA_KV_HEADS
GQA_HEAD_DIM = 128
N_EXPERTS = 16
EXPERT_FF = 2048
CAPACITY_FACTOR = 2

LANES = 128
VMEM_LIMIT = 56 * 1024 * 1024
LOG2E = math.log2(math.e)


def _params(sem, vmem=None):
    return pltpu.CompilerParams(dimension_semantics=sem, vmem_limit_bytes=vmem)


def _rms(x, g):
    return x * lax.rsqrt(jnp.mean(x * x, axis=-1, keepdims=True) + EPS) * g


def _modulate(h, g, shift, scale):
    return _rms(h, g) * (1.0 + scale) + shift


def _rope(x, cos, sin_fwd, sin_bwd, shift):
    return (x * cos + pltpu.roll(x, LANES - shift, 1) * sin_fwd
            + pltpu.roll(x, shift, 1) * sin_bwd)


def _ada_kernel(c_ref, w_ref, b_ref, o_ref):
    c = c_ref[...]
    x = c * jax.nn.sigmoid(c)
    o_ref[0] = jnp.dot(x, w_ref[0], precision=HIGHEST, preferred_element_type=F32) + b_ref[0]


def _ada(c8, ada_w, ada_b):
    depth, d, n = ada_w.shape
    tn = 1536
    return pl.pallas_call(
        _ada_kernel,
        out_shape=jax.ShapeDtypeStruct((depth, 8, n), F32),
        grid=(depth, n // tn),
        in_specs=[pl.BlockSpec((8, d), lambda l, j: (0, 0)),
                  pl.BlockSpec((1, d, tn), lambda l, j: (l, 0, j)),
                  pl.BlockSpec((1, 1, tn), lambda l, j: (l, 0, j))],
        out_specs=pl.BlockSpec((1, 8, tn), lambda l, j: (l, 0, j)),
        compiler_params=_params(("arbitrary", "arbitrary"), VMEM_LIMIT),
        name="ada",
    )(c8, ada_w, ada_b.reshape(depth, 1, n))


def _in_ab_kernel(scale, *refs):
    (h_ref, mod_ref, n1_ref, win_ref, qn_ref, wuq_ref, kvn_ref, wuk_ref, wuv_ref, ekr_ref,
     cos_ref, sf_ref, sb_ref, f_ref, q_ref, k_ref, v_ref) = refs
    D = D_MODEL
    h = h_ref[0]
    mod = mod_ref[0]
    a = _modulate(h, n1_ref[...], mod[:, 0:D], mod[:, D:2 * D]).astype(BF16)
    p = jnp.dot(a, win_ref[...], preferred_element_type=F32)
    o = FNET_WIDTH
    f_ref[0] = p[:, :o]
    cq = p[:, o:o + MLA_Q_LORA]
    o += MLA_Q_LORA
    ckv = p[:, o:o + MLA_KV_LORA]
    o += MLA_KV_LORA
    kr = p[:, o:o + MLA_ROPE]
    cqn = _rms(cq, qn_ref[...]).astype(BF16)
    ckvn = _rms(ckv, kvn_ref[...]).astype(BF16)
    q = jnp.dot(cqn, wuq_ref[...], preferred_element_type=F32)
    k = (jnp.dot(ckvn, wuk_ref[...], preferred_element_type=F32)
         + jnp.dot(kr.astype(BF16), ekr_ref[...], preferred_element_type=F32))
    v = jnp.dot(ckvn, wuv_ref[...], preferred_element_type=F32)
    cos, sf, sb = cos_ref[...], sf_ref[...], sb_ref[...]
    shift = MLA_ROPE // 4
    for hd in range(MLA_HEADS):
        sl = slice(hd * LANES, (hd + 1) * LANES)
        q_ref[0, hd] = (_rope(q[:, sl], cos, sf, sb, shift) * scale).astype(BF16)
        k_ref[0, hd] = _rope(k[:, sl], cos, sf, sb, shift).astype(BF16)
        v_ref[0, hd] = v[:, sl].astype(BF16)


def _in_ab(h, mod, n1, w, tables, tm):
    B, R, D = h.shape
    H = MLA_HEADS
    full = lambda a: pl.BlockSpec(a.shape, lambda b, i: (0,) * a.ndim)
    row = lambda w_: pl.BlockSpec((1, tm, w_), lambda b, i: (b, i, 0))
    tab = pl.BlockSpec((tm, LANES), lambda b, i: (i, 0))
    hd = pl.BlockSpec((1, H, tm, LANES), lambda b, i: (b, 0, i, 0))
    weights = [n1, w["win"], w["qn"], w["wuq"], w["kvn"], w["wuk"], w["wuv"], w["ekr"]]
    return pl.pallas_call(
        functools.partial(_in_ab_kernel, MLA_QK ** -0.5 * LOG2E),
        out_shape=(jax.ShapeDtypeStruct((B, R, FNET_WIDTH), F32),
                   jax.ShapeDtypeStruct((B, H, R, LANES), BF16),
                   jax.ShapeDtypeStruct((B, H, R, LANES), BF16),
                   jax.ShapeDtypeStruct((B, H, R, LANES), BF16)),
        grid=(B, R // tm),
        in_specs=[row(D), pl.BlockSpec((1, 1, N_MOD * D), lambda b, i: (b, 0, 0))]
        + [full(a) for a in weights] + [tab, tab, tab],
        out_specs=(row(FNET_WIDTH), hd, hd, hd),
        compiler_params=_params(("arbitrary", "arbitrary"), VMEM_LIMIT),
        name="in_ab",
    )(h, mod, *weights, *tables)


def _in_c_kernel(scale, *refs):
    (h_ref, moe_ref, modp_ref, mod_ref, n1_ref, win_ref, qg_ref, kg_ref,
     cos_ref, sf_ref, sb_ref, h2_ref, q_ref, k_ref, v_ref) = refs
    D = D_MODEL
    h = h_ref[0] + modp_ref[0][:, 5 * D:6 * D] * moe_ref[0]
    h2_ref[0] = h
    mod = mod_ref[0]
    a = _modulate(h, n1_ref[...], mod[:, 0:D], mod[:, D:2 * D]).astype(BF16)
    p = jnp.dot(a, win_ref[...], preferred_element_type=F32)
    cos, sf, sb = cos_ref[...], sf_ref[...], sb_ref[...]
    shift = GQA_HEAD_DIM // 4
    nq = GQA_HEADS * GQA_HEAD_DIM
    nk = GQA_KV_HEADS * GQA_HEAD_DIM
    for hd in range(GQA_HEADS):
        x = _rms(p[:, hd * LANES:(hd + 1) * LANES], qg_ref[...])
        q_ref[0, hd] = (_rope(x, cos, sf, sb, shift) * scale).astype(BF16)
    for hd in range(GQA_KV_HEADS):
        x = _rms(p[:, nq + hd * LANES:nq + (hd + 1) * LANES], kg_ref[...])
        k_ref[0, hd] = _rope(x, cos, sf, sb, shift).astype(BF16)
        v_ref[0, hd] = p[:, nq + nk + hd * LANES:nq + nk + (hd + 1) * LANES].astype(BF16)


def _in_c(h, moe, mod_prev, mod, n1, w, tables, tm):
    B, R, D = h.shape
    full = lambda a: pl.BlockSpec(a.shape, lambda b, i: (0,) * a.ndim)
    row = lambda w_: pl.BlockSpec((1, tm, w_), lambda b, i: (b, i, 0))
    modspec = pl.BlockSpec((1, 1, N_MOD * D), lambda b, i: (b, 0, 0))
    tab = pl.BlockSpec((tm, LANES), lambda b, i: (i, 0))
    hd = lambda n: pl.BlockSpec((1, n, tm, LANES), lambda b, i: (b, 0, i, 0))
    weights = [n1, w["win"], w["qg"], w["kg"]]
    return pl.pallas_call(
        functools.partial(_in_c_kernel, GQA_HEAD_DIM ** -0.5 * LOG2E),
        out_shape=(jax.ShapeDtypeStruct((B, R, D), F32),
                   jax.ShapeDtypeStruct((B, GQA_HEADS, R, LANES), BF16),
                   jax.ShapeDtypeStruct((B, GQA_KV_HEADS, R, LANES), BF16),
                   jax.ShapeDtypeStruct((B, GQA_KV_HEADS, R, LANES), BF16)),
        grid=(B, R // tm),
        in_specs=[row(D), row(D), modspec, modspec] + [full(a) for a in weights] + [tab, tab, tab],
        out_specs=(row(D), hd(GQA_HEADS), hd(GQA_KV_HEADS), hd(GQA_KV_HEADS)),
        compiler_params=_params(("arbitrary", "arbitrary"), VMEM_LIMIT),
        name="in_c",
    )(h, moe, mod_prev, mod, *weights, *tables)


ATTN_UNROLL = 4


def _attn_kernel(n_lat, tk, *refs):
    if n_lat:
        q_ref, kc_ref, vc_ref, kl_ref, vl_ref, o_ref, acc_ref = refs
    else:
        q_ref, kc_ref, vc_ref, o_ref, acc_ref = refs
    G, tq = q_ref.shape[2], q_ref.shape[3]
    q = q_ref[0, 0].reshape(G * tq, LANES)

    def scores(k):
        return lax.dot_general(q, k, (((1,), (1,)), ((), ())), preferred_element_type=F32)

    s = scores(kc_ref[0, 0])
    m = jnp.max(s, axis=1, keepdims=True)
    p = jnp.exp2(s - m)
    l = jnp.sum(p, axis=1, keepdims=True)
    acc_ref[...] = jnp.dot(p.astype(BF16), vc_ref[0, 0], preferred_element_type=F32)

    if n_lat:
        unroll = math.gcd(n_lat, ATTN_UNROLL)

        def body(i, carry):
            m, l = carry
            starts = [pl.multiple_of((i * unroll + u) * tk, tk) for u in range(unroll)]
            ss = [scores(kl_ref[0, 0, pl.ds(st, tk), :]) for st in starts]
            for s, st in zip(ss, starts):
                m_new = jnp.maximum(m, jnp.max(s, axis=1, keepdims=True))
                p = jnp.exp2(s - m_new)
                alpha = jnp.exp2(m - m_new)
                l = alpha * l + jnp.sum(p, axis=1, keepdims=True)
                pv = jnp.dot(p.astype(BF16), vl_ref[0, 0, pl.ds(st, tk), :],
                             preferred_element_type=F32)
                acc_ref[...] = alpha * acc_ref[...] + pv
                m = m_new
            return m, l

        m, l = lax.fori_loop(0, n_lat // unroll, body, (m, l))
    o_ref[0, 0] = (acc_ref[...] / l).reshape(G, tq, LANES).astype(BF16)


def _attention(q, kc, vc, kl, vl, tq, tk):
    B, Hk, G, R, _ = q.shape
    Lc = kc.shape[2]
    n_lat = 0 if kl is None else kl.shape[2] // tk
    qspec = pl.BlockSpec((1, 1, G, tq, LANES), lambda b, h, i: (b, h, 0, i, 0))
    kv = lambda n: pl.BlockSpec((1, 1, n, LANES), lambda b, h, i: (b, h, 0, 0))
    ins = [q, kc, vc] + ([kl, vl] if n_lat else [])
    specs = [qspec, kv(Lc), kv(Lc)] + ([kv(kl.shape[2])] * 2 if n_lat else [])
    return pl.pallas_call(
        functools.partial(_attn_kernel, n_lat, tk),
        out_shape=jax.ShapeDtypeStruct(q.shape, BF16),
        grid=(B, Hk, R // tq),
        in_specs=specs,
        out_specs=qspec,
        scratch_shapes=[pltpu.VMEM((G * tq, LANES), F32)],
        compiler_params=_params(("arbitrary", "arbitrary", "arbitrary"), VMEM_LIMIT),
        name="attn",
    )(*ins)


def _dft_mats(n):
    k = np.arange(n, dtype=np.float64)
    ang = 2.0 * np.pi * np.outer(k, k) / n
    return np.cos(ang), np.sin(ang)


def _dft1_kernel(ns2, x_ref, m1_ref, tw_ref, o_ref):
    s1 = x_ref.shape[1]
    a = jnp.dot(m1_ref[...], x_ref[0], precision=HIGHEST, preferred_element_type=F32)
    are, aim = a[:s1], a[s1:]
    for i in range(ns2):
        sl = slice(i * FNET_WIDTH, (i + 1) * FNET_WIDTH)
        tre = jnp.tile(tw_ref[0, :, i * LANES:(i + 1) * LANES], (1, FNET_GROUPS))
        tim = jnp.tile(tw_ref[1, :, i * LANES:(i + 1) * LANES], (1, FNET_GROUPS))
        o_ref[0, 0, :, sl] = are[:, sl] * tre - aim[:, sl] * tim
        o_ref[0, 1, :, sl] = are[:, sl] * tim + aim[:, sl] * tre


def _dft2_kernel(kb, a_ref, m2_ref, m3_ref, o_ref):
    n2 = a_ref.shape[3]
    for j in range(kb):
        rhs = jnp.concatenate([a_ref[0, 0, j], a_ref[0, 1, j]], axis=0)
        y = jnp.dot(m2_ref[...], rhs, precision=HIGHEST, preferred_element_type=F32)
        for g in range(FNET_GROUPS):
            sl = slice(g * LANES, (g + 1) * LANES)
            lhs = jnp.concatenate([y[:n2, sl], y[n2:, sl]], axis=1)
            o_ref[0, :, j * FNET_WIDTH + g * LANES:j * FNET_WIDTH + (g + 1) * LANES] = jnp.dot(
                lhs, m3_ref[...], precision=HIGHEST, preferred_element_type=F32)


def _fourier_lat(f):
    B, S, W = f.shape
    n2 = LANES
    s1 = S // n2
    c1, sn1 = _dft_mats(s1)
    c2, sn2 = _dft_mats(n2)
    cc, sc = _dft_mats(FNET_GROUP_DIM)
    m1 = jnp.asarray(np.concatenate([c1, -sn1], axis=0), F32)
    m2 = jnp.asarray(np.block([[c2, sn2], [-sn2, c2]]), F32)
    norm = 1.0 / math.sqrt(S * FNET_GROUP_DIM)
    m3 = jnp.asarray(np.concatenate([cc, sc], axis=0) * norm, F32)
    ang = 2.0 * np.pi * np.outer(np.arange(s1), np.arange(n2)) / S
    tw = np.stack([np.cos(ang), -np.sin(ang)])
    tw = jnp.asarray(np.repeat(tw[:, :, :, None], LANES, axis=3).reshape(2, s1, n2 * LANES), F32)

    ns2 = 8
    x = f.reshape(B, s1, n2 * W)
    a = pl.pallas_call(
        functools.partial(_dft1_kernel, ns2),
        out_shape=jax.ShapeDtypeStruct((B, 2, s1, n2 * W), F32),
        grid=(B, n2 // ns2),
        in_specs=[pl.BlockSpec((1, s1, ns2 * W), lambda b, j: (b, 0, j)),
                  pl.BlockSpec(m1.shape, lambda b, j: (0, 0)),
                  pl.BlockSpec((2, s1, ns2 * LANES), lambda b, j: (0, 0, j))],
        out_specs=pl.BlockSpec((1, 2, s1, ns2 * W), lambda b, j: (b, 0, 0, j)),
        compiler_params=_params(("arbitrary", "arbitrary"), VMEM_LIMIT),
        name="dft1",
    )(x, m1, tw)
    a = a.reshape(B, 2, s1, n2, W)
    kb = min(8, s1)
    y = pl.pallas_call(
        functools.partial(_dft2_kernel, kb),
        out_shape=jax.ShapeDtypeStruct((B, n2, s1 * W), F32),
        grid=(B, s1 // kb),
        in_specs=[pl.BlockSpec((1, 2, kb, n2, W), lambda b, j: (b, 0, j, 0, 0)),
                  pl.BlockSpec(m2.shape, lambda b, j: (0, 0)),
                  pl.BlockSpec(m3.shape, lambda b, j: (0, 0))],
        out_specs=pl.BlockSpec((1, n2, kb * W), lambda b, j: (b, 0, j)),
        compiler_params=_params(("arbitrary", "arbitrary"), VMEM_LIMIT),
        name="dft2",
    )(a, m2, m3)
    return y.reshape(B, S, W)


def _dftc_kernel(f_ref, mc_ref, m3_ref, o_ref):
    n = f_ref.shape[1]
    a = jnp.dot(mc_ref[...], f_ref[0], precision=HIGHEST, preferred_element_type=F32)
    for g in range(FNET_GROUPS):
        sl = slice(g * LANES, (g + 1) * LANES)
        lhs = jnp.concatenate([a[:n, sl], a[n:, sl]], axis=1)
        o_ref[0, :, sl] = jnp.dot(lhs, m3_ref[...], precision=HIGHEST, preferred_element_type=F32)


def _fourier_ctx(f):
    B, L, W = f.shape
    c, s = _dft_mats(L)
    cc, sc = _dft_mats(FNET_GROUP_DIM)
    mc = jnp.asarray(np.concatenate([c, -s], axis=0), F32)
    m3 = jnp.asarray(np.concatenate([cc, sc], axis=0) / math.sqrt(L * FNET_GROUP_DIM), F32)
    return pl.pallas_call(
        _dftc_kernel,
        out_shape=jax.ShapeDtypeStruct((B, L, W), F32),
        grid=(B,),
        in_specs=[pl.BlockSpec((1, L, W), lambda b: (b, 0, 0)),
                  pl.BlockSpec(mc.shape, lambda b: (0, 0)),
                  pl.BlockSpec(m3.shape, lambda b: (0, 0))],
        out_specs=pl.BlockSpec((1, L, W), lambda b: (b, 0, 0)),
        compiler_params=_params(("arbitrary",), VMEM_LIMIT),
        name="dftc",
    )(f, mc, m3)


def _out_kernel(has_f, *refs):
    if has_f:
        (h_ref, mod_ref, yf_ref, wof_ref, o_ref, woa_ref, n2_ref, wr_ref,
         h1_ref, m_ref, aff_ref) = refs
    else:
        (h_ref, mod_ref, o_ref, woa_ref, n2_ref, wr_ref, h1_ref, m_ref, aff_ref) = refs
    D = D_MODEL
    Hk, G = o_ref.shape[1], o_ref.shape[2]
    ocat = jnp.concatenate([o_ref[0, hk, g] for hk in range(Hk) for g in range(G)], axis=1)
    y = jnp.dot(ocat, woa_ref[...], preferred_element_type=F32)
    if has_f:
        y = y + jnp.dot(yf_ref[0].astype(BF16), wof_ref[...], preferred_element_type=F32)
    mod = mod_ref[0]
    h1 = h_ref[0] + mod[:, 2 * D:3 * D] * y
    h1_ref[0] = h1
    m = _modulate(h1, n2_ref[...], mod[:, 3 * D:4 * D], mod[:, 4 * D:5 * D])
    m_ref[0] = m
    logit = lax.dot_general(wr_ref[...], m, (((1,), (1,)), ((), ())), precision=HIGHEST,
                            preferred_element_type=F32)
    e = jnp.exp(logit - jnp.max(logit, axis=0, keepdims=True))
    aff_ref[0] = e / jnp.sum(e, axis=0, keepdims=True)


def _out_proj(h, mod, yf, wof, o, woa, n2, wr_t, tm):
    B, R, D = h.shape
    _, Hk, G, _, _ = o.shape
    full = lambda a: pl.BlockSpec(a.shape, lambda b, i: (0,) * a.ndim)
    row = lambda w_: pl.BlockSpec((1, tm, w_), lambda b, i: (b, i, 0))
    modspec = pl.BlockSpec((1, 1, N_MOD * D), lambda b, i: (b, 0, 0))
    ospec = pl.BlockSpec((1, Hk, G, tm, LANES), lambda b, i: (b, 0, 0, i, 0))
    has_f = yf is not None
    ins = [h, mod] + ([yf, wof] if has_f else []) + [o, woa, n2, wr_t]
    specs = ([row(D), modspec] + ([row(FNET_WIDTH), full(wof)] if has_f else [])
             + [ospec, full(woa), full(n2), full(wr_t)])
    return pl.pallas_call(
        functools.partial(_out_kernel, has_f),
        out_shape=(jax.ShapeDtypeStruct((B, R, D), F32), jax.ShapeDtypeStruct((B, R, D), F32),
                   jax.ShapeDtypeStruct((B, N_EXPERTS, R), F32)),
        grid=(B, R // tm),
        in_specs=specs,
        out_specs=(row(D), row(D), pl.BlockSpec((1, N_EXPERTS, tm), lambda b, i: (b, 0, i))),
        compiler_params=_params(("arbitrary", "arbitrary"), VMEM_LIMIT),
        name="out_proj",
    )(*ins)


def _topk_kernel(cap, aff_ref, u_ref, ones_ref, lmat_ref, lc_ref, lg_ref, cnt_ref, off_ref,
                 pos_ref, ac_ref):
    a = aff_ref[0]
    E, N = a.shape
    NC = N // LANES
    R = NC * E
    keys = pltpu.bitcast(a, I32)

    def bit_step(i, tau):
        cand = tau | jnp.left_shift(jnp.int32(1), 30 - i)
        cnt = jnp.sum((keys >= cand).astype(I32), axis=1, keepdims=True)
        return jnp.where(cnt >= cap, cand, tau)

    tau = lax.fori_loop(0, 31, bit_step, jnp.zeros((E, 1), I32))
    gt = (keys > tau).astype(F32)
    eq = (keys == tau).astype(F32)
    need = (cap - jnp.sum(gt, axis=1, keepdims=True))

    def chunked(x):
        return jnp.concatenate([x[:, c * LANES:(c + 1) * LANES] for c in range(NC)], axis=0)

    a_c, gt_c, eq_c = chunked(a), chunked(gt), chunked(eq)
    need_c = jnp.tile(need, (NC, 1))

    def prefix(x):
        xb = x.astype(BF16)
        loc = jnp.dot(xb, u_ref[...], preferred_element_type=F32)
        tot = jnp.dot(xb, ones_ref[...], preferred_element_type=F32)
        offs = jnp.dot(lmat_ref[...], tot.astype(BF16), preferred_element_type=F32)
        return loc, tot, offs

    loc, tot, offs = prefix(eq_c)
    sel = jnp.maximum(gt_c, jnp.where(loc + offs < need_c, eq_c, 0.0))
    loc, tot, offs = prefix(sel)
    pos_ref[...] = jnp.where(sel > 0.0, loc, -1.0)
    ac_ref[...] = a_c
    cnt_ref[0] = tot.astype(I32)
    off_ref[0] = offs.astype(I32)
    rb = 16
    lane = lax.broadcasted_iota(I32, (rb, LANES), 1).astype(F32)

    def compact(i, carry):
        rows = pl.ds(pl.multiple_of(i * rb, rb), rb)
        selpos = pos_ref[rows, :]
        aff = ac_ref[rows, :]
        lc = jnp.zeros((rb, LANES), F32)
        lg = jnp.zeros((rb, LANES), F32)
        for t in range(LANES):
            hit = selpos[:, t:t + 1] == lane
            lc = jnp.where(hit, float(t), lc)
            lg = jnp.where(hit, aff[:, t:t + 1], lg)
        lc_ref[0, rows, :] = lc.astype(I32)
        lg_ref[0, rows, :] = lg
        return carry

    lax.fori_loop(0, R // rb, compact, 0)


def _topk(aff, cap):
    B, E, N = aff.shape
    NC = N // LANES
    R = NC * E
    i = np.arange(LANES)
    u = jnp.asarray(i[:, None] < i[None, :], BF16)
    ones = jnp.ones((LANES, LANES), BF16)
    r = np.arange(R)
    lmat = jnp.asarray((r[:, None] % E == r[None, :] % E) & (r[None, :] // E < r[:, None] // E), BF16)
    full = lambda a: pl.BlockSpec(a.shape, lambda b: (0,) * a.ndim)
    ospec = pl.BlockSpec((1, R, LANES), lambda b: (b, 0, 0))
    sds = lambda dt: jax.ShapeDtypeStruct((B, R, LANES), dt)
    lc, lg, cnt, off = pl.pallas_call(
        functools.partial(_topk_kernel, cap),
        out_shape=(sds(I32), sds(F32), sds(I32), sds(I32)),
        grid=(B,),
        in_specs=[pl.BlockSpec((1, E, N), lambda b: (b, 0, 0)), full(u), full(ones), full(lmat)],
        out_specs=(ospec, ospec, ospec, ospec),
        scratch_shapes=[pltpu.VMEM((R, LANES), F32), pltpu.VMEM((R, LANES), F32)],
        compiler_params=_params(("arbitrary",), VMEM_LIMIT),
        name="topk",
    )(aff, u, ones, lmat)
    by_expert = lambda x: x.reshape(B, NC, E, LANES).transpose(0, 2, 1, 3)
    return (by_expert(lc), by_expert(lg), by_expert(cnt)[..., 0], by_expert(off)[..., 0])


def _gather_kernel(cap, cnt_ref, off_ref, lc_ref, m_hbm, xs_hbm, sem):
    b = pl.program_id(0)
    e = pl.program_id(1)
    NC = lc_ref.shape[0]

    def chunk_body(c, carry):
        base = off_ref[b, e, c] + b * cap

        def row_body(r, carry):
            t = c * LANES + lc_ref[c, r]
            pltpu.make_async_copy(m_hbm.at[b, t], xs_hbm.at[e, base + r], sem).start()
            return carry

        return lax.fori_loop(0, cnt_ref[b, e, c], row_body, carry)

    lax.fori_loop(0, NC, chunk_body, 0)
    done = xs_hbm.at[e, pl.ds(b * cap, cap)]
    pltpu.make_async_copy(done, done, sem).wait()


def _gather(m, lc, cnt, off, cap):
    B, N, sub, _ = m.shape
    E, NC = lc.shape[1], lc.shape[2]
    return pl.pallas_call(
        functools.partial(_gather_kernel, cap),
        out_shape=jax.ShapeDtypeStruct((E, B * cap, sub, LANES), m.dtype),
        grid_spec=pltpu.PrefetchScalarGridSpec(
            num_scalar_prefetch=2,
            grid=(B, E),
            in_specs=[pl.BlockSpec((None, None, NC, LANES), lambda b, e, *_: (b, e, 0, 0),
                                   memory_space=pltpu.SMEM),
                      pl.BlockSpec(memory_space=pl.ANY)],
            out_specs=pl.BlockSpec(memory_space=pl.ANY),
            scratch_shapes=[pltpu.SemaphoreType.DMA(())]),
        compiler_params=_params(("arbitrary", "arbitrary")),
        name="gather",
    )(cnt, off, lc, m)


def _ffn_kernel(n_parts, chunk_rows, *refs):
    wg_ref, wu_ref, wd_ref = refs[:3]
    x_refs = refs[3:3 + n_parts]
    o_refs = refs[3 + n_parts:3 + 2 * n_parts]
    j = pl.program_id(1)
    wg = wg_ref[0].astype(BF16)
    wu = wu_ref[0].astype(BF16)
    wd = wd_ref[0].astype(BF16)
    for x_ref, o_ref, rows in zip(x_refs, o_refs, chunk_rows):
        for r0 in range(0, x_ref.shape[1], rows):
            x = x_ref[0, r0:r0 + rows, :]
            a = jnp.dot(x, wg, preferred_element_type=F32)
            u = jnp.dot(x, wu, preferred_element_type=F32)
            hh = (a * jax.nn.sigmoid(a) * u).astype(BF16)
            y = jnp.dot(hh, wd, preferred_element_type=F32)

            @pl.when(j == 0)
            def _():
                o_ref[0, r0:r0 + rows, :] = y

            @pl.when(j > 0)
            def _():
                o_ref[0, r0:r0 + rows, :] += y


def _ffn(xs, w_gate, w_up, w_down, chunk_rows):
    E, D, F = w_gate.shape
    tf = 512
    xspec = lambda x: pl.BlockSpec((1, x.shape[1], D), lambda e, j: (e, 0, 0))
    outs = pl.pallas_call(
        functools.partial(_ffn_kernel, len(xs), chunk_rows),
        out_shape=tuple(jax.ShapeDtypeStruct(x.shape, F32) for x in xs),
        grid=(E, F // tf),
        in_specs=[pl.BlockSpec((1, D, tf), lambda e, j: (e, 0, j)),
                  pl.BlockSpec((1, D, tf), lambda e, j: (e, 0, j)),
                  pl.BlockSpec((1, tf, D), lambda e, j: (e, j, 0))] + [xspec(x) for x in xs],
        out_specs=tuple(xspec(x) for x in xs),
        compiler_params=_params(("arbitrary", "arbitrary"), VMEM_LIMIT),
        name="ffn",
    )(w_gate, w_up, w_down, *xs)
    return outs


def _combine_kernel(cnt_ref, off_ref, lc_ref, lg_ref, y_ref, o_ref):
    b = pl.program_id(0)
    hf = pl.program_id(1)
    e = pl.program_id(2)
    nch = o_ref.shape[1] // LANES

    @pl.when(e == 0)
    def _():
        o_ref[...] = jnp.zeros_like(o_ref)

    def chunk_body(ci, carry):
        c = hf * nch + ci
        base = off_ref[b, e, c]

        def row_body(r, carry):
            t = ci * LANES + lc_ref[c, r]
            o_ref[0, t] = o_ref[0, t] + lg_ref[c, r] * y_ref[0, base + r]
            return carry

        return lax.fori_loop(0, cnt_ref[b, e, c], row_body, carry)

    lax.fori_loop(0, nch, chunk_body, 0)


def _combine(y, lc, lg, cnt, off, n_tokens, cap, n_split):
    E, _, D = y.shape
    B, _, NC, _ = lc.shape
    sub = D // LANES
    y3 = y.reshape(E, y.shape[1], sub, LANES)
    nh = n_tokens // n_split
    smem = lambda: pl.BlockSpec((None, None, NC, LANES), lambda b, h, e, *_: (b, e, 0, 0),
                                memory_space=pltpu.SMEM)
    out = pl.pallas_call(
        _combine_kernel,
        out_shape=jax.ShapeDtypeStruct((B, n_tokens, sub, LANES), F32),
        grid_spec=pltpu.PrefetchScalarGridSpec(
            num_scalar_prefetch=2,
            grid=(B, n_split, E),
            in_specs=[smem(), smem(),
                      pl.BlockSpec((1, cap, sub, LANES), lambda b, h, e, *_: (e, b, 0, 0))],
            out_specs=pl.BlockSpec((1, nh, sub, LANES), lambda b, h, e, *_: (b, h, 0, 0))),
        compiler_params=_params(("arbitrary", "arbitrary", "arbitrary"), VMEM_LIMIT),
        name="combine",
    )(cnt, off, lc, lg, y3)
    return out.reshape(B, n_tokens, D)


def _final_kernel(h_ref, moe_ref, mod_ref, g_ref, o_ref):
    D = D_MODEL
    h = h_ref[0] + mod_ref[0][:, 5 * D:6 * D] * moe_ref[0]
    o_ref[0] = _rms(h, g_ref[...])


def _final(h, moe, mod, g, tm):
    B, R, D = h.shape
    row = pl.BlockSpec((1, tm, D), lambda b, i: (b, i, 0))
    return pl.pallas_call(
        _final_kernel,
        out_shape=jax.ShapeDtypeStruct((B, R, D), F32),
        grid=(B, R // tm),
        in_specs=[row, row, pl.BlockSpec((1, 1, N_MOD * D), lambda b, i: (b, 0, 0)),
                  pl.BlockSpec((1, D), lambda b, i: (0, 0))],
        out_specs=row,
        compiler_params=_params(("arbitrary", "arbitrary"), VMEM_LIMIT),
        name="final",
    )(h, moe, mod, g)


def _rope_tables(n_lat, n_ctx, segments):
    t = jnp.arange(n_lat)
    pos = {"row": (t // GRID_W).astype(F32), "col": (t % GRID_W).astype(F32)}
    cos = jnp.ones((n_lat, LANES), F32)
    sf = jnp.zeros((n_lat, LANES), F32)
    sb = jnp.zeros((n_lat, LANES), F32)
    for lane0, width, which in segments:
        half = width // 2
        inv = ROPE_THETA ** (-jnp.arange(half, dtype=F32) / half)
        ang = pos[which][:, None] * inv[None, :]
        c, s = jnp.cos(ang), jnp.sin(ang)
        cos = cos.at[:, lane0:lane0 + half].set(c).at[:, lane0 + half:lane0 + width].set(c)
        sf = sf.at[:, lane0:lane0 + half].set(-s)
        sb = sb.at[:, lane0 + half:lane0 + width].set(s)
    lat = (cos, sf, sb)
    ctx = (jnp.ones((n_ctx, LANES), F32), jnp.zeros((n_ctx, LANES), F32), jnp.zeros((n_ctx, LANES), F32))
    return lat, ctx


def _head_slots(w, n_heads, width, lo, hi):
    k = w.shape[0]
    w3 = w.reshape(k, n_heads, width)[:, :, lo:hi]
    return jnp.pad(w3, ((0, 0), (0, 0), (0, LANES - (hi - lo)))).reshape(k, n_heads * LANES)


def _moe(m_l, aff_l, m_c, aff_c, w_gate, w_up, w_down):
    B, S, D = m_l.shape
    cap_l = CAPACITY_FACTOR * S // N_EXPERTS
    lc, lg, cnt, off = _topk(aff_l, cap_l)
    tiled = lambda m: m.reshape(m.shape[0], m.shape[1], D // LANES, LANES)
    rows = lambda x: x.reshape(x.shape[0], x.shape[1], D).astype(BF16)
    xs = [rows(_gather(tiled(m_l), lc, cnt, off, cap_l))]
    chunk_rows = [cap_l]
    if m_c is not None:
        L = m_c.shape[1]
        cap_c = CAPACITY_FACTOR * L // N_EXPERTS
        lcc, lgc, cntc, offc = _topk(aff_c, cap_c)
        xs.append(rows(_gather(tiled(m_c), lcc, cntc, offc, cap_c)))
        chunk_rows.append(B * cap_c)
    ys = _ffn(xs, w_gate, w_up, w_down, tuple(chunk_rows))
    out_l = _combine(ys[0], lc, lg, cnt, off, S, cap_l, 2)
    out_c = None
    if m_c is not None:
        out_c = _combine(ys[1], lcc, lgc, cntc, offc, L, cap_c, 1)
    return out_l, out_c


def kernel(x, c, ctx, c_ctx, ada_w, ada_b, norm1, norm2, ab_w_in, ab_q_norm, ab_w_uq, ab_kv_norm,
           ab_w_ukv, ab_w_o, c_w_in, c_q_gain, c_k_gain, c_w_o, moe_router, moe_w_gate, moe_w_up,
           moe_w_down, final_norm):
    B, S, D = x.shape
    L = ctx.shape[1]
    depth = ada_w.shape[0]
    tm_l, tm_c = 512, L
    row2 = lambda v: v.reshape(1, -1)

    c8 = jnp.zeros((8, D), F32).at[:B].set(c).at[B].set(c_ctx)
    mod = _ada(c8, ada_w, ada_b)
    mod_l = [mod[i, :B][:, None, :] for i in range(depth)]
    mod_c = [jnp.broadcast_to(mod[i, B][None, None, :], (B, 1, N_MOD * D)) for i in range(depth)]

    h_l, h_c = x, ctx
    moe_l = moe_c = None
    for i in range(depth):
        last = i == depth - 1
        j = i // 2
        wr_t = moe_router[i].T
        n2 = row2(norm2[i])
        if i % 2 == 0:
            assert moe_l is None
            tabs_l, tabs_c = _rope_tables(S, L, ((MLA_NOPE, MLA_ROPE // 2, "row"),
                                                 (MLA_NOPE + MLA_ROPE // 2, MLA_ROPE // 2, "col")))
            w_in = ab_w_in[j]
            w_ukv = ab_w_ukv[j]
            ekr = np.zeros((MLA_ROPE, MLA_HEADS * LANES), np.float32)
            for hd in range(MLA_HEADS):
                ekr[np.arange(MLA_ROPE), hd * LANES + MLA_NOPE + np.arange(MLA_ROPE)] = 1.0
            w = {
                "win": w_in.astype(BF16),
                "qn": row2(ab_q_norm[j]),
                "wuq": _head_slots(ab_w_uq[j], MLA_HEADS, MLA_QK, 0, MLA_QK).astype(BF16),
                "kvn": row2(ab_kv_norm[j]),
                "wuk": _head_slots(w_ukv, MLA_HEADS, MLA_NOPE + MLA_V, 0, MLA_NOPE).astype(BF16),
                "wuv": _head_slots(w_ukv, MLA_HEADS, MLA_NOPE + MLA_V, MLA_NOPE,
                                   MLA_NOPE + MLA_V).astype(BF16),
                "ekr": jnp.asarray(ekr, BF16),
            }
            n1 = row2(norm1[i])
            f_l, q_l, k_l, v_l = _in_ab(h_l, mod_l[i], n1, w, tabs_l, tm_l)
            f_c, q_c, k_c, v_c = _in_ab(h_c, mod_c[i], n1, w, tabs_c, tm_c)
            o_l = _attention(q_l[:, :, None], k_c, v_c, k_l, v_l, 512, 512)
            yf_l = _fourier_lat(f_l)
            wof = ab_w_o[j][:FNET_WIDTH].astype(BF16)
            woa = jnp.pad(ab_w_o[j][FNET_WIDTH:].reshape(MLA_HEADS, MLA_V, D),
                          ((0, 0), (0, LANES - MLA_V), (0, 0))).reshape(MLA_HEADS * LANES, D).astype(BF16)
            h_l, m_l, aff_l = _out_proj(h_l, mod_l[i], yf_l, wof, o_l, woa, n2, wr_t, tm_l)
            m_c = aff_c = None
            if not last:
                o_c = _attention(q_c[:, :, None], k_c, v_c, None, None, L, L)
                yf_c = _fourier_ctx(f_c)
                h_c, m_c, aff_c = _out_proj(h_c, mod_c[i], yf_c, wof, o_c, woa, n2, wr_t, tm_c)
        else:
            tabs_l, tabs_c = _rope_tables(S, L, ((0, GQA_HEAD_DIM // 2, "row"),
                                                 (GQA_HEAD_DIM // 2, GQA_HEAD_DIM // 2, "col")))
            w = {"win": c_w_in[j].astype(BF16), "qg": row2(c_q_gain[j]), "kg": row2(c_k_gain[j])}
            n1 = row2(norm1[i])
            h_l, q_l, k_l, v_l = _in_c(h_l, moe_l, mod_l[i - 1], mod_l[i], n1, w, tabs_l, tm_l)
            h_c, q_c, k_c, v_c = _in_c(h_c, moe_c, mod_c[i - 1], mod_c[i], n1, w, tabs_c, tm_c)
            grp = lambda q: q.reshape(B, GQA_KV_HEADS, GQA_GROUP, q.shape[2], LANES)
            o_l = _attention(grp(q_l), k_c, v_c, k_l, v_l, 256, 512)
            woa = c_w_o[j].astype(BF16)
            h_l, m_l, aff_l = _out_proj(h_l, mod_l[i], None, None, o_l, woa, n2, wr_t, tm_l)
            m_c = aff_c = None
            if not last:
                o_c = _attention(grp(q_c), k_c, v_c, None, None, L, L)
                h_c, m_c, aff_c = _out_proj(h_c, mod_c[i], None, None, o_c, woa, n2, wr_t, tm_c)
        moe_l, moe_c = _moe(m_l, aff_l, m_c, aff_c, moe_w_gate[i], moe_w_up[i], moe_w_down[i])
    return _final(h_l, moe_l, mod_l[depth - 1], row2(final_norm), tm_l)
```

```python
import functools
import math

import jax
import jax.numpy as jnp
import numpy as np
from jax import lax
from jax.experimental import pallas as pl
from jax.experimental.pallas import tpu as pltpu

F32 = jnp.float32
BF16 = jnp.bfloat16
I32 = jnp.int32
HIGHEST = lax.Precision.HIGHEST

D_MODEL = 1024
GRID_W = 64
EPS = 1e-6
ROPE_THETA = 10000.0
N_MOD = 6
FNET_GROUPS = 4
FNET_GROUP_DIM = 128
FNET_WIDTH = FNET_GROUPS * FNET_GROUP_DIM
MLA_HEADS = 8
MLA_Q_LORA = 256
MLA_KV_LORA = 128
MLA_NOPE = 64
MLA_ROPE = 32
MLA_V = 64
MLA_QK = MLA_NOPE + MLA_ROPE
GQA_HEADS = 8
GQA_KV_HEADS = 2
GQA_GROUP = GQA_HEADS // GQA_KV_HEADS
GQA_HEAD_DIM = 128
N_EXPERTS = 16
EXPERT_FF = 2048
CAPACITY_FACTOR = 2

LANES = 128
SUB = D_MODEL // LANES
VMEM_LIMIT = 56 * 1024 * 1024
LOG2E = math.log2(math.e)
NT = (((1,), (1,)), ((), ()))


def _params(sem, vmem=None):
    return pltpu.CompilerParams(dimension_semantics=sem, vmem_limit_bytes=vmem)


def _rms(x, g):
    return x * lax.rsqrt(jnp.mean(x * x, axis=-1, keepdims=True) + EPS) * g


def _modulate(h, g, shift, scale):
    return _rms(h, g) * (1.0 + scale) + shift


def _rope(x, cos, sin_fwd, sin_bwd, shift):
    return (x * cos + pltpu.roll(x, LANES - shift, 1) * sin_fwd
            + pltpu.roll(x, shift, 1) * sin_bwd)


def _untile(ref, rows):
    return jnp.concatenate([ref[rows, s, :] for s in range(SUB)], axis=1)


def _store_tiled(ref, rows, x):
    for s in range(SUB):
        ref[rows, s, :] = x[:, s * LANES:(s + 1) * LANES]


def _ada_kernel(c_ref, w_ref, b_ref, o_ref):
    c = c_ref[...]
    x = c * jax.nn.sigmoid(c)
    o_ref[0] = jnp.dot(x, w_ref[0], precision=HIGHEST, preferred_element_type=F32) + b_ref[0]


def _ada(c8, ada_w, ada_b):
    depth, d, n = ada_w.shape
    tn = 1536
    return pl.pallas_call(
        _ada_kernel,
        out_shape=jax.ShapeDtypeStruct((depth, 8, n), F32),
        grid=(depth, n // tn),
        in_specs=[pl.BlockSpec((8, d), lambda l, j: (0, 0)),
                  pl.BlockSpec((1, d, tn), lambda l, j: (l, 0, j)),
                  pl.BlockSpec((1, 1, tn), lambda l, j: (l, 0, j))],
        out_specs=pl.BlockSpec((1, 8, tn), lambda l, j: (l, 0, j)),
        compiler_params=_params(("arbitrary", "arbitrary"), VMEM_LIMIT),
        name="ada",
    )(c8, ada_w, ada_b.reshape(depth, 1, n))


def _in_ab_kernel(scale, *refs):
    (h_ref, mod_ref, n1_ref, win_ref, qn_ref, wuq_ref, kvn_ref, wuk_ref, wuvt_ref, ekr_ref,
     cos_ref, sf_ref, sb_ref, f_ref, q_ref, k_ref, vt_ref) = refs
    D = D_MODEL
    h = h_ref[0]
    mod = mod_ref[0]
    a = _modulate(h, n1_ref[...], mod[:, 0:D], mod[:, D:2 * D]).astype(BF16)
    p = jnp.dot(a, win_ref[...], preferred_element_type=F32)
    o = FNET_WIDTH
    f_ref[0] = p[:, :o]
    cq = p[:, o:o + MLA_Q_LORA]
    o += MLA_Q_LORA
    ckv = p[:, o:o + MLA_KV_LORA]
    o += MLA_KV_LORA
    kr = p[:, o:o + MLA_ROPE]
    cqn = _rms(cq, qn_ref[...]).astype(BF16)
    ckvn = _rms(ckv, kvn_ref[...]).astype(BF16)
    q = jnp.dot(cqn, wuq_ref[...], preferred_element_type=F32)
    k = (jnp.dot(ckvn, wuk_ref[...], preferred_element_type=F32)
         + jnp.dot(kr.astype(BF16), ekr_ref[...], preferred_element_type=F32))
    vt = lax.dot_general(wuvt_ref[...], ckvn, NT, preferred_element_type=F32)
    cos, sf, sb = cos_ref[...], sf_ref[...], sb_ref[...]
    shift = MLA_ROPE // 4
    for hd in range(MLA_HEADS):
        sl = slice(hd * LANES, (hd + 1) * LANES)
        q_ref[0, hd] = (_rope(q[:, sl], cos, sf, sb, shift) * scale).astype(BF16)
        k_ref[0, hd] = _rope(k[:, sl], cos, sf, sb, shift).astype(BF16)
        vt_ref[0, hd] = vt[sl, :].astype(BF16)


def _in_ab(h, mod, n1, w, tables, tm):
    B, R, D = h.shape
    H = MLA_HEADS
    full = lambda a: pl.BlockSpec(a.shape, lambda b, i: (0,) * a.ndim)
    row = lambda w_: pl.BlockSpec((1, tm, w_), lambda b, i: (b, i, 0))
    tab = pl.BlockSpec((tm, LANES), lambda b, i: (i, 0))
    hd = pl.BlockSpec((1, H, tm, LANES), lambda b, i: (b, 0, i, 0))
    hdt = pl.BlockSpec((1, H, LANES, tm), lambda b, i: (b, 0, 0, i))
    weights = [n1, w["win"], w["qn"], w["wuq"], w["kvn"], w["wuk"], w["wuvt"], w["ekr"]]
    return pl.pallas_call(
        functools.partial(_in_ab_kernel, MLA_QK ** -0.5 * LOG2E),
        out_shape=(jax.ShapeDtypeStruct((B, R, FNET_WIDTH), F32),
                   jax.ShapeDtypeStruct((B, H, R, LANES), BF16),
                   jax.ShapeDtypeStruct((B, H, R, LANES), BF16),
                   jax.ShapeDtypeStruct((B, H, LANES, R), BF16)),
        grid=(B, R // tm),
        in_specs=[row(D), pl.BlockSpec((1, 1, N_MOD * D), lambda b, i: (b, 0, 0))]
        + [full(a) for a in weights] + [tab, tab, tab],
        out_specs=(row(FNET_WIDTH), hd, hd, hdt),
        compiler_params=_params(("arbitrary", "arbitrary"), VMEM_LIMIT),
        name="in_ab",
    )(h, mod, *weights, *tables)


def _in_c_kernel(scale, *refs):
    (h_ref, moe_ref, modp_ref, mod_ref, n1_ref, wqk_ref, wvt_ref, qg_ref, kg_ref,
     cos_ref, sf_ref, sb_ref, h2_ref, q_ref, k_ref, vt_ref) = refs
    D = D_MODEL
    h = h_ref[0] + modp_ref[0][:, 5 * D:6 * D] * _untile(moe_ref.at[0], slice(None))
    h2_ref[0] = h
    mod = mod_ref[0]
    a = _modulate(h, n1_ref[...], mod[:, 0:D], mod[:, D:2 * D]).astype(BF16)
    p = jnp.dot(a, wqk_ref[...], preferred_element_type=F32)
    vt = lax.dot_general(wvt_ref[...], a, NT, preferred_element_type=F32)
    cos, sf, sb = cos_ref[...], sf_ref[...], sb_ref[...]
    shift = GQA_HEAD_DIM // 4
    nq = GQA_HEADS * GQA_HEAD_DIM
    for hd in range(GQA_HEADS):
        x = _rms(p[:, hd * LANES:(hd + 1) * LANES], qg_ref[...])
        q_ref[0, hd] = (_rope(x, cos, sf, sb, shift) * scale).astype(BF16)
    for hd in range(GQA_KV_HEADS):
        x = _rms(p[:, nq + hd * LANES:nq + (hd + 1) * LANES], kg_ref[...])
        k_ref[0, hd] = _rope(x, cos, sf, sb, shift).astype(BF16)
        vt_ref[0, hd] = vt[hd * LANES:(hd + 1) * LANES, :].astype(BF16)


def _in_c(h, moe, mod_prev, mod, n1, w, tables, tm):
    B, R, D = h.shape
    full = lambda a: pl.BlockSpec(a.shape, lambda b, i: (0,) * a.ndim)
    row = lambda w_: pl.BlockSpec((1, tm, w_), lambda b, i: (b, i, 0))
    tiled = pl.BlockSpec((1, tm, SUB, LANES), lambda b, i: (b, i, 0, 0))
    modspec = pl.BlockSpec((1, 1, N_MOD * D), lambda b, i: (b, 0, 0))
    tab = pl.BlockSpec((tm, LANES), lambda b, i: (i, 0))
    hd = lambda n: pl.BlockSpec((1, n, tm, LANES), lambda b, i: (b, 0, i, 0))
    hdt = pl.BlockSpec((1, GQA_KV_HEADS, LANES, tm), lambda b, i: (b, 0, 0, i))
    weights = [n1, w["wqk"], w["wvt"], w["qg"], w["kg"]]
    return pl.pallas_call(
        functools.partial(_in_c_kernel, GQA_HEAD_DIM ** -0.5 * LOG2E),
        out_shape=(jax.ShapeDtypeStruct((B, R, D), F32),
                   jax.ShapeDtypeStruct((B, GQA_HEADS, R, LANES), BF16),
                   jax.ShapeDtypeStruct((B, GQA_KV_HEADS, R, LANES), BF16),
                   jax.ShapeDtypeStruct((B, GQA_KV_HEADS, LANES, R), BF16)),
        grid=(B, R // tm),
        in_specs=[row(D), tiled, modspec, modspec] + [full(a) for a in weights] + [tab, tab, tab],
        out_specs=(row(D), hd(GQA_HEADS), hd(GQA_KV_HEADS), hdt),
        compiler_params=_params(("arbitrary", "arbitrary"), VMEM_LIMIT),
        name="in_c",
    )(h, moe, mod_prev, mod, *weights, *tables)


ATTN_UNROLL = 2


def _attn_kernel(n_lat, tk, *refs):
    if n_lat:
        q_ref, kc_ref, vct_ref, kl_ref, vlt_ref, o_ref, acc_ref = refs
    else:
        q_ref, kc_ref, vct_ref, o_ref, acc_ref = refs
    G, tq = q_ref.shape[2], q_ref.shape[3]
    q = q_ref[0, 0].reshape(G * tq, LANES)

    def scores(k):
        return lax.dot_general(k, q, NT, preferred_element_type=F32)

    s = scores(kc_ref[0, 0])
    m = jnp.max(s, axis=0, keepdims=True)
    p = jnp.exp2(s - m)
    l = jnp.sum(p, axis=0, keepdims=True)
    acc_ref[...] = jnp.dot(vct_ref[0, 0], p.astype(BF16), preferred_element_type=F32)

    if n_lat:
        unroll = math.gcd(n_lat, ATTN_UNROLL)

        def body(i, carry):
            m, l = carry
            starts = [pl.multiple_of((i * unroll + u) * tk, tk) for u in range(unroll)]
            ss = [scores(kl_ref[0, 0, pl.ds(st, tk), :]) for st in starts]
            for s, st in zip(ss, starts):
                m_new = jnp.maximum(m, jnp.max(s, axis=0, keepdims=True))
                p = jnp.exp2(s - m_new)
                alpha = jnp.exp2(m - m_new)
                l = alpha * l + jnp.sum(p, axis=0, keepdims=True)
                pv = jnp.dot(vlt_ref[0, 0, :, pl.ds(st, tk)], p.astype(BF16),
                             preferred_element_type=F32)
                acc_ref[...] = alpha * acc_ref[...] + pv
                m = m_new
            return m, l

        m, l = lax.fori_loop(0, n_lat // unroll, body, (m, l))
    o_ref[0, 0] = (acc_ref[...] / l).T.reshape(G, tq, LANES).astype(BF16)


def _attention(q, kc, vct, kl, vlt, tq, tk):
    B, Hk, G, R, _ = q.shape
    Lc = kc.shape[2]
    n_lat = 0 if kl is None else kl.shape[2] // tk
    qspec = pl.BlockSpec((1, 1, G, tq, LANES), lambda b, h, i: (b, h, 0, i, 0))
    kspec = lambda n: pl.BlockSpec((1, 1, n, LANES), lambda b, h, i: (b, h, 0, 0))
    vspec = lambda n: pl.BlockSpec((1, 1, LANES, n), lambda b, h, i: (b, h, 0, 0))
    ins = [q, kc, vct] + ([kl, vlt] if n_lat else [])
    specs = [qspec, kspec(Lc), vspec(Lc)] + ([kspec(kl.shape[2]), vspec(kl.shape[2])] if n_lat else [])
    return pl.pallas_call(
        functools.partial(_attn_kernel, n_lat, tk),
        out_shape=jax.ShapeDtypeStruct(q.shape, BF16),
        grid=(B, Hk, R // tq),
        in_specs=specs,
        out_specs=qspec,
        scratch_shapes=[pltpu.VMEM((LANES, G * tq), F32)],
        compiler_params=_params(("arbitrary", "arbitrary", "arbitrary"), VMEM_LIMIT),
        name="attn",
    )(*ins)


def _dft_mats(n):
    k = np.arange(n, dtype=np.float64)
    ang = 2.0 * np.pi * np.outer(k, k) / n
    return np.cos(ang), np.sin(ang)


def _dft1_kernel(x_ref, m1_ref, tw_ref, o_ref):
    s1, ns2 = x_ref.shape[1], x_ref.shape[2]
    for i in range(ns2):
        a = jnp.dot(m1_ref[...], x_ref[0, :, i, :], precision=HIGHEST, preferred_element_type=F32)
        are, aim = a[:s1], a[s1:]
        tre = jnp.tile(tw_ref[0, :, i * LANES:(i + 1) * LANES], (1, FNET_GROUPS))
        tim = jnp.tile(tw_ref[1, :, i * LANES:(i + 1) * LANES], (1, FNET_GROUPS))
        o_ref[0, 0, :, i, :] = are * tre - aim * tim
        o_ref[0, 1, :, i, :] = are * tim + aim * tre


def _dft2_kernel(a_ref, m2_ref, m3_ref, o_ref):
    kb, n2 = a_ref.shape[2], a_ref.shape[3]
    for j in range(kb):
        rhs = jnp.concatenate([a_ref[0, 0, j], a_ref[0, 1, j]], axis=0)
        y = jnp.dot(m2_ref[...], rhs, precision=HIGHEST, preferred_element_type=F32)
        for g in range(FNET_GROUPS):
            sl = slice(g * LANES, (g + 1) * LANES)
            lhs = jnp.concatenate([y[:n2, sl], y[n2:, sl]], axis=1)
            o_ref[0, :, j, sl] = jnp.dot(lhs, m3_ref[...], precision=HIGHEST,
                                         preferred_element_type=F32)


def _fourier_lat(f):
    B, S, W = f.shape
    n2 = LANES
    s1 = S // n2
    c1, sn1 = _dft_mats(s1)
    c2, sn2 = _dft_mats(n2)
    cc, sc = _dft_mats(FNET_GROUP_DIM)
    m1 = jnp.asarray(np.concatenate([c1, -sn1], axis=0), F32)
    m2 = jnp.asarray(np.block([[c2, sn2], [-sn2, c2]]), F32)
    norm = 1.0 / math.sqrt(S * FNET_GROUP_DIM)
    m3 = jnp.asarray(np.concatenate([cc, sc], axis=0) * norm, F32)
    ang = 2.0 * np.pi * np.outer(np.arange(s1), np.arange(n2)) / S
    tw = np.stack([np.cos(ang), -np.sin(ang)])
    tw = jnp.asarray(np.repeat(tw[:, :, :, None], LANES, axis=3).reshape(2, s1, n2 * LANES), F32)

    ns2 = 8
    a = pl.pallas_call(
        _dft1_kernel,
        out_shape=jax.ShapeDtypeStruct((B, 2, s1, n2, W), F32),
        grid=(B, n2 // ns2),
        in_specs=[pl.BlockSpec((1, s1, ns2, W), lambda b, j: (b, 0, j, 0)),
                  pl.BlockSpec(m1.shape, lambda b, j: (0, 0)),
                  pl.BlockSpec((2, s1, ns2 * LANES), lambda b, j: (0, 0, j))],
        out_specs=pl.BlockSpec((1, 2, s1, ns2, W), lambda b, j: (b, 0, 0, j, 0)),
        compiler_params=_params(("arbitrary", "arbitrary"), VMEM_LIMIT),
        name="dft1",
    )(f.reshape(B, s1, n2, W), m1, tw)
    kb = min(8, s1)
    y = pl.pallas_call(
        _dft2_kernel,
        out_shape=jax.ShapeDtypeStruct((B, n2, s1, W), F32),
        grid=(B, s1 // kb),
        in_specs=[pl.BlockSpec((1, 2, kb, n2, W), lambda b, j: (b, 0, j, 0, 0)),
                  pl.BlockSpec(m2.shape, lambda b, j: (0, 0)),
                  pl.BlockSpec(m3.shape, lambda b, j: (0, 0))],
        out_specs=pl.BlockSpec((1, n2, kb, W), lambda b, j: (b, 0, j, 0)),
        compiler_params=_params(("arbitrary", "arbitrary"), VMEM_LIMIT),
        name="dft2",
    )(a, m2, m3)
    return y.reshape(B, S, W)


def _dftc_kernel(f_ref, mc_ref, m3_ref, o_ref):
    n = f_ref.shape[1]
    a = jnp.dot(mc_ref[...], f_ref[0], precision=HIGHEST, preferred_element_type=F32)
    for g in range(FNET_GROUPS):
        sl = slice(g * LANES, (g + 1) * LANES)
        lhs = jnp.concatenate([a[:n, sl], a[n:, sl]], axis=1)
        o_ref[0, :, sl] = jnp.dot(lhs, m3_ref[...], precision=HIGHEST, preferred_element_type=F32)


def _fourier_ctx(f):
    B, L, W = f.shape
    c, s = _dft_mats(L)
    cc, sc = _dft_mats(FNET_GROUP_DIM)
    mc = jnp.asarray(np.concatenate([c, -s], axis=0), F32)
    m3 = jnp.asarray(np.concatenate([cc, sc], axis=0) / math.sqrt(L * FNET_GROUP_DIM), F32)
    return pl.pallas_call(
        _dftc_kernel,
        out_shape=jax.ShapeDtypeStruct((B, L, W), F32),
        grid=(B,),
        in_specs=[pl.BlockSpec((1, L, W), lambda b: (b, 0, 0)),
                  pl.BlockSpec(mc.shape, lambda b: (0, 0)),
                  pl.BlockSpec(m3.shape, lambda b: (0, 0))],
        out_specs=pl.BlockSpec((1, L, W), lambda b: (b, 0, 0)),
        compiler_params=_params(("arbitrary",), VMEM_LIMIT),
        name="dftc",
    )(f, mc, m3)


def _out_kernel(has_f, *refs):
    if has_f:
        (h_ref, mod_ref, yf_ref, wof_ref, o_ref, woa_ref, n2_ref, wr_ref,
         h1_ref, m_ref, aff_ref) = refs
    else:
        (h_ref, mod_ref, o_ref, woa_ref, n2_ref, wr_ref, h1_ref, m_ref, aff_ref) = refs
    D = D_MODEL
    Hk, G = o_ref.shape[1], o_ref.shape[2]
    ocat = jnp.concatenate([o_ref[0, hk, g] for hk in range(Hk) for g in range(G)], axis=1)
    y = jnp.dot(ocat, woa_ref[...], preferred_element_type=F32)
    if has_f:
        y = y + jnp.dot(yf_ref[0].astype(BF16), wof_ref[...], preferred_element_type=F32)
    mod = mod_ref[0]
    h1 = h_ref[0] + mod[:, 2 * D:3 * D] * y
    h1_ref[0] = h1
    m = _modulate(h1, n2_ref[...], mod[:, 3 * D:4 * D], mod[:, 4 * D:5 * D])
    _store_tiled(m_ref.at[0], slice(None), m)
    logit = lax.dot_general(wr_ref[...], m, NT, precision=HIGHEST,
                            preferred_element_type=F32)
    e = jnp.exp(logit - jnp.max(logit, axis=0, keepdims=True))
    aff_ref[0] = e / jnp.sum(e, axis=0, keepdims=True)


def _out_proj(h, mod, yf, wof, o, woa, n2, wr_t, tm):
    B, R, D = h.shape
    _, Hk, G, _, _ = o.shape
    full = lambda a: pl.BlockSpec(a.shape, lambda b, i: (0,) * a.ndim)
    row = lambda w_: pl.BlockSpec((1, tm, w_), lambda b, i: (b, i, 0))
    modspec = pl.BlockSpec((1, 1, N_MOD * D), lambda b, i: (b, 0, 0))
    ospec = pl.BlockSpec((1, Hk, G, tm, LANES), lambda b, i: (b, 0, 0, i, 0))
    has_f = yf is not None
    ins = [h, mod] + ([yf, wof] if has_f else []) + [o, woa, n2, wr_t]
    specs = ([row(D), modspec] + ([row(FNET_WIDTH), full(wof)] if has_f else [])
             + [ospec, full(woa), full(n2), full(wr_t)])
    return pl.pallas_call(
        functools.partial(_out_kernel, has_f),
        out_shape=(jax.ShapeDtypeStruct((B, R, D), F32),
                   jax.ShapeDtypeStruct((B, R, SUB, LANES), F32),
                   jax.ShapeDtypeStruct((B, N_EXPERTS, R), F32)),
        grid=(B, R // tm),
        in_specs=specs,
        out_specs=(row(D), pl.BlockSpec((1, tm, SUB, LANES), lambda b, i: (b, i, 0, 0)),
                   pl.BlockSpec((1, N_EXPERTS, tm), lambda b, i: (b, 0, i))),
        compiler_params=_params(("arbitrary", "arbitrary"), VMEM_LIMIT),
        name="out_proj",
    )(*ins)


def _topk_kernel(cap, aff_ref, u_ref, ones_ref, lmat_ref, lc_ref, lg_ref, cnt_ref, off_ref,
                 pos_ref, ac_ref):
    a = aff_ref[0]
    E, N = a.shape
    NC = N // LANES
    R = NC * E
    keys = pltpu.bitcast(a, I32)

    def bit_step(i, tau):
        cand = tau | jnp.left_shift(jnp.int32(1), 30 - i)
        cnt = jnp.sum((keys >= cand).astype(I32), axis=1, keepdims=True)
        return jnp.where(cnt >= cap, cand, tau)

    tau = lax.fori_loop(0, 31, bit_step, jnp.zeros((E, 1), I32))
    gt = (keys > tau).astype(F32)
    eq = (keys == tau).astype(F32)
    need = (cap - jnp.sum(gt, axis=1, keepdims=True))

    def chunked(x):
        return jnp.concatenate([x[:, c * LANES:(c + 1) * LANES] for c in range(NC)], axis=0)

    a_c, gt_c, eq_c = chunked(a), chunked(gt), chunked(eq)
    need_c = jnp.tile(need, (NC, 1))

    def prefix(x):
        xb = x.astype(BF16)
        loc = jnp.dot(xb, u_ref[...], preferred_element_type=F32)
        tot = jnp.dot(xb, ones_ref[...], preferred_element_type=F32)
        offs = jnp.dot(lmat_ref[...], tot.astype(BF16), preferred_element_type=F32)
        return loc, tot, offs

    loc, tot, offs = prefix(eq_c)
    sel = jnp.maximum(gt_c, jnp.where(loc + offs < need_c, eq_c, 0.0))
    loc, tot, offs = prefix(sel)
    pos_ref[...] = jnp.where(sel > 0.0, loc, -1.0)
    ac_ref[...] = a_c
    cnt_ref[0] = tot.astype(I32)
    off_ref[0] = offs.astype(I32)
    rb = 16
    lane = lax.broadcasted_iota(I32, (rb, LANES), 1).astype(F32)

    def compact(i, carry):
        rows = pl.ds(pl.multiple_of(i * rb, rb), rb)
        selpos = pos_ref[rows, :]
        aff = ac_ref[rows, :]
        lc = jnp.zeros((rb, LANES), F32)
        lg = jnp.zeros((rb, LANES), F32)
        for t in range(LANES):
            hit = selpos[:, t:t + 1] == lane
            lc = jnp.where(hit, float(t), lc)
            lg = jnp.where(hit, aff[:, t:t + 1], lg)
        lc_ref[0, rows, :] = lc.astype(I32)
        lg_ref[0, rows, :] = lg
        return carry

    lax.fori_loop(0, R // rb, compact, 0)


def _topk(aff, cap):
    B, E, N = aff.shape
    NC = N // LANES
    R = NC * E
    i = np.arange(LANES)
    u = jnp.asarray(i[:, None] < i[None, :], BF16)
    ones = jnp.ones((LANES, LANES), BF16)
    r = np.arange(R)
    lmat = jnp.asarray((r[:, None] % E == r[None, :] % E) & (r[None, :] // E < r[:, None] // E), BF16)
    full = lambda a: pl.BlockSpec(a.shape, lambda b: (0,) * a.ndim)
    ospec = pl.BlockSpec((1, R, LANES), lambda b: (b, 0, 0))
    sds = lambda dt: jax.ShapeDtypeStruct((B, R, LANES), dt)
    lc, lg, cnt, off = pl.pallas_call(
        functools.partial(_topk_kernel, cap),
        out_shape=(sds(I32), sds(F32), sds(I32), sds(I32)),
        grid=(B,),
        in_specs=[pl.BlockSpec((1, E, N), lambda b: (b, 0, 0)), full(u), full(ones), full(lmat)],
        out_specs=(ospec, ospec, ospec, ospec),
        scratch_shapes=[pltpu.VMEM((R, LANES), F32), pltpu.VMEM((R, LANES), F32)],
        compiler_params=_params(("arbitrary",), VMEM_LIMIT),
        name="topk",
    )(aff, u, ones, lmat)
    by_expert = lambda x: x.reshape(B, NC, E, LANES).transpose(0, 2, 1, 3)
    return (by_expert(lc), by_expert(lg), by_expert(cnt)[..., 0], by_expert(off)[..., 0])


GATHER_ROWS = 256


def _gather_kernel(cap, cnt_ref, off_ref, lc_ref, m_hbm, o_ref, x_ref, sems):
    e = pl.program_id(0)
    B, NC = lc_ref.shape[0], lc_ref.shape[1]
    for b in range(B):
        def chunk_body(c, carry, b=b):
            base = off_ref[b, e, c] + b * cap

            def row_body(r, carry):
                t = c * LANES + lc_ref[b, c, r]
                pltpu.make_async_copy(m_hbm.at[b, t], x_ref.at[base + r], sems.at[b]).start()
                return carry

            return lax.fori_loop(0, cnt_ref[b, e, c], row_body, carry)

        lax.fori_loop(0, NC, chunk_body, 0)
    step = min(GATHER_ROWS, cap)
    for b in range(B):
        done = x_ref.at[pl.ds(b * cap, cap)]
        pltpu.make_async_copy(done, done, sems.at[b]).wait()
        for r0 in range(b * cap, (b + 1) * cap, step):
            rows = slice(r0, r0 + step)
            o_ref[0, rows, :] = _untile(x_ref, rows).astype(BF16)


def _gather(m, lc, cnt, off, cap):
    B, N, sub, _ = m.shape
    E, NC = lc.shape[1], lc.shape[2]
    return pl.pallas_call(
        functools.partial(_gather_kernel, cap),
        out_shape=jax.ShapeDtypeStruct((E, B * cap, sub * LANES), BF16),
        grid_spec=pltpu.PrefetchScalarGridSpec(
            num_scalar_prefetch=2,
            grid=(E,),
            in_specs=[pl.BlockSpec((B, None, NC, LANES), lambda e, *_: (0, e, 0, 0),
                                   memory_space=pltpu.SMEM),
                      pl.BlockSpec(memory_space=pl.ANY)],
            out_specs=pl.BlockSpec((1, B * cap, sub * LANES), lambda e, *_: (e, 0, 0)),
            scratch_shapes=[pltpu.VMEM((B * cap, sub, LANES), m.dtype),
                            pltpu.SemaphoreType.DMA((B,))]),
        compiler_params=_params(("arbitrary",), VMEM_LIMIT),
        name="gather",
    )(cnt, off, lc, m)


def _ffn_kernel(n_parts, chunk_rows, *refs):
    wg_ref, wu_ref, wd_ref = refs[:3]
    x_refs = refs[3:3 + n_parts]
    o_refs = refs[3 + n_parts:3 + 2 * n_parts]
    acc_refs = refs[3 + 2 * n_parts:]
    j = pl.program_id(1)
    last = pl.num_programs(1) - 1
    wg = wg_ref[0].astype(BF16)
    wu = wu_ref[0].astype(BF16)
    wd = wd_ref[0].astype(BF16)
    for x_ref, o_ref, acc_ref, nrows in zip(x_refs, o_refs, acc_refs, chunk_rows):
        for r0 in range(0, x_ref.shape[1], nrows):
            rows = slice(r0, r0 + nrows)
            x = x_ref[0, rows, :]
            a = jnp.dot(x, wg, preferred_element_type=F32)
            u = jnp.dot(x, wu, preferred_element_type=F32)
            hh = (a * jax.nn.sigmoid(a) * u).astype(BF16)
            y = jnp.dot(hh, wd, preferred_element_type=F32)

            @pl.when(j == 0)
            def _():
                acc_ref[rows, :] = y

            @pl.when((j > 0) & (j < last))
            def _():
                acc_ref[rows, :] += y

            @pl.when(j == last)
            def _():
                _store_tiled(o_ref.at[0], rows, acc_ref[rows, :] + y)


def _ffn(xs, w_gate, w_up, w_down, chunk_rows):
    E, D, F = w_gate.shape
    tf = 512
    assert F // tf >= 2
    xspec = lambda x: pl.BlockSpec((1, x.shape[1], D), lambda e, j: (e, 0, 0))
    ospec = lambda x: pl.BlockSpec((1, x.shape[1], SUB, LANES), lambda e, j: (e, 0, 0, 0))
    outs = pl.pallas_call(
        functools.partial(_ffn_kernel, len(xs), chunk_rows),
        out_shape=tuple(jax.ShapeDtypeStruct((E, x.shape[1], SUB, LANES), F32) for x in xs),
        grid=(E, F // tf),
        in_specs=[pl.BlockSpec((1, D, tf), lambda e, j: (e, 0, j)),
                  pl.BlockSpec((1, D, tf), lambda e, j: (e, 0, j)),
                  pl.BlockSpec((1, tf, D), lambda e, j: (e, j, 0))] + [xspec(x) for x in xs],
        out_specs=tuple(ospec(x) for x in xs),
        scratch_shapes=[pltpu.VMEM((x.shape[1], D), F32) for x in xs],
        compiler_params=_params(("arbitrary", "arbitrary"), VMEM_LIMIT),
        name="ffn",
    )(w_gate, w_up, w_down, *xs)
    return outs


def _combine_kernel(cnt_ref, off_ref, lc_ref, lg_ref, y_ref, o_ref):
    b = pl.program_id(0)
    hf = pl.program_id(1)
    e = pl.program_id(2)
    nch = o_ref.shape[1] // LANES

    @pl.when(e == 0)
    def _():
        o_ref[...] = jnp.zeros_like(o_ref)

    def chunk_body(ci, carry):
        c = hf * nch + ci
        base = off_ref[b, e, c]

        def row_body(r, carry):
            t = ci * LANES + lc_ref[c, r]
            o_ref[0, t] = o_ref[0, t] + lg_ref[c, r] * y_ref[0, base + r]
            return carry

        return lax.fori_loop(0, cnt_ref[b, e, c], row_body, carry)

    lax.fori_loop(0, nch, chunk_body, 0)


def _combine(y, lc, lg, cnt, off, n_tokens, cap, n_split):
    E = y.shape[0]
    B, _, NC, _ = lc.shape
    nh = n_tokens // n_split
    smem = lambda: pl.BlockSpec((None, None, NC, LANES), lambda b, h, e, *_: (b, e, 0, 0),
                                memory_space=pltpu.SMEM)
    return pl.pallas_call(
        _combine_kernel,
        out_shape=jax.ShapeDtypeStruct((B, n_tokens, SUB, LANES), F32),
        grid_spec=pltpu.PrefetchScalarGridSpec(
            num_scalar_prefetch=2,
            grid=(B, n_split, E),
            in_specs=[smem(), smem(),
                      pl.BlockSpec((1, cap, SUB, LANES), lambda b, h, e, *_: (e, b, 0, 0))],
            out_specs=pl.BlockSpec((1, nh, SUB, LANES), lambda b, h, e, *_: (b, h, 0, 0))),
        compiler_params=_params(("arbitrary", "arbitrary", "arbitrary"), VMEM_LIMIT),
        name="combine",
    )(cnt, off, lc, lg, y)


def _final_kernel(h_ref, moe_ref, mod_ref, g_ref, o_ref):
    D = D_MODEL
    h = h_ref[0] + mod_ref[0][:, 5 * D:6 * D] * _untile(moe_ref.at[0], slice(None))
    o_ref[0] = _rms(h, g_ref[...])


def _final(h, moe, mod, g, tm):
    B, R, D = h.shape
    row = pl.BlockSpec((1, tm, D), lambda b, i: (b, i, 0))
    return pl.pallas_call(
        _final_kernel,
        out_shape=jax.ShapeDtypeStruct((B, R, D), F32),
        grid=(B, R // tm),
        in_specs=[row, pl.BlockSpec((1, tm, SUB, LANES), lambda b, i: (b, i, 0, 0)),
                  pl.BlockSpec((1, 1, N_MOD * D), lambda b, i: (b, 0, 0)),
                  pl.BlockSpec((1, D), lambda b, i: (0, 0))],
        out_specs=row,
        compiler_params=_params(("arbitrary", "arbitrary"), VMEM_LIMIT),
        name="final",
    )(h, moe, mod, g)


def _rope_tables(n_lat, n_ctx, segments):
    t = jnp.arange(n_lat)
    pos = {"row": (t // GRID_W).astype(F32), "col": (t % GRID_W).astype(F32)}
    cos = jnp.ones((n_lat, LANES), F32)
    sf = jnp.zeros((n_lat, LANES), F32)
    sb = jnp.zeros((n_lat, LANES), F32)
    for lane0, width, which in segments:
        half = width // 2
        inv = ROPE_THETA ** (-jnp.arange(half, dtype=F32) / half)
        ang = pos[which][:, None] * inv[None, :]
        c, s = jnp.cos(ang), jnp.sin(ang)
        cos = cos.at[:, lane0:lane0 + half].set(c).at[:, lane0 + half:lane0 + width].set(c)
        sf = sf.at[:, lane0:lane0 + half].set(-s)
        sb = sb.at[:, lane0 + half:lane0 + width].set(s)
    lat = (cos, sf, sb)
    ctx = (jnp.ones((n_ctx, LANES), F32), jnp.zeros((n_ctx, LANES), F32), jnp.zeros((n_ctx, LANES), F32))
    return lat, ctx


def _head_slots(w, n_heads, width, lo, hi):
    k = w.shape[0]
    w3 = w.reshape(k, n_heads, width)[:, :, lo:hi]
    return jnp.pad(w3, ((0, 0), (0, 0), (0, LANES - (hi - lo)))).reshape(k, n_heads * LANES)


def _moe(m_l, aff_l, m_c, aff_c, w_gate, w_up, w_down):
    B, S = m_l.shape[:2]
    cap_l = CAPACITY_FACTOR * S // N_EXPERTS
    lc, lg, cnt, off = _topk(aff_l, cap_l)
    xs = [_gather(m_l, lc, cnt, off, cap_l)]
    chunk_rows = [cap_l]
    if m_c is not None:
        L = m_c.shape[1]
        cap_c = CAPACITY_FACTOR * L // N_EXPERTS
        lcc, lgc, cntc, offc = _topk(aff_c, cap_c)
        xs.append(_gather(m_c, lcc, cntc, offc, cap_c))
        chunk_rows.append(B * cap_c)
    ys = _ffn(xs, w_gate, w_up, w_down, tuple(chunk_rows))
    out_l = _combine(ys[0], lc, lg, cnt, off, S, cap_l, 2)
    out_c = None
    if m_c is not None:
        out_c = _combine(ys[1], lcc, lgc, cntc, offc, L, cap_c, 1)
    return out_l, out_c


def kernel(x, c, ctx, c_ctx, ada_w, ada_b, norm1, norm2, ab_w_in, ab_q_norm, ab_w_uq, ab_kv_norm,
           ab_w_ukv, ab_w_o, c_w_in, c_q_gain, c_k_gain, c_w_o, moe_router, moe_w_gate, moe_w_up,
           moe_w_down, final_norm):
    B, S, D = x.shape
    L = ctx.shape[1]
    depth = ada_w.shape[0]
    tm_l, tm_c = 512, L
    row2 = lambda v: v.reshape(1, -1)

    c8 = jnp.zeros((8, D), F32).at[:B].set(c).at[B].set(c_ctx)
    mod = _ada(c8, ada_w, ada_b)
    mod_l = [mod[i, :B][:, None, :] for i in range(depth)]
    mod_c = [jnp.broadcast_to(mod[i, B][None, None, :], (B, 1, N_MOD * D)) for i in range(depth)]

    h_l, h_c = x, ctx
    moe_l = moe_c = None
    for i in range(depth):
        last = i == depth - 1
        j = i // 2
        wr_t = moe_router[i].T
        n2 = row2(norm2[i])
        if i % 2 == 0:
            assert moe_l is None
            tabs_l, tabs_c = _rope_tables(S, L, ((MLA_NOPE, MLA_ROPE // 2, "row"),
                                                 (MLA_NOPE + MLA_ROPE // 2, MLA_ROPE // 2, "col")))
            w_ukv = ab_w_ukv[j]
            ekr = np.zeros((MLA_ROPE, MLA_HEADS * LANES), np.float32)
            for hd in range(MLA_HEADS):
                ekr[np.arange(MLA_ROPE), hd * LANES + MLA_NOPE + np.arange(MLA_ROPE)] = 1.0
            w = {
                "win": ab_w_in[j].astype(BF16),
                "qn": row2(ab_q_norm[j]),
                "wuq": _head_slots(ab_w_uq[j], MLA_HEADS, MLA_QK, 0, MLA_QK).astype(BF16),
                "kvn": row2(ab_kv_norm[j]),
                "wuk": _head_slots(w_ukv, MLA_HEADS, MLA_NOPE + MLA_V, 0, MLA_NOPE).astype(BF16),
                "wuvt": _head_slots(w_ukv, MLA_HEADS, MLA_NOPE + MLA_V, MLA_NOPE,
                                    MLA_NOPE + MLA_V).T.astype(BF16),
                "ekr": jnp.asarray(ekr, BF16),
            }
            n1 = row2(norm1[i])
            f_l, q_l, k_l, vt_l = _in_ab(h_l, mod_l[i], n1, w, tabs_l, tm_l)
            f_c, q_c, k_c, vt_c = _in_ab(h_c, mod_c[i], n1, w, tabs_c, tm_c)
            o_l = _attention(q_l[:, :, None], k_c, vt_c, k_l, vt_l, 512, 512)
            yf_l = _fourier_lat(f_l)
            wof = ab_w_o[j][:FNET_WIDTH].astype(BF16)
            woa = jnp.pad(ab_w_o[j][FNET_WIDTH:].reshape(MLA_HEADS, MLA_V, D),
                          ((0, 0), (0, LANES - MLA_V), (0, 0))).reshape(MLA_HEADS * LANES, D).astype(BF16)
            h_l, m_l, aff_l = _out_proj(h_l, mod_l[i], yf_l, wof, o_l, woa, n2, wr_t, tm_l)
            m_c = aff_c = None
            if not last:
                o_c = _attention(q_c[:, :, None], k_c, vt_c, None, None, L, L)
                yf_c = _fourier_ctx(f_c)
                h_c, m_c, aff_c = _out_proj(h_c, mod_c[i], yf_c, wof, o_c, woa, n2, wr_t, tm_c)
        else:
            tabs_l, tabs_c = _rope_tables(S, L, ((0, GQA_HEAD_DIM // 2, "row"),
                                                 (GQA_HEAD_DIM // 2, GQA_HEAD_DIM // 2, "col")))
            nqk = (GQA_HEADS + GQA_KV_HEADS) * GQA_HEAD_DIM
            w = {"wqk": c_w_in[j][:, :nqk].astype(BF16), "wvt": c_w_in[j][:, nqk:].T.astype(BF16),
                 "qg": row2(c_q_gain[j]), "kg": row2(c_k_gain[j])}
            n1 = row2(norm1[i])
            h_l, q_l, k_l, vt_l = _in_c(h_l, moe_l, mod_l[i - 1], mod_l[i], n1, w, tabs_l, tm_l)
            h_c, q_c, k_c, vt_c = _in_c(h_c, moe_c, mod_c[i - 1], mod_c[i], n1, w, tabs_c, tm_c)
            grp = lambda q: q.reshape(B, GQA_KV_HEADS, GQA_GROUP, q.shape[2], LANES)
            o_l = _attention(grp(q_l), k_c, vt_c, k_l, vt_l, 256, 512)
            woa = c_w_o[j].astype(BF16)
            h_l, m_l, aff_l = _out_proj(h_l, mod_l[i], None, None, o_l, woa, n2, wr_t, tm_l)
            m_c = aff_c = None
            if not last:
                o_c = _attention(grp(q_c), k_c, vt_c, None, None, L, L)
                h_c, m_c, aff_c = _out_proj(h_c, mod_c[i], None, None, o_c, woa, n2, wr_t, tm_c)
        moe_l, moe_c = _moe(m_l, aff_l, m_c, aff_c, moe_w_gate[i], moe_w_up[i], moe_w_down[i])
    return _final(h_l, moe_l, mod_l[depth - 1], row2(final_norm), tm_l)
```

```python
import functools
import math

import jax
import jax.numpy as jnp
import numpy as np
from jax import lax
from jax.experimental import pallas as pl
from jax.experimental.pallas import tpu as pltpu

F32 = jnp.float32
BF16 = jnp.bfloat16
I32 = jnp.int32
HIGHEST = lax.Precision.HIGHEST

D_MODEL = 1024
GRID_W = 64
EPS = 1e-6
ROPE_THETA = 10000.0
N_MOD = 6
FNET_GROUPS = 4
FNET_GROUP_DIM = 128
FNET_WIDTH = FNET_GROUPS * FNET_GROUP_DIM
MLA_HEADS = 8
MLA_Q_LORA = 256
MLA_KV_LORA = 128
MLA_NOPE = 64
MLA_ROPE = 32
MLA_V = 64
MLA_QK = MLA_NOPE + MLA_ROPE
GQA_HEADS = 8
GQA_KV_HEADS = 2
GQA_GROUP = GQA_HEADS // GQA_KV_HEADS
GQA_HEAD_DIM = 128
N_EXPERTS = 16
EXPERT_FF = 2048
CAPACITY_FACTOR = 2

LANES = 128
SUB = D_MODEL // LANES
VMEM_LIMIT = 56 * 1024 * 1024
LOG2E = math.log2(math.e)
NT = (((1,), (1,)), ((), ()))


def _params(sem, vmem=None):
    return pltpu.CompilerParams(dimension_semantics=sem, vmem_limit_bytes=vmem)


def _rms(x, g):
    return x * lax.rsqrt(jnp.mean(x * x, axis=-1, keepdims=True) + EPS) * g


def _modulate(h, g, shift, scale):
    return _rms(h, g) * (1.0 + scale) + shift


def _rope(x, cos, sin_fwd, sin_bwd, shift):
    return (x * cos + pltpu.roll(x, LANES - shift, 1) * sin_fwd
            + pltpu.roll(x, shift, 1) * sin_bwd)


def _untile(ref, rows):
    return jnp.concatenate([ref[rows, s, :] for s in range(SUB)], axis=1)


def _store_tiled(ref, rows, x):
    for s in range(SUB):
        ref[rows, s, :] = x[:, s * LANES:(s + 1) * LANES]


def _ada_kernel(c_ref, w_ref, b_ref, o_ref):
    c = c_ref[...]
    x = c * jax.nn.sigmoid(c)
    o_ref[0] = jnp.dot(x, w_ref[0], precision=HIGHEST, preferred_element_type=F32) + b_ref[0]


def _ada(c8, ada_w, ada_b):
    depth, d, n = ada_w.shape
    tn = 1536
    return pl.pallas_call(
        _ada_kernel,
        out_shape=jax.ShapeDtypeStruct((depth, 8, n), F32),
        grid=(depth, n // tn),
        in_specs=[pl.BlockSpec((8, d), lambda l, j: (0, 0)),
                  pl.BlockSpec((1, d, tn), lambda l, j: (l, 0, j)),
                  pl.BlockSpec((1, 1, tn), lambda l, j: (l, 0, j))],
        out_specs=pl.BlockSpec((1, 8, tn), lambda l, j: (l, 0, j)),
        compiler_params=_params(("arbitrary", "arbitrary"), VMEM_LIMIT),
        name="ada",
    )(c8, ada_w, ada_b.reshape(depth, 1, n))


def _in_ab_kernel(scale, *refs):
    (h_ref, mod_ref, n1_ref, win_ref, qn_ref, wuq_ref, kvn_ref, wuk_ref, wuvt_ref, ekr_ref,
     cos_ref, sf_ref, sb_ref, f_ref, q_ref, k_ref, vt_ref) = refs
    D = D_MODEL
    h = h_ref[0]
    mod = mod_ref[0]
    a = _modulate(h, n1_ref[...], mod[:, 0:D], mod[:, D:2 * D]).astype(BF16)
    p = jnp.dot(a, win_ref[...], preferred_element_type=F32)
    o = FNET_WIDTH
    f_ref[0] = p[:, :o]
    cq = p[:, o:o + MLA_Q_LORA]
    o += MLA_Q_LORA
    ckv = p[:, o:o + MLA_KV_LORA]
    o += MLA_KV_LORA
    kr = p[:, o:o + MLA_ROPE]
    cqn = _rms(cq, qn_ref[...]).astype(BF16)
    ckvn = _rms(ckv, kvn_ref[...]).astype(BF16)
    q = jnp.dot(cqn, wuq_ref[...], preferred_element_type=F32)
    k = (jnp.dot(ckvn, wuk_ref[...], preferred_element_type=F32)
         + jnp.dot(kr.astype(BF16), ekr_ref[...], preferred_element_type=F32))
    vt = lax.dot_general(wuvt_ref[...], ckvn, NT, preferred_element_type=F32)
    cos, sf, sb = cos_ref[...], sf_ref[...], sb_ref[...]
    shift = MLA_ROPE // 4
    for hd in range(MLA_HEADS):
        sl = slice(hd * LANES, (hd + 1) * LANES)
        q_ref[0, hd] = (_rope(q[:, sl], cos, sf, sb, shift) * scale).astype(BF16)
        k_ref[0, hd] = _rope(k[:, sl], cos, sf, sb, shift).astype(BF16)
        vt_ref[0, hd] = vt[sl, :].astype(BF16)


def _in_ab(h, mod, n1, w, tables, tm):
    B, R, D = h.shape
    H = MLA_HEADS
    full = lambda a: pl.BlockSpec(a.shape, lambda b, i: (0,) * a.ndim)
    row = lambda w_: pl.BlockSpec((1, tm, w_), lambda b, i: (b, i, 0))
    tab = pl.BlockSpec((tm, LANES), lambda b, i: (i, 0))
    hd = pl.BlockSpec((1, H, tm, LANES), lambda b, i: (b, 0, i, 0))
    hdt = pl.BlockSpec((1, H, LANES, tm), lambda b, i: (b, 0, 0, i))
    weights = [n1, w["win"], w["qn"], w["wuq"], w["kvn"], w["wuk"], w["wuvt"], w["ekr"]]
    return pl.pallas_call(
        functools.partial(_in_ab_kernel, MLA_QK ** -0.5 * LOG2E),
        out_shape=(jax.ShapeDtypeStruct((B, R, FNET_WIDTH), F32),
                   jax.ShapeDtypeStruct((B, H, R, LANES), BF16),
                   jax.ShapeDtypeStruct((B, H, R, LANES), BF16),
                   jax.ShapeDtypeStruct((B, H, LANES, R), BF16)),
        grid=(B, R // tm),
        in_specs=[row(D), pl.BlockSpec((1, 1, N_MOD * D), lambda b, i: (b, 0, 0))]
        + [full(a) for a in weights] + [tab, tab, tab],
        out_specs=(row(FNET_WIDTH), hd, hd, hdt),
        compiler_params=_params(("arbitrary", "arbitrary"), VMEM_LIMIT),
        name="in_ab",
    )(h, mod, *weights, *tables)


def _in_c_kernel(scale, *refs):
    (h_ref, moe_ref, modp_ref, mod_ref, n1_ref, wqk_ref, wvt_ref, qg_ref, kg_ref,
     cos_ref, sf_ref, sb_ref, h2_ref, q_ref, k_ref, vt_ref) = refs
    D = D_MODEL
    h = h_ref[0] + modp_ref[0][:, 5 * D:6 * D] * _untile(moe_ref.at[0], slice(None))
    h2_ref[0] = h
    mod = mod_ref[0]
    a = _modulate(h, n1_ref[...], mod[:, 0:D], mod[:, D:2 * D]).astype(BF16)
    p = jnp.dot(a, wqk_ref[...], preferred_element_type=F32)
    vt = lax.dot_general(wvt_ref[...], a, NT, preferred_element_type=F32)
    cos, sf, sb = cos_ref[...], sf_ref[...], sb_ref[...]
    shift = GQA_HEAD_DIM // 4
    nq = GQA_HEADS * GQA_HEAD_DIM
    for hd in range(GQA_HEADS):
        x = _rms(p[:, hd * LANES:(hd + 1) * LANES], qg_ref[...])
        q_ref[0, hd] = (_rope(x, cos, sf, sb, shift) * scale).astype(BF16)
    for hd in range(GQA_KV_HEADS):
        x = _rms(p[:, nq + hd * LANES:nq + (hd + 1) * LANES], kg_ref[...])
        k_ref[0, hd] = _rope(x, cos, sf, sb, shift).astype(BF16)
        vt_ref[0, hd] = vt[hd * LANES:(hd + 1) * LANES, :].astype(BF16)


def _in_c(h, moe, mod_prev, mod, n1, w, tables, tm):
    B, R, D = h.shape
    full = lambda a: pl.BlockSpec(a.shape, lambda b, i: (0,) * a.ndim)
    row = lambda w_: pl.BlockSpec((1, tm, w_), lambda b, i: (b, i, 0))
    tiled = pl.BlockSpec((1, tm, SUB, LANES), lambda b, i: (b, i, 0, 0))
    modspec = pl.BlockSpec((1, 1, N_MOD * D), lambda b, i: (b, 0, 0))
    tab = pl.BlockSpec((tm, LANES), lambda b, i: (i, 0))
    hd = lambda n: pl.BlockSpec((1, n, tm, LANES), lambda b, i: (b, 0, i, 0))
    hdt = pl.BlockSpec((1, GQA_KV_HEADS, LANES, tm), lambda b, i: (b, 0, 0, i))
    weights = [n1, w["wqk"], w["wvt"], w["qg"], w["kg"]]
    return pl.pallas_call(
        functools.partial(_in_c_kernel, GQA_HEAD_DIM ** -0.5 * LOG2E),
        out_shape=(jax.ShapeDtypeStruct((B, R, D), F32),
                   jax.ShapeDtypeStruct((B, GQA_HEADS, R, LANES), BF16),
                   jax.ShapeDtypeStruct((B, GQA_KV_HEADS, R, LANES), BF16),
                   jax.ShapeDtypeStruct((B, GQA_KV_HEADS, LANES, R), BF16)),
        grid=(B, R // tm),
        in_specs=[row(D), tiled, modspec, modspec] + [full(a) for a in weights] + [tab, tab, tab],
        out_specs=(row(D), hd(GQA_HEADS), hd(GQA_KV_HEADS), hdt),
        compiler_params=_params(("arbitrary", "arbitrary"), VMEM_LIMIT),
        name="in_c",
    )(h, moe, mod_prev, mod, *weights, *tables)


def _attn_kernel(n_lat, tk, *refs):
    if n_lat:
        q_ref, kc_ref, vct_ref, kl_ref, vlt_ref, o_ref, acc_ref, s0, s1, p0, p1 = refs
        s_bufs, p_bufs = (s0, s1), (p0, p1)
    else:
        q_ref, kc_ref, vct_ref, o_ref, acc_ref = refs
    G, tq = q_ref.shape[2], q_ref.shape[3]
    q = q_ref[0, 0].reshape(G * tq, LANES)

    def scores(k):
        return lax.dot_general(k, q, NT, preferred_element_type=F32)

    s = scores(kc_ref[0, 0])
    m = jnp.max(s, axis=0, keepdims=True)
    p = jnp.exp2(s - m)
    l = jnp.sum(p, axis=0, keepdims=True)
    acc_ref[...] = jnp.dot(vct_ref[0, 0], p.astype(BF16), preferred_element_type=F32)

    if n_lat:
        assert n_lat == 1 or n_lat % 2 == 0

        def chunk(c):
            return pl.ds(pl.multiple_of(c * tk, tk), tk)

        def score_stage(c, slot):
            s_bufs[slot][...] = scores(kl_ref[0, 0, chunk(c), :])

        def softmax_stage(slot, m, l):
            s = s_bufs[slot][...]
            m_new = jnp.maximum(m, jnp.max(s, axis=0, keepdims=True))
            p = jnp.exp2(s - m_new)
            p_bufs[slot][...] = p.astype(BF16)
            alpha = jnp.exp2(m - m_new)
            return m_new, alpha * l + jnp.sum(p, axis=0, keepdims=True), alpha

        def value_stage(c, slot, alpha):
            pv = jnp.dot(vlt_ref[0, 0, :, chunk(c)], p_bufs[slot][...], preferred_element_type=F32)
            acc_ref[...] = alpha * acc_ref[...] + pv

        score_stage(0, 0)
        m, l, alpha = softmax_stage(0, m, l)
        if n_lat > 1:
            score_stage(1, 1)

            def body(i, carry):
                m, l, alpha = carry
                for slot in (0, 1):
                    c = 2 * i + slot
                    score_stage(c + 2, slot)
                    value_stage(c, slot, alpha)
                    m, l, alpha = softmax_stage(1 - slot, m, l)
                return m, l, alpha

            m, l, alpha = lax.fori_loop(0, (n_lat - 2) // 2, body, (m, l, alpha))
            value_stage(n_lat - 2, 0, alpha)
            m, l, alpha = softmax_stage(1, m, l)
        value_stage(n_lat - 1, (n_lat - 1) % 2, alpha)
    o_ref[0, 0] = (acc_ref[...] / l).T.reshape(G, tq, LANES).astype(BF16)


def _attention(q, kc, vct, kl, vlt, tq, tk):
    B, Hk, G, R, _ = q.shape
    Lc = kc.shape[2]
    n_lat = 0 if kl is None else kl.shape[2] // tk
    qspec = pl.BlockSpec((1, 1, G, tq, LANES), lambda b, h, i: (b, h, 0, i, 0))
    kspec = lambda n: pl.BlockSpec((1, 1, n, LANES), lambda b, h, i: (b, h, 0, 0))
    vspec = lambda n: pl.BlockSpec((1, 1, LANES, n), lambda b, h, i: (b, h, 0, 0))
    ins = [q, kc, vct] + ([kl, vlt] if n_lat else [])
    specs = [qspec, kspec(Lc), vspec(Lc)] + ([kspec(kl.shape[2]), vspec(kl.shape[2])] if n_lat else [])
    return pl.pallas_call(
        functools.partial(_attn_kernel, n_lat, tk),
        out_shape=jax.ShapeDtypeStruct(q.shape, BF16),
        grid=(B, Hk, R // tq),
        in_specs=specs,
        out_specs=qspec,
        scratch_shapes=[pltpu.VMEM((LANES, G * tq), F32)]
        + ([pltpu.VMEM((tk, G * tq), F32)] * 2 + [pltpu.VMEM((tk, G * tq), BF16)] * 2 if n_lat else []),
        compiler_params=_params(("arbitrary", "arbitrary", "arbitrary"), VMEM_LIMIT),
        name="attn",
    )(*ins)


def _dft_mats(n):
    k = np.arange(n, dtype=np.float64)
    ang = 2.0 * np.pi * np.outer(k, k) / n
    return np.cos(ang), np.sin(ang)


def _dft1_kernel(x_ref, m1_ref, tw_ref, o_ref):
    s1, ns2 = x_ref.shape[1], x_ref.shape[2]
    for i in range(ns2):
        a = jnp.dot(m1_ref[...], x_ref[0, :, i, :], precision=HIGHEST, preferred_element_type=F32)
        are, aim = a[:s1], a[s1:]
        tre = jnp.tile(tw_ref[0, :, i * LANES:(i + 1) * LANES], (1, FNET_GROUPS))
        tim = jnp.tile(tw_ref[1, :, i * LANES:(i + 1) * LANES], (1, FNET_GROUPS))
        o_ref[0, 0, :, i, :] = are * tre - aim * tim
        o_ref[0, 1, :, i, :] = are * tim + aim * tre


def _dft2_kernel(a_ref, m2_ref, m3_ref, o_ref):
    kb, n2 = a_ref.shape[2], a_ref.shape[3]
    for j in range(kb):
        rhs = jnp.concatenate([a_ref[0, 0, j], a_ref[0, 1, j]], axis=0)
        y = jnp.dot(m2_ref[...], rhs, precision=HIGHEST, preferred_element_type=F32)
        for g in range(FNET_GROUPS):
            sl = slice(g * LANES, (g + 1) * LANES)
            lhs = jnp.concatenate([y[:n2, sl], y[n2:, sl]], axis=1)
            o_ref[0, :, j, sl] = jnp.dot(lhs, m3_ref[...], precision=HIGHEST,
                                         preferred_element_type=F32)


def _fourier_lat(f):
    B, S, W = f.shape
    n2 = LANES
    s1 = S // n2
    c1, sn1 = _dft_mats(s1)
    c2, sn2 = _dft_mats(n2)
    cc, sc = _dft_mats(FNET_GROUP_DIM)
    m1 = jnp.asarray(np.concatenate([c1, -sn1], axis=0), F32)
    m2 = jnp.asarray(np.block([[c2, sn2], [-sn2, c2]]), F32)
    norm = 1.0 / math.sqrt(S * FNET_GROUP_DIM)
    m3 = jnp.asarray(np.concatenate([cc, sc], axis=0) * norm, F32)
    ang = 2.0 * np.pi * np.outer(np.arange(s1), np.arange(n2)) / S
    tw = np.stack([np.cos(ang), -np.sin(ang)])
    tw = jnp.asarray(np.repeat(tw[:, :, :, None], LANES, axis=3).reshape(2, s1, n2 * LANES), F32)

    ns2 = 8
    a = pl.pallas_call(
        _dft1_kernel,
        out_shape=jax.ShapeDtypeStruct((B, 2, s1, n2, W), F32),
        grid=(B, n2 // ns2),
        in_specs=[pl.BlockSpec((1, s1, ns2, W), lambda b, j: (b, 0, j, 0)),
                  pl.BlockSpec(m1.shape, lambda b, j: (0, 0)),
                  pl.BlockSpec((2, s1, ns2 * LANES), lambda b, j: (0, 0, j))],
        out_specs=pl.BlockSpec((1, 2, s1, ns2, W), lambda b, j: (b, 0, 0, j, 0)),
        compiler_params=_params(("arbitrary", "arbitrary"), VMEM_LIMIT),
        name="dft1",
    )(f.reshape(B, s1, n2, W), m1, tw)
    kb = min(8, s1)
    y = pl.pallas_call(
        _dft2_kernel,
        out_shape=jax.ShapeDtypeStruct((B, n2, s1, W), F32),
        grid=(B, s1 // kb),
        in_specs=[pl.BlockSpec((1, 2, kb, n2, W), lambda b, j: (b, 0, j, 0, 0)),
                  pl.BlockSpec(m2.shape, lambda b, j: (0, 0)),
                  pl.BlockSpec(m3.shape, lambda b, j: (0, 0))],
        out_specs=pl.BlockSpec((1, n2, kb, W), lambda b, j: (b, 0, j, 0)),
        compiler_params=_params(("arbitrary", "arbitrary"), VMEM_LIMIT),
        name="dft2",
    )(a, m2, m3)
    return y.reshape(B, S, W)


def _dftc_kernel(f_ref, mc_ref, m3_ref, o_ref):
    n = f_ref.shape[1]
    a = jnp.dot(mc_ref[...], f_ref[0], precision=HIGHEST, preferred_element_type=F32)
    for g in range(FNET_GROUPS):
        sl = slice(g * LANES, (g + 1) * LANES)
        lhs = jnp.concatenate([a[:n, sl], a[n:, sl]], axis=1)
        o_ref[0, :, sl] = jnp.dot(lhs, m3_ref[...], precision=HIGHEST, preferred_element_type=F32)


def _fourier_ctx(f):
    B, L, W = f.shape
    c, s = _dft_mats(L)
    cc, sc = _dft_mats(FNET_GROUP_DIM)
    mc = jnp.asarray(np.concatenate([c, -s], axis=0), F32)
    m3 = jnp.asarray(np.concatenate([cc, sc], axis=0) / math.sqrt(L * FNET_GROUP_DIM), F32)
    return pl.pallas_call(
        _dftc_kernel,
        out_shape=jax.ShapeDtypeStruct((B, L, W), F32),
        grid=(B,),
        in_specs=[pl.BlockSpec((1, L, W), lambda b: (b, 0, 0)),
                  pl.BlockSpec(mc.shape, lambda b: (0, 0)),
                  pl.BlockSpec(m3.shape, lambda b: (0, 0))],
        out_specs=pl.BlockSpec((1, L, W), lambda b: (b, 0, 0)),
        compiler_params=_params(("arbitrary",), VMEM_LIMIT),
        name="dftc",
    )(f, mc, m3)


def _out_kernel(has_f, *refs):
    if has_f:
        (h_ref, mod_ref, yf_ref, wof_ref, o_ref, woa_ref, n2_ref, wr_ref,
         h1_ref, m_ref, aff_ref) = refs
    else:
        (h_ref, mod_ref, o_ref, woa_ref, n2_ref, wr_ref, h1_ref, m_ref, aff_ref) = refs
    D = D_MODEL
    Hk, G = o_ref.shape[1], o_ref.shape[2]
    ocat = jnp.concatenate([o_ref[0, hk, g] for hk in range(Hk) for g in range(G)], axis=1)
    y = jnp.dot(ocat, woa_ref[...], preferred_element_type=F32)
    if has_f:
        y = y + jnp.dot(yf_ref[0].astype(BF16), wof_ref[...], preferred_element_type=F32)
    mod = mod_ref[0]
    h1 = h_ref[0] + mod[:, 2 * D:3 * D] * y
    h1_ref[0] = h1
    m = _modulate(h1, n2_ref[...], mod[:, 3 * D:4 * D], mod[:, 4 * D:5 * D])
    _store_tiled(m_ref.at[0], slice(None), m)
    logit = lax.dot_general(wr_ref[...], m, NT, precision=HIGHEST,
                            preferred_element_type=F32)
    e = jnp.exp(logit - jnp.max(logit, axis=0, keepdims=True))
    aff_ref[0] = e / jnp.sum(e, axis=0, keepdims=True)


def _out_proj(h, mod, yf, wof, o, woa, n2, wr_t, tm):
    B, R, D = h.shape
    _, Hk, G, _, _ = o.shape
    full = lambda a: pl.BlockSpec(a.shape, lambda b, i: (0,) * a.ndim)
    row = lambda w_: pl.BlockSpec((1, tm, w_), lambda b, i: (b, i, 0))
    modspec = pl.BlockSpec((1, 1, N_MOD * D), lambda b, i: (b, 0, 0))
    ospec = pl.BlockSpec((1, Hk, G, tm, LANES), lambda b, i: (b, 0, 0, i, 0))
    has_f = yf is not None
    ins = [h, mod] + ([yf, wof] if has_f else []) + [o, woa, n2, wr_t]
    specs = ([row(D), modspec] + ([row(FNET_WIDTH), full(wof)] if has_f else [])
             + [ospec, full(woa), full(n2), full(wr_t)])
    return pl.pallas_call(
        functools.partial(_out_kernel, has_f),
        out_shape=(jax.ShapeDtypeStruct((B, R, D), F32),
                   jax.ShapeDtypeStruct((B, R, SUB, LANES), F32),
                   jax.ShapeDtypeStruct((B, N_EXPERTS, R), F32)),
        grid=(B, R // tm),
        in_specs=specs,
        out_specs=(row(D), pl.BlockSpec((1, tm, SUB, LANES), lambda b, i: (b, i, 0, 0)),
                   pl.BlockSpec((1, N_EXPERTS, tm), lambda b, i: (b, 0, i))),
        compiler_params=_params(("arbitrary", "arbitrary"), VMEM_LIMIT),
        name="out_proj",
    )(*ins)


def _topk_kernel(cap, aff_ref, u_ref, ones_ref, lmat_ref, lc_ref, lg_ref, cnt_ref, off_ref,
                 pos_ref, ac_ref):
    a = aff_ref[0]
    E, N = a.shape
    NC = N // LANES
    R = NC * E
    keys = pltpu.bitcast(a, I32)

    def bit_step(i, tau):
        cand = tau | jnp.left_shift(jnp.int32(1), 30 - i)
        cnt = jnp.sum((keys >= cand).astype(I32), axis=1, keepdims=True)
        return jnp.where(cnt >= cap, cand, tau)

    tau = lax.fori_loop(0, 31, bit_step, jnp.zeros((E, 1), I32))
    gt = (keys > tau).astype(F32)
    eq = (keys == tau).astype(F32)
    need = (cap - jnp.sum(gt, axis=1, keepdims=True))

    def chunked(x):
        return jnp.concatenate([x[:, c * LANES:(c + 1) * LANES] for c in range(NC)], axis=0)

    a_c, gt_c, eq_c = chunked(a), chunked(gt), chunked(eq)
    need_c = jnp.tile(need, (NC, 1))

    def prefix(x):
        xb = x.astype(BF16)
        loc = jnp.dot(xb, u_ref[...], preferred_element_type=F32)
        tot = jnp.dot(xb, ones_ref[...], preferred_element_type=F32)
        offs = jnp.dot(lmat_ref[...], tot.astype(BF16), preferred_element_type=F32)
        return loc, tot, offs

    loc, tot, offs = prefix(eq_c)
    sel = jnp.maximum(gt_c, jnp.where(loc + offs < need_c, eq_c, 0.0))
    loc, tot, offs = prefix(sel)
    pos_ref[...] = jnp.where(sel > 0.0, loc, -1.0)
    ac_ref[...] = a_c
    cnt_ref[0] = tot.astype(I32)
    off_ref[0] = offs.astype(I32)
    rb = 16
    lane = lax.broadcasted_iota(I32, (rb, LANES), 1).astype(F32)

    def compact(i, carry):
        rows = pl.ds(pl.multiple_of(i * rb, rb), rb)
        selpos = pos_ref[rows, :]
        aff = ac_ref[rows, :]
        lc = jnp.zeros((rb, LANES), F32)
        lg = jnp.zeros((rb, LANES), F32)
        for t in range(LANES):
            hit = selpos[:, t:t + 1] == lane
            lc = jnp.where(hit, float(t), lc)
            lg = jnp.where(hit, aff[:, t:t + 1], lg)
        lc_ref[0, rows, :] = lc.astype(I32)
        lg_ref[0, rows, :] = lg
        return carry

    lax.fori_loop(0, R // rb, compact, 0)


def _topk(aff, cap):
    B, E, N = aff.shape
    NC = N // LANES
    R = NC * E
    i = np.arange(LANES)
    u = jnp.asarray(i[:, None] < i[None, :], BF16)
    ones = jnp.ones((LANES, LANES), BF16)
    r = np.arange(R)
    lmat = jnp.asarray((r[:, None] % E == r[None, :] % E) & (r[None, :] // E < r[:, None] // E), BF16)
    full = lambda a: pl.BlockSpec(a.shape, lambda b: (0,) * a.ndim)
    ospec = pl.BlockSpec((1, R, LANES), lambda b: (b, 0, 0))
    sds = lambda dt: jax.ShapeDtypeStruct((B, R, LANES), dt)
    lc, lg, cnt, off = pl.pallas_call(
        functools.partial(_topk_kernel, cap),
        out_shape=(sds(I32), sds(F32), sds(I32), sds(I32)),
        grid=(B,),
        in_specs=[pl.BlockSpec((1, E, N), lambda b: (b, 0, 0)), full(u), full(ones), full(lmat)],
        out_specs=(ospec, ospec, ospec, ospec),
        scratch_shapes=[pltpu.VMEM((R, LANES), F32), pltpu.VMEM((R, LANES), F32)],
        compiler_params=_params(("arbitrary",), VMEM_LIMIT),
        name="topk",
    )(aff, u, ones, lmat)
    by_expert = lambda x: x.reshape(B, NC, E, LANES).transpose(0, 2, 1, 3)
    return (by_expert(lc), by_expert(lg), by_expert(cnt)[..., 0], by_expert(off)[..., 0])


GATHER_ROWS = 256


def _gather_kernel(cap, cnt_ref, off_ref, lc_ref, m_hbm, o_ref, x_ref, sems):
    e = pl.program_id(0)
    B, NC = lc_ref.shape[0], lc_ref.shape[1]
    for b in range(B):
        def chunk_body(c, carry, b=b):
            base = off_ref[b, e, c] + b * cap

            def row_body(r, carry):
                t = c * LANES + lc_ref[b, c, r]
                pltpu.make_async_copy(m_hbm.at[b, t], x_ref.at[base + r], sems.at[b]).start()
                return carry

            return lax.fori_loop(0, cnt_ref[b, e, c], row_body, carry)

        lax.fori_loop(0, NC, chunk_body, 0)
    step = min(GATHER_ROWS, cap)
    for b in range(B):
        done = x_ref.at[pl.ds(b * cap, cap)]
        pltpu.make_async_copy(done, done, sems.at[b]).wait()
        for r0 in range(b * cap, (b + 1) * cap, step):
            rows = slice(r0, r0 + step)
            o_ref[0, rows, :] = _untile(x_ref, rows).astype(BF16)


def _gather(m, lc, cnt, off, cap):
    B, N, sub, _ = m.shape
    E, NC = lc.shape[1], lc.shape[2]
    return pl.pallas_call(
        functools.partial(_gather_kernel, cap),
        out_shape=jax.ShapeDtypeStruct((E, B * cap, sub * LANES), BF16),
        grid_spec=pltpu.PrefetchScalarGridSpec(
            num_scalar_prefetch=2,
            grid=(E,),
            in_specs=[pl.BlockSpec((B, None, NC, LANES), lambda e, *_: (0, e, 0, 0),
                                   memory_space=pltpu.SMEM),
                      pl.BlockSpec(memory_space=pl.ANY)],
            out_specs=pl.BlockSpec((1, B * cap, sub * LANES), lambda e, *_: (e, 0, 0)),
            scratch_shapes=[pltpu.VMEM((B * cap, sub, LANES), m.dtype),
                            pltpu.SemaphoreType.DMA((B,))]),
        compiler_params=_params(("arbitrary",), VMEM_LIMIT),
        name="gather",
    )(cnt, off, lc, m)


def _ffn_kernel(n_parts, chunk_rows, *refs):
    wg_ref, wu_ref, wd_ref = refs[:3]
    x_refs = refs[3:3 + n_parts]
    o_refs = refs[3 + n_parts:3 + 2 * n_parts]
    acc_refs = refs[3 + 2 * n_parts:]
    j = pl.program_id(1)
    last = pl.num_programs(1) - 1
    wg = wg_ref[0].astype(BF16)
    wu = wu_ref[0].astype(BF16)
    wd = wd_ref[0].astype(BF16)
    for x_ref, o_ref, acc_ref, nrows in zip(x_refs, o_refs, acc_refs, chunk_rows):
        for r0 in range(0, x_ref.shape[1], nrows):
            rows = slice(r0, r0 + nrows)
            x = x_ref[0, rows, :]
            a = jnp.dot(x, wg, preferred_element_type=F32)
            u = jnp.dot(x, wu, preferred_element_type=F32)
            hh = (a * jax.nn.sigmoid(a) * u).astype(BF16)
            y = jnp.dot(hh, wd, preferred_element_type=F32)

            @pl.when(j == 0)
            def _():
                acc_ref[rows, :] = y

            @pl.when((j > 0) & (j < last))
            def _():
                acc_ref[rows, :] += y

            @pl.when(j == last)
            def _():
                _store_tiled(o_ref.at[0], rows, acc_ref[rows, :] + y)


def _ffn(xs, layer, w_gate, w_up, w_down, chunk_rows):
    _, E, D, F = w_gate.shape
    tf = 512
    assert F // tf >= 2
    xspec = lambda x: pl.BlockSpec((1, x.shape[1], D), lambda e, j: (e, 0, 0))
    ospec = lambda x: pl.BlockSpec((1, x.shape[1], SUB, LANES), lambda e, j: (e, 0, 0, 0))
    outs = pl.pallas_call(
        functools.partial(_ffn_kernel, len(xs), chunk_rows),
        out_shape=tuple(jax.ShapeDtypeStruct((E, x.shape[1], SUB, LANES), F32) for x in xs),
        grid=(E, F // tf),
        in_specs=[pl.BlockSpec((None, 1, D, tf), lambda e, j: (layer, e, 0, j)),
                  pl.BlockSpec((None, 1, D, tf), lambda e, j: (layer, e, 0, j)),
                  pl.BlockSpec((None, 1, tf, D), lambda e, j: (layer, e, j, 0))]
        + [xspec(x) for x in xs],
        out_specs=tuple(ospec(x) for x in xs),
        scratch_shapes=[pltpu.VMEM((x.shape[1], D), F32) for x in xs],
        compiler_params=_params(("arbitrary", "arbitrary"), VMEM_LIMIT),
        name="ffn",
    )(w_gate, w_up, w_down, *xs)
    return outs


def _combine_kernel(cnt_ref, off_ref, lc_ref, lg_ref, y_ref, o_ref):
    b = pl.program_id(0)
    hf = pl.program_id(1)
    e = pl.program_id(2)
    nch = o_ref.shape[1] // LANES

    @pl.when(e == 0)
    def _():
        o_ref[...] = jnp.zeros_like(o_ref)

    def chunk_body(ci, carry):
        c = hf * nch + ci
        base = off_ref[b, e, c]

        def row_body(r, carry):
            t = ci * LANES + lc_ref[c, r]
            o_ref[0, t] = o_ref[0, t] + lg_ref[c, r] * y_ref[0, base + r]
            return carry

        return lax.fori_loop(0, cnt_ref[b, e, c], row_body, carry)

    lax.fori_loop(0, nch, chunk_body, 0)


def _combine(y, lc, lg, cnt, off, n_tokens, cap, n_split):
    E = y.shape[0]
    B, _, NC, _ = lc.shape
    nh = n_tokens // n_split
    smem = lambda: pl.BlockSpec((None, None, NC, LANES), lambda b, h, e, *_: (b, e, 0, 0),
                                memory_space=pltpu.SMEM)
    return pl.pallas_call(
        _combine_kernel,
        out_shape=jax.ShapeDtypeStruct((B, n_tokens, SUB, LANES), F32),
        grid_spec=pltpu.PrefetchScalarGridSpec(
            num_scalar_prefetch=2,
            grid=(B, n_split, E),
            in_specs=[smem(), smem(),
                      pl.BlockSpec((1, cap, SUB, LANES), lambda b, h, e, *_: (e, b, 0, 0))],
            out_specs=pl.BlockSpec((1, nh, SUB, LANES), lambda b, h, e, *_: (b, h, 0, 0))),
        compiler_params=_params(("arbitrary", "arbitrary", "arbitrary"), VMEM_LIMIT),
        name="combine",
    )(cnt, off, lc, lg, y)


def _final_kernel(h_ref, moe_ref, mod_ref, g_ref, o_ref):
    D = D_MODEL
    h = h_ref[0] + mod_ref[0][:, 5 * D:6 * D] * _untile(moe_ref.at[0], slice(None))
    o_ref[0] = _rms(h, g_ref[...])


def _final(h, moe, mod, g, tm):
    B, R, D = h.shape
    row = pl.BlockSpec((1, tm, D), lambda b, i: (b, i, 0))
    return pl.pallas_call(
        _final_kernel,
        out_shape=jax.ShapeDtypeStruct((B, R, D), F32),
        grid=(B, R // tm),
        in_specs=[row, pl.BlockSpec((1, tm, SUB, LANES), lambda b, i: (b, i, 0, 0)),
                  pl.BlockSpec((1, 1, N_MOD * D), lambda b, i: (b, 0, 0)),
                  pl.BlockSpec((1, D), lambda b, i: (0, 0))],
        out_specs=row,
        compiler_params=_params(("arbitrary", "arbitrary"), VMEM_LIMIT),
        name="final",
    )(h, moe, mod, g)


def _rope_tables(n_lat, n_ctx, segments):
    t = jnp.arange(n_lat)
    pos = {"row": (t // GRID_W).astype(F32), "col": (t % GRID_W).astype(F32)}
    freq = {"row": np.zeros((2, LANES), np.float32), "col": np.zeros((2, LANES), np.float32)}
    first = np.zeros(LANES, np.float32)
    second = np.zeros(LANES, np.float32)
    for lane0, width, which in segments:
        half = width // 2
        idx = np.arange(half, dtype=np.float32)
        for lo, mask in ((lane0, first), (lane0 + half, second)):
            freq[which][0, lo:lo + half] = idx / half
            freq[which][1, lo:lo + half] = 1.0
            mask[lo:lo + half] = 1.0
    ang = jnp.zeros((n_lat, LANES), F32)
    for which in ("row", "col"):
        inv = (ROPE_THETA ** (-jnp.asarray(freq[which][0]))) * jnp.asarray(freq[which][1])
        ang = ang + pos[which][:, None] * inv[None, :]
    sin = jnp.sin(ang)
    lat = (jnp.cos(ang), -sin * first[None, :], sin * second[None, :])
    ctx = (jnp.ones((n_ctx, LANES), F32), jnp.zeros((n_ctx, LANES), F32), jnp.zeros((n_ctx, LANES), F32))
    return lat, ctx


def _head_slots(w, n_heads, width, lo, hi):
    k = w.shape[0]
    w3 = w.reshape(k, n_heads, width)[:, :, lo:hi]
    return jnp.pad(w3, ((0, 0), (0, 0), (0, LANES - (hi - lo)))).reshape(k, n_heads * LANES)


def _moe(m_l, aff_l, m_c, aff_c, layer, w_gate, w_up, w_down):
    B, S = m_l.shape[:2]
    cap_l = CAPACITY_FACTOR * S // N_EXPERTS
    lc, lg, cnt, off = _topk(aff_l, cap_l)
    xs = [_gather(m_l, lc, cnt, off, cap_l)]
    chunk_rows = [cap_l]
    if m_c is not None:
        L = m_c.shape[1]
        cap_c = CAPACITY_FACTOR * L // N_EXPERTS
        lcc, lgc, cntc, offc = _topk(aff_c, cap_c)
        xs.append(_gather(m_c, lcc, cntc, offc, cap_c))
        chunk_rows.append(B * cap_c)
    ys = _ffn(xs, layer, w_gate, w_up, w_down, tuple(chunk_rows))
    out_l = _combine(ys[0], lc, lg, cnt, off, S, cap_l, 2)
    out_c = None
    if m_c is not None:
        out_c = _combine(ys[1], lcc, lgc, cntc, offc, L, cap_c, 1)
    return out_l, out_c


def kernel(x, c, ctx, c_ctx, ada_w, ada_b, norm1, norm2, ab_w_in, ab_q_norm, ab_w_uq, ab_kv_norm,
           ab_w_ukv, ab_w_o, c_w_in, c_q_gain, c_k_gain, c_w_o, moe_router, moe_w_gate, moe_w_up,
           moe_w_down, final_norm):
    B, S, D = x.shape
    L = ctx.shape[1]
    depth = ada_w.shape[0]
    tm_l, tm_c = 512, L
    row2 = lambda v: v.reshape(1, -1)

    c8 = jnp.zeros((8, D), F32).at[:B].set(c).at[B].set(c_ctx)
    mod = _ada(c8, ada_w, ada_b)
    mod_l = [mod[i, :B][:, None, :] for i in range(depth)]
    mod_c = [jnp.broadcast_to(mod[i, B][None, None, :], (B, 1, N_MOD * D)) for i in range(depth)]

    h_l, h_c = x, ctx
    moe_l = moe_c = None
    for i in range(depth):
        last = i == depth - 1
        j = i // 2
        wr_t = moe_router[i].T
        n2 = row2(norm2[i])
        if i % 2 == 0:
            assert moe_l is None
            tabs_l, tabs_c = _rope_tables(S, L, ((MLA_NOPE, MLA_ROPE // 2, "row"),
                                                 (MLA_NOPE + MLA_ROPE // 2, MLA_ROPE // 2, "col")))
            w_ukv = ab_w_ukv[j]
            ekr = np.zeros((MLA_ROPE, MLA_HEADS * LANES), np.float32)
            for hd in range(MLA_HEADS):
                ekr[np.arange(MLA_ROPE), hd * LANES + MLA_NOPE + np.arange(MLA_ROPE)] = 1.0
            w = {
                "win": ab_w_in[j].astype(BF16),
                "qn": row2(ab_q_norm[j]),
                "wuq": _head_slots(ab_w_uq[j], MLA_HEADS, MLA_QK, 0, MLA_QK).astype(BF16),
                "kvn": row2(ab_kv_norm[j]),
                "wuk": _head_slots(w_ukv, MLA_HEADS, MLA_NOPE + MLA_V, 0, MLA_NOPE).astype(BF16),
                "wuvt": _head_slots(w_ukv, MLA_HEADS, MLA_NOPE + MLA_V, MLA_NOPE,
                                    MLA_NOPE + MLA_V).T.astype(BF16),
                "ekr": jnp.asarray(ekr, BF16),
            }
            n1 = row2(norm1[i])
            f_l, q_l, k_l, vt_l = _in_ab(h_l, mod_l[i], n1, w, tabs_l, tm_l)
            f_c, q_c, k_c, vt_c = _in_ab(h_c, mod_c[i], n1, w, tabs_c, tm_c)
            o_l = _attention(q_l[:, :, None], k_c, vt_c, k_l, vt_l, 512, 512)
            yf_l = _fourier_lat(f_l)
            wof = ab_w_o[j][:FNET_WIDTH].astype(BF16)
            woa = jnp.pad(ab_w_o[j][FNET_WIDTH:].reshape(MLA_HEADS, MLA_V, D),
                          ((0, 0), (0, LANES - MLA_V), (0, 0))).reshape(MLA_HEADS * LANES, D).astype(BF16)
            h_l, m_l, aff_l = _out_proj(h_l, mod_l[i], yf_l, wof, o_l, woa, n2, wr_t, tm_l)
            m_c = aff_c = None
            if not last:
                o_c = _attention(q_c[:, :, None], k_c, vt_c, None, None, L, L)
                yf_c = _fourier_ctx(f_c)
                h_c, m_c, aff_c = _out_proj(h_c, mod_c[i], yf_c, wof, o_c, woa, n2, wr_t, tm_c)
        else:
            tabs_l, tabs_c = _rope_tables(S, L, ((0, GQA_HEAD_DIM // 2, "row"),
                                                 (GQA_HEAD_DIM // 2, GQA_HEAD_DIM // 2, "col")))
            nqk = (GQA_HEADS + GQA_KV_HEADS) * GQA_HEAD_DIM
            w = {"wqk": c_w_in[j][:, :nqk].astype(BF16), "wvt": c_w_in[j][:, nqk:].T.astype(BF16),
                 "qg": row2(c_q_gain[j]), "kg": row2(c_k_gain[j])}
            n1 = row2(norm1[i])
            h_l, q_l, k_l, vt_l = _in_c(h_l, moe_l, mod_l[i - 1], mod_l[i], n1, w, tabs_l, tm_l)
            h_c, q_c, k_c, vt_c = _in_c(h_c, moe_c, mod_c[i - 1], mod_c[i], n1, w, tabs_c, tm_c)
            grp = lambda q: q.reshape(B, GQA_KV_HEADS, GQA_GROUP, q.shape[2], LANES)
            o_l = _attention(grp(q_l), k_c, vt_c, k_l, vt_l, 256, 512)
            woa = c_w_o[j].astype(BF16)
            h_l, m_l, aff_l = _out_proj(h_l, mod_l[i], None, None, o_l, woa, n2, wr_t, tm_l)
            m_c = aff_c = None
            if not last:
                o_c = _attention(grp(q_c), k_c, vt_c, None, None, L, L)
                h_c, m_c, aff_c = _out_proj(h_c, mod_c[i], None, None, o_c, woa, n2, wr_t, tm_c)
        moe_l, moe_c = _moe(m_l, aff_l, m_c, aff_c, i, moe_w_gate, moe_w_up, moe_w_down)
    return _final(h_l, moe_l, mod_l[depth - 1], row2(final_norm), tm_l)
```

```python
import functools
import math

import jax
import jax.numpy as jnp
import numpy as np
from jax import lax
from jax.experimental import pallas as pl
from jax.experimental.pallas import tpu as pltpu

F32 = jnp.float32
BF16 = jnp.bfloat16
I32 = jnp.int32
HIGHEST = lax.Precision.HIGHEST

D_MODEL = 1024
GRID_W = 64
EPS = 1e-6
ROPE_THETA = 10000.0
N_MOD = 6
FNET_GROUPS = 4
FNET_GROUP_DIM = 128
FNET_WIDTH = FNET_GROUPS * FNET_GROUP_DIM
MLA_HEADS = 8
MLA_Q_LORA = 256
MLA_KV_LORA = 128
MLA_NOPE = 64
MLA_ROPE = 32
MLA_V = 64
MLA_QK = MLA_NOPE + MLA_ROPE
GQA_HEADS = 8
GQA_KV_HEADS = 2
GQA_GROUP = GQA_HEADS // GQA_KV_HEADS
GQA_HEAD_DIM = 128
N_EXPERTS = 16
EXPERT_FF = 2048
CAPACITY_FACTOR = 2

LANES = 128
SUB = D_MODEL // LANES
VMEM_LIMIT = 56 * 1024 * 1024
LOG2E = math.log2(math.e)
NT = (((1,), (1,)), ((), ()))


def _params(sem, vmem=None):
    return pltpu.CompilerParams(dimension_semantics=sem, vmem_limit_bytes=vmem)


def _rms(x, g):
    return x * lax.rsqrt(jnp.mean(x * x, axis=-1, keepdims=True) + EPS) * g


def _modulate(h, g, shift, scale):
    return _rms(h, g) * (1.0 + scale) + shift


def _rope(x, cos, sin_fwd, sin_bwd, shift):
    return (x * cos + pltpu.roll(x, LANES - shift, 1) * sin_fwd
            + pltpu.roll(x, shift, 1) * sin_bwd)


def _untile(ref, rows):
    return jnp.concatenate([ref[rows, s, :] for s in range(SUB)], axis=1)


def _store_tiled(ref, rows, x):
    for s in range(SUB):
        ref[rows, s, :] = x[:, s * LANES:(s + 1) * LANES]


def _ada_kernel(c_ref, w_ref, b_ref, o_ref):
    c = c_ref[...]
    x = c * jax.nn.sigmoid(c)
    o_ref[0] = jnp.dot(x, w_ref[0], precision=HIGHEST, preferred_element_type=F32) + b_ref[0]


def _ada(c8, ada_w, ada_b):
    depth, d, n = ada_w.shape
    tn = 1536
    return pl.pallas_call(
        _ada_kernel,
        out_shape=jax.ShapeDtypeStruct((depth, 8, n), F32),
        grid=(depth, n // tn),
        in_specs=[pl.BlockSpec((8, d), lambda l, j: (0, 0)),
                  pl.BlockSpec((1, d, tn), lambda l, j: (l, 0, j)),
                  pl.BlockSpec((1, 1, tn), lambda l, j: (l, 0, j))],
        out_specs=pl.BlockSpec((1, 8, tn), lambda l, j: (l, 0, j)),
        compiler_params=_params(("arbitrary", "arbitrary"), VMEM_LIMIT),
        name="ada",
    )(c8, ada_w, ada_b.reshape(depth, 1, n))


def _in_ab_kernel(scale, *refs):
    (h_ref, mod_ref, n1_ref, win_ref, qn_ref, wuq_ref, kvn_ref, wuk_ref, wuvt_ref, ekr_ref,
     cos_ref, sf_ref, sb_ref, f_ref, q_ref, k_ref, vt_ref) = refs
    D = D_MODEL
    h = h_ref[0]
    mod = mod_ref[0]
    a = _modulate(h, n1_ref[...], mod[:, 0:D], mod[:, D:2 * D]).astype(BF16)
    p = jnp.dot(a, win_ref[...], preferred_element_type=F32)
    o = FNET_WIDTH
    f_ref[0] = p[:, :o]
    cq = p[:, o:o + MLA_Q_LORA]
    o += MLA_Q_LORA
    ckv = p[:, o:o + MLA_KV_LORA]
    o += MLA_KV_LORA
    kr = p[:, o:o + MLA_ROPE]
    cqn = _rms(cq, qn_ref[...]).astype(BF16)
    ckvn = _rms(ckv, kvn_ref[...]).astype(BF16)
    q = jnp.dot(cqn, wuq_ref[...], preferred_element_type=F32)
    k = (jnp.dot(ckvn, wuk_ref[...], preferred_element_type=F32)
         + jnp.dot(kr.astype(BF16), ekr_ref[...], preferred_element_type=F32))
    vt = lax.dot_general(wuvt_ref[...], ckvn, NT, preferred_element_type=F32)
    cos, sf, sb = cos_ref[...], sf_ref[...], sb_ref[...]
    shift = MLA_ROPE // 4
    for hd in range(MLA_HEADS):
        sl = slice(hd * LANES, (hd + 1) * LANES)
        q_ref[0, hd] = (_rope(q[:, sl], cos, sf, sb, shift) * scale).astype(BF16)
        k_ref[0, hd] = _rope(k[:, sl], cos, sf, sb, shift).astype(BF16)
        vt_ref[0, hd] = vt[sl, :].astype(BF16)


def _in_ab(h, mod, n1, w, tables, tm):
    B, R, D = h.shape
    H = MLA_HEADS
    full = lambda a: pl.BlockSpec(a.shape, lambda b, i: (0,) * a.ndim)
    row = lambda w_: pl.BlockSpec((1, tm, w_), lambda b, i: (b, i, 0))
    tab = pl.BlockSpec((tm, LANES), lambda b, i: (i, 0))
    hd = pl.BlockSpec((1, H, tm, LANES), lambda b, i: (b, 0, i, 0))
    hdt = pl.BlockSpec((1, H, LANES, tm), lambda b, i: (b, 0, 0, i))
    weights = [n1, w["win"], w["qn"], w["wuq"], w["kvn"], w["wuk"], w["wuvt"], w["ekr"]]
    return pl.pallas_call(
        functools.partial(_in_ab_kernel, MLA_QK ** -0.5 * LOG2E),
        out_shape=(jax.ShapeDtypeStruct((B, R, FNET_WIDTH), F32),
                   jax.ShapeDtypeStruct((B, H, R, LANES), BF16),
                   jax.ShapeDtypeStruct((B, H, R, LANES), BF16),
                   jax.ShapeDtypeStruct((B, H, LANES, R), BF16)),
        grid=(B, R // tm),
        in_specs=[row(D), pl.BlockSpec((1, 1, N_MOD * D), lambda b, i: (b, 0, 0))]
        + [full(a) for a in weights] + [tab, tab, tab],
        out_specs=(row(FNET_WIDTH), hd, hd, hdt),
        compiler_params=_params(("arbitrary", "arbitrary"), VMEM_LIMIT),
        name="in_ab",
    )(h, mod, *weights, *tables)


def _in_c_kernel(scale, *refs):
    (h_ref, moe_ref, modp_ref, mod_ref, n1_ref, wqk_ref, wvt_ref, qg_ref, kg_ref,
     cos_ref, sf_ref, sb_ref, h2_ref, q_ref, k_ref, vt_ref) = refs
    D = D_MODEL
    h = h_ref[0] + modp_ref[0][:, 5 * D:6 * D] * _untile(moe_ref.at[0], slice(None))
    h2_ref[0] = h
    mod = mod_ref[0]
    a = _modulate(h, n1_ref[...], mod[:, 0:D], mod[:, D:2 * D]).astype(BF16)
    p = jnp.dot(a, wqk_ref[...], preferred_element_type=F32)
    vt = lax.dot_general(wvt_ref[...], a, NT, preferred_element_type=F32)
    cos, sf, sb = cos_ref[...], sf_ref[...], sb_ref[...]
    shift = GQA_HEAD_DIM // 4
    nq = GQA_HEADS * GQA_HEAD_DIM
    for hd in range(GQA_HEADS):
        x = _rms(p[:, hd * LANES:(hd + 1) * LANES], qg_ref[...])
        q_ref[0, hd] = (_rope(x, cos, sf, sb, shift) * scale).astype(BF16)
    for hd in range(GQA_KV_HEADS):
        x = _rms(p[:, nq + hd * LANES:nq + (hd + 1) * LANES], kg_ref[...])
        k_ref[0, hd] = _rope(x, cos, sf, sb, shift).astype(BF16)
        vt_ref[0, hd] = vt[hd * LANES:(hd + 1) * LANES, :].astype(BF16)


def _in_c(h, moe, mod_prev, mod, n1, w, tables, tm):
    B, R, D = h.shape
    full = lambda a: pl.BlockSpec(a.shape, lambda b, i: (0,) * a.ndim)
    row = lambda w_: pl.BlockSpec((1, tm, w_), lambda b, i: (b, i, 0))
    tiled = pl.BlockSpec((1, tm, SUB, LANES), lambda b, i: (b, i, 0, 0))
    modspec = pl.BlockSpec((1, 1, N_MOD * D), lambda b, i: (b, 0, 0))
    tab = pl.BlockSpec((tm, LANES), lambda b, i: (i, 0))
    hd = lambda n: pl.BlockSpec((1, n, tm, LANES), lambda b, i: (b, 0, i, 0))
    hdt = pl.BlockSpec((1, GQA_KV_HEADS, LANES, tm), lambda b, i: (b, 0, 0, i))
    weights = [n1, w["wqk"], w["wvt"], w["qg"], w["kg"]]
    return pl.pallas_call(
        functools.partial(_in_c_kernel, GQA_HEAD_DIM ** -0.5 * LOG2E),
        out_shape=(jax.ShapeDtypeStruct((B, R, D), F32),
                   jax.ShapeDtypeStruct((B, GQA_HEADS, R, LANES), BF16),
                   jax.ShapeDtypeStruct((B, GQA_KV_HEADS, R, LANES), BF16),
                   jax.ShapeDtypeStruct((B, GQA_KV_HEADS, LANES, R), BF16)),
        grid=(B, R // tm),
        in_specs=[row(D), tiled, modspec, modspec] + [full(a) for a in weights] + [tab, tab, tab],
        out_specs=(row(D), hd(GQA_HEADS), hd(GQA_KV_HEADS), hdt),
        compiler_params=_params(("arbitrary", "arbitrary"), VMEM_LIMIT),
        name="in_c",
    )(h, moe, mod_prev, mod, *weights, *tables)


ATTN_QUERIES = 2048


def _attn_kernel(n_lat, tk, *refs):
    if n_lat:
        q_ref, kc_ref, vct_ref, kl_ref, vlt_ref, o_ref, acc_ref, s0, s1, p0, p1 = refs
        s_bufs, p_bufs = (s0, s1), (p0, p1)
    else:
        q_ref, kc_ref, vct_ref, o_ref, acc_ref = refs
    G, tq = q_ref.shape[2], q_ref.shape[3]
    q = q_ref[0, 0].reshape(G * tq, LANES)

    def scores(k):
        return lax.dot_general(k, q, NT, preferred_element_type=F32)

    s = scores(kc_ref[0, 0])
    m = jnp.max(s, axis=0, keepdims=True)
    p = jnp.exp2(s - m)
    l = jnp.sum(p, axis=0, keepdims=True)
    acc_ref[...] = jnp.dot(vct_ref[0, 0], p.astype(BF16), preferred_element_type=F32)

    if n_lat:
        assert n_lat == 1 or n_lat % 2 == 0

        def chunk(c):
            return pl.ds(pl.multiple_of(c * tk, tk), tk)

        def score_stage(c, slot):
            s_bufs[slot][...] = scores(kl_ref[0, 0, chunk(c), :])

        def softmax_stage(slot, m, l):
            s = s_bufs[slot][...]
            m_new = jnp.maximum(m, jnp.max(s, axis=0, keepdims=True))
            p = jnp.exp2(s - m_new)
            p_bufs[slot][...] = p.astype(BF16)
            alpha = jnp.exp2(m - m_new)
            return m_new, alpha * l + jnp.sum(p, axis=0, keepdims=True), alpha

        def value_stage(c, slot, alpha):
            pv = jnp.dot(vlt_ref[0, 0, :, chunk(c)], p_bufs[slot][...], preferred_element_type=F32)
            acc_ref[...] = alpha * acc_ref[...] + pv

        score_stage(0, 0)
        m, l, alpha = softmax_stage(0, m, l)
        if n_lat > 1:
            score_stage(1, 1)

            def body(i, carry):
                m, l, alpha = carry
                for slot in (0, 1):
                    c = 2 * i + slot
                    score_stage(c + 2, slot)
                    value_stage(c, slot, alpha)
                    m, l, alpha = softmax_stage(1 - slot, m, l)
                return m, l, alpha

            m, l, alpha = lax.fori_loop(0, (n_lat - 2) // 2, body, (m, l, alpha))
            value_stage(n_lat - 2, 0, alpha)
            m, l, alpha = softmax_stage(1, m, l)
        value_stage(n_lat - 1, (n_lat - 1) % 2, alpha)
    o_ref[0, 0] = (acc_ref[...] / l).T.reshape(G, tq, LANES).astype(BF16)


def _attention(q, kc, vct, kl, vlt, tq, tk):
    B, Hk, G, R, _ = q.shape
    Lc = kc.shape[2]
    n_lat = 0 if kl is None else kl.shape[2] // tk
    qspec = pl.BlockSpec((1, 1, G, tq, LANES), lambda b, h, i: (b, h, 0, i, 0))
    kspec = lambda n: pl.BlockSpec((1, 1, n, LANES), lambda b, h, i: (b, h, 0, 0))
    vspec = lambda n: pl.BlockSpec((1, 1, LANES, n), lambda b, h, i: (b, h, 0, 0))
    ins = [q, kc, vct] + ([kl, vlt] if n_lat else [])
    specs = [qspec, kspec(Lc), vspec(Lc)] + ([kspec(kl.shape[2]), vspec(kl.shape[2])] if n_lat else [])
    return pl.pallas_call(
        functools.partial(_attn_kernel, n_lat, tk),
        out_shape=jax.ShapeDtypeStruct(q.shape, BF16),
        grid=(B, Hk, R // tq),
        in_specs=specs,
        out_specs=qspec,
        scratch_shapes=[pltpu.VMEM((LANES, G * tq), F32)]
        + ([pltpu.VMEM((tk, G * tq), F32)] * 2 + [pltpu.VMEM((tk, G * tq), BF16)] * 2 if n_lat else []),
        compiler_params=_params(("arbitrary", "arbitrary", "arbitrary"), VMEM_LIMIT),
        name="attn",
    )(*ins)


def _dft_mats(n):
    k = np.arange(n, dtype=np.float64)
    ang = 2.0 * np.pi * np.outer(k, k) / n
    return np.cos(ang), np.sin(ang)


def _dft1_kernel(x_ref, m1_ref, tw_ref, o_ref):
    s1, ns2 = x_ref.shape[1], x_ref.shape[2]
    for i in range(ns2):
        a = jnp.dot(m1_ref[...], x_ref[0, :, i, :], precision=HIGHEST, preferred_element_type=F32)
        are, aim = a[:s1], a[s1:]
        tre = jnp.tile(tw_ref[0, :, i * LANES:(i + 1) * LANES], (1, FNET_GROUPS))
        tim = jnp.tile(tw_ref[1, :, i * LANES:(i + 1) * LANES], (1, FNET_GROUPS))
        o_ref[0, 0, :, i, :] = are * tre - aim * tim
        o_ref[0, 1, :, i, :] = are * tim + aim * tre


def _dft2_kernel(a_ref, m2_ref, m3_ref, o_ref):
    kb, n2 = a_ref.shape[2], a_ref.shape[3]
    for j in range(kb):
        rhs = jnp.concatenate([a_ref[0, 0, j], a_ref[0, 1, j]], axis=0)
        y = jnp.dot(m2_ref[...], rhs, precision=HIGHEST, preferred_element_type=F32)
        for g in range(FNET_GROUPS):
            sl = slice(g * LANES, (g + 1) * LANES)
            lhs = jnp.concatenate([y[:n2, sl], y[n2:, sl]], axis=1)
            o_ref[0, :, j, sl] = jnp.dot(lhs, m3_ref[...], precision=HIGHEST,
                                         preferred_element_type=F32)


def _fourier_lat(f):
    B, S, W = f.shape
    n2 = LANES
    s1 = S // n2
    c1, sn1 = _dft_mats(s1)
    c2, sn2 = _dft_mats(n2)
    cc, sc = _dft_mats(FNET_GROUP_DIM)
    m1 = jnp.asarray(np.concatenate([c1, -sn1], axis=0), F32)
    m2 = jnp.asarray(np.block([[c2, sn2], [-sn2, c2]]), F32)
    norm = 1.0 / math.sqrt(S * FNET_GROUP_DIM)
    m3 = jnp.asarray(np.concatenate([cc, sc], axis=0) * norm, F32)
    ang = 2.0 * np.pi * np.outer(np.arange(s1), np.arange(n2)) / S
    tw = np.stack([np.cos(ang), -np.sin(ang)])
    tw = jnp.asarray(np.repeat(tw[:, :, :, None], LANES, axis=3).reshape(2, s1, n2 * LANES), F32)

    ns2 = 8
    a = pl.pallas_call(
        _dft1_kernel,
        out_shape=jax.ShapeDtypeStruct((B, 2, s1, n2, W), F32),
        grid=(B, n2 // ns2),
        in_specs=[pl.BlockSpec((1, s1, ns2, W), lambda b, j: (b, 0, j, 0)),
                  pl.BlockSpec(m1.shape, lambda b, j: (0, 0)),
                  pl.BlockSpec((2, s1, ns2 * LANES), lambda b, j: (0, 0, j))],
        out_specs=pl.BlockSpec((1, 2, s1, ns2, W), lambda b, j: (b, 0, 0, j, 0)),
        compiler_params=_params(("arbitrary", "arbitrary"), VMEM_LIMIT),
        name="dft1",
    )(f.reshape(B, s1, n2, W), m1, tw)
    kb = min(8, s1)
    y = pl.pallas_call(
        _dft2_kernel,
        out_shape=jax.ShapeDtypeStruct((B, n2, s1, W), F32),
        grid=(B, s1 // kb),
        in_specs=[pl.BlockSpec((1, 2, kb, n2, W), lambda b, j: (b, 0, j, 0, 0)),
                  pl.BlockSpec(m2.shape, lambda b, j: (0, 0)),
                  pl.BlockSpec(m3.shape, lambda b, j: (0, 0))],
        out_specs=pl.BlockSpec((1, n2, kb, W), lambda b, j: (b, 0, j, 0)),
        compiler_params=_params(("arbitrary", "arbitrary"), VMEM_LIMIT),
        name="dft2",
    )(a, m2, m3)
    return y.reshape(B, S, W)


def _dftc_kernel(f_ref, mc_ref, m3_ref, o_ref):
    n = f_ref.shape[1]
    a = jnp.dot(mc_ref[...], f_ref[0], precision=HIGHEST, preferred_element_type=F32)
    for g in range(FNET_GROUPS):
        sl = slice(g * LANES, (g + 1) * LANES)
        lhs = jnp.concatenate([a[:n, sl], a[n:, sl]], axis=1)
        o_ref[0, :, sl] = jnp.dot(lhs, m3_ref[...], precision=HIGHEST, preferred_element_type=F32)


def _fourier_ctx(f):
    B, L, W = f.shape
    c, s = _dft_mats(L)
    cc, sc = _dft_mats(FNET_GROUP_DIM)
    mc = jnp.asarray(np.concatenate([c, -s], axis=0), F32)
    m3 = jnp.asarray(np.concatenate([cc, sc], axis=0) / math.sqrt(L * FNET_GROUP_DIM), F32)
    return pl.pallas_call(
        _dftc_kernel,
        out_shape=jax.ShapeDtypeStruct((B, L, W), F32),
        grid=(B,),
        in_specs=[pl.BlockSpec((1, L, W), lambda b: (b, 0, 0)),
                  pl.BlockSpec(mc.shape, lambda b: (0, 0)),
                  pl.BlockSpec(m3.shape, lambda b: (0, 0))],
        out_specs=pl.BlockSpec((1, L, W), lambda b: (b, 0, 0)),
        compiler_params=_params(("arbitrary",), VMEM_LIMIT),
        name="dftc",
    )(f, mc, m3)


def _out_kernel(has_f, *refs):
    if has_f:
        (h_ref, mod_ref, yf_ref, wof_ref, o_ref, woa_ref, n2_ref, wr_ref,
         h1_ref, m_ref, aff_ref) = refs
    else:
        (h_ref, mod_ref, o_ref, woa_ref, n2_ref, wr_ref, h1_ref, m_ref, aff_ref) = refs
    D = D_MODEL
    Hk, G = o_ref.shape[1], o_ref.shape[2]
    ocat = jnp.concatenate([o_ref[0, hk, g] for hk in range(Hk) for g in range(G)], axis=1)
    y = jnp.dot(ocat, woa_ref[...], preferred_element_type=F32)
    if has_f:
        y = y + jnp.dot(yf_ref[0].astype(BF16), wof_ref[...], preferred_element_type=F32)
    mod = mod_ref[0]
    h1 = h_ref[0] + mod[:, 2 * D:3 * D] * y
    h1_ref[0] = h1
    m = _modulate(h1, n2_ref[...], mod[:, 3 * D:4 * D], mod[:, 4 * D:5 * D])
    _store_tiled(m_ref.at[0], slice(None), m)
    logit = lax.dot_general(wr_ref[...], m, NT, precision=HIGHEST,
                            preferred_element_type=F32)
    e = jnp.exp(logit - jnp.max(logit, axis=0, keepdims=True))
    aff_ref[0] = e / jnp.sum(e, axis=0, keepdims=True)


def _out_proj(h, mod, yf, wof, o, woa, n2, wr_t, tm):
    B, R, D = h.shape
    _, Hk, G, _, _ = o.shape
    full = lambda a: pl.BlockSpec(a.shape, lambda b, i: (0,) * a.ndim)
    row = lambda w_: pl.BlockSpec((1, tm, w_), lambda b, i: (b, i, 0))
    modspec = pl.BlockSpec((1, 1, N_MOD * D), lambda b, i: (b, 0, 0))
    ospec = pl.BlockSpec((1, Hk, G, tm, LANES), lambda b, i: (b, 0, 0, i, 0))
    has_f = yf is not None
    ins = [h, mod] + ([yf, wof] if has_f else []) + [o, woa, n2, wr_t]
    specs = ([row(D), modspec] + ([row(FNET_WIDTH), full(wof)] if has_f else [])
             + [ospec, full(woa), full(n2), full(wr_t)])
    return pl.pallas_call(
        functools.partial(_out_kernel, has_f),
        out_shape=(jax.ShapeDtypeStruct((B, R, D), F32),
                   jax.ShapeDtypeStruct((B, R, SUB, LANES), F32),
                   jax.ShapeDtypeStruct((B, N_EXPERTS, R), F32)),
        grid=(B, R // tm),
        in_specs=specs,
        out_specs=(row(D), pl.BlockSpec((1, tm, SUB, LANES), lambda b, i: (b, i, 0, 0)),
                   pl.BlockSpec((1, N_EXPERTS, tm), lambda b, i: (b, 0, i))),
        compiler_params=_params(("arbitrary", "arbitrary"), VMEM_LIMIT),
        name="out_proj",
    )(*ins)


def _topk_kernel(cap, aff_ref, u_ref, ones_ref, lmat_ref, lc_ref, lg_ref, cnt_ref, off_ref,
                 pos_ref, ac_ref):
    a = aff_ref[0]
    E, N = a.shape
    NC = N // LANES
    R = NC * E
    keys = pltpu.bitcast(a, I32)

    def bit_step(i, tau):
        cand = tau | jnp.left_shift(jnp.int32(1), 30 - i)
        cnt = jnp.sum((keys >= cand).astype(I32), axis=1, keepdims=True)
        return jnp.where(cnt >= cap, cand, tau)

    tau = lax.fori_loop(0, 31, bit_step, jnp.zeros((E, 1), I32))
    gt = (keys > tau).astype(F32)
    eq = (keys == tau).astype(F32)
    need = (cap - jnp.sum(gt, axis=1, keepdims=True))

    def chunked(x):
        return jnp.concatenate([x[:, c * LANES:(c + 1) * LANES] for c in range(NC)], axis=0)

    a_c, gt_c, eq_c = chunked(a), chunked(gt), chunked(eq)
    need_c = jnp.tile(need, (NC, 1))

    def prefix(x):
        xb = x.astype(BF16)
        loc = jnp.dot(xb, u_ref[...], preferred_element_type=F32)
        tot = jnp.dot(xb, ones_ref[...], preferred_element_type=F32)
        offs = jnp.dot(lmat_ref[...], tot.astype(BF16), preferred_element_type=F32)
        return loc, tot, offs

    loc, tot, offs = prefix(eq_c)
    sel = jnp.maximum(gt_c, jnp.where(loc + offs < need_c, eq_c, 0.0))
    loc, tot, offs = prefix(sel)
    pos_ref[...] = jnp.where(sel > 0.0, loc, -1.0)
    ac_ref[...] = a_c
    cnt_ref[0] = tot.astype(I32)
    off_ref[0] = offs.astype(I32)
    rb = 16
    lane = lax.broadcasted_iota(I32, (rb, LANES), 1).astype(F32)

    def compact(i, carry):
        rows = pl.ds(pl.multiple_of(i * rb, rb), rb)
        selpos = pos_ref[rows, :]
        aff = ac_ref[rows, :]
        lc = jnp.zeros((rb, LANES), F32)
        lg = jnp.zeros((rb, LANES), F32)
        for t in range(LANES):
            hit = selpos[:, t:t + 1] == lane
            lc = jnp.where(hit, float(t), lc)
            lg = jnp.where(hit, aff[:, t:t + 1], lg)
        lc_ref[0, rows, :] = lc.astype(I32)
        lg_ref[0, rows, :] = lg
        return carry

    lax.fori_loop(0, R // rb, compact, 0)


def _topk(aff, cap):
    B, E, N = aff.shape
    NC = N // LANES
    R = NC * E
    i = np.arange(LANES)
    u = jnp.asarray(i[:, None] < i[None, :], BF16)
    ones = jnp.ones((LANES, LANES), BF16)
    r = np.arange(R)
    lmat = jnp.asarray((r[:, None] % E == r[None, :] % E) & (r[None, :] // E < r[:, None] // E), BF16)
    full = lambda a: pl.BlockSpec(a.shape, lambda b: (0,) * a.ndim)
    ospec = pl.BlockSpec((1, R, LANES), lambda b: (b, 0, 0))
    sds = lambda dt: jax.ShapeDtypeStruct((B, R, LANES), dt)
    lc, lg, cnt, off = pl.pallas_call(
        functools.partial(_topk_kernel, cap),
        out_shape=(sds(I32), sds(F32), sds(I32), sds(I32)),
        grid=(B,),
        in_specs=[pl.BlockSpec((1, E, N), lambda b: (b, 0, 0)), full(u), full(ones), full(lmat)],
        out_specs=(ospec, ospec, ospec, ospec),
        scratch_shapes=[pltpu.VMEM((R, LANES), F32), pltpu.VMEM((R, LANES), F32)],
        compiler_params=_params(("arbitrary",), VMEM_LIMIT),
        name="topk",
    )(aff, u, ones, lmat)
    by_expert = lambda x: x.reshape(B, NC, E, LANES).transpose(0, 2, 1, 3)
    return (by_expert(lc), by_expert(lg), by_expert(cnt)[..., 0], by_expert(off)[..., 0])


GATHER_ROWS = 256
ROW_UNROLL = 4


def _gather_kernel(cap, cnt_ref, off_ref, lc_ref, m_hbm, o_ref, x_ref, sems):
    e = pl.program_id(0)
    B, NC = lc_ref.shape[0], lc_ref.shape[1]
    for b in range(B):
        def chunk_body(c, carry, b=b):
            base = off_ref[b, e, c] + b * cap

            def row_body(r, carry):
                t = c * LANES + lc_ref[b, c, r]
                pltpu.make_async_copy(m_hbm.at[b, t], x_ref.at[base + r], sems.at[b]).start()
                return carry

            return lax.fori_loop(0, cnt_ref[b, e, c], row_body, carry)

        lax.fori_loop(0, NC, chunk_body, 0)
    step = min(GATHER_ROWS, cap)
    for b in range(B):
        done = x_ref.at[pl.ds(b * cap, cap)]
        pltpu.make_async_copy(done, done, sems.at[b]).wait()
        for r0 in range(b * cap, (b + 1) * cap, step):
            rows = slice(r0, r0 + step)
            o_ref[0, rows, :] = _untile(x_ref, rows).astype(BF16)


def _gather(m, lc, cnt, off, cap):
    B, N, sub, _ = m.shape
    E, NC = lc.shape[1], lc.shape[2]
    return pl.pallas_call(
        functools.partial(_gather_kernel, cap),
        out_shape=jax.ShapeDtypeStruct((E, B * cap, sub * LANES), BF16),
        grid_spec=pltpu.PrefetchScalarGridSpec(
            num_scalar_prefetch=2,
            grid=(E,),
            in_specs=[pl.BlockSpec((B, None, NC, LANES), lambda e, *_: (0, e, 0, 0),
                                   memory_space=pltpu.SMEM),
                      pl.BlockSpec(memory_space=pl.ANY)],
            out_specs=pl.BlockSpec((1, B * cap, sub * LANES), lambda e, *_: (e, 0, 0)),
            scratch_shapes=[pltpu.VMEM((B * cap, sub, LANES), m.dtype),
                            pltpu.SemaphoreType.DMA((B,))]),
        compiler_params=_params(("arbitrary",), VMEM_LIMIT),
        name="gather",
    )(cnt, off, lc, m)


def _ffn_kernel(n_parts, chunk_rows, *refs):
    wg_ref, wu_ref, wd_ref = refs[:3]
    x_refs = refs[3:3 + n_parts]
    o_refs = refs[3 + n_parts:3 + 2 * n_parts]
    acc_refs = refs[3 + 2 * n_parts:]
    j = pl.program_id(1)
    last = pl.num_programs(1) - 1
    wg = wg_ref[0].astype(BF16)
    wu = wu_ref[0].astype(BF16)
    wd = wd_ref[0].astype(BF16)
    for x_ref, o_ref, acc_ref, nrows in zip(x_refs, o_refs, acc_refs, chunk_rows):
        for r0 in range(0, x_ref.shape[1], nrows):
            rows = slice(r0, r0 + nrows)
            x = x_ref[0, rows, :]
            a = jnp.dot(x, wg, preferred_element_type=F32)
            u = jnp.dot(x, wu, preferred_element_type=F32)
            hh = (a * jax.nn.sigmoid(a) * u).astype(BF16)
            y = jnp.dot(hh, wd, preferred_element_type=F32)

            @pl.when(j == 0)
            def _():
                acc_ref[rows, :] = y

            @pl.when((j > 0) & (j < last))
            def _():
                acc_ref[rows, :] += y

            @pl.when(j == last)
            def _():
                _store_tiled(o_ref.at[0], rows, acc_ref[rows, :] + y)


def _ffn(xs, layer, w_gate, w_up, w_down, chunk_rows):
    _, E, D, F = w_gate.shape
    tf = 512
    assert F // tf >= 2
    xspec = lambda x: pl.BlockSpec((1, x.shape[1], D), lambda e, j: (e, 0, 0))
    ospec = lambda x: pl.BlockSpec((1, x.shape[1], SUB, LANES), lambda e, j: (e, 0, 0, 0))
    outs = pl.pallas_call(
        functools.partial(_ffn_kernel, len(xs), chunk_rows),
        out_shape=tuple(jax.ShapeDtypeStruct((E, x.shape[1], SUB, LANES), F32) for x in xs),
        grid=(E, F // tf),
        in_specs=[pl.BlockSpec((None, 1, D, tf), lambda e, j: (layer, e, 0, j)),
                  pl.BlockSpec((None, 1, D, tf), lambda e, j: (layer, e, 0, j)),
                  pl.BlockSpec((None, 1, tf, D), lambda e, j: (layer, e, j, 0))]
        + [xspec(x) for x in xs],
        out_specs=tuple(ospec(x) for x in xs),
        scratch_shapes=[pltpu.VMEM((x.shape[1], D), F32) for x in xs],
        compiler_params=_params(("arbitrary", "arbitrary"), VMEM_LIMIT),
        name="ffn",
    )(w_gate, w_up, w_down, *xs)
    return outs


def _combine_kernel(cnt_ref, off_ref, lc_ref, lg_ref, y_ref, o_ref):
    b = pl.program_id(0)
    hf = pl.program_id(1)
    e = pl.program_id(2)
    nch = o_ref.shape[1] // LANES

    @pl.when(e == 0)
    def _():
        o_ref[...] = jnp.zeros_like(o_ref)

    def chunk_body(ci, carry):
        c = hf * nch + ci
        base = off_ref[b, e, c]

        n = cnt_ref[b, e, c]

        def rows_body(r0, width):
            ts = [ci * LANES + lc_ref[c, r0 + u] for u in range(width)]
            vals = [o_ref[0, ts[u]] + lg_ref[c, r0 + u] * y_ref[0, base + r0 + u]
                    for u in range(width)]
            for u in range(width):
                o_ref[0, ts[u]] = vals[u]

        def group_body(i, carry):
            rows_body(i * ROW_UNROLL, ROW_UNROLL)
            return carry

        def tail_body(r, carry):
            rows_body(r, 1)
            return carry

        full = n // ROW_UNROLL
        lax.fori_loop(0, full, group_body, carry)
        return lax.fori_loop(full * ROW_UNROLL, n, tail_body, carry)

    lax.fori_loop(0, nch, chunk_body, 0)


def _combine(y, lc, lg, cnt, off, n_tokens, cap, n_split):
    E = y.shape[0]
    B, _, NC, _ = lc.shape
    nh = n_tokens // n_split
    smem = lambda: pl.BlockSpec((None, None, NC, LANES), lambda b, h, e, *_: (b, e, 0, 0),
                                memory_space=pltpu.SMEM)
    return pl.pallas_call(
        _combine_kernel,
        out_shape=jax.ShapeDtypeStruct((B, n_tokens, SUB, LANES), F32),
        grid_spec=pltpu.PrefetchScalarGridSpec(
            num_scalar_prefetch=2,
            grid=(B, n_split, E),
            in_specs=[smem(), smem(),
                      pl.BlockSpec((1, cap, SUB, LANES), lambda b, h, e, *_: (e, b, 0, 0))],
            out_specs=pl.BlockSpec((1, nh, SUB, LANES), lambda b, h, e, *_: (b, h, 0, 0))),
        compiler_params=_params(("arbitrary", "arbitrary", "arbitrary"), VMEM_LIMIT),
        name="combine",
    )(cnt, off, lc, lg, y)


def _final_kernel(h_ref, moe_ref, mod_ref, g_ref, o_ref):
    D = D_MODEL
    h = h_ref[0] + mod_ref[0][:, 5 * D:6 * D] * _untile(moe_ref.at[0], slice(None))
    o_ref[0] = _rms(h, g_ref[...])


def _final(h, moe, mod, g, tm):
    B, R, D = h.shape
    row = pl.BlockSpec((1, tm, D), lambda b, i: (b, i, 0))
    return pl.pallas_call(
        _final_kernel,
        out_shape=jax.ShapeDtypeStruct((B, R, D), F32),
        grid=(B, R // tm),
        in_specs=[row, pl.BlockSpec((1, tm, SUB, LANES), lambda b, i: (b, i, 0, 0)),
                  pl.BlockSpec((1, 1, N_MOD * D), lambda b, i: (b, 0, 0)),
                  pl.BlockSpec((1, D), lambda b, i: (0, 0))],
        out_specs=row,
        compiler_params=_params(("arbitrary", "arbitrary"), VMEM_LIMIT),
        name="final",
    )(h, moe, mod, g)


def _rope_tables(n_lat, n_ctx, segments):
    t = jnp.arange(n_lat)
    pos = {"row": (t // GRID_W).astype(F32), "col": (t % GRID_W).astype(F32)}
    freq = {"row": np.zeros((2, LANES), np.float32), "col": np.zeros((2, LANES), np.float32)}
    first = np.zeros(LANES, np.float32)
    second = np.zeros(LANES, np.float32)
    for lane0, width, which in segments:
        half = width // 2
        idx = np.arange(half, dtype=np.float32)
        for lo, mask in ((lane0, first), (lane0 + half, second)):
            freq[which][0, lo:lo + half] = idx / half
            freq[which][1, lo:lo + half] = 1.0
            mask[lo:lo + half] = 1.0
    ang = jnp.zeros((n_lat, LANES), F32)
    for which in ("row", "col"):
        inv = (ROPE_THETA ** (-jnp.asarray(freq[which][0]))) * jnp.asarray(freq[which][1])
        ang = ang + pos[which][:, None] * inv[None, :]
    sin = jnp.sin(ang)
    lat = (jnp.cos(ang), -sin * first[None, :], sin * second[None, :])
    ctx = (jnp.ones((n_ctx, LANES), F32), jnp.zeros((n_ctx, LANES), F32), jnp.zeros((n_ctx, LANES), F32))
    return lat, ctx


def _head_slots(w, n_heads, width, lo, hi):
    k = w.shape[0]
    w3 = w.reshape(k, n_heads, width)[:, :, lo:hi]
    return jnp.pad(w3, ((0, 0), (0, 0), (0, LANES - (hi - lo)))).reshape(k, n_heads * LANES)


def _moe(m_l, aff_l, m_c, aff_c, layer, w_gate, w_up, w_down):
    B, S = m_l.shape[:2]
    cap_l = CAPACITY_FACTOR * S // N_EXPERTS
    lc, lg, cnt, off = _topk(aff_l, cap_l)
    xs = [_gather(m_l, lc, cnt, off, cap_l)]
    chunk_rows = [cap_l]
    if m_c is not None:
        L = m_c.shape[1]
        cap_c = CAPACITY_FACTOR * L // N_EXPERTS
        lcc, lgc, cntc, offc = _topk(aff_c, cap_c)
        xs.append(_gather(m_c, lcc, cntc, offc, cap_c))
        chunk_rows.append(B * cap_c)
    ys = _ffn(xs, layer, w_gate, w_up, w_down, tuple(chunk_rows))
    out_l = _combine(ys[0], lc, lg, cnt, off, S, cap_l, 2)
    out_c = None
    if m_c is not None:
        out_c = _combine(ys[1], lcc, lgc, cntc, offc, L, cap_c, 1)
    return out_l, out_c


def kernel(x, c, ctx, c_ctx, ada_w, ada_b, norm1, norm2, ab_w_in, ab_q_norm, ab_w_uq, ab_kv_norm,
           ab_w_ukv, ab_w_o, c_w_in, c_q_gain, c_k_gain, c_w_o, moe_router, moe_w_gate, moe_w_up,
           moe_w_down, final_norm):
    B, S, D = x.shape
    L = ctx.shape[1]
    depth = ada_w.shape[0]
    tm_l, tm_c = 512, L
    row2 = lambda v: v.reshape(1, -1)

    c8 = jnp.zeros((8, D), F32).at[:B].set(c).at[B].set(c_ctx)
    mod = _ada(c8, ada_w, ada_b)
    mod_l = [mod[i, :B][:, None, :] for i in range(depth)]
    mod_c = [jnp.broadcast_to(mod[i, B][None, None, :], (B, 1, N_MOD * D)) for i in range(depth)]

    h_l, h_c = x, ctx
    moe_l = moe_c = None
    for i in range(depth):
        last = i == depth - 1
        j = i // 2
        wr_t = moe_router[i].T
        n2 = row2(norm2[i])
        if i % 2 == 0:
            assert moe_l is None
            tabs_l, tabs_c = _rope_tables(S, L, ((MLA_NOPE, MLA_ROPE // 2, "row"),
                                                 (MLA_NOPE + MLA_ROPE // 2, MLA_ROPE // 2, "col")))
            w_ukv = ab_w_ukv[j]
            ekr = np.zeros((MLA_ROPE, MLA_HEADS * LANES), np.float32)
            for hd in range(MLA_HEADS):
                ekr[np.arange(MLA_ROPE), hd * LANES + MLA_NOPE + np.arange(MLA_ROPE)] = 1.0
            w = {
                "win": ab_w_in[j].astype(BF16),
                "qn": row2(ab_q_norm[j]),
                "wuq": _head_slots(ab_w_uq[j], MLA_HEADS, MLA_QK, 0, MLA_QK).astype(BF16),
                "kvn": row2(ab_kv_norm[j]),
                "wuk": _head_slots(w_ukv, MLA_HEADS, MLA_NOPE + MLA_V, 0, MLA_NOPE).astype(BF16),
                "wuvt": _head_slots(w_ukv, MLA_HEADS, MLA_NOPE + MLA_V, MLA_NOPE,
                                    MLA_NOPE + MLA_V).T.astype(BF16),
                "ekr": jnp.asarray(ekr, BF16),
            }
            n1 = row2(norm1[i])
            f_l, q_l, k_l, vt_l = _in_ab(h_l, mod_l[i], n1, w, tabs_l, tm_l)
            f_c, q_c, k_c, vt_c = _in_ab(h_c, mod_c[i], n1, w, tabs_c, tm_c)
            o_l = _attention(q_l[:, :, None], k_c, vt_c, k_l, vt_l, min(S, ATTN_QUERIES), 512)
            yf_l = _fourier_lat(f_l)
            wof = ab_w_o[j][:FNET_WIDTH].astype(BF16)
            woa = jnp.pad(ab_w_o[j][FNET_WIDTH:].reshape(MLA_HEADS, MLA_V, D),
                          ((0, 0), (0, LANES - MLA_V), (0, 0))).reshape(MLA_HEADS * LANES, D).astype(BF16)
            h_l, m_l, aff_l = _out_proj(h_l, mod_l[i], yf_l, wof, o_l, woa, n2, wr_t, tm_l)
            m_c = aff_c = None
            if not last:
                o_c = _attention(q_c[:, :, None], k_c, vt_c, None, None, L, L)
                yf_c = _fourier_ctx(f_c)
                h_c, m_c, aff_c = _out_proj(h_c, mod_c[i], yf_c, wof, o_c, woa, n2, wr_t, tm_c)
        else:
            tabs_l, tabs_c = _rope_tables(S, L, ((0, GQA_HEAD_DIM // 2, "row"),
                                                 (GQA_HEAD_DIM // 2, GQA_HEAD_DIM // 2, "col")))
            nqk = (GQA_HEADS + GQA_KV_HEADS) * GQA_HEAD_DIM
            w = {"wqk": c_w_in[j][:, :nqk].astype(BF16), "wvt": c_w_in[j][:, nqk:].T.astype(BF16),
                 "qg": row2(c_q_gain[j]), "kg": row2(c_k_gain[j])}
            n1 = row2(norm1[i])
            h_l, q_l, k_l, vt_l = _in_c(h_l, moe_l, mod_l[i - 1], mod_l[i], n1, w, tabs_l, tm_l)
            h_c, q_c, k_c, vt_c = _in_c(h_c, moe_c, mod_c[i - 1], mod_c[i], n1, w, tabs_c, tm_c)
            grp = lambda q: q.reshape(B, GQA_KV_HEADS, GQA_GROUP, q.shape[2], LANES)
            o_l = _attention(grp(q_l), k_c, vt_c, k_l, vt_l, min(S, ATTN_QUERIES // GQA_GROUP), 512)
            woa = c_w_o[j].astype(BF16)
            h_l, m_l, aff_l = _out_proj(h_l, mod_l[i], None, None, o_l, woa, n2, wr_t, tm_l)
            m_c = aff_c = None
            if not last:
                o_c = _attention(grp(q_c), k_c, vt_c, None, None, L, L)
                h_c, m_c, aff_c = _out_proj(h_c, mod_c[i], None, None, o_c, woa, n2, wr_t, tm_c)
        moe_l, moe_c = _moe(m_l, aff_l, m_c, aff_c, i, moe_w_gate, moe_w_up, moe_w_down)
    return _final(h_l, moe_l, mod_l[depth - 1], row2(final_norm), tm_l)
```

```python
import functools
import math

import jax
import jax.numpy as jnp
import numpy as np
from jax import lax
from jax.experimental import pallas as pl
from jax.experimental.pallas import tpu as pltpu

F32 = jnp.float32
BF16 = jnp.bfloat16
I32 = jnp.int32
HIGHEST = lax.Precision.HIGHEST

D_MODEL = 1024
GRID_W = 64
EPS = 1e-6
ROPE_THETA = 10000.0
N_MOD = 6
FNET_GROUPS = 4
FNET_GROUP_DIM = 128
FNET_WIDTH = FNET_GROUPS * FNET_GROUP_DIM
MLA_HEADS = 8
MLA_Q_LORA = 256
MLA_KV_LORA = 128
MLA_NOPE = 64
MLA_ROPE = 32
MLA_V = 64
MLA_QK = MLA_NOPE + MLA_ROPE
GQA_HEADS = 8
GQA_KV_HEADS = 2
GQA_GROUP = GQA_HEADS // GQA_KV_HEADS
GQA_HEAD_DIM = 128
N_EXPERTS = 16
EXPERT_FF = 2048
CAPACITY_FACTOR = 2

LANES = 128
SUB = D_MODEL // LANES
VMEM_LIMIT = 56 * 1024 * 1024
LOG2E = math.log2(math.e)
NT = (((1,), (1,)), ((), ()))


def _params(sem, vmem=None):
    return pltpu.CompilerParams(dimension_semantics=sem, vmem_limit_bytes=vmem)


def _rms(x, g):
    return x * lax.rsqrt(jnp.mean(x * x, axis=-1, keepdims=True) + EPS) * g


def _modulate(h, g, shift, scale):
    return _rms(h, g) * (1.0 + scale) + shift


def _rope(x, cos, sin_fwd, sin_bwd, shift):
    return (x * cos + pltpu.roll(x, LANES - shift, 1) * sin_fwd
            + pltpu.roll(x, shift, 1) * sin_bwd)


def _untile(ref, rows):
    return jnp.concatenate([ref[rows, s, :] for s in range(SUB)], axis=1)


def _store_tiled(ref, rows, x):
    for s in range(SUB):
        ref[rows, s, :] = x[:, s * LANES:(s + 1) * LANES]


def _ada_kernel(c_ref, w_ref, b_ref, o_ref):
    c = c_ref[...]
    x = c * jax.nn.sigmoid(c)
    o_ref[0] = jnp.dot(x, w_ref[0], precision=HIGHEST, preferred_element_type=F32) + b_ref[0]


def _ada(c8, ada_w, ada_b):
    depth, d, n = ada_w.shape
    tn = 1536
    return pl.pallas_call(
        _ada_kernel,
        out_shape=jax.ShapeDtypeStruct((depth, 8, n), F32),
        grid=(depth, n // tn),
        in_specs=[pl.BlockSpec((8, d), lambda l, j: (0, 0)),
                  pl.BlockSpec((1, d, tn), lambda l, j: (l, 0, j)),
                  pl.BlockSpec((1, 1, tn), lambda l, j: (l, 0, j))],
        out_specs=pl.BlockSpec((1, 8, tn), lambda l, j: (l, 0, j)),
        compiler_params=_params(("arbitrary", "arbitrary"), VMEM_LIMIT),
        name="ada",
    )(c8, ada_w, ada_b.reshape(depth, 1, n))


def _in_ab_kernel(scale, *refs):
    (h_ref, mod_ref, n1_ref, win_ref, qn_ref, wuq_ref, kvn_ref, wuk_ref, wuvt_ref, ekr_ref,
     cos_ref, sf_ref, sb_ref, f_ref, q_ref, k_ref, vt_ref) = refs
    D = D_MODEL
    h = h_ref[0]
    mod = mod_ref[0]
    a = _modulate(h, n1_ref[...], mod[:, 0:D], mod[:, D:2 * D]).astype(BF16)
    p = jnp.dot(a, win_ref[...], preferred_element_type=F32)
    o = FNET_WIDTH
    f_ref[0] = p[:, :o]
    cq = p[:, o:o + MLA_Q_LORA]
    o += MLA_Q_LORA
    ckv = p[:, o:o + MLA_KV_LORA]
    o += MLA_KV_LORA
    kr = p[:, o:o + MLA_ROPE]
    cqn = _rms(cq, qn_ref[...]).astype(BF16)
    ckvn = _rms(ckv, kvn_ref[...]).astype(BF16)
    q = jnp.dot(cqn, wuq_ref[...], preferred_element_type=F32)
    k = (jnp.dot(ckvn, wuk_ref[...], preferred_element_type=F32)
         + jnp.dot(kr.astype(BF16), ekr_ref[...], preferred_element_type=F32))
    vt = lax.dot_general(wuvt_ref[...], ckvn, NT, preferred_element_type=F32)
    cos, sf, sb = cos_ref[...], sf_ref[...], sb_ref[...]
    shift = MLA_ROPE // 4
    for hd in range(MLA_HEADS):
        sl = slice(hd * LANES, (hd + 1) * LANES)
        q_ref[0, hd] = (_rope(q[:, sl], cos, sf, sb, shift) * scale).astype(BF16)
        k_ref[0, hd] = _rope(k[:, sl], cos, sf, sb, shift).astype(BF16)
        vt_ref[0, hd] = vt[sl, :].astype(BF16)


def _in_ab(h, mod, n1, w, tables, tm):
    B, R, D = h.shape
    H = MLA_HEADS
    full = lambda a: pl.BlockSpec(a.shape, lambda b, i: (0,) * a.ndim)
    row = lambda w_: pl.BlockSpec((1, tm, w_), lambda b, i: (b, i, 0))
    tab = pl.BlockSpec((tm, LANES), lambda b, i: (i, 0))
    hd = pl.BlockSpec((1, H, tm, LANES), lambda b, i: (b, 0, i, 0))
    hdt = pl.BlockSpec((1, H, LANES, tm), lambda b, i: (b, 0, 0, i))
    weights = [n1, w["win"], w["qn"], w["wuq"], w["kvn"], w["wuk"], w["wuvt"], w["ekr"]]
    return pl.pallas_call(
        functools.partial(_in_ab_kernel, MLA_QK ** -0.5 * LOG2E),
        out_shape=(jax.ShapeDtypeStruct((B, R, FNET_WIDTH), F32),
                   jax.ShapeDtypeStruct((B, H, R, LANES), BF16),
                   jax.ShapeDtypeStruct((B, H, R, LANES), BF16),
                   jax.ShapeDtypeStruct((B, H, LANES, R), BF16)),
        grid=(B, R // tm),
        in_specs=[row(D), pl.BlockSpec((1, 1, N_MOD * D), lambda b, i: (b, 0, 0))]
        + [full(a) for a in weights] + [tab, tab, tab],
        out_specs=(row(FNET_WIDTH), hd, hd, hdt),
        compiler_params=_params(("arbitrary", "arbitrary"), VMEM_LIMIT),
        name="in_ab",
    )(h, mod, *weights, *tables)


def _in_c_kernel(scale, *refs):
    (h_ref, moe_ref, modp_ref, mod_ref, n1_ref, wqk_ref, wvt_ref, qg_ref, kg_ref,
     cos_ref, sf_ref, sb_ref, h2_ref, q_ref, k_ref, vt_ref) = refs
    D = D_MODEL
    h = h_ref[0] + modp_ref[0][:, 5 * D:6 * D] * _untile(moe_ref.at[0], slice(None))
    h2_ref[0] = h
    mod = mod_ref[0]
    a = _modulate(h, n1_ref[...], mod[:, 0:D], mod[:, D:2 * D]).astype(BF16)
    p = jnp.dot(a, wqk_ref[...], preferred_element_type=F32)
    vt = lax.dot_general(wvt_ref[...], a, NT, preferred_element_type=F32)
    cos, sf, sb = cos_ref[...], sf_ref[...], sb_ref[...]
    shift = GQA_HEAD_DIM // 4
    nq = GQA_HEADS * GQA_HEAD_DIM
    for hd in range(GQA_HEADS):
        x = _rms(p[:, hd * LANES:(hd + 1) * LANES], qg_ref[...])
        q_ref[0, hd] = (_rope(x, cos, sf, sb, shift) * scale).astype(BF16)
    for hd in range(GQA_KV_HEADS):
        x = _rms(p[:, nq + hd * LANES:nq + (hd + 1) * LANES], kg_ref[...])
        k_ref[0, hd] = _rope(x, cos, sf, sb, shift).astype(BF16)
        vt_ref[0, hd] = vt[hd * LANES:(hd + 1) * LANES, :].astype(BF16)


def _in_c(h, moe, mod_prev, mod, n1, w, tables, tm):
    B, R, D = h.shape
    full = lambda a: pl.BlockSpec(a.shape, lambda b, i: (0,) * a.ndim)
    row = lambda w_: pl.BlockSpec((1, tm, w_), lambda b, i: (b, i, 0))
    tiled = pl.BlockSpec((1, tm, SUB, LANES), lambda b, i: (b, i, 0, 0))
    modspec = pl.BlockSpec((1, 1, N_MOD * D), lambda b, i: (b, 0, 0))
    tab = pl.BlockSpec((tm, LANES), lambda b, i: (i, 0))
    hd = lambda n: pl.BlockSpec((1, n, tm, LANES), lambda b, i: (b, 0, i, 0))
    hdt = pl.BlockSpec((1, GQA_KV_HEADS, LANES, tm), lambda b, i: (b, 0, 0, i))
    weights = [n1, w["wqk"], w["wvt"], w["qg"], w["kg"]]
    return pl.pallas_call(
        functools.partial(_in_c_kernel, GQA_HEAD_DIM ** -0.5 * LOG2E),
        out_shape=(jax.ShapeDtypeStruct((B, R, D), F32),
                   jax.ShapeDtypeStruct((B, GQA_HEADS, R, LANES), BF16),
                   jax.ShapeDtypeStruct((B, GQA_KV_HEADS, R, LANES), BF16),
                   jax.ShapeDtypeStruct((B, GQA_KV_HEADS, LANES, R), BF16)),
        grid=(B, R // tm),
        in_specs=[row(D), tiled, modspec, modspec] + [full(a) for a in weights] + [tab, tab, tab],
        out_specs=(row(D), hd(GQA_HEADS), hd(GQA_KV_HEADS), hdt),
        compiler_params=_params(("arbitrary", "arbitrary"), VMEM_LIMIT),
        name="in_c",
    )(h, moe, mod_prev, mod, *weights, *tables)


ATTN_QUERIES = 2048
ATTN_KEYS = 512


def _attn_kernel(n_lat, tk, *refs):
    if n_lat:
        q_ref, kc_ref, vct_ref, kl_ref, vlt_ref, o_ref, acc_ref, s0, s1, p0, p1 = refs
        s_bufs, p_bufs = (s0, s1), (p0, p1)
    else:
        q_ref, kc_ref, vct_ref, o_ref, acc_ref = refs
    G, tq = q_ref.shape[2], q_ref.shape[3]
    q = q_ref[0, 0].reshape(G * tq, LANES)

    def scores(k):
        return lax.dot_general(k, q, NT, preferred_element_type=F32)

    s = scores(kc_ref[0, 0])
    m = jnp.max(s, axis=0, keepdims=True)
    p = jnp.exp2(s - m)
    l = jnp.sum(p, axis=0, keepdims=True)
    acc_ref[...] = jnp.dot(vct_ref[0, 0], p.astype(BF16), preferred_element_type=F32)

    if n_lat:
        assert n_lat == 1 or n_lat % 2 == 0

        def chunk(c):
            return pl.ds(pl.multiple_of(c * tk, tk), tk)

        def score_stage(c, slot):
            s_bufs[slot][...] = scores(kl_ref[0, 0, chunk(c), :])

        def softmax_stage(slot, m, l):
            s = s_bufs[slot][...]
            m_new = jnp.maximum(m, jnp.max(s, axis=0, keepdims=True))
            p = jnp.exp2(s - m_new)
            p_bufs[slot][...] = p.astype(BF16)
            alpha = jnp.exp2(m - m_new)
            return m_new, alpha * l + jnp.sum(p, axis=0, keepdims=True), alpha

        def value_stage(c, slot, alpha):
            pv = jnp.dot(vlt_ref[0, 0, :, chunk(c)], p_bufs[slot][...], preferred_element_type=F32)
            acc_ref[...] = alpha * acc_ref[...] + pv

        score_stage(0, 0)
        m, l, alpha = softmax_stage(0, m, l)
        if n_lat > 1:
            score_stage(1, 1)

            def body(i, carry):
                m, l, alpha = carry
                for slot in (0, 1):
                    c = 2 * i + slot
                    score_stage(c + 2, slot)
                    value_stage(c, slot, alpha)
                    m, l, alpha = softmax_stage(1 - slot, m, l)
                return m, l, alpha

            m, l, alpha = lax.fori_loop(0, (n_lat - 2) // 2, body, (m, l, alpha))
            value_stage(n_lat - 2, 0, alpha)
            m, l, alpha = softmax_stage(1, m, l)
        value_stage(n_lat - 1, (n_lat - 1) % 2, alpha)
    o_ref[0, 0] = (acc_ref[...] / l).T.reshape(G, tq, LANES).astype(BF16)


def _attention(q, kc, vct, kl, vlt, tq, tk):
    B, Hk, G, R, _ = q.shape
    Lc = kc.shape[2]
    n_lat = 0 if kl is None else kl.shape[2] // tk
    qspec = pl.BlockSpec((1, 1, G, tq, LANES), lambda b, h, i: (b, h, 0, i, 0))
    kspec = lambda n: pl.BlockSpec((1, 1, n, LANES), lambda b, h, i: (b, h, 0, 0))
    vspec = lambda n: pl.BlockSpec((1, 1, LANES, n), lambda b, h, i: (b, h, 0, 0))
    ins = [q, kc, vct] + ([kl, vlt] if n_lat else [])
    specs = [qspec, kspec(Lc), vspec(Lc)] + ([kspec(kl.shape[2]), vspec(kl.shape[2])] if n_lat else [])
    return pl.pallas_call(
        functools.partial(_attn_kernel, n_lat, tk),
        out_shape=jax.ShapeDtypeStruct(q.shape, BF16),
        grid=(B, Hk, R // tq),
        in_specs=specs,
        out_specs=qspec,
        scratch_shapes=[pltpu.VMEM((LANES, G * tq), F32)]
        + ([pltpu.VMEM((tk, G * tq), F32)] * 2 + [pltpu.VMEM((tk, G * tq), BF16)] * 2 if n_lat else []),
        compiler_params=_params(("arbitrary", "arbitrary", "arbitrary"), VMEM_LIMIT),
        name="attn",
    )(*ins)


def _dft_mats(n):
    k = np.arange(n, dtype=np.float64)
    ang = 2.0 * np.pi * np.outer(k, k) / n
    return np.cos(ang), np.sin(ang)


def _split(x):
    hi = x.astype(BF16)
    return hi, (x - hi.astype(F32)).astype(BF16)


def _dot3(a, b):
    d = lambda p, q: jnp.dot(p, q, preferred_element_type=F32)
    return d(a[0], b[0]) + d(a[0], b[1]) + d(a[1], b[0])


def _split_const(m):
    return jnp.stack(_split(jnp.asarray(m, F32)))


def _dft1_kernel(x_ref, m1_ref, tw_ref, o_ref):
    s1, ns2 = x_ref.shape[1], x_ref.shape[2]
    for i in range(ns2):
        a = jnp.dot(m1_ref[...], x_ref[0, :, i, :], precision=HIGHEST, preferred_element_type=F32)
        are, aim = a[:s1], a[s1:]
        tre = jnp.tile(tw_ref[0, :, i * LANES:(i + 1) * LANES], (1, FNET_GROUPS))
        tim = jnp.tile(tw_ref[1, :, i * LANES:(i + 1) * LANES], (1, FNET_GROUPS))
        o_ref[0, 0, :, i, :] = are * tre - aim * tim
        o_ref[0, 1, :, i, :] = are * tim + aim * tre


def _dft2_kernel(a_ref, m2_ref, m3_ref, o_ref):
    kb, n2 = a_ref.shape[2], a_ref.shape[3]
    for j in range(kb):
        rhs = jnp.concatenate([a_ref[0, 0, j], a_ref[0, 1, j]], axis=0)
        y = _dot3((m2_ref[0], m2_ref[1]), _split(rhs))
        for g in range(FNET_GROUPS):
            sl = slice(g * LANES, (g + 1) * LANES)
            lhs = jnp.concatenate([y[:n2, sl], y[n2:, sl]], axis=1)
            o_ref[0, :, j, sl] = _dot3(_split(lhs), (m3_ref[0], m3_ref[1]))


def _fourier_lat(f):
    B, S, W = f.shape
    n2 = LANES
    s1 = S // n2
    c1, sn1 = _dft_mats(s1)
    c2, sn2 = _dft_mats(n2)
    cc, sc = _dft_mats(FNET_GROUP_DIM)
    m1 = jnp.asarray(np.concatenate([c1, -sn1], axis=0), F32)
    m2 = _split_const(np.block([[c2, sn2], [-sn2, c2]]))
    norm = 1.0 / math.sqrt(S * FNET_GROUP_DIM)
    m3 = _split_const(np.concatenate([cc, sc], axis=0) * norm)
    ang = 2.0 * np.pi * np.outer(np.arange(s1), np.arange(n2)) / S
    tw = np.stack([np.cos(ang), -np.sin(ang)])
    tw = jnp.asarray(np.repeat(tw[:, :, :, None], LANES, axis=3).reshape(2, s1, n2 * LANES), F32)

    ns2 = 8
    a = pl.pallas_call(
        _dft1_kernel,
        out_shape=jax.ShapeDtypeStruct((B, 2, s1, n2, W), F32),
        grid=(B, n2 // ns2),
        in_specs=[pl.BlockSpec((1, s1, ns2, W), lambda b, j: (b, 0, j, 0)),
                  pl.BlockSpec(m1.shape, lambda b, j: (0, 0)),
                  pl.BlockSpec((2, s1, ns2 * LANES), lambda b, j: (0, 0, j))],
        out_specs=pl.BlockSpec((1, 2, s1, ns2, W), lambda b, j: (b, 0, 0, j, 0)),
        compiler_params=_params(("arbitrary", "arbitrary"), VMEM_LIMIT),
        name="dft1",
    )(f.reshape(B, s1, n2, W), m1, tw)
    kb = min(8, s1)
    y = pl.pallas_call(
        _dft2_kernel,
        out_shape=jax.ShapeDtypeStruct((B, n2, s1, W), F32),
        grid=(B, s1 // kb),
        in_specs=[pl.BlockSpec((1, 2, kb, n2, W), lambda b, j: (b, 0, j, 0, 0)),
                  pl.BlockSpec(m2.shape, lambda b, j: (0, 0, 0)),
                  pl.BlockSpec(m3.shape, lambda b, j: (0, 0, 0))],
        out_specs=pl.BlockSpec((1, n2, kb, W), lambda b, j: (b, 0, j, 0)),
        compiler_params=_params(("arbitrary", "arbitrary"), VMEM_LIMIT),
        name="dft2",
    )(a, m2, m3)
    return y.reshape(B, S, W)


def _dftc_kernel(f_ref, mc_ref, m3_ref, o_ref):
    n = f_ref.shape[1]
    a = _dot3((mc_ref[0], mc_ref[1]), _split(f_ref[0]))
    for g in range(FNET_GROUPS):
        sl = slice(g * LANES, (g + 1) * LANES)
        lhs = jnp.concatenate([a[:n, sl], a[n:, sl]], axis=1)
        o_ref[0, :, sl] = _dot3(_split(lhs), (m3_ref[0], m3_ref[1]))


def _fourier_ctx(f):
    B, L, W = f.shape
    c, s = _dft_mats(L)
    cc, sc = _dft_mats(FNET_GROUP_DIM)
    mc = _split_const(np.concatenate([c, -s], axis=0))
    m3 = _split_const(np.concatenate([cc, sc], axis=0) / math.sqrt(L * FNET_GROUP_DIM))
    return pl.pallas_call(
        _dftc_kernel,
        out_shape=jax.ShapeDtypeStruct((B, L, W), F32),
        grid=(B,),
        in_specs=[pl.BlockSpec((1, L, W), lambda b: (b, 0, 0)),
                  pl.BlockSpec(mc.shape, lambda b: (0, 0, 0)),
                  pl.BlockSpec(m3.shape, lambda b: (0, 0, 0))],
        out_specs=pl.BlockSpec((1, L, W), lambda b: (b, 0, 0)),
        compiler_params=_params(("arbitrary",), VMEM_LIMIT),
        name="dftc",
    )(f, mc, m3)


def _out_kernel(has_f, *refs):
    if has_f:
        (h_ref, mod_ref, yf_ref, wof_ref, o_ref, woa_ref, n2_ref, wr_ref,
         h1_ref, m_ref, aff_ref) = refs
    else:
        (h_ref, mod_ref, o_ref, woa_ref, n2_ref, wr_ref, h1_ref, m_ref, aff_ref) = refs
    D = D_MODEL
    Hk, G = o_ref.shape[1], o_ref.shape[2]
    ocat = jnp.concatenate([o_ref[0, hk, g] for hk in range(Hk) for g in range(G)], axis=1)
    y = jnp.dot(ocat, woa_ref[...], preferred_element_type=F32)
    if has_f:
        y = y + jnp.dot(yf_ref[0].astype(BF16), wof_ref[...], preferred_element_type=F32)
    mod = mod_ref[0]
    h1 = h_ref[0] + mod[:, 2 * D:3 * D] * y
    h1_ref[0] = h1
    m = _modulate(h1, n2_ref[...], mod[:, 3 * D:4 * D], mod[:, 4 * D:5 * D])
    _store_tiled(m_ref.at[0], slice(None), m)
    logit = lax.dot_general(wr_ref[...], m, NT, precision=HIGHEST,
                            preferred_element_type=F32)
    e = jnp.exp(logit - jnp.max(logit, axis=0, keepdims=True))
    aff_ref[0] = e / jnp.sum(e, axis=0, keepdims=True)


def _out_proj(h, mod, yf, wof, o, woa, n2, wr_t, tm):
    B, R, D = h.shape
    _, Hk, G, _, _ = o.shape
    full = lambda a: pl.BlockSpec(a.shape, lambda b, i: (0,) * a.ndim)
    row = lambda w_: pl.BlockSpec((1, tm, w_), lambda b, i: (b, i, 0))
    modspec = pl.BlockSpec((1, 1, N_MOD * D), lambda b, i: (b, 0, 0))
    ospec = pl.BlockSpec((1, Hk, G, tm, LANES), lambda b, i: (b, 0, 0, i, 0))
    has_f = yf is not None
    ins = [h, mod] + ([yf, wof] if has_f else []) + [o, woa, n2, wr_t]
    specs = ([row(D), modspec] + ([row(FNET_WIDTH), full(wof)] if has_f else [])
             + [ospec, full(woa), full(n2), full(wr_t)])
    return pl.pallas_call(
        functools.partial(_out_kernel, has_f),
        out_shape=(jax.ShapeDtypeStruct((B, R, D), F32),
                   jax.ShapeDtypeStruct((B, R, SUB, LANES), F32),
                   jax.ShapeDtypeStruct((B, N_EXPERTS, R), F32)),
        grid=(B, R // tm),
        in_specs=specs,
        out_specs=(row(D), pl.BlockSpec((1, tm, SUB, LANES), lambda b, i: (b, i, 0, 0)),
                   pl.BlockSpec((1, N_EXPERTS, tm), lambda b, i: (b, 0, i))),
        compiler_params=_params(("arbitrary", "arbitrary"), VMEM_LIMIT),
        name="out_proj",
    )(*ins)


def _topk_kernel(cap, aff_ref, u_ref, ones_ref, lmat_ref, lc_ref, lg_ref, cnt_ref, off_ref,
                 pos_ref, ac_ref):
    a = aff_ref[0]
    E, N = a.shape
    NC = N // LANES
    R = NC * E
    keys = pltpu.bitcast(a, I32)

    def bit_step(i, tau):
        cand = tau | jnp.left_shift(jnp.int32(1), 30 - i)
        cnt = jnp.sum((keys >= cand).astype(I32), axis=1, keepdims=True)
        return jnp.where(cnt >= cap, cand, tau)

    tau = lax.fori_loop(0, 31, bit_step, jnp.zeros((E, 1), I32))
    gt = (keys > tau).astype(F32)
    eq = (keys == tau).astype(F32)
    need = (cap - jnp.sum(gt, axis=1, keepdims=True))

    def chunked(x):
        return jnp.concatenate([x[:, c * LANES:(c + 1) * LANES] for c in range(NC)], axis=0)

    a_c, gt_c, eq_c = chunked(a), chunked(gt), chunked(eq)
    need_c = jnp.tile(need, (NC, 1))

    def prefix(x):
        xb = x.astype(BF16)
        loc = jnp.dot(xb, u_ref[...], preferred_element_type=F32)
        tot = jnp.dot(xb, ones_ref[...], preferred_element_type=F32)
        offs = jnp.dot(lmat_ref[...], tot.astype(BF16), preferred_element_type=F32)
        return loc, tot, offs

    loc, tot, offs = prefix(eq_c)
    sel = jnp.maximum(gt_c, jnp.where(loc + offs < need_c, eq_c, 0.0))
    loc, tot, offs = prefix(sel)
    pos_ref[...] = jnp.where(sel > 0.0, loc, -1.0)
    ac_ref[...] = a_c
    cnt_ref[0] = tot.astype(I32)
    off_ref[0] = offs.astype(I32)
    rb = 16
    lane = lax.broadcasted_iota(I32, (rb, LANES), 1).astype(F32)

    def compact(i, carry):
        rows = pl.ds(pl.multiple_of(i * rb, rb), rb)
        selpos = pos_ref[rows, :]
        aff = ac_ref[rows, :]
        lc = jnp.zeros((rb, LANES), F32)
        lg = jnp.zeros((rb, LANES), F32)
        for t in range(LANES):
            hit = selpos[:, t:t + 1] == lane
            lc = jnp.where(hit, float(t), lc)
            lg = jnp.where(hit, aff[:, t:t + 1], lg)
        lc_ref[0, rows, :] = lc.astype(I32)
        lg_ref[0, rows, :] = lg
        return carry

    lax.fori_loop(0, R // rb, compact, 0)


def _topk(aff, cap):
    B, E, N = aff.shape
    NC = N // LANES
    R = NC * E
    i = np.arange(LANES)
    u = jnp.asarray(i[:, None] < i[None, :], BF16)
    ones = jnp.ones((LANES, LANES), BF16)
    r = np.arange(R)
    lmat = jnp.asarray((r[:, None] % E == r[None, :] % E) & (r[None, :] // E < r[:, None] // E), BF16)
    full = lambda a: pl.BlockSpec(a.shape, lambda b: (0,) * a.ndim)
    ospec = pl.BlockSpec((1, R, LANES), lambda b: (b, 0, 0))
    sds = lambda dt: jax.ShapeDtypeStruct((B, R, LANES), dt)
    lc, lg, cnt, off = pl.pallas_call(
        functools.partial(_topk_kernel, cap),
        out_shape=(sds(I32), sds(F32), sds(I32), sds(I32)),
        grid=(B,),
        in_specs=[pl.BlockSpec((1, E, N), lambda b: (b, 0, 0)), full(u), full(ones), full(lmat)],
        out_specs=(ospec, ospec, ospec, ospec),
        scratch_shapes=[pltpu.VMEM((R, LANES), F32), pltpu.VMEM((R, LANES), F32)],
        compiler_params=_params(("arbitrary",), VMEM_LIMIT),
        name="topk",
    )(aff, u, ones, lmat)
    by_expert = lambda x: x.reshape(B, NC, E, LANES).transpose(0, 2, 1, 3)
    return (by_expert(lc), by_expert(lg), by_expert(cnt)[..., 0], by_expert(off)[..., 0])


GATHER_ROWS = 256
ROW_UNROLL = 4


def _gather_kernel(cap, cnt_ref, off_ref, lc_ref, m_hbm, o_ref, x_ref, sems):
    e = pl.program_id(0)
    B, NC = lc_ref.shape[0], lc_ref.shape[1]
    for b in range(B):
        def chunk_body(c, carry, b=b):
            base = off_ref[b, e, c] + b * cap

            n = cnt_ref[b, e, c]

            def start_row(r, priority):
                t = c * LANES + lc_ref[b, c, r]
                pltpu.make_async_copy(m_hbm.at[b, t], x_ref.at[base + r],
                                      sems.at[b]).start(priority=priority)

            def pair_body(i, carry):
                start_row(2 * i, 0)
                start_row(2 * i + 1, 1)
                return carry

            lax.fori_loop(0, n // 2, pair_body, carry)

            @pl.when(n % 2 == 1)
            def _():
                start_row(n - 1, 0)

            return carry

        lax.fori_loop(0, NC, chunk_body, 0)
    step = min(GATHER_ROWS, cap)
    for b in range(B):
        done = x_ref.at[pl.ds(b * cap, cap)]
        pltpu.make_async_copy(done, done, sems.at[b]).wait()
        for r0 in range(b * cap, (b + 1) * cap, step):
            rows = slice(r0, r0 + step)
            o_ref[0, rows, :] = _untile(x_ref, rows).astype(BF16)


def _gather(m, lc, cnt, off, cap):
    B, N, sub, _ = m.shape
    E, NC = lc.shape[1], lc.shape[2]
    return pl.pallas_call(
        functools.partial(_gather_kernel, cap),
        out_shape=jax.ShapeDtypeStruct((E, B * cap, sub * LANES), BF16),
        grid_spec=pltpu.PrefetchScalarGridSpec(
            num_scalar_prefetch=2,
            grid=(E,),
            in_specs=[pl.BlockSpec((B, None, NC, LANES), lambda e, *_: (0, e, 0, 0),
                                   memory_space=pltpu.SMEM),
                      pl.BlockSpec(memory_space=pl.ANY)],
            out_specs=pl.BlockSpec((1, B * cap, sub * LANES), lambda e, *_: (e, 0, 0)),
            scratch_shapes=[pltpu.VMEM((B * cap, sub, LANES), m.dtype),
                            pltpu.SemaphoreType.DMA((B,))]),
        compiler_params=_params(("arbitrary",), VMEM_LIMIT),
        name="gather",
    )(cnt, off, lc, m)


def _ffn_kernel(n_parts, chunk_rows, *refs):
    wg_ref, wu_ref, wd_ref = refs[:3]
    x_refs = refs[3:3 + n_parts]
    o_refs = refs[3 + n_parts:3 + 2 * n_parts]
    acc_refs = refs[3 + 2 * n_parts:]
    j = pl.program_id(1)
    last = pl.num_programs(1) - 1
    wg = wg_ref[0].astype(BF16)
    wu = wu_ref[0].astype(BF16)
    wd = wd_ref[0].astype(BF16)
    for x_ref, o_ref, acc_ref, nrows in zip(x_refs, o_refs, acc_refs, chunk_rows):
        for r0 in range(0, x_ref.shape[1], nrows):
            rows = slice(r0, r0 + nrows)
            x = x_ref[0, rows, :]
            a = jnp.dot(x, wg, preferred_element_type=F32)
            u = jnp.dot(x, wu, preferred_element_type=F32)
            hh = (a * jax.nn.sigmoid(a) * u).astype(BF16)
            y = jnp.dot(hh, wd, preferred_element_type=F32)

            @pl.when(j == 0)
            def _():
                acc_ref[rows, :] = y

            @pl.when((j > 0) & (j < last))
            def _():
                acc_ref[rows, :] += y

            @pl.when(j == last)
            def _():
                _store_tiled(o_ref.at[0], rows, acc_ref[rows, :] + y)


def _ffn(xs, layer, w_gate, w_up, w_down, chunk_rows):
    _, E, D, F = w_gate.shape
    tf = 512
    assert F // tf >= 2
    xspec = lambda x: pl.BlockSpec((1, x.shape[1], D), lambda e, j: (e, 0, 0))
    ospec = lambda x: pl.BlockSpec((1, x.shape[1], SUB, LANES), lambda e, j: (e, 0, 0, 0))
    outs = pl.pallas_call(
        functools.partial(_ffn_kernel, len(xs), chunk_rows),
        out_shape=tuple(jax.ShapeDtypeStruct((E, x.shape[1], SUB, LANES), F32) for x in xs),
        grid=(E, F // tf),
        in_specs=[pl.BlockSpec((None, 1, D, tf), lambda e, j: (layer, e, 0, j)),
                  pl.BlockSpec((None, 1, D, tf), lambda e, j: (layer, e, 0, j)),
                  pl.BlockSpec((None, 1, tf, D), lambda e, j: (layer, e, j, 0))]
        + [xspec(x) for x in xs],
        out_specs=tuple(ospec(x) for x in xs),
        scratch_shapes=[pltpu.VMEM((x.shape[1], D), F32) for x in xs],
        compiler_params=_params(("arbitrary", "arbitrary"), VMEM_LIMIT),
        name="ffn",
    )(w_gate, w_up, w_down, *xs)
    return outs


def _combine_kernel(cnt_ref, off_ref, lc_ref, lg_ref, y_ref, o_ref):
    b = pl.program_id(0)
    hf = pl.program_id(1)
    e = pl.program_id(2)
    nch = o_ref.shape[1] // LANES

    @pl.when(e == 0)
    def _():
        o_ref[...] = jnp.zeros_like(o_ref)

    def chunk_body(ci, carry):
        c = hf * nch + ci
        base = off_ref[b, e, c]

        n = cnt_ref[b, e, c]

        def rows_body(r0, width):
            ts = [ci * LANES + lc_ref[c, r0 + u] for u in range(width)]
            vals = [o_ref[0, ts[u]] + lg_ref[c, r0 + u] * y_ref[0, base + r0 + u]
                    for u in range(width)]
            for u in range(width):
                o_ref[0, ts[u]] = vals[u]

        def group_body(i, carry):
            rows_body(i * ROW_UNROLL, ROW_UNROLL)
            return carry

        def tail_body(r, carry):
            rows_body(r, 1)
            return carry

        full = n // ROW_UNROLL
        lax.fori_loop(0, full, group_body, carry)
        return lax.fori_loop(full * ROW_UNROLL, n, tail_body, carry)

    lax.fori_loop(0, nch, chunk_body, 0)


def _combine(y, lc, lg, cnt, off, n_tokens, cap, n_split):
    E = y.shape[0]
    B, _, NC, _ = lc.shape
    nh = n_tokens // n_split
    smem = lambda: pl.BlockSpec((None, None, NC, LANES), lambda b, h, e, *_: (b, e, 0, 0),
                                memory_space=pltpu.SMEM)
    return pl.pallas_call(
        _combine_kernel,
        out_shape=jax.ShapeDtypeStruct((B, n_tokens, SUB, LANES), F32),
        grid_spec=pltpu.PrefetchScalarGridSpec(
            num_scalar_prefetch=2,
            grid=(B, n_split, E),
            in_specs=[smem(), smem(),
                      pl.BlockSpec((1, cap, SUB, LANES), lambda b, h, e, *_: (e, b, 0, 0))],
            out_specs=pl.BlockSpec((1, nh, SUB, LANES), lambda b, h, e, *_: (b, h, 0, 0))),
        compiler_params=_params(("arbitrary", "arbitrary", "arbitrary"), VMEM_LIMIT),
        name="combine",
    )(cnt, off, lc, lg, y)


def _final_kernel(h_ref, moe_ref, mod_ref, g_ref, o_ref):
    D = D_MODEL
    h = h_ref[0] + mod_ref[0][:, 5 * D:6 * D] * _untile(moe_ref.at[0], slice(None))
    o_ref[0] = _rms(h, g_ref[...])


def _final(h, moe, mod, g, tm):
    B, R, D = h.shape
    row = pl.BlockSpec((1, tm, D), lambda b, i: (b, i, 0))
    return pl.pallas_call(
        _final_kernel,
        out_shape=jax.ShapeDtypeStruct((B, R, D), F32),
        grid=(B, R // tm),
        in_specs=[row, pl.BlockSpec((1, tm, SUB, LANES), lambda b, i: (b, i, 0, 0)),
                  pl.BlockSpec((1, 1, N_MOD * D), lambda b, i: (b, 0, 0)),
                  pl.BlockSpec((1, D), lambda b, i: (0, 0))],
        out_specs=row,
        compiler_params=_params(("arbitrary", "arbitrary"), VMEM_LIMIT),
        name="final",
    )(h, moe, mod, g)


def _rope_tables(n_lat, n_ctx, segments):
    t = jnp.arange(n_lat)
    pos = {"row": (t // GRID_W).astype(F32), "col": (t % GRID_W).astype(F32)}
    freq = {"row": np.zeros((2, LANES), np.float32), "col": np.zeros((2, LANES), np.float32)}
    first = np.zeros(LANES, np.float32)
    second = np.zeros(LANES, np.float32)
    for lane0, width, which in segments:
        half = width // 2
        idx = np.arange(half, dtype=np.float32)
        for lo, mask in ((lane0, first), (lane0 + half, second)):
            freq[which][0, lo:lo + half] = idx / half
            freq[which][1, lo:lo + half] = 1.0
            mask[lo:lo + half] = 1.0
    ang = jnp.zeros((n_lat, LANES), F32)
    for which in ("row", "col"):
        inv = (ROPE_THETA ** (-jnp.asarray(freq[which][0]))) * jnp.asarray(freq[which][1])
        ang = ang + pos[which][:, None] * inv[None, :]
    sin = jnp.sin(ang)
    lat = (jnp.cos(ang), -sin * first[None, :], sin * second[None, :])
    ctx = (jnp.ones((n_ctx, LANES), F32), jnp.zeros((n_ctx, LANES), F32), jnp.zeros((n_ctx, LANES), F32))
    return lat, ctx


def _head_slots(w, n_heads, width, lo, hi):
    k = w.shape[0]
    w3 = w.reshape(k, n_heads, width)[:, :, lo:hi]
    return jnp.pad(w3, ((0, 0), (0, 0), (0, LANES - (hi - lo)))).reshape(k, n_heads * LANES)


def _moe(m_l, aff_l, m_c, aff_c, layer, w_gate, w_up, w_down):
    B, S = m_l.shape[:2]
    cap_l = CAPACITY_FACTOR * S // N_EXPERTS
    lc, lg, cnt, off = _topk(aff_l, cap_l)
    xs = [_gather(m_l, lc, cnt, off, cap_l)]
    chunk_rows = [cap_l]
    if m_c is not None:
        L = m_c.shape[1]
        cap_c = CAPACITY_FACTOR * L // N_EXPERTS
        lcc, lgc, cntc, offc = _topk(aff_c, cap_c)
        xs.append(_gather(m_c, lcc, cntc, offc, cap_c))
        chunk_rows.append(B * cap_c)
    ys = _ffn(xs, layer, w_gate, w_up, w_down, tuple(chunk_rows))
    out_l = _combine(ys[0], lc, lg, cnt, off, S, cap_l, 2)
    out_c = None
    if m_c is not None:
        out_c = _combine(ys[1], lcc, lgc, cntc, offc, L, cap_c, 1)
    return out_l, out_c


def kernel(x, c, ctx, c_ctx, ada_w, ada_b, norm1, norm2, ab_w_in, ab_q_norm, ab_w_uq, ab_kv_norm,
           ab_w_ukv, ab_w_o, c_w_in, c_q_gain, c_k_gain, c_w_o, moe_router, moe_w_gate, moe_w_up,
           moe_w_down, final_norm):
    B, S, D = x.shape
    L = ctx.shape[1]
    depth = ada_w.shape[0]
    tm_l, tm_c = 512, L
    row2 = lambda v: v.reshape(1, -1)

    c8 = jnp.zeros((8, D), F32).at[:B].set(c).at[B].set(c_ctx)
    mod = _ada(c8, ada_w, ada_b)
    mod_l = [mod[i, :B][:, None, :] for i in range(depth)]
    mod_c = [jnp.broadcast_to(mod[i, B][None, None, :], (B, 1, N_MOD * D)) for i in range(depth)]

    h_l, h_c = x, ctx
    moe_l = moe_c = None
    for i in range(depth):
        last = i == depth - 1
        j = i // 2
        wr_t = moe_router[i].T
        n2 = row2(norm2[i])
        if i % 2 == 0:
            assert moe_l is None
            tabs_l, tabs_c = _rope_tables(S, L, ((MLA_NOPE, MLA_ROPE // 2, "row"),
                                                 (MLA_NOPE + MLA_ROPE // 2, MLA_ROPE // 2, "col")))
            w_ukv = ab_w_ukv[j]
            ekr = np.zeros((MLA_ROPE, MLA_HEADS * LANES), np.float32)
            for hd in range(MLA_HEADS):
                ekr[np.arange(MLA_ROPE), hd * LANES + MLA_NOPE + np.arange(MLA_ROPE)] = 1.0
            w = {
                "win": ab_w_in[j].astype(BF16),
                "qn": row2(ab_q_norm[j]),
                "wuq": _head_slots(ab_w_uq[j], MLA_HEADS, MLA_QK, 0, MLA_QK).astype(BF16),
                "kvn": row2(ab_kv_norm[j]),
                "wuk": _head_slots(w_ukv, MLA_HEADS, MLA_NOPE + MLA_V, 0, MLA_NOPE).astype(BF16),
                "wuvt": _head_slots(w_ukv, MLA_HEADS, MLA_NOPE + MLA_V, MLA_NOPE,
                                    MLA_NOPE + MLA_V).T.astype(BF16),
                "ekr": jnp.asarray(ekr, BF16),
            }
            n1 = row2(norm1[i])
            f_l, q_l, k_l, vt_l = _in_ab(h_l, mod_l[i], n1, w, tabs_l, tm_l)
            f_c, q_c, k_c, vt_c = _in_ab(h_c, mod_c[i], n1, w, tabs_c, tm_c)
            o_l = _attention(q_l[:, :, None], k_c, vt_c, k_l, vt_l, min(S, ATTN_QUERIES), ATTN_KEYS)
            yf_l = _fourier_lat(f_l)
            wof = ab_w_o[j][:FNET_WIDTH].astype(BF16)
            woa = jnp.pad(ab_w_o[j][FNET_WIDTH:].reshape(MLA_HEADS, MLA_V, D),
                          ((0, 0), (0, LANES - MLA_V), (0, 0))).reshape(MLA_HEADS * LANES, D).astype(BF16)
            h_l, m_l, aff_l = _out_proj(h_l, mod_l[i], yf_l, wof, o_l, woa, n2, wr_t, tm_l)
            m_c = aff_c = None
            if not last:
                o_c = _attention(q_c[:, :, None], k_c, vt_c, None, None, L, L)
                yf_c = _fourier_ctx(f_c)
                h_c, m_c, aff_c = _out_proj(h_c, mod_c[i], yf_c, wof, o_c, woa, n2, wr_t, tm_c)
        else:
            tabs_l, tabs_c = _rope_tables(S, L, ((0, GQA_HEAD_DIM // 2, "row"),
                                                 (GQA_HEAD_DIM // 2, GQA_HEAD_DIM // 2, "col")))
            nqk = (GQA_HEADS + GQA_KV_HEADS) * GQA_HEAD_DIM
            w = {"wqk": c_w_in[j][:, :nqk].astype(BF16), "wvt": c_w_in[j][:, nqk:].T.astype(BF16),
                 "qg": row2(c_q_gain[j]), "kg": row2(c_k_gain[j])}
            n1 = row2(norm1[i])
            h_l, q_l, k_l, vt_l = _in_c(h_l, moe_l, mod_l[i - 1], mod_l[i], n1, w, tabs_l, tm_l)
            h_c, q_c, k_c, vt_c = _in_c(h_c, moe_c, mod_c[i - 1], mod_c[i], n1, w, tabs_c, tm_c)
            grp = lambda q: q.reshape(B, GQA_KV_HEADS, GQA_GROUP, q.shape[2], LANES)
            o_l = _attention(grp(q_l), k_c, vt_c, k_l, vt_l, min(S, ATTN_QUERIES // GQA_GROUP),
                             ATTN_KEYS)
            woa = c_w_o[j].astype(BF16)
            h_l, m_l, aff_l = _out_proj(h_l, mod_l[i], None, None, o_l, woa, n2, wr_t, tm_l)
            m_c = aff_c = None
            if not last:
                o_c = _attention(grp(q_c), k_c, vt_c, None, None, L, L)
                h_c, m_c, aff_c = _out_proj(h_c, mod_c[i], None, None, o_c, woa, n2, wr_t, tm_c)
        moe_l, moe_c = _moe(m_l, aff_l, m_c, aff_c, i, moe_w_gate, moe_w_up, moe_w_down)
    return _final(h_l, moe_l, mod_l[depth - 1], row2(final_norm), tm_l)
```

```python
import functools
import math

import jax
import jax.numpy as jnp
import numpy as np
from jax import lax
from jax.experimental import pallas as pl
from jax.experimental.pallas import tpu as pltpu

F32 = jnp.float32
BF16 = jnp.bfloat16
I32 = jnp.int32
HIGHEST = lax.Precision.HIGHEST

D_MODEL = 1024
GRID_W = 64
EPS = 1e-6
ROPE_THETA = 10000.0
N_MOD = 6
FNET_GROUPS = 4
FNET_GROUP_DIM = 128
FNET_WIDTH = FNET_GROUPS * FNET_GROUP_DIM
MLA_HEADS = 8
MLA_Q_LORA = 256
MLA_KV_LORA = 128
MLA_NOPE = 64
MLA_ROPE = 32
MLA_V = 64
MLA_QK = MLA_NOPE + MLA_ROPE
GQA_HEADS = 8
GQA_KV_HEADS = 2
GQA_GROUP = GQA_HEADS // GQA_KV_HEADS
GQA_HEAD_DIM = 128
N_EXPERTS = 16
EXPERT_FF = 2048
CAPACITY_FACTOR = 2

LANES = 128
SUB = D_MODEL // LANES
VMEM_LIMIT = 56 * 1024 * 1024
LOG2E = math.log2(math.e)
NT = (((1,), (1,)), ((), ()))


def _params(sem, vmem=None):
    return pltpu.CompilerParams(dimension_semantics=sem, vmem_limit_bytes=vmem)


def _rms(x, g):
    return x * lax.rsqrt(jnp.mean(x * x, axis=-1, keepdims=True) + EPS) * g


def _modulate(h, g, shift, scale):
    return _rms(h, g) * (1.0 + scale) + shift


def _rope(x, cos, sin_fwd, sin_bwd, shift):
    return (x * cos + pltpu.roll(x, LANES - shift, 1) * sin_fwd
            + pltpu.roll(x, shift, 1) * sin_bwd)


def _untile(ref, rows):
    x = ref[rows]
    n = x.shape[0]
    y = pltpu.einshape("grsl->gsrl", x.reshape(n // 8, 8, SUB, LANES))
    return jnp.concatenate([y[:, s].reshape(n, LANES) for s in range(SUB)], axis=1)


def _store_tiled(ref, rows, x):
    n = x.shape[0]
    y = jnp.concatenate([x[:, s * LANES:(s + 1) * LANES].reshape(n // 8, 1, 8, LANES)
                         for s in range(SUB)], axis=1)
    ref[rows] = pltpu.einshape("gsrl->grsl", y).reshape(n, SUB, LANES)


def _ada_kernel(c_ref, w_ref, b_ref, o_ref):
    c = c_ref[...]
    x = c * jax.nn.sigmoid(c)
    o_ref[0] = jnp.dot(x, w_ref[0], precision=HIGHEST, preferred_element_type=F32) + b_ref[0]


def _ada(c8, ada_w, ada_b):
    depth, d, n = ada_w.shape
    tn = 1536
    return pl.pallas_call(
        _ada_kernel,
        out_shape=jax.ShapeDtypeStruct((depth, 8, n), F32),
        grid=(depth, n // tn),
        in_specs=[pl.BlockSpec((8, d), lambda l, j: (0, 0)),
                  pl.BlockSpec((1, d, tn), lambda l, j: (l, 0, j)),
                  pl.BlockSpec((1, 1, tn), lambda l, j: (l, 0, j))],
        out_specs=pl.BlockSpec((1, 8, tn), lambda l, j: (l, 0, j)),
        compiler_params=_params(("arbitrary", "arbitrary"), VMEM_LIMIT),
        name="ada",
    )(c8, ada_w, ada_b.reshape(depth, 1, n))


def _in_ab_kernel(scale, *refs):
    (h_ref, mod_ref, n1_ref, win_ref, qn_ref, wuq_ref, kvn_ref, wuk_ref, wuvt_ref, ekr_ref,
     cos_ref, sf_ref, sb_ref, f_ref, q_ref, k_ref, vt_ref) = refs
    D = D_MODEL
    h = h_ref[0]
    mod = mod_ref[0]
    a = _modulate(h, n1_ref[...], mod[:, 0:D], mod[:, D:2 * D]).astype(BF16)
    p = jnp.dot(a, win_ref[...], preferred_element_type=F32)
    o = FNET_WIDTH
    f_ref[0] = p[:, :o]
    cq = p[:, o:o + MLA_Q_LORA]
    o += MLA_Q_LORA
    ckv = p[:, o:o + MLA_KV_LORA]
    o += MLA_KV_LORA
    kr = p[:, o:o + MLA_ROPE]
    cqn = _rms(cq, qn_ref[...]).astype(BF16)
    ckvn = _rms(ckv, kvn_ref[...]).astype(BF16)
    q = jnp.dot(cqn, wuq_ref[...], preferred_element_type=F32)
    k = (jnp.dot(ckvn, wuk_ref[...], preferred_element_type=F32)
         + jnp.dot(kr.astype(BF16), ekr_ref[...], preferred_element_type=F32))
    vt = lax.dot_general(wuvt_ref[...], ckvn, NT, preferred_element_type=F32)
    cos, sf, sb = cos_ref[...], sf_ref[...], sb_ref[...]
    shift = MLA_ROPE // 4
    for hd in range(MLA_HEADS):
        sl = slice(hd * LANES, (hd + 1) * LANES)
        q_ref[0, hd] = (_rope(q[:, sl], cos, sf, sb, shift) * scale).astype(BF16)
        k_ref[0, hd] = _rope(k[:, sl], cos, sf, sb, shift).astype(BF16)
        vt_ref[0, hd] = vt[sl, :].astype(BF16)


def _in_ab(h, mod, n1, w, tables, tm):
    B, R, D = h.shape
    H = MLA_HEADS
    full = lambda a: pl.BlockSpec(a.shape, lambda b, i: (0,) * a.ndim)
    row = lambda w_: pl.BlockSpec((1, tm, w_), lambda b, i: (b, i, 0))
    tab = pl.BlockSpec((tm, LANES), lambda b, i: (i, 0))
    hd = pl.BlockSpec((1, H, tm, LANES), lambda b, i: (b, 0, i, 0))
    hdt = pl.BlockSpec((1, H, LANES, tm), lambda b, i: (b, 0, 0, i))
    weights = [n1, w["win"], w["qn"], w["wuq"], w["kvn"], w["wuk"], w["wuvt"], w["ekr"]]
    return pl.pallas_call(
        functools.partial(_in_ab_kernel, MLA_QK ** -0.5 * LOG2E),
        out_shape=(jax.ShapeDtypeStruct((B, R, FNET_WIDTH), F32),
                   jax.ShapeDtypeStruct((B, H, R, LANES), BF16),
                   jax.ShapeDtypeStruct((B, H, R, LANES), BF16),
                   jax.ShapeDtypeStruct((B, H, LANES, R), BF16)),
        grid=(B, R // tm),
        in_specs=[row(D), pl.BlockSpec((1, 1, N_MOD * D), lambda b, i: (b, 0, 0))]
        + [full(a) for a in weights] + [tab, tab, tab],
        out_specs=(row(FNET_WIDTH), hd, hd, hdt),
        compiler_params=_params(("arbitrary", "arbitrary"), VMEM_LIMIT),
        name="in_ab",
    )(h, mod, *weights, *tables)


def _in_c_kernel(scale, *refs):
    (h_ref, moe_ref, modp_ref, mod_ref, n1_ref, wqk_ref, wvt_ref, qg_ref, kg_ref,
     cos_ref, sf_ref, sb_ref, h2_ref, q_ref, k_ref, vt_ref) = refs
    D = D_MODEL
    h = h_ref[0] + modp_ref[0][:, 5 * D:6 * D] * _untile(moe_ref.at[0], slice(None))
    h2_ref[0] = h
    mod = mod_ref[0]
    a = _modulate(h, n1_ref[...], mod[:, 0:D], mod[:, D:2 * D]).astype(BF16)
    p = jnp.dot(a, wqk_ref[...], preferred_element_type=F32)
    vt = lax.dot_general(wvt_ref[...], a, NT, preferred_element_type=F32)
    cos, sf, sb = cos_ref[...], sf_ref[...], sb_ref[...]
    shift = GQA_HEAD_DIM // 4
    nq = GQA_HEADS * GQA_HEAD_DIM
    for hd in range(GQA_HEADS):
        x = _rms(p[:, hd * LANES:(hd + 1) * LANES], qg_ref[...])
        q_ref[0, hd] = (_rope(x, cos, sf, sb, shift) * scale).astype(BF16)
    for hd in range(GQA_KV_HEADS):
        x = _rms(p[:, nq + hd * LANES:nq + (hd + 1) * LANES], kg_ref[...])
        k_ref[0, hd] = _rope(x, cos, sf, sb, shift).astype(BF16)
        vt_ref[0, hd] = vt[hd * LANES:(hd + 1) * LANES, :].astype(BF16)


def _in_c(h, moe, mod_prev, mod, n1, w, tables, tm):
    B, R, D = h.shape
    full = lambda a: pl.BlockSpec(a.shape, lambda b, i: (0,) * a.ndim)
    row = lambda w_: pl.BlockSpec((1, tm, w_), lambda b, i: (b, i, 0))
    tiled = pl.BlockSpec((1, tm, SUB, LANES), lambda b, i: (b, i, 0, 0))
    modspec = pl.BlockSpec((1, 1, N_MOD * D), lambda b, i: (b, 0, 0))
    tab = pl.BlockSpec((tm, LANES), lambda b, i: (i, 0))
    hd = lambda n: pl.BlockSpec((1, n, tm, LANES), lambda b, i: (b, 0, i, 0))
    hdt = pl.BlockSpec((1, GQA_KV_HEADS, LANES, tm), lambda b, i: (b, 0, 0, i))
    weights = [n1, w["wqk"], w["wvt"], w["qg"], w["kg"]]
    return pl.pallas_call(
        functools.partial(_in_c_kernel, GQA_HEAD_DIM ** -0.5 * LOG2E),
        out_shape=(jax.ShapeDtypeStruct((B, R, D), F32),
                   jax.ShapeDtypeStruct((B, GQA_HEADS, R, LANES), BF16),
                   jax.ShapeDtypeStruct((B, GQA_KV_HEADS, R, LANES), BF16),
                   jax.ShapeDtypeStruct((B, GQA_KV_HEADS, LANES, R), BF16)),
        grid=(B, R // tm),
        in_specs=[row(D), tiled, modspec, modspec] + [full(a) for a in weights] + [tab, tab, tab],
        out_specs=(row(D), hd(GQA_HEADS), hd(GQA_KV_HEADS), hdt),
        compiler_params=_params(("arbitrary", "arbitrary"), VMEM_LIMIT),
        name="in_c",
    )(h, moe, mod_prev, mod, *weights, *tables)


ATTN_QUERIES = 2048
ATTN_KEYS = 512


def _attn_kernel(n_lat, tk, *refs):
    if n_lat:
        q_ref, kc_ref, vct_ref, kl_ref, vlt_ref, o_ref, acc_ref, s0, s1, p0, p1 = refs
        s_bufs, p_bufs = (s0, s1), (p0, p1)
    else:
        q_ref, kc_ref, vct_ref, o_ref, acc_ref = refs
    G, tq = q_ref.shape[2], q_ref.shape[3]
    q = q_ref[0, 0].reshape(G * tq, LANES)

    def scores(k):
        return lax.dot_general(k, q, NT, preferred_element_type=F32)

    s = scores(kc_ref[0, 0])
    m = jnp.max(s, axis=0, keepdims=True)
    p = jnp.exp2(s - m)
    l = jnp.sum(p, axis=0, keepdims=True)
    acc_ref[...] = jnp.dot(vct_ref[0, 0], p.astype(BF16), preferred_element_type=F32)

    if n_lat:
        assert n_lat == 1 or n_lat % 2 == 0

        def chunk(c):
            return pl.ds(pl.multiple_of(c * tk, tk), tk)

        def score_stage(c, slot):
            s_bufs[slot][...] = scores(kl_ref[0, 0, chunk(c), :])

        def softmax_stage(slot, m, l):
            s = s_bufs[slot][...]
            m_new = jnp.maximum(m, jnp.max(s, axis=0, keepdims=True))
            p = jnp.exp2(s - m_new)
            p_bufs[slot][...] = p.astype(BF16)
            alpha = jnp.exp2(m - m_new)
            return m_new, alpha * l + jnp.sum(p, axis=0, keepdims=True), alpha

        def value_stage(c, slot, alpha):
            pv = jnp.dot(vlt_ref[0, 0, :, chunk(c)], p_bufs[slot][...], preferred_element_type=F32)
            acc_ref[...] = alpha * acc_ref[...] + pv

        score_stage(0, 0)
        m, l, alpha = softmax_stage(0, m, l)
        if n_lat > 1:
            score_stage(1, 1)

            def body(i, carry):
                m, l, alpha = carry
                for slot in (0, 1):
                    c = 2 * i + slot
                    score_stage(c + 2, slot)
                    value_stage(c, slot, alpha)
                    m, l, alpha = softmax_stage(1 - slot, m, l)
                return m, l, alpha

            m, l, alpha = lax.fori_loop(0, (n_lat - 2) // 2, body, (m, l, alpha))
            value_stage(n_lat - 2, 0, alpha)
            m, l, alpha = softmax_stage(1, m, l)
        value_stage(n_lat - 1, (n_lat - 1) % 2, alpha)
    o_ref[0, 0] = (acc_ref[...] / l).T.reshape(G, tq, LANES).astype(BF16)


def _attention(q, kc, vct, kl, vlt, tq, tk):
    B, Hk, G, R, _ = q.shape
    Lc = kc.shape[2]
    n_lat = 0 if kl is None else kl.shape[2] // tk
    qspec = pl.BlockSpec((1, 1, G, tq, LANES), lambda b, h, i: (b, h, 0, i, 0))
    kspec = lambda n: pl.BlockSpec((1, 1, n, LANES), lambda b, h, i: (b, h, 0, 0))
    vspec = lambda n: pl.BlockSpec((1, 1, LANES, n), lambda b, h, i: (b, h, 0, 0))
    ins = [q, kc, vct] + ([kl, vlt] if n_lat else [])
    specs = [qspec, kspec(Lc), vspec(Lc)] + ([kspec(kl.shape[2]), vspec(kl.shape[2])] if n_lat else [])
    return pl.pallas_call(
        functools.partial(_attn_kernel, n_lat, tk),
        out_shape=jax.ShapeDtypeStruct(q.shape, BF16),
        grid=(B, Hk, R // tq),
        in_specs=specs,
        out_specs=qspec,
        scratch_shapes=[pltpu.VMEM((LANES, G * tq), F32)]
        + ([pltpu.VMEM((tk, G * tq), F32)] * 2 + [pltpu.VMEM((tk, G * tq), BF16)] * 2 if n_lat else []),
        compiler_params=_params(("arbitrary", "arbitrary", "arbitrary"), VMEM_LIMIT),
        name="attn",
    )(*ins)


def _dft_mats(n):
    k = np.arange(n, dtype=np.float64)
    ang = 2.0 * np.pi * np.outer(k, k) / n
    return np.cos(ang), np.sin(ang)


def _split(x):
    hi = x.astype(BF16)
    return hi, (x - hi.astype(F32)).astype(BF16)


def _dot3(a, b):
    d = lambda p, q: jnp.dot(p, q, preferred_element_type=F32)
    return d(a[0], b[0]) + d(a[0], b[1]) + d(a[1], b[0])


def _split_const(m):
    return jnp.stack(_split(jnp.asarray(m, F32)))


def _dft1_kernel(x_ref, m1_ref, tw_ref, o_ref):
    s1, ns2 = x_ref.shape[1], x_ref.shape[2]
    for i in range(ns2):
        a = jnp.dot(m1_ref[...], x_ref[0, :, i, :], precision=HIGHEST, preferred_element_type=F32)
        are, aim = a[:s1], a[s1:]
        tre = jnp.tile(tw_ref[0, :, i * LANES:(i + 1) * LANES], (1, FNET_GROUPS))
        tim = jnp.tile(tw_ref[1, :, i * LANES:(i + 1) * LANES], (1, FNET_GROUPS))
        o_ref[0, 0, :, i, :] = are * tre - aim * tim
        o_ref[0, 1, :, i, :] = are * tim + aim * tre


def _dft2_kernel(a_ref, m2_ref, m3_ref, o_ref):
    kb, n2 = a_ref.shape[2], a_ref.shape[3]
    for j in range(kb):
        rhs = jnp.concatenate([a_ref[0, 0, j], a_ref[0, 1, j]], axis=0)
        y = _dot3((m2_ref[0], m2_ref[1]), _split(rhs))
        for g in range(FNET_GROUPS):
            sl = slice(g * LANES, (g + 1) * LANES)
            lhs = jnp.concatenate([y[:n2, sl], y[n2:, sl]], axis=1)
            o_ref[0, :, j, sl] = _dot3(_split(lhs), (m3_ref[0], m3_ref[1]))


def _fourier_lat(f):
    B, S, W = f.shape
    n2 = LANES
    s1 = S // n2
    c1, sn1 = _dft_mats(s1)
    c2, sn2 = _dft_mats(n2)
    cc, sc = _dft_mats(FNET_GROUP_DIM)
    m1 = jnp.asarray(np.concatenate([c1, -sn1], axis=0), F32)
    m2 = _split_const(np.block([[c2, sn2], [-sn2, c2]]))
    norm = 1.0 / math.sqrt(S * FNET_GROUP_DIM)
    m3 = _split_const(np.concatenate([cc, sc], axis=0) * norm)
    ang = 2.0 * np.pi * np.outer(np.arange(s1), np.arange(n2)) / S
    tw = np.stack([np.cos(ang), -np.sin(ang)])
    tw = jnp.asarray(np.repeat(tw[:, :, :, None], LANES, axis=3).reshape(2, s1, n2 * LANES), F32)

    ns2 = 8
    a = pl.pallas_call(
        _dft1_kernel,
        out_shape=jax.ShapeDtypeStruct((B, 2, s1, n2, W), F32),
        grid=(B, n2 // ns2),
        in_specs=[pl.BlockSpec((1, s1, ns2, W), lambda b, j: (b, 0, j, 0)),
                  pl.BlockSpec(m1.shape, lambda b, j: (0, 0)),
                  pl.BlockSpec((2, s1, ns2 * LANES), lambda b, j: (0, 0, j))],
        out_specs=pl.BlockSpec((1, 2, s1, ns2, W), lambda b, j: (b, 0, 0, j, 0)),
        compiler_params=_params(("arbitrary", "arbitrary"), VMEM_LIMIT),
        name="dft1",
    )(f.reshape(B, s1, n2, W), m1, tw)
    kb = min(8, s1)
    y = pl.pallas_call(
        _dft2_kernel,
        out_shape=jax.ShapeDtypeStruct((B, n2, s1, W), F32),
        grid=(B, s1 // kb),
        in_specs=[pl.BlockSpec((1, 2, kb, n2, W), lambda b, j: (b, 0, j, 0, 0)),
                  pl.BlockSpec(m2.shape, lambda b, j: (0, 0, 0)),
                  pl.BlockSpec(m3.shape, lambda b, j: (0, 0, 0))],
        out_specs=pl.BlockSpec((1, n2, kb, W), lambda b, j: (b, 0, j, 0)),
        compiler_params=_params(("arbitrary", "arbitrary"), VMEM_LIMIT),
        name="dft2",
    )(a, m2, m3)
    return y.reshape(B, S, W)


def _dftc_kernel(f_ref, mc_ref, m3_ref, o_ref):
    n = f_ref.shape[1]
    a = _dot3((mc_ref[0], mc_ref[1]), _split(f_ref[0]))
    for g in range(FNET_GROUPS):
        sl = slice(g * LANES, (g + 1) * LANES)
        lhs = jnp.concatenate([a[:n, sl], a[n:, sl]], axis=1)
        o_ref[0, :, sl] = _dot3(_split(lhs), (m3_ref[0], m3_ref[1]))


def _fourier_ctx(f):
    B, L, W = f.shape
    c, s = _dft_mats(L)
    cc, sc = _dft_mats(FNET_GROUP_DIM)
    mc = _split_const(np.concatenate([c, -s], axis=0))
    m3 = _split_const(np.concatenate([cc, sc], axis=0) / math.sqrt(L * FNET_GROUP_DIM))
    return pl.pallas_call(
        _dftc_kernel,
        out_shape=jax.ShapeDtypeStruct((B, L, W), F32),
        grid=(B,),
        in_specs=[pl.BlockSpec((1, L, W), lambda b: (b, 0, 0)),
                  pl.BlockSpec(mc.shape, lambda b: (0, 0, 0)),
                  pl.BlockSpec(m3.shape, lambda b: (0, 0, 0))],
        out_specs=pl.BlockSpec((1, L, W), lambda b: (b, 0, 0)),
        compiler_params=_params(("arbitrary",), VMEM_LIMIT),
        name="dftc",
    )(f, mc, m3)


def _out_kernel(has_f, *refs):
    if has_f:
        (h_ref, mod_ref, yf_ref, wof_ref, o_ref, woa_ref, n2_ref, wr_ref,
         h1_ref, m_ref, aff_ref) = refs
    else:
        (h_ref, mod_ref, o_ref, woa_ref, n2_ref, wr_ref, h1_ref, m_ref, aff_ref) = refs
    D = D_MODEL
    Hk, G = o_ref.shape[1], o_ref.shape[2]
    ocat = jnp.concatenate([o_ref[0, hk, g] for hk in range(Hk) for g in range(G)], axis=1)
    y = jnp.dot(ocat, woa_ref[...], preferred_element_type=F32)
    if has_f:
        y = y + jnp.dot(yf_ref[0].astype(BF16), wof_ref[...], preferred_element_type=F32)
    mod = mod_ref[0]
    h1 = h_ref[0] + mod[:, 2 * D:3 * D] * y
    h1_ref[0] = h1
    m = _modulate(h1, n2_ref[...], mod[:, 3 * D:4 * D], mod[:, 4 * D:5 * D])
    _store_tiled(m_ref.at[0], slice(None), m)
    logit = lax.dot_general(wr_ref[...], m, NT, precision=HIGHEST,
                            preferred_element_type=F32)
    e = jnp.exp(logit - jnp.max(logit, axis=0, keepdims=True))
    aff_ref[0] = e / jnp.sum(e, axis=0, keepdims=True)


def _out_proj(h, mod, yf, wof, o, woa, n2, wr_t, tm):
    B, R, D = h.shape
    _, Hk, G, _, _ = o.shape
    full = lambda a: pl.BlockSpec(a.shape, lambda b, i: (0,) * a.ndim)
    row = lambda w_: pl.BlockSpec((1, tm, w_), lambda b, i: (b, i, 0))
    modspec = pl.BlockSpec((1, 1, N_MOD * D), lambda b, i: (b, 0, 0))
    ospec = pl.BlockSpec((1, Hk, G, tm, LANES), lambda b, i: (b, 0, 0, i, 0))
    has_f = yf is not None
    ins = [h, mod] + ([yf, wof] if has_f else []) + [o, woa, n2, wr_t]
    specs = ([row(D), modspec] + ([row(FNET_WIDTH), full(wof)] if has_f else [])
             + [ospec, full(woa), full(n2), full(wr_t)])
    return pl.pallas_call(
        functools.partial(_out_kernel, has_f),
        out_shape=(jax.ShapeDtypeStruct((B, R, D), F32),
                   jax.ShapeDtypeStruct((B, R, SUB, LANES), F32),
                   jax.ShapeDtypeStruct((B, N_EXPERTS, R), F32)),
        grid=(B, R // tm),
        in_specs=specs,
        out_specs=(row(D), pl.BlockSpec((1, tm, SUB, LANES), lambda b, i: (b, i, 0, 0)),
                   pl.BlockSpec((1, N_EXPERTS, tm), lambda b, i: (b, 0, i))),
        compiler_params=_params(("arbitrary", "arbitrary"), VMEM_LIMIT),
        name="out_proj",
    )(*ins)


COMPACT_UNROLL = 4


def _topk_kernel(cap, aff_ref, u_ref, ones_ref, lmat_ref, lc_ref, lg_ref, cnt_ref, off_ref,
                 pos_ref, ac_ref, lct_ref, lgt_ref):
    a = aff_ref[0]
    E, N = a.shape
    NC = N // LANES
    R = NC * E
    keys = pltpu.bitcast(a, I32)

    def bit_step(i, tau):
        cand = tau | jnp.left_shift(jnp.int32(1), 30 - i)
        cnt = jnp.sum((keys >= cand).astype(I32), axis=1, keepdims=True)
        return jnp.where(cnt >= cap, cand, tau)

    tau = lax.fori_loop(0, 31, bit_step, jnp.zeros((E, 1), I32))
    gt = (keys > tau).astype(F32)
    eq = (keys == tau).astype(F32)
    need = (cap - jnp.sum(gt, axis=1, keepdims=True))

    def chunked(x):
        return jnp.concatenate([x[:, c * LANES:(c + 1) * LANES] for c in range(NC)], axis=0)

    a_c, gt_c, eq_c = chunked(a), chunked(gt), chunked(eq)
    need_c = jnp.tile(need, (NC, 1))

    def prefix(x):
        xb = x.astype(BF16)
        loc = jnp.dot(xb, u_ref[...], preferred_element_type=F32)
        tot = jnp.dot(xb, ones_ref[...], preferred_element_type=F32)
        offs = jnp.dot(lmat_ref[...], tot.astype(BF16), preferred_element_type=F32)
        return loc, tot, offs

    loc, tot, offs = prefix(eq_c)
    sel = jnp.maximum(gt_c, jnp.where(loc + offs < need_c, eq_c, 0.0))
    loc, tot, offs = prefix(sel)
    cnt_ref[0] = tot.astype(I32)
    off_ref[0] = offs.astype(I32)
    Rp = pos_ref.shape[1]
    selpos = jnp.where(sel > 0.0, loc, -1.0)
    if Rp > R:
        selpos = jnp.concatenate([selpos, jnp.full((Rp - R, LANES), -1.0, F32)], axis=0)
        a_c = jnp.concatenate([a_c, jnp.zeros((Rp - R, LANES), F32)], axis=0)
    pos_ref[...] = selpos.T
    ac_ref[...] = a_c.T
    tok = lax.broadcasted_iota(I32, (LANES, Rp), 0).astype(F32)

    def compact(i, carry):
        p = pos_ref[...]
        a_t = ac_ref[...]
        for u in range(COMPACT_UNROLL):
            j = i * COMPACT_UNROLL + u
            hit = p == lax.convert_element_type(j, F32)
            lct_ref[pl.ds(j, 1), :] = jnp.sum(jnp.where(hit, tok, 0.0), axis=0, keepdims=True)
            lgt_ref[pl.ds(j, 1), :] = jnp.sum(jnp.where(hit, a_t, 0.0), axis=0, keepdims=True)
        return carry

    lax.fori_loop(0, LANES // COMPACT_UNROLL, compact, 0)
    lc_ref[0] = lct_ref[...].T[:R].astype(I32)
    lg_ref[0] = lgt_ref[...].T[:R]


def _topk(aff, cap):
    B, E, N = aff.shape
    NC = N // LANES
    R = NC * E
    i = np.arange(LANES)
    u = jnp.asarray(i[:, None] < i[None, :], BF16)
    ones = jnp.ones((LANES, LANES), BF16)
    r = np.arange(R)
    lmat = jnp.asarray((r[:, None] % E == r[None, :] % E) & (r[None, :] // E < r[:, None] // E), BF16)
    full = lambda a: pl.BlockSpec(a.shape, lambda b: (0,) * a.ndim)
    ospec = pl.BlockSpec((1, R, LANES), lambda b: (b, 0, 0))
    sds = lambda dt: jax.ShapeDtypeStruct((B, R, LANES), dt)
    lc, lg, cnt, off = pl.pallas_call(
        functools.partial(_topk_kernel, cap),
        out_shape=(sds(I32), sds(F32), sds(I32), sds(I32)),
        grid=(B,),
        in_specs=[pl.BlockSpec((1, E, N), lambda b: (b, 0, 0)), full(u), full(ones), full(lmat)],
        out_specs=(ospec, ospec, ospec, ospec),
        scratch_shapes=[pltpu.VMEM((LANES, -(-R // LANES) * LANES), F32)] * 4,
        compiler_params=_params(("arbitrary",), VMEM_LIMIT),
        name="topk",
    )(aff, u, ones, lmat)
    by_expert = lambda x: x.reshape(B, NC, E, LANES).transpose(0, 2, 1, 3)
    return (by_expert(lc), by_expert(lg), by_expert(cnt)[..., 0], by_expert(off)[..., 0])


GATHER_ROWS = 256
ROW_UNROLL = 4


def _gather_kernel(cap, cnt_ref, off_ref, lc_ref, m_hbm, o_ref, x_ref, sems):
    e = pl.program_id(0)
    B, NC = lc_ref.shape[0], lc_ref.shape[1]
    for b in range(B):
        def chunk_body(c, carry, b=b):
            base = off_ref[b, e, c] + b * cap

            n = cnt_ref[b, e, c]

            def start_row(r, priority):
                t = c * LANES + lc_ref[b, c, r]
                pltpu.make_async_copy(m_hbm.at[b, t], x_ref.at[base + r],
                                      sems.at[b]).start(priority=priority)

            def pair_body(i, carry):
                start_row(2 * i, 0)
                start_row(2 * i + 1, 1)
                return carry

            lax.fori_loop(0, n // 2, pair_body, carry)

            @pl.when(n % 2 == 1)
            def _():
                start_row(n - 1, 0)

            return carry

        lax.fori_loop(0, NC, chunk_body, 0)
    step = min(GATHER_ROWS, cap)
    for b in range(B):
        done = x_ref.at[pl.ds(b * cap, cap)]
        pltpu.make_async_copy(done, done, sems.at[b]).wait()
        for r0 in range(b * cap, (b + 1) * cap, step):
            rows = slice(r0, r0 + step)
            o_ref[0, rows, :] = _untile(x_ref, rows).astype(BF16)


def _gather(m, lc, cnt, off, cap):
    B, N, sub, _ = m.shape
    E, NC = lc.shape[1], lc.shape[2]
    return pl.pallas_call(
        functools.partial(_gather_kernel, cap),
        out_shape=jax.ShapeDtypeStruct((E, B * cap, sub * LANES), BF16),
        grid_spec=pltpu.PrefetchScalarGridSpec(
            num_scalar_prefetch=2,
            grid=(E,),
            in_specs=[pl.BlockSpec((B, None, NC, LANES), lambda e, *_: (0, e, 0, 0),
                                   memory_space=pltpu.SMEM),
                      pl.BlockSpec(memory_space=pl.ANY)],
            out_specs=pl.BlockSpec((1, B * cap, sub * LANES), lambda e, *_: (e, 0, 0)),
            scratch_shapes=[pltpu.VMEM((B * cap, sub, LANES), m.dtype),
                            pltpu.SemaphoreType.DMA((B,))]),
        compiler_params=_params(("arbitrary",), VMEM_LIMIT),
        name="gather",
    )(cnt, off, lc, m)


def _ffn_kernel(n_parts, chunk_rows, *refs):
    wg_ref, wu_ref, wd_ref = refs[:3]
    x_refs = refs[3:3 + n_parts]
    o_refs = refs[3 + n_parts:3 + 2 * n_parts]
    acc_refs = refs[3 + 2 * n_parts:]
    j = pl.program_id(1)
    last = pl.num_programs(1) - 1
    wg = wg_ref[0].astype(BF16)
    wu = wu_ref[0].astype(BF16)
    wd = wd_ref[0].astype(BF16)
    for x_ref, o_ref, acc_ref, nrows in zip(x_refs, o_refs, acc_refs, chunk_rows):
        for r0 in range(0, x_ref.shape[1], nrows):
            rows = slice(r0, r0 + nrows)
            x = x_ref[0, rows, :]
            a = jnp.dot(x, wg, preferred_element_type=F32)
            u = jnp.dot(x, wu, preferred_element_type=F32)
            hh = (a * jax.nn.sigmoid(a) * u).astype(BF16)
            y = jnp.dot(hh, wd, preferred_element_type=F32)

            @pl.when(j == 0)
            def _():
                acc_ref[rows, :] = y

            @pl.when((j > 0) & (j < last))
            def _():
                acc_ref[rows, :] += y

            @pl.when(j == last)
            def _():
                _store_tiled(o_ref.at[0], rows, acc_ref[rows, :] + y)


def _ffn(xs, layer, w_gate, w_up, w_down, chunk_rows):
    _, E, D, F = w_gate.shape
    tf = 512
    assert F // tf >= 2
    xspec = lambda x: pl.BlockSpec((1, x.shape[1], D), lambda e, j: (e, 0, 0))
    ospec = lambda x: pl.BlockSpec((1, x.shape[1], SUB, LANES), lambda e, j: (e, 0, 0, 0))
    outs = pl.pallas_call(
        functools.partial(_ffn_kernel, len(xs), chunk_rows),
        out_shape=tuple(jax.ShapeDtypeStruct((E, x.shape[1], SUB, LANES), F32) for x in xs),
        grid=(E, F // tf),
        in_specs=[pl.BlockSpec((None, 1, D, tf), lambda e, j: (layer, e, 0, j)),
                  pl.BlockSpec((None, 1, D, tf), lambda e, j: (layer, e, 0, j)),
                  pl.BlockSpec((None, 1, tf, D), lambda e, j: (layer, e, j, 0))]
        + [xspec(x) for x in xs],
        out_specs=tuple(ospec(x) for x in xs),
        scratch_shapes=[pltpu.VMEM((x.shape[1], D), F32) for x in xs],
        compiler_params=_params(("arbitrary", "arbitrary"), VMEM_LIMIT),
        name="ffn",
    )(w_gate, w_up, w_down, *xs)
    return outs


def _combine_kernel(cnt_ref, off_ref, lc_ref, lg_ref, y_ref, o_ref):
    b = pl.program_id(0)
    hf = pl.program_id(1)
    e = pl.program_id(2)
    nch = o_ref.shape[1] // LANES

    @pl.when(e == 0)
    def _():
        o_ref[...] = jnp.zeros_like(o_ref)

    def chunk_body(ci, carry):
        c = hf * nch + ci
        base = off_ref[b, e, c]

        n = cnt_ref[b, e, c]

        def rows_body(r0, width):
            ts = [ci * LANES + lc_ref[c, r0 + u] for u in range(width)]
            vals = [o_ref[0, ts[u]] + lg_ref[c, r0 + u] * y_ref[0, base + r0 + u]
                    for u in range(width)]
            for u in range(width):
                o_ref[0, ts[u]] = vals[u]

        def group_body(i, carry):
            rows_body(i * ROW_UNROLL, ROW_UNROLL)
            return carry

        def tail_body(r, carry):
            rows_body(r, 1)
            return carry

        full = n // ROW_UNROLL
        lax.fori_loop(0, full, group_body, carry)
        return lax.fori_loop(full * ROW_UNROLL, n, tail_body, carry)

    lax.fori_loop(0, nch, chunk_body, 0)


def _combine(y, lc, lg, cnt, off, n_tokens, cap, n_split):
    E = y.shape[0]
    B, _, NC, _ = lc.shape
    nh = n_tokens // n_split
    smem = lambda: pl.BlockSpec((None, None, NC, LANES), lambda b, h, e, *_: (b, e, 0, 0),
                                memory_space=pltpu.SMEM)
    return pl.pallas_call(
        _combine_kernel,
        out_shape=jax.ShapeDtypeStruct((B, n_tokens, SUB, LANES), F32),
        grid_spec=pltpu.PrefetchScalarGridSpec(
            num_scalar_prefetch=2,
            grid=(B, n_split, E),
            in_specs=[smem(), smem(),
                      pl.BlockSpec((1, cap, SUB, LANES), lambda b, h, e, *_: (e, b, 0, 0))],
            out_specs=pl.BlockSpec((1, nh, SUB, LANES), lambda b, h, e, *_: (b, h, 0, 0))),
        compiler_params=_params(("arbitrary", "arbitrary", "arbitrary"), VMEM_LIMIT),
        name="combine",
    )(cnt, off, lc, lg, y)


def _final_kernel(h_ref, moe_ref, mod_ref, g_ref, o_ref):
    D = D_MODEL
    h = h_ref[0] + mod_ref[0][:, 5 * D:6 * D] * _untile(moe_ref.at[0], slice(None))
    o_ref[0] = _rms(h, g_ref[...])


def _final(h, moe, mod, g, tm):
    B, R, D = h.shape
    row = pl.BlockSpec((1, tm, D), lambda b, i: (b, i, 0))
    return pl.pallas_call(
        _final_kernel,
        out_shape=jax.ShapeDtypeStruct((B, R, D), F32),
        grid=(B, R // tm),
        in_specs=[row, pl.BlockSpec((1, tm, SUB, LANES), lambda b, i: (b, i, 0, 0)),
                  pl.BlockSpec((1, 1, N_MOD * D), lambda b, i: (b, 0, 0)),
                  pl.BlockSpec((1, D), lambda b, i: (0, 0))],
        out_specs=row,
        compiler_params=_params(("arbitrary", "arbitrary"), VMEM_LIMIT),
        name="final",
    )(h, moe, mod, g)


def _rope_tables(n_lat, n_ctx, segments):
    t = jnp.arange(n_lat)
    pos = {"row": (t // GRID_W).astype(F32), "col": (t % GRID_W).astype(F32)}
    freq = {"row": np.zeros((2, LANES), np.float32), "col": np.zeros((2, LANES), np.float32)}
    first = np.zeros(LANES, np.float32)
    second = np.zeros(LANES, np.float32)
    for lane0, width, which in segments:
        half = width // 2
        idx = np.arange(half, dtype=np.float32)
        for lo, mask in ((lane0, first), (lane0 + half, second)):
            freq[which][0, lo:lo + half] = idx / half
            freq[which][1, lo:lo + half] = 1.0
            mask[lo:lo + half] = 1.0
    ang = jnp.zeros((n_lat, LANES), F32)
    for which in ("row", "col"):
        inv = (ROPE_THETA ** (-jnp.asarray(freq[which][0]))) * jnp.asarray(freq[which][1])
        ang = ang + pos[which][:, None] * inv[None, :]
    sin = jnp.sin(ang)
    lat = (jnp.cos(ang), -sin * first[None, :], sin * second[None, :])
    ctx = (jnp.ones((n_ctx, LANES), F32), jnp.zeros((n_ctx, LANES), F32), jnp.zeros((n_ctx, LANES), F32))
    return lat, ctx


def _head_slots(w, n_heads, width, lo, hi):
    k = w.shape[0]
    w3 = w.reshape(k, n_heads, width)[:, :, lo:hi]
    return jnp.pad(w3, ((0, 0), (0, 0), (0, LANES - (hi - lo)))).reshape(k, n_heads * LANES)


def _moe(m_l, aff_l, m_c, aff_c, layer, w_gate, w_up, w_down):
    B, S = m_l.shape[:2]
    cap_l = CAPACITY_FACTOR * S // N_EXPERTS
    lc, lg, cnt, off = _topk(aff_l, cap_l)
    xs = [_gather(m_l, lc, cnt, off, cap_l)]
    chunk_rows = [cap_l]
    if m_c is not None:
        L = m_c.shape[1]
        cap_c = CAPACITY_FACTOR * L // N_EXPERTS
        lcc, lgc, cntc, offc = _topk(aff_c, cap_c)
        xs.append(_gather(m_c, lcc, cntc, offc, cap_c))
        chunk_rows.append(B * cap_c)
    ys = _ffn(xs, layer, w_gate, w_up, w_down, tuple(chunk_rows))
    out_l = _combine(ys[0], lc, lg, cnt, off, S, cap_l, 2)
    out_c = None
    if m_c is not None:
        out_c = _combine(ys[1], lcc, lgc, cntc, offc, L, cap_c, 1)
    return out_l, out_c


def kernel(x, c, ctx, c_ctx, ada_w, ada_b, norm1, norm2, ab_w_in, ab_q_norm, ab_w_uq, ab_kv_norm,
           ab_w_ukv, ab_w_o, c_w_in, c_q_gain, c_k_gain, c_w_o, moe_router, moe_w_gate, moe_w_up,
           moe_w_down, final_norm):
    B, S, D = x.shape
    L = ctx.shape[1]
    depth = ada_w.shape[0]
    tm_l, tm_c = 512, L
    row2 = lambda v: v.reshape(1, -1)

    c8 = jnp.zeros((8, D), F32).at[:B].set(c).at[B].set(c_ctx)
    mod = _ada(c8, ada_w, ada_b)
    mod_l = [mod[i, :B][:, None, :] for i in range(depth)]
    mod_c = [jnp.broadcast_to(mod[i, B][None, None, :], (B, 1, N_MOD * D)) for i in range(depth)]

    h_l, h_c = x, ctx
    moe_l = moe_c = None
    for i in range(depth):
        last = i == depth - 1
        j = i // 2
        wr_t = moe_router[i].T
        n2 = row2(norm2[i])
        if i % 2 == 0:
            assert moe_l is None
            tabs_l, tabs_c = _rope_tables(S, L, ((MLA_NOPE, MLA_ROPE // 2, "row"),
                                                 (MLA_NOPE + MLA_ROPE // 2, MLA_ROPE // 2, "col")))
            w_ukv = ab_w_ukv[j]
            ekr = np.zeros((MLA_ROPE, MLA_HEADS * LANES), np.float32)
            for hd in range(MLA_HEADS):
                ekr[np.arange(MLA_ROPE), hd * LANES + MLA_NOPE + np.arange(MLA_ROPE)] = 1.0
            w = {
                "win": ab_w_in[j].astype(BF16),
                "qn": row2(ab_q_norm[j]),
                "wuq": _head_slots(ab_w_uq[j], MLA_HEADS, MLA_QK, 0, MLA_QK).astype(BF16),
                "kvn": row2(ab_kv_norm[j]),
                "wuk": _head_slots(w_ukv, MLA_HEADS, MLA_NOPE + MLA_V, 0, MLA_NOPE).astype(BF16),
                "wuvt": _head_slots(w_ukv, MLA_HEADS, MLA_NOPE + MLA_V, MLA_NOPE,
                                    MLA_NOPE + MLA_V).T.astype(BF16),
                "ekr": jnp.asarray(ekr, BF16),
            }
            n1 = row2(norm1[i])
            f_l, q_l, k_l, vt_l = _in_ab(h_l, mod_l[i], n1, w, tabs_l, tm_l)
            f_c, q_c, k_c, vt_c = _in_ab(h_c, mod_c[i], n1, w, tabs_c, tm_c)
            o_l = _attention(q_l[:, :, None], k_c, vt_c, k_l, vt_l, min(S, 2 * ATTN_QUERIES), ATTN_KEYS)
            yf_l = _fourier_lat(f_l)
            wof = ab_w_o[j][:FNET_WIDTH].astype(BF16)
            woa = jnp.pad(ab_w_o[j][FNET_WIDTH:].reshape(MLA_HEADS, MLA_V, D),
                          ((0, 0), (0, LANES - MLA_V), (0, 0))).reshape(MLA_HEADS * LANES, D).astype(BF16)
            h_l, m_l, aff_l = _out_proj(h_l, mod_l[i], yf_l, wof, o_l, woa, n2, wr_t, tm_l)
            m_c = aff_c = None
            if not last:
                o_c = _attention(q_c[:, :, None], k_c, vt_c, None, None, L, L)
                yf_c = _fourier_ctx(f_c)
                h_c, m_c, aff_c = _out_proj(h_c, mod_c[i], yf_c, wof, o_c, woa, n2, wr_t, tm_c)
        else:
            tabs_l, tabs_c = _rope_tables(S, L, ((0, GQA_HEAD_DIM // 2, "row"),
                                                 (GQA_HEAD_DIM // 2, GQA_HEAD_DIM // 2, "col")))
            nqk = (GQA_HEADS + GQA_KV_HEADS) * GQA_HEAD_DIM
            w = {"wqk": c_w_in[j][:, :nqk].astype(BF16), "wvt": c_w_in[j][:, nqk:].T.astype(BF16),
                 "qg": row2(c_q_gain[j]), "kg": row2(c_k_gain[j])}
            n1 = row2(norm1[i])
            h_l, q_l, k_l, vt_l = _in_c(h_l, moe_l, mod_l[i - 1], mod_l[i], n1, w, tabs_l, tm_l)
            h_c, q_c, k_c, vt_c = _in_c(h_c, moe_c, mod_c[i - 1], mod_c[i], n1, w, tabs_c, tm_c)
            grp = lambda q: q.reshape(B, GQA_KV_HEADS, GQA_GROUP, q.shape[2], LANES)
            o_l = _attention(grp(q_l), k_c, vt_c, k_l, vt_l, min(S, ATTN_QUERIES // GQA_GROUP),
                             ATTN_KEYS)
            woa = c_w_o[j].astype(BF16)
            h_l, m_l, aff_l = _out_proj(h_l, mod_l[i], None, None, o_l, woa, n2, wr_t, tm_l)
            m_c = aff_c = None
            if not last:
                o_c = _attention(grp(q_c), k_c, vt_c, None, None, L, L)
                h_c, m_c, aff_c = _out_proj(h_c, mod_c[i], None, None, o_c, woa, n2, wr_t, tm_c)
        moe_l, moe_c = _moe(m_l, aff_l, m_c, aff_c, i, moe_w_gate, moe_w_up, moe_w_down)
    return _final(h_l, moe_l, mod_l[depth - 1], row2(final_norm), tm_l)
```

```python
import functools
import math

import jax
import jax.numpy as jnp
import numpy as np
from jax import lax
from jax.experimental import pallas as pl
from jax.experimental.pallas import tpu as pltpu

F32 = jnp.float32
BF16 = jnp.bfloat16
I32 = jnp.int32
HIGHEST = lax.Precision.HIGHEST

D_MODEL = 1024
GRID_W = 64
EPS = 1e-6
ROPE_THETA = 10000.0
N_MOD = 6
FNET_GROUPS = 4
FNET_GROUP_DIM = 128
FNET_WIDTH = FNET_GROUPS * FNET_GROUP_DIM
MLA_HEADS = 8
MLA_Q_LORA = 256
MLA_KV_LORA = 128
MLA_NOPE = 64
MLA_ROPE = 32
MLA_V = 64
MLA_QK = MLA_NOPE + MLA_ROPE
GQA_HEADS = 8
GQA_KV_HEADS = 2
GQA_GROUP = GQA_HEADS // GQA_KV_HEADS
GQA_HEAD_DIM = 128
N_EXPERTS = 16
EXPERT_FF = 2048
CAPACITY_FACTOR = 2

LANES = 128
SUB = D_MODEL // LANES
VMEM_LIMIT = 56 * 1024 * 1024
LOG2E = math.log2(math.e)
NT = (((1,), (1,)), ((), ()))


def _params(sem, vmem=None):
    return pltpu.CompilerParams(dimension_semantics=sem, vmem_limit_bytes=vmem)


def _rms(x, g):
    return x * lax.rsqrt(jnp.mean(x * x, axis=-1, keepdims=True) + EPS) * g


def _modulate(h, g, shift, scale):
    return _rms(h, g) * (1.0 + scale) + shift


def _rope(x, cos, sin_fwd, sin_bwd, shift):
    return (x * cos + pltpu.roll(x, LANES - shift, 1) * sin_fwd
            + pltpu.roll(x, shift, 1) * sin_bwd)


def _untile(ref, rows):
    x = ref[rows]
    n = x.shape[0]
    y = pltpu.einshape("grsl->gsrl", x.reshape(n // 8, 8, SUB, LANES))
    return jnp.concatenate([y[:, s].reshape(n, LANES) for s in range(SUB)], axis=1)


def _store_tiled(ref, rows, x):
    n = x.shape[0]
    y = jnp.concatenate([x[:, s * LANES:(s + 1) * LANES].reshape(n // 8, 1, 8, LANES)
                         for s in range(SUB)], axis=1)
    ref[rows] = pltpu.einshape("gsrl->grsl", y).reshape(n, SUB, LANES)


def _ada_kernel(c_ref, w_ref, b_ref, o_ref):
    c = c_ref[...]
    x = c * jax.nn.sigmoid(c)
    o_ref[0] = jnp.dot(x, w_ref[0], precision=HIGHEST, preferred_element_type=F32) + b_ref[0]


def _ada(c8, ada_w, ada_b):
    depth, d, n = ada_w.shape
    tn = 1536
    return pl.pallas_call(
        _ada_kernel,
        out_shape=jax.ShapeDtypeStruct((depth, 8, n), F32),
        grid=(depth, n // tn),
        in_specs=[pl.BlockSpec((8, d), lambda l, j: (0, 0)),
                  pl.BlockSpec((1, d, tn), lambda l, j: (l, 0, j)),
                  pl.BlockSpec((1, 1, tn), lambda l, j: (l, 0, j))],
        out_specs=pl.BlockSpec((1, 8, tn), lambda l, j: (l, 0, j)),
        compiler_params=_params(("arbitrary", "arbitrary"), VMEM_LIMIT),
        name="ada",
    )(c8, ada_w, ada_b.reshape(depth, 1, n))


def _in_ab_kernel(scale, *refs):
    (h_ref, mod_ref, n1_ref, win_ref, qn_ref, wuq_ref, kvn_ref, wuk_ref, wuvt_ref, ekr_ref,
     cos_ref, sf_ref, sb_ref, f_ref, q_ref, k_ref, vt_ref) = refs
    D = D_MODEL
    h = h_ref[0]
    mod = mod_ref[0]
    a = _modulate(h, n1_ref[...], mod[:, 0:D], mod[:, D:2 * D]).astype(BF16)
    p = jnp.dot(a, win_ref[...], preferred_element_type=F32)
    o = FNET_WIDTH
    f_ref[0] = p[:, :o]
    cq = p[:, o:o + MLA_Q_LORA]
    o += MLA_Q_LORA
    ckv = p[:, o:o + MLA_KV_LORA]
    o += MLA_KV_LORA
    kr = p[:, o:o + MLA_ROPE]
    cqn = _rms(cq, qn_ref[...]).astype(BF16)
    ckvn = _rms(ckv, kvn_ref[...]).astype(BF16)
    q = jnp.dot(cqn, wuq_ref[...], preferred_element_type=F32)
    k = (jnp.dot(ckvn, wuk_ref[...], preferred_element_type=F32)
         + jnp.dot(kr.astype(BF16), ekr_ref[...], preferred_element_type=F32))
    vt = lax.dot_general(wuvt_ref[...], ckvn, NT, preferred_element_type=F32)
    cos, sf, sb = cos_ref[...], sf_ref[...], sb_ref[...]
    shift = MLA_ROPE // 4
    for hd in range(MLA_HEADS):
        sl = slice(hd * LANES, (hd + 1) * LANES)
        q_ref[0, hd] = (_rope(q[:, sl], cos, sf, sb, shift) * scale).astype(BF16)
        k_ref[0, hd] = _rope(k[:, sl], cos, sf, sb, shift).astype(BF16)
        vt_ref[0, hd] = vt[sl, :].astype(BF16)


def _in_ab(h, mod, n1, w, tables, tm):
    B, R, D = h.shape
    H = MLA_HEADS
    full = lambda a: pl.BlockSpec(a.shape, lambda b, i: (0,) * a.ndim)
    row = lambda w_: pl.BlockSpec((1, tm, w_), lambda b, i: (b, i, 0))
    tab = pl.BlockSpec((tm, LANES), lambda b, i: (i, 0))
    hd = pl.BlockSpec((1, H, tm, LANES), lambda b, i: (b, 0, i, 0))
    hdt = pl.BlockSpec((1, H, LANES, tm), lambda b, i: (b, 0, 0, i))
    weights = [n1, w["win"], w["qn"], w["wuq"], w["kvn"], w["wuk"], w["wuvt"], w["ekr"]]
    return pl.pallas_call(
        functools.partial(_in_ab_kernel, MLA_QK ** -0.5 * LOG2E),
        out_shape=(jax.ShapeDtypeStruct((B, R, FNET_WIDTH), F32),
                   jax.ShapeDtypeStruct((B, H, R, LANES), BF16),
                   jax.ShapeDtypeStruct((B, H, R, LANES), BF16),
                   jax.ShapeDtypeStruct((B, H, LANES, R), BF16)),
        grid=(B, R // tm),
        in_specs=[row(D), pl.BlockSpec((1, 1, N_MOD * D), lambda b, i: (b, 0, 0))]
        + [full(a) for a in weights] + [tab, tab, tab],
        out_specs=(row(FNET_WIDTH), hd, hd, hdt),
        compiler_params=_params(("arbitrary", "arbitrary"), VMEM_LIMIT),
        name="in_ab",
    )(h, mod, *weights, *tables)


def _in_c_kernel(scale, *refs):
    (h_ref, moe_ref, modp_ref, mod_ref, n1_ref, wqk_ref, wvt_ref, qg_ref, kg_ref,
     cos_ref, sf_ref, sb_ref, h2_ref, q_ref, k_ref, vt_ref) = refs
    D = D_MODEL
    h = h_ref[0] + modp_ref[0][:, 5 * D:6 * D] * _untile(moe_ref.at[0], slice(None))
    h2_ref[0] = h
    mod = mod_ref[0]
    a = _modulate(h, n1_ref[...], mod[:, 0:D], mod[:, D:2 * D]).astype(BF16)
    p = jnp.dot(a, wqk_ref[...], preferred_element_type=F32)
    vt = lax.dot_general(wvt_ref[...], a, NT, preferred_element_type=F32)
    cos, sf, sb = cos_ref[...], sf_ref[...], sb_ref[...]
    shift = GQA_HEAD_DIM // 4
    nq = GQA_HEADS * GQA_HEAD_DIM
    for hd in range(GQA_HEADS):
        x = _rms(p[:, hd * LANES:(hd + 1) * LANES], qg_ref[...])
        q_ref[0, hd] = (_rope(x, cos, sf, sb, shift) * scale).astype(BF16)
    for hd in range(GQA_KV_HEADS):
        x = _rms(p[:, nq + hd * LANES:nq + (hd + 1) * LANES], kg_ref[...])
        k_ref[0, hd] = _rope(x, cos, sf, sb, shift).astype(BF16)
        vt_ref[0, hd] = vt[hd * LANES:(hd + 1) * LANES, :].astype(BF16)


def _in_c(h, moe, mod_prev, mod, n1, w, tables, tm):
    B, R, D = h.shape
    full = lambda a: pl.BlockSpec(a.shape, lambda b, i: (0,) * a.ndim)
    row = lambda w_: pl.BlockSpec((1, tm, w_), lambda b, i: (b, i, 0))
    tiled = pl.BlockSpec((1, tm, SUB, LANES), lambda b, i: (b, i, 0, 0))
    modspec = pl.BlockSpec((1, 1, N_MOD * D), lambda b, i: (b, 0, 0))
    tab = pl.BlockSpec((tm, LANES), lambda b, i: (i, 0))
    hd = lambda n: pl.BlockSpec((1, n, tm, LANES), lambda b, i: (b, 0, i, 0))
    hdt = pl.BlockSpec((1, GQA_KV_HEADS, LANES, tm), lambda b, i: (b, 0, 0, i))
    weights = [n1, w["wqk"], w["wvt"], w["qg"], w["kg"]]
    return pl.pallas_call(
        functools.partial(_in_c_kernel, GQA_HEAD_DIM ** -0.5 * LOG2E),
        out_shape=(jax.ShapeDtypeStruct((B, R, D), F32),
                   jax.ShapeDtypeStruct((B, GQA_HEADS, R, LANES), BF16),
                   jax.ShapeDtypeStruct((B, GQA_KV_HEADS, R, LANES), BF16),
                   jax.ShapeDtypeStruct((B, GQA_KV_HEADS, LANES, R), BF16)),
        grid=(B, R // tm),
        in_specs=[row(D), tiled, modspec, modspec] + [full(a) for a in weights] + [tab, tab, tab],
        out_specs=(row(D), hd(GQA_HEADS), hd(GQA_KV_HEADS), hdt),
        compiler_params=_params(("arbitrary", "arbitrary"), VMEM_LIMIT),
        name="in_c",
    )(h, moe, mod_prev, mod, *weights, *tables)


ATTN_QUERIES = 4096
ATTN_KEYS = 512


def _attn_kernel(n_lat, tk, dv, *refs):
    if n_lat:
        q_ref, kc_ref, vct_ref, kl_ref, vlt_ref, o_ref, acc_ref, s0, s1, p0, p1 = refs
        s_bufs, p_bufs = (s0, s1), (p0, p1)
    else:
        q_ref, kc_ref, vct_ref, o_ref, acc_ref = refs
    G, tq = q_ref.shape[2], q_ref.shape[3]
    q = q_ref[0, 0].reshape(G * tq, LANES)

    def scores(k):
        return lax.dot_general(k, q, NT, preferred_element_type=F32)

    s = scores(kc_ref[0, 0])
    m = jnp.max(s, axis=0, keepdims=True)
    p = jnp.exp2(s - m)
    l = jnp.sum(p, axis=0, keepdims=True)
    if dv < LANES:
        acc_ref[...] = jnp.zeros_like(acc_ref)
    acc_ref[:dv, :] = jnp.dot(vct_ref[0, 0, :dv, :], p.astype(BF16), preferred_element_type=F32)

    if n_lat:
        assert n_lat == 1 or n_lat % 2 == 0

        def chunk(c):
            return pl.ds(pl.multiple_of(c * tk, tk), tk)

        def score_stage(c, slot):
            s_bufs[slot][...] = scores(kl_ref[0, 0, chunk(c), :])

        def softmax_stage(slot, m, l):
            s = s_bufs[slot][...]
            m_new = jnp.maximum(m, jnp.max(s, axis=0, keepdims=True))
            p = jnp.exp2(s - m_new)
            p_bufs[slot][...] = p.astype(BF16)
            alpha = jnp.exp2(m - m_new)
            return m_new, alpha * l + jnp.sum(p, axis=0, keepdims=True), alpha

        def value_stage(c, slot, alpha):
            pv = jnp.dot(vlt_ref[0, 0, :dv, chunk(c)], p_bufs[slot][...],
                         preferred_element_type=F32)
            acc_ref[:dv, :] = alpha * acc_ref[:dv, :] + pv

        score_stage(0, 0)
        m, l, alpha = softmax_stage(0, m, l)
        if n_lat > 1:
            score_stage(1, 1)

            def body(i, carry):
                m, l, alpha = carry
                for slot in (0, 1):
                    c = 2 * i + slot
                    score_stage(c + 2, slot)
                    value_stage(c, slot, alpha)
                    m, l, alpha = softmax_stage(1 - slot, m, l)
                return m, l, alpha

            m, l, alpha = lax.fori_loop(0, (n_lat - 2) // 2, body, (m, l, alpha))
            value_stage(n_lat - 2, 0, alpha)
            m, l, alpha = softmax_stage(1, m, l)
        value_stage(n_lat - 1, (n_lat - 1) % 2, alpha)
    o_ref[0, 0] = (acc_ref[...] / l).T.reshape(G, tq, LANES).astype(BF16)


def _attention(q, kc, vct, kl, vlt, tq, tk, dv):
    B, Hk, G, R, _ = q.shape
    Lc = kc.shape[2]
    n_lat = 0 if kl is None else kl.shape[2] // tk
    qspec = pl.BlockSpec((1, 1, G, tq, LANES), lambda b, h, i: (b, h, 0, i, 0))
    kspec = lambda n: pl.BlockSpec((1, 1, n, LANES), lambda b, h, i: (b, h, 0, 0))
    vspec = lambda n: pl.BlockSpec((1, 1, LANES, n), lambda b, h, i: (b, h, 0, 0))
    ins = [q, kc, vct] + ([kl, vlt] if n_lat else [])
    specs = [qspec, kspec(Lc), vspec(Lc)] + ([kspec(kl.shape[2]), vspec(kl.shape[2])] if n_lat else [])
    return pl.pallas_call(
        functools.partial(_attn_kernel, n_lat, tk, dv),
        out_shape=jax.ShapeDtypeStruct(q.shape, BF16),
        grid=(B, Hk, R // tq),
        in_specs=specs,
        out_specs=qspec,
        scratch_shapes=[pltpu.VMEM((LANES, G * tq), F32)]
        + ([pltpu.VMEM((tk, G * tq), F32)] * 2 + [pltpu.VMEM((tk, G * tq), BF16)] * 2 if n_lat else []),
        compiler_params=_params(("arbitrary", "arbitrary", "arbitrary"), VMEM_LIMIT),
        name="attn",
    )(*ins)


def _dft_mats(n):
    k = np.arange(n, dtype=np.float64)
    ang = 2.0 * np.pi * np.outer(k, k) / n
    return np.cos(ang), np.sin(ang)


def _split(x):
    hi = x.astype(BF16)
    return hi, (x - hi.astype(F32)).astype(BF16)


def _dot3(a, b):
    d = lambda p, q: jnp.dot(p, q, preferred_element_type=F32)
    return d(a[0], b[0]) + d(a[0], b[1]) + d(a[1], b[0])


def _split_const(m):
    return jnp.stack(_split(jnp.asarray(m, F32)))


def _dft1_kernel(x_ref, m1_ref, tw_ref, o_ref):
    s1, ns2 = x_ref.shape[1], x_ref.shape[2]
    for i in range(ns2):
        a = jnp.dot(m1_ref[...], x_ref[0, :, i, :], precision=HIGHEST, preferred_element_type=F32)
        are, aim = a[:s1], a[s1:]
        tre = jnp.tile(tw_ref[0, :, i * LANES:(i + 1) * LANES], (1, FNET_GROUPS))
        tim = jnp.tile(tw_ref[1, :, i * LANES:(i + 1) * LANES], (1, FNET_GROUPS))
        o_ref[0, 0, :, i, :] = are * tre - aim * tim
        o_ref[0, 1, :, i, :] = are * tim + aim * tre


def _dft2_kernel(a_ref, m2_ref, m3_ref, o_ref):
    kb, n2 = a_ref.shape[2], a_ref.shape[3]
    for j in range(kb):
        rhs = jnp.concatenate([a_ref[0, 0, j], a_ref[0, 1, j]], axis=0)
        y = _dot3((m2_ref[0], m2_ref[1]), _split(rhs))
        for g in range(FNET_GROUPS):
            sl = slice(g * LANES, (g + 1) * LANES)
            lhs = jnp.concatenate([y[:n2, sl], y[n2:, sl]], axis=1)
            o_ref[0, :, j, sl] = _dot3(_split(lhs), (m3_ref[0], m3_ref[1]))


def _fourier_lat(f):
    B, S, W = f.shape
    n2 = LANES
    s1 = S // n2
    c1, sn1 = _dft_mats(s1)
    c2, sn2 = _dft_mats(n2)
    cc, sc = _dft_mats(FNET_GROUP_DIM)
    m1 = jnp.asarray(np.concatenate([c1, -sn1], axis=0), F32)
    m2 = _split_const(np.block([[c2, sn2], [-sn2, c2]]))
    norm = 1.0 / math.sqrt(S * FNET_GROUP_DIM)
    m3 = _split_const(np.concatenate([cc, sc], axis=0) * norm)
    ang = 2.0 * np.pi * np.outer(np.arange(s1), np.arange(n2)) / S
    tw = np.stack([np.cos(ang), -np.sin(ang)])
    tw = jnp.asarray(np.repeat(tw[:, :, :, None], LANES, axis=3).reshape(2, s1, n2 * LANES), F32)

    ns2 = 8
    a = pl.pallas_call(
        _dft1_kernel,
        out_shape=jax.ShapeDtypeStruct((B, 2, s1, n2, W), F32),
        grid=(B, n2 // ns2),
        in_specs=[pl.BlockSpec((1, s1, ns2, W), lambda b, j: (b, 0, j, 0)),
                  pl.BlockSpec(m1.shape, lambda b, j: (0, 0)),
                  pl.BlockSpec((2, s1, ns2 * LANES), lambda b, j: (0, 0, j))],
        out_specs=pl.BlockSpec((1, 2, s1, ns2, W), lambda b, j: (b, 0, 0, j, 0)),
        compiler_params=_params(("arbitrary", "arbitrary"), VMEM_LIMIT),
        name="dft1",
    )(f.reshape(B, s1, n2, W), m1, tw)
    kb = min(8, s1)
    y = pl.pallas_call(
        _dft2_kernel,
        out_shape=jax.ShapeDtypeStruct((B, n2, s1, W), F32),
        grid=(B, s1 // kb),
        in_specs=[pl.BlockSpec((1, 2, kb, n2, W), lambda b, j: (b, 0, j, 0, 0)),
                  pl.BlockSpec(m2.shape, lambda b, j: (0, 0, 0)),
                  pl.BlockSpec(m3.shape, lambda b, j: (0, 0, 0))],
        out_specs=pl.BlockSpec((1, n2, kb, W), lambda b, j: (b, 0, j, 0)),
        compiler_params=_params(("arbitrary", "arbitrary"), VMEM_LIMIT),
        name="dft2",
    )(a, m2, m3)
    return y.reshape(B, S, W)


def _dftc_kernel(f_ref, mc_ref, m3_ref, o_ref):
    n = f_ref.shape[1]
    a = _dot3((mc_ref[0], mc_ref[1]), _split(f_ref[0]))
    for g in range(FNET_GROUPS):
        sl = slice(g * LANES, (g + 1) * LANES)
        lhs = jnp.concatenate([a[:n, sl], a[n:, sl]], axis=1)
        o_ref[0, :, sl] = _dot3(_split(lhs), (m3_ref[0], m3_ref[1]))


def _fourier_ctx(f):
    B, L, W = f.shape
    c, s = _dft_mats(L)
    cc, sc = _dft_mats(FNET_GROUP_DIM)
    mc = _split_const(np.concatenate([c, -s], axis=0))
    m3 = _split_const(np.concatenate([cc, sc], axis=0) / math.sqrt(L * FNET_GROUP_DIM))
    return pl.pallas_call(
        _dftc_kernel,
        out_shape=jax.ShapeDtypeStruct((B, L, W), F32),
        grid=(B,),
        in_specs=[pl.BlockSpec((1, L, W), lambda b: (b, 0, 0)),
                  pl.BlockSpec(mc.shape, lambda b: (0, 0, 0)),
                  pl.BlockSpec(m3.shape, lambda b: (0, 0, 0))],
        out_specs=pl.BlockSpec((1, L, W), lambda b: (b, 0, 0)),
        compiler_params=_params(("arbitrary",), VMEM_LIMIT),
        name="dftc",
    )(f, mc, m3)


def _out_kernel(has_f, *refs):
    if has_f:
        (h_ref, mod_ref, yf_ref, wof_ref, o_ref, woa_ref, n2_ref, wr_ref,
         h1_ref, m_ref, aff_ref) = refs
    else:
        (h_ref, mod_ref, o_ref, woa_ref, n2_ref, wr_ref, h1_ref, m_ref, aff_ref) = refs
    D = D_MODEL
    Hk, G = o_ref.shape[1], o_ref.shape[2]
    ocat = jnp.concatenate([o_ref[0, hk, g] for hk in range(Hk) for g in range(G)], axis=1)
    y = jnp.dot(ocat, woa_ref[...], preferred_element_type=F32)
    if has_f:
        y = y + jnp.dot(yf_ref[0].astype(BF16), wof_ref[...], preferred_element_type=F32)
    mod = mod_ref[0]
    h1 = h_ref[0] + mod[:, 2 * D:3 * D] * y
    h1_ref[0] = h1
    m = _modulate(h1, n2_ref[...], mod[:, 3 * D:4 * D], mod[:, 4 * D:5 * D])
    _store_tiled(m_ref.at[0], slice(None), m)
    logit = lax.dot_general(wr_ref[...], m, NT, precision=HIGHEST,
                            preferred_element_type=F32)
    e = jnp.exp(logit - jnp.max(logit, axis=0, keepdims=True))
    aff_ref[0] = e / jnp.sum(e, axis=0, keepdims=True)


def _out_proj(h, mod, yf, wof, o, woa, n2, wr_t, tm):
    B, R, D = h.shape
    _, Hk, G, _, _ = o.shape
    full = lambda a: pl.BlockSpec(a.shape, lambda b, i: (0,) * a.ndim)
    row = lambda w_: pl.BlockSpec((1, tm, w_), lambda b, i: (b, i, 0))
    modspec = pl.BlockSpec((1, 1, N_MOD * D), lambda b, i: (b, 0, 0))
    ospec = pl.BlockSpec((1, Hk, G, tm, LANES), lambda b, i: (b, 0, 0, i, 0))
    has_f = yf is not None
    ins = [h, mod] + ([yf, wof] if has_f else []) + [o, woa, n2, wr_t]
    specs = ([row(D), modspec] + ([row(FNET_WIDTH), full(wof)] if has_f else [])
             + [ospec, full(woa), full(n2), full(wr_t)])
    return pl.pallas_call(
        functools.partial(_out_kernel, has_f),
        out_shape=(jax.ShapeDtypeStruct((B, R, D), F32),
                   jax.ShapeDtypeStruct((B, R, SUB, LANES), F32),
                   jax.ShapeDtypeStruct((B, N_EXPERTS, R), F32)),
        grid=(B, R // tm),
        in_specs=specs,
        out_specs=(row(D), pl.BlockSpec((1, tm, SUB, LANES), lambda b, i: (b, i, 0, 0)),
                   pl.BlockSpec((1, N_EXPERTS, tm), lambda b, i: (b, 0, i))),
        compiler_params=_params(("arbitrary", "arbitrary"), VMEM_LIMIT),
        name="out_proj",
    )(*ins)


COMPACT_UNROLL = 4


def _topk_kernel(cap, aff_ref, u_ref, ones_ref, lmat_ref, lc_ref, lg_ref, cnt_ref, off_ref,
                 pos_ref, ac_ref, lct_ref, lgt_ref):
    a = aff_ref[0]
    E, N = a.shape
    NC = N // LANES
    R = NC * E
    keys = pltpu.bitcast(a, I32)

    def bit_step(i, tau):
        cand = tau | jnp.left_shift(jnp.int32(1), 30 - i)
        cnt = jnp.sum((keys >= cand).astype(I32), axis=1, keepdims=True)
        return jnp.where(cnt >= cap, cand, tau)

    tau = lax.fori_loop(0, 31, bit_step, jnp.zeros((E, 1), I32))
    gt = (keys > tau).astype(F32)
    eq = (keys == tau).astype(F32)
    need = (cap - jnp.sum(gt, axis=1, keepdims=True))

    def chunked(x):
        return jnp.concatenate([x[:, c * LANES:(c + 1) * LANES] for c in range(NC)], axis=0)

    a_c, gt_c, eq_c = chunked(a), chunked(gt), chunked(eq)
    need_c = jnp.tile(need, (NC, 1))

    def prefix(x):
        xb = x.astype(BF16)
        loc = jnp.dot(xb, u_ref[...], preferred_element_type=F32)
        tot = jnp.dot(xb, ones_ref[...], preferred_element_type=F32)
        offs = jnp.dot(lmat_ref[...], tot.astype(BF16), preferred_element_type=F32)
        return loc, tot, offs

    loc, tot, offs = prefix(eq_c)
    sel = jnp.maximum(gt_c, jnp.where(loc + offs < need_c, eq_c, 0.0))
    loc, tot, offs = prefix(sel)
    cnt_ref[0] = tot.astype(I32)
    off_ref[0] = offs.astype(I32)
    Rp = pos_ref.shape[1]
    selpos = jnp.where(sel > 0.0, loc, -1.0)
    if Rp > R:
        selpos = jnp.concatenate([selpos, jnp.full((Rp - R, LANES), -1.0, F32)], axis=0)
        a_c = jnp.concatenate([a_c, jnp.zeros((Rp - R, LANES), F32)], axis=0)
    pos_ref[...] = selpos.T
    ac_ref[...] = a_c.T
    tok = lax.broadcasted_iota(I32, (LANES, Rp), 0).astype(F32)

    def compact(i, carry):
        p = pos_ref[...]
        a_t = ac_ref[...]
        for u in range(COMPACT_UNROLL):
            j = i * COMPACT_UNROLL + u
            hit = p == lax.convert_element_type(j, F32)
            lct_ref[pl.ds(j, 1), :] = jnp.sum(jnp.where(hit, tok, 0.0), axis=0, keepdims=True)
            lgt_ref[pl.ds(j, 1), :] = jnp.sum(jnp.where(hit, a_t, 0.0), axis=0, keepdims=True)
        return carry

    lax.fori_loop(0, LANES // COMPACT_UNROLL, compact, 0)
    lc_ref[0] = lct_ref[...].T[:R].astype(I32)
    lg_ref[0] = lgt_ref[...].T[:R]


def _topk(aff, cap):
    B, E, N = aff.shape
    NC = N // LANES
    R = NC * E
    i = np.arange(LANES)
    u = jnp.asarray(i[:, None] < i[None, :], BF16)
    ones = jnp.ones((LANES, LANES), BF16)
    r = np.arange(R)
    lmat = jnp.asarray((r[:, None] % E == r[None, :] % E) & (r[None, :] // E < r[:, None] // E), BF16)
    full = lambda a: pl.BlockSpec(a.shape, lambda b: (0,) * a.ndim)
    ospec = pl.BlockSpec((1, R, LANES), lambda b: (b, 0, 0))
    sds = lambda dt: jax.ShapeDtypeStruct((B, R, LANES), dt)
    lc, lg, cnt, off = pl.pallas_call(
        functools.partial(_topk_kernel, cap),
        out_shape=(sds(I32), sds(F32), sds(I32), sds(I32)),
        grid=(B,),
        in_specs=[pl.BlockSpec((1, E, N), lambda b: (b, 0, 0)), full(u), full(ones), full(lmat)],
        out_specs=(ospec, ospec, ospec, ospec),
        scratch_shapes=[pltpu.VMEM((LANES, -(-R // LANES) * LANES), F32)] * 4,
        compiler_params=_params(("arbitrary",), VMEM_LIMIT),
        name="topk",
    )(aff, u, ones, lmat)
    by_expert = lambda x: x.reshape(B, NC, E, LANES).transpose(0, 2, 1, 3)
    return (by_expert(lc), by_expert(lg), by_expert(cnt)[..., 0], by_expert(off)[..., 0])


GATHER_ROWS = 256
ROW_UNROLL = 4


def _gather_kernel(cap, cnt_ref, off_ref, lc_ref, m_hbm, o_ref, x_ref, sems):
    e = pl.program_id(0)
    B, NC = lc_ref.shape[0], lc_ref.shape[1]
    for b in range(B):
        def chunk_body(c, carry, b=b):
            base = off_ref[b, e, c] + b * cap

            n = cnt_ref[b, e, c]

            def start_row(r, priority):
                t = c * LANES + lc_ref[b, c, r]
                pltpu.make_async_copy(m_hbm.at[b, t], x_ref.at[base + r],
                                      sems.at[b]).start(priority=priority)

            def pair_body(i, carry):
                start_row(2 * i, 0)
                start_row(2 * i + 1, 1)
                return carry

            lax.fori_loop(0, n // 2, pair_body, carry)

            @pl.when(n % 2 == 1)
            def _():
                start_row(n - 1, 0)

            return carry

        lax.fori_loop(0, NC, chunk_body, 0)
    step = min(GATHER_ROWS, cap)
    for b in range(B):
        done = x_ref.at[pl.ds(b * cap, cap)]
        pltpu.make_async_copy(done, done, sems.at[b]).wait()
        for r0 in range(b * cap, (b + 1) * cap, step):
            rows = slice(r0, r0 + step)
            o_ref[0, rows, :] = _untile(x_ref, rows).astype(BF16)


def _gather(m, lc, cnt, off, cap):
    B, N, sub, _ = m.shape
    E, NC = lc.shape[1], lc.shape[2]
    return pl.pallas_call(
        functools.partial(_gather_kernel, cap),
        out_shape=jax.ShapeDtypeStruct((E, B * cap, sub * LANES), BF16),
        grid_spec=pltpu.PrefetchScalarGridSpec(
            num_scalar_prefetch=2,
            grid=(E,),
            in_specs=[pl.BlockSpec((B, None, NC, LANES), lambda e, *_: (0, e, 0, 0),
                                   memory_space=pltpu.SMEM),
                      pl.BlockSpec(memory_space=pl.ANY)],
            out_specs=pl.BlockSpec((1, B * cap, sub * LANES), lambda e, *_: (e, 0, 0)),
            scratch_shapes=[pltpu.VMEM((B * cap, sub, LANES), m.dtype),
                            pltpu.SemaphoreType.DMA((B,))]),
        compiler_params=_params(("arbitrary",), VMEM_LIMIT),
        name="gather",
    )(cnt, off, lc, m)


def _ffn_kernel(n_parts, chunk_rows, *refs):
    wg_ref, wu_ref, wd_ref = refs[:3]
    x_refs = refs[3:3 + n_parts]
    o_refs = refs[3 + n_parts:3 + 2 * n_parts]
    acc_refs = refs[3 + 2 * n_parts:]
    j = pl.program_id(1)
    last = pl.num_programs(1) - 1
    wg = wg_ref[0].astype(BF16)
    wu = wu_ref[0].astype(BF16)
    wd = wd_ref[0].astype(BF16)
    for x_ref, o_ref, acc_ref, nrows in zip(x_refs, o_refs, acc_refs, chunk_rows):
        for r0 in range(0, x_ref.shape[1], nrows):
            rows = slice(r0, r0 + nrows)
            x = x_ref[0, rows, :]
            a = jnp.dot(x, wg, preferred_element_type=F32)
            u = jnp.dot(x, wu, preferred_element_type=F32)
            hh = (a * jax.nn.sigmoid(a) * u).astype(BF16)
            y = jnp.dot(hh, wd, preferred_element_type=F32)

            @pl.when(j == 0)
            def _():
                acc_ref[rows, :] = y

            @pl.when((j > 0) & (j < last))
            def _():
                acc_ref[rows, :] += y

            @pl.when(j == last)
            def _():
                _store_tiled(o_ref.at[0], rows, acc_ref[rows, :] + y)


def _ffn(xs, layer, w_gate, w_up, w_down, chunk_rows):
    _, E, D, F = w_gate.shape
    tf = 512
    assert F // tf >= 2
    xspec = lambda x: pl.BlockSpec((1, x.shape[1], D), lambda e, j: (e, 0, 0))
    ospec = lambda x: pl.BlockSpec((1, x.shape[1], SUB, LANES), lambda e, j: (e, 0, 0, 0))
    outs = pl.pallas_call(
        functools.partial(_ffn_kernel, len(xs), chunk_rows),
        out_shape=tuple(jax.ShapeDtypeStruct((E, x.shape[1], SUB, LANES), F32) for x in xs),
        grid=(E, F // tf),
        in_specs=[pl.BlockSpec((None, 1, D, tf), lambda e, j: (layer, e, 0, j)),
                  pl.BlockSpec((None, 1, D, tf), lambda e, j: (layer, e, 0, j)),
                  pl.BlockSpec((None, 1, tf, D), lambda e, j: (layer, e, j, 0))]
        + [xspec(x) for x in xs],
        out_specs=tuple(ospec(x) for x in xs),
        scratch_shapes=[pltpu.VMEM((x.shape[1], D), F32) for x in xs],
        compiler_params=_params(("arbitrary", "arbitrary"), VMEM_LIMIT),
        name="ffn",
    )(w_gate, w_up, w_down, *xs)
    return outs


def _combine_kernel(cnt_ref, off_ref, lc_ref, lg_ref, y_ref, o_ref):
    b = pl.program_id(0)
    hf = pl.program_id(1)
    e = pl.program_id(2)
    nch = o_ref.shape[1] // LANES

    @pl.when(e == 0)
    def _():
        o_ref[...] = jnp.zeros_like(o_ref)

    def chunk_body(ci, carry):
        c = hf * nch + ci
        base = off_ref[b, e, c]

        n = cnt_ref[b, e, c]

        def rows_body(r0, width):
            ts = [ci * LANES + lc_ref[c, r0 + u] for u in range(width)]
            vals = [o_ref[0, ts[u]] + lg_ref[c, r0 + u] * y_ref[0, base + r0 + u]
                    for u in range(width)]
            for u in range(width):
                o_ref[0, ts[u]] = vals[u]

        def group_body(i, carry):
            rows_body(i * ROW_UNROLL, ROW_UNROLL)
            return carry

        def tail_body(r, carry):
            rows_body(r, 1)
            return carry

        full = n // ROW_UNROLL
        lax.fori_loop(0, full, group_body, carry)
        return lax.fori_loop(full * ROW_UNROLL, n, tail_body, carry)

    lax.fori_loop(0, nch, chunk_body, 0)


def _combine(y, lc, lg, cnt, off, n_tokens, cap, n_split):
    E = y.shape[0]
    B, _, NC, _ = lc.shape
    nh = n_tokens // n_split
    smem = lambda: pl.BlockSpec((None, None, NC, LANES), lambda b, h, e, *_: (b, e, 0, 0),
                                memory_space=pltpu.SMEM)
    return pl.pallas_call(
        _combine_kernel,
        out_shape=jax.ShapeDtypeStruct((B, n_tokens, SUB, LANES), F32),
        grid_spec=pltpu.PrefetchScalarGridSpec(
            num_scalar_prefetch=2,
            grid=(B, n_split, E),
            in_specs=[smem(), smem(),
                      pl.BlockSpec((1, cap, SUB, LANES), lambda b, h, e, *_: (e, b, 0, 0))],
            out_specs=pl.BlockSpec((1, nh, SUB, LANES), lambda b, h, e, *_: (b, h, 0, 0))),
        compiler_params=_params(("arbitrary", "arbitrary", "arbitrary"), VMEM_LIMIT),
        name="combine",
    )(cnt, off, lc, lg, y)


def _final_kernel(h_ref, moe_ref, mod_ref, g_ref, o_ref):
    D = D_MODEL
    h = h_ref[0] + mod_ref[0][:, 5 * D:6 * D] * _untile(moe_ref.at[0], slice(None))
    o_ref[0] = _rms(h, g_ref[...])


def _final(h, moe, mod, g, tm):
    B, R, D = h.shape
    row = pl.BlockSpec((1, tm, D), lambda b, i: (b, i, 0))
    return pl.pallas_call(
        _final_kernel,
        out_shape=jax.ShapeDtypeStruct((B, R, D), F32),
        grid=(B, R // tm),
        in_specs=[row, pl.BlockSpec((1, tm, SUB, LANES), lambda b, i: (b, i, 0, 0)),
                  pl.BlockSpec((1, 1, N_MOD * D), lambda b, i: (b, 0, 0)),
                  pl.BlockSpec((1, D), lambda b, i: (0, 0))],
        out_specs=row,
        compiler_params=_params(("arbitrary", "arbitrary"), VMEM_LIMIT),
        name="final",
    )(h, moe, mod, g)


def _rope_tables(n_lat, n_ctx, segments):
    t = jnp.arange(n_lat)
    pos = {"row": (t // GRID_W).astype(F32), "col": (t % GRID_W).astype(F32)}
    freq = {"row": np.zeros((2, LANES), np.float32), "col": np.zeros((2, LANES), np.float32)}
    first = np.zeros(LANES, np.float32)
    second = np.zeros(LANES, np.float32)
    for lane0, width, which in segments:
        half = width // 2
        idx = np.arange(half, dtype=np.float32)
        for lo, mask in ((lane0, first), (lane0 + half, second)):
            freq[which][0, lo:lo + half] = idx / half
            freq[which][1, lo:lo + half] = 1.0
            mask[lo:lo + half] = 1.0
    ang = jnp.zeros((n_lat, LANES), F32)
    for which in ("row", "col"):
        inv = (ROPE_THETA ** (-jnp.asarray(freq[which][0]))) * jnp.asarray(freq[which][1])
        ang = ang + pos[which][:, None] * inv[None, :]
    sin = jnp.sin(ang)
    lat = (jnp.cos(ang), -sin * first[None, :], sin * second[None, :])
    ctx = (jnp.ones((n_ctx, LANES), F32), jnp.zeros((n_ctx, LANES), F32), jnp.zeros((n_ctx, LANES), F32))
    return lat, ctx


def _head_slots(w, n_heads, width, lo, hi):
    k = w.shape[0]
    w3 = w.reshape(k, n_heads, width)[:, :, lo:hi]
    return jnp.pad(w3, ((0, 0), (0, 0), (0, LANES - (hi - lo)))).reshape(k, n_heads * LANES)


def _moe(m_l, aff_l, m_c, aff_c, layer, w_gate, w_up, w_down):
    B, S = m_l.shape[:2]
    cap_l = CAPACITY_FACTOR * S // N_EXPERTS
    lc, lg, cnt, off = _topk(aff_l, cap_l)
    xs = [_gather(m_l, lc, cnt, off, cap_l)]
    chunk_rows = [cap_l]
    if m_c is not None:
        L = m_c.shape[1]
        cap_c = CAPACITY_FACTOR * L // N_EXPERTS
        lcc, lgc, cntc, offc = _topk(aff_c, cap_c)
        xs.append(_gather(m_c, lcc, cntc, offc, cap_c))
        chunk_rows.append(B * cap_c)
    ys = _ffn(xs, layer, w_gate, w_up, w_down, tuple(chunk_rows))
    out_l = _combine(ys[0], lc, lg, cnt, off, S, cap_l, 2)
    out_c = None
    if m_c is not None:
        out_c = _combine(ys[1], lcc, lgc, cntc, offc, L, cap_c, 1)
    return out_l, out_c


def kernel(x, c, ctx, c_ctx, ada_w, ada_b, norm1, norm2, ab_w_in, ab_q_norm, ab_w_uq, ab_kv_norm,
           ab_w_ukv, ab_w_o, c_w_in, c_q_gain, c_k_gain, c_w_o, moe_router, moe_w_gate, moe_w_up,
           moe_w_down, final_norm):
    B, S, D = x.shape
    L = ctx.shape[1]
    depth = ada_w.shape[0]
    tm_l, tm_c = 512, L
    row2 = lambda v: v.reshape(1, -1)

    c8 = jnp.zeros((8, D), F32).at[:B].set(c).at[B].set(c_ctx)
    mod = _ada(c8, ada_w, ada_b)
    mod_l = [mod[i, :B][:, None, :] for i in range(depth)]
    mod_c = [jnp.broadcast_to(mod[i, B][None, None, :], (B, 1, N_MOD * D)) for i in range(depth)]

    h_l, h_c = x, ctx
    moe_l = moe_c = None
    for i in range(depth):
        last = i == depth - 1
        j = i // 2
        wr_t = moe_router[i].T
        n2 = row2(norm2[i])
        if i % 2 == 0:
            assert moe_l is None
            tabs_l, tabs_c = _rope_tables(S, L, ((MLA_NOPE, MLA_ROPE // 2, "row"),
                                                 (MLA_NOPE + MLA_ROPE // 2, MLA_ROPE // 2, "col")))
            w_ukv = ab_w_ukv[j]
            ekr = np.zeros((MLA_ROPE, MLA_HEADS * LANES), np.float32)
            for hd in range(MLA_HEADS):
                ekr[np.arange(MLA_ROPE), hd * LANES + MLA_NOPE + np.arange(MLA_ROPE)] = 1.0
            w = {
                "win": ab_w_in[j].astype(BF16),
                "qn": row2(ab_q_norm[j]),
                "wuq": _head_slots(ab_w_uq[j], MLA_HEADS, MLA_QK, 0, MLA_QK).astype(BF16),
                "kvn": row2(ab_kv_norm[j]),
                "wuk": _head_slots(w_ukv, MLA_HEADS, MLA_NOPE + MLA_V, 0, MLA_NOPE).astype(BF16),
                "wuvt": _head_slots(w_ukv, MLA_HEADS, MLA_NOPE + MLA_V, MLA_NOPE,
                                    MLA_NOPE + MLA_V).T.astype(BF16),
                "ekr": jnp.asarray(ekr, BF16),
            }
            n1 = row2(norm1[i])
            f_l, q_l, k_l, vt_l = _in_ab(h_l, mod_l[i], n1, w, tabs_l, tm_l)
            f_c, q_c, k_c, vt_c = _in_ab(h_c, mod_c[i], n1, w, tabs_c, tm_c)
            o_l = _attention(q_l[:, :, None], k_c, vt_c, k_l, vt_l, min(S, ATTN_QUERIES), ATTN_KEYS,
                             MLA_V)
            yf_l = _fourier_lat(f_l)
            wof = ab_w_o[j][:FNET_WIDTH].astype(BF16)
            woa = jnp.pad(ab_w_o[j][FNET_WIDTH:].reshape(MLA_HEADS, MLA_V, D),
                          ((0, 0), (0, LANES - MLA_V), (0, 0))).reshape(MLA_HEADS * LANES, D).astype(BF16)
            h_l, m_l, aff_l = _out_proj(h_l, mod_l[i], yf_l, wof, o_l, woa, n2, wr_t, tm_l)
            m_c = aff_c = None
            if not last:
                o_c = _attention(q_c[:, :, None], k_c, vt_c, None, None, L, L, MLA_V)
                yf_c = _fourier_ctx(f_c)
                h_c, m_c, aff_c = _out_proj(h_c, mod_c[i], yf_c, wof, o_c, woa, n2, wr_t, tm_c)
        else:
            tabs_l, tabs_c = _rope_tables(S, L, ((0, GQA_HEAD_DIM // 2, "row"),
                                                 (GQA_HEAD_DIM // 2, GQA_HEAD_DIM // 2, "col")))
            nqk = (GQA_HEADS + GQA_KV_HEADS) * GQA_HEAD_DIM
            w = {"wqk": c_w_in[j][:, :nqk].astype(BF16), "wvt": c_w_in[j][:, nqk:].T.astype(BF16),
                 "qg": row2(c_q_gain[j]), "kg": row2(c_k_gain[j])}
            n1 = row2(norm1[i])
            h_l, q_l, k_l, vt_l = _in_c(h_l, moe_l, mod_l[i - 1], mod_l[i], n1, w, tabs_l, tm_l)
            h_c, q_c, k_c, vt_c = _in_c(h_c, moe_c, mod_c[i - 1], mod_c[i], n1, w, tabs_c, tm_c)
            grp = lambda q: q.reshape(B, GQA_KV_HEADS, GQA_GROUP, q.shape[2], LANES)
            o_l = _attention(grp(q_l), k_c, vt_c, k_l, vt_l, min(S, ATTN_QUERIES // GQA_GROUP),
                             ATTN_KEYS, GQA_HEAD_DIM)
            woa = c_w_o[j].astype(BF16)
            h_l, m_l, aff_l = _out_proj(h_l, mod_l[i], None, None, o_l, woa, n2, wr_t, tm_l)
            m_c = aff_c = None
            if not last:
                o_c = _attention(grp(q_c), k_c, vt_c, None, None, L, L, GQA_HEAD_DIM)
                h_c, m_c, aff_c = _out_proj(h_c, mod_c[i], None, None, o_c, woa, n2, wr_t, tm_c)
        moe_l, moe_c = _moe(m_l, aff_l, m_c, aff_c, i, moe_w_gate, moe_w_up, moe_w_down)
    return _final(h_l, moe_l, mod_l[depth - 1], row2(final_norm), tm_l)
```

```python
import functools
import math

import jax
import jax.numpy as jnp
import numpy as np
from jax import lax
from jax.experimental import pallas as pl
from jax.experimental.pallas import tpu as pltpu

F32 = jnp.float32
BF16 = jnp.bfloat16
I32 = jnp.int32
HIGHEST = lax.Precision.HIGHEST

D_MODEL = 1024
GRID_W = 64
EPS = 1e-6
ROPE_THETA = 10000.0
N_MOD = 6
FNET_GROUPS = 4
FNET_GROUP_DIM = 128
FNET_WIDTH = FNET_GROUPS * FNET_GROUP_DIM
MLA_HEADS = 8
MLA_Q_LORA = 256
MLA_KV_LORA = 128
MLA_NOPE = 64
MLA_ROPE = 32
MLA_V = 64
MLA_QK = MLA_NOPE + MLA_ROPE
GQA_HEADS = 8
GQA_KV_HEADS = 2
GQA_GROUP = GQA_HEADS // GQA_KV_HEADS
GQA_HEAD_DIM = 128
N_EXPERTS = 16
EXPERT_FF = 2048
CAPACITY_FACTOR = 2

LANES = 128
MLA_VROWS = MLA_V + 16
SUB = D_MODEL // LANES
VMEM_LIMIT = 56 * 1024 * 1024
LOG2E = math.log2(math.e)
NT = (((1,), (1,)), ((), ()))


def _params(sem, vmem=None):
    return pltpu.CompilerParams(dimension_semantics=sem, vmem_limit_bytes=vmem)


def _rms(x, g):
    return x * lax.rsqrt(jnp.mean(x * x, axis=-1, keepdims=True) + EPS) * g


def _modulate(h, g, shift, scale):
    return _rms(h, g) * (1.0 + scale) + shift


def _rope(x, cos, sin_fwd, sin_bwd, shift):
    return (x * cos + pltpu.roll(x, LANES - shift, 1) * sin_fwd
            + pltpu.roll(x, shift, 1) * sin_bwd)


def _untile(ref, rows):
    x = ref[rows]
    n = x.shape[0]
    y = pltpu.einshape("grsl->gsrl", x.reshape(n // 8, 8, SUB, LANES))
    return jnp.concatenate([y[:, s].reshape(n, LANES) for s in range(SUB)], axis=1)


def _store_tiled(ref, rows, x):
    n = x.shape[0]
    y = jnp.concatenate([x[:, s * LANES:(s + 1) * LANES].reshape(n // 8, 1, 8, LANES)
                         for s in range(SUB)], axis=1)
    ref[rows] = pltpu.einshape("gsrl->grsl", y).reshape(n, SUB, LANES)


def _ada_kernel(c_ref, w_ref, b_ref, o_ref):
    c = c_ref[...]
    x = c * jax.nn.sigmoid(c)
    o_ref[0] = jnp.dot(x, w_ref[0], precision=HIGHEST, preferred_element_type=F32) + b_ref[0]


def _ada(c8, ada_w, ada_b):
    depth, d, n = ada_w.shape
    tn = 1536
    return pl.pallas_call(
        _ada_kernel,
        out_shape=jax.ShapeDtypeStruct((depth, 8, n), F32),
        grid=(depth, n // tn),
        in_specs=[pl.BlockSpec((8, d), lambda l, j: (0, 0)),
                  pl.BlockSpec((1, d, tn), lambda l, j: (l, 0, j)),
                  pl.BlockSpec((1, 1, tn), lambda l, j: (l, 0, j))],
        out_specs=pl.BlockSpec((1, 8, tn), lambda l, j: (l, 0, j)),
        compiler_params=_params(("arbitrary", "arbitrary"), VMEM_LIMIT),
        name="ada",
    )(c8, ada_w, ada_b.reshape(depth, 1, n))


def _in_ab_kernel(scale, *refs):
    (h_ref, mod_ref, n1_ref, win_ref, qn_ref, wuq_ref, kvn_ref, wuk_ref, wuvt_ref, ekr_ref,
     cos_ref, sf_ref, sb_ref, f_ref, q_ref, k_ref, vt_ref) = refs
    D = D_MODEL
    h = h_ref[0]
    mod = mod_ref[0]
    a = _modulate(h, n1_ref[...], mod[:, 0:D], mod[:, D:2 * D]).astype(BF16)
    p = jnp.dot(a, win_ref[...], preferred_element_type=F32)
    o = FNET_WIDTH
    f_ref[0] = p[:, :o]
    cq = p[:, o:o + MLA_Q_LORA]
    o += MLA_Q_LORA
    ckv = p[:, o:o + MLA_KV_LORA]
    o += MLA_KV_LORA
    kr = p[:, o:o + MLA_ROPE]
    cqn = _rms(cq, qn_ref[...]).astype(BF16)
    ckvn = _rms(ckv, kvn_ref[...]).astype(BF16)
    q = jnp.dot(cqn, wuq_ref[...], preferred_element_type=F32)
    k = (jnp.dot(ckvn, wuk_ref[...], preferred_element_type=F32)
         + jnp.dot(kr.astype(BF16), ekr_ref[...], preferred_element_type=F32))
    vt = lax.dot_general(wuvt_ref[...], ckvn, NT, preferred_element_type=F32)
    cos, sf, sb = cos_ref[...], sf_ref[...], sb_ref[...]
    shift = MLA_ROPE // 4
    ones = lax.broadcasted_iota(I32, (LANES, vt.shape[1]), 0) == MLA_V
    for hd in range(MLA_HEADS):
        sl = slice(hd * LANES, (hd + 1) * LANES)
        q_ref[0, hd] = (_rope(q[:, sl], cos, sf, sb, shift) * scale).astype(BF16)
        k_ref[0, hd] = _rope(k[:, sl], cos, sf, sb, shift).astype(BF16)
        vt_ref[0, hd] = jnp.where(ones, 1.0, vt[sl, :]).astype(BF16)


def _in_ab(h, mod, n1, w, tables, tm):
    B, R, D = h.shape
    H = MLA_HEADS
    full = lambda a: pl.BlockSpec(a.shape, lambda b, i: (0,) * a.ndim)
    row = lambda w_: pl.BlockSpec((1, tm, w_), lambda b, i: (b, i, 0))
    tab = pl.BlockSpec((tm, LANES), lambda b, i: (i, 0))
    hd = pl.BlockSpec((1, H, tm, LANES), lambda b, i: (b, 0, i, 0))
    hdt = pl.BlockSpec((1, H, LANES, tm), lambda b, i: (b, 0, 0, i))
    weights = [n1, w["win"], w["qn"], w["wuq"], w["kvn"], w["wuk"], w["wuvt"], w["ekr"]]
    return pl.pallas_call(
        functools.partial(_in_ab_kernel, MLA_QK ** -0.5 * LOG2E),
        out_shape=(jax.ShapeDtypeStruct((B, R, FNET_WIDTH), F32),
                   jax.ShapeDtypeStruct((B, H, R, LANES), BF16),
                   jax.ShapeDtypeStruct((B, H, R, LANES), BF16),
                   jax.ShapeDtypeStruct((B, H, LANES, R), BF16)),
        grid=(B, R // tm),
        in_specs=[row(D), pl.BlockSpec((1, 1, N_MOD * D), lambda b, i: (b, 0, 0))]
        + [full(a) for a in weights] + [tab, tab, tab],
        out_specs=(row(FNET_WIDTH), hd, hd, hdt),
        compiler_params=_params(("arbitrary", "arbitrary"), VMEM_LIMIT),
        name="in_ab",
    )(h, mod, *weights, *tables)


def _in_c_kernel(scale, *refs):
    (h_ref, moe_ref, modp_ref, mod_ref, n1_ref, wqk_ref, wvt_ref, qg_ref, kg_ref,
     cos_ref, sf_ref, sb_ref, h2_ref, q_ref, k_ref, vt_ref) = refs
    D = D_MODEL
    h = h_ref[0] + modp_ref[0][:, 5 * D:6 * D] * _untile(moe_ref.at[0], slice(None))
    h2_ref[0] = h
    mod = mod_ref[0]
    a = _modulate(h, n1_ref[...], mod[:, 0:D], mod[:, D:2 * D]).astype(BF16)
    p = jnp.dot(a, wqk_ref[...], preferred_element_type=F32)
    vt = lax.dot_general(wvt_ref[...], a, NT, preferred_element_type=F32)
    cos, sf, sb = cos_ref[...], sf_ref[...], sb_ref[...]
    shift = GQA_HEAD_DIM // 4
    nq = GQA_HEADS * GQA_HEAD_DIM
    for hd in range(GQA_HEADS):
        x = _rms(p[:, hd * LANES:(hd + 1) * LANES], qg_ref[...])
        q_ref[0, hd] = (_rope(x, cos, sf, sb, shift) * scale).astype(BF16)
    for hd in range(GQA_KV_HEADS):
        x = _rms(p[:, nq + hd * LANES:nq + (hd + 1) * LANES], kg_ref[...])
        k_ref[0, hd] = _rope(x, cos, sf, sb, shift).astype(BF16)
        vt_ref[0, hd] = vt[hd * LANES:(hd + 1) * LANES, :].astype(BF16)


def _in_c(h, moe, mod_prev, mod, n1, w, tables, tm):
    B, R, D = h.shape
    full = lambda a: pl.BlockSpec(a.shape, lambda b, i: (0,) * a.ndim)
    row = lambda w_: pl.BlockSpec((1, tm, w_), lambda b, i: (b, i, 0))
    tiled = pl.BlockSpec((1, tm, SUB, LANES), lambda b, i: (b, i, 0, 0))
    modspec = pl.BlockSpec((1, 1, N_MOD * D), lambda b, i: (b, 0, 0))
    tab = pl.BlockSpec((tm, LANES), lambda b, i: (i, 0))
    hd = lambda n: pl.BlockSpec((1, n, tm, LANES), lambda b, i: (b, 0, i, 0))
    hdt = pl.BlockSpec((1, GQA_KV_HEADS, LANES, tm), lambda b, i: (b, 0, 0, i))
    weights = [n1, w["wqk"], w["wvt"], w["qg"], w["kg"]]
    return pl.pallas_call(
        functools.partial(_in_c_kernel, GQA_HEAD_DIM ** -0.5 * LOG2E),
        out_shape=(jax.ShapeDtypeStruct((B, R, D), F32),
                   jax.ShapeDtypeStruct((B, GQA_HEADS, R, LANES), BF16),
                   jax.ShapeDtypeStruct((B, GQA_KV_HEADS, R, LANES), BF16),
                   jax.ShapeDtypeStruct((B, GQA_KV_HEADS, LANES, R), BF16)),
        grid=(B, R // tm),
        in_specs=[row(D), tiled, modspec, modspec] + [full(a) for a in weights] + [tab, tab, tab],
        out_specs=(row(D), hd(GQA_HEADS), hd(GQA_KV_HEADS), hdt),
        compiler_params=_params(("arbitrary", "arbitrary"), VMEM_LIMIT),
        name="in_c",
    )(h, moe, mod_prev, mod, *weights, *tables)


ATTN_QUERIES = 4096
ATTN_KEYS = 512


def _attn_kernel(n_lat, tk, dv, ones_row, *refs):
    if n_lat:
        q_ref, kc_ref, vct_ref, kl_ref, vlt_ref, o_ref, acc_ref, s0, s1, p0, p1 = refs
        s_bufs, p_bufs = (s0, s1), (p0, p1)
    else:
        q_ref, kc_ref, vct_ref, o_ref, acc_ref = refs
    G, tq = q_ref.shape[2], q_ref.shape[3]
    q = q_ref[0, 0].reshape(G * tq, LANES)

    def scores(k):
        return lax.dot_general(k, q, NT, preferred_element_type=F32)

    s = scores(kc_ref[0, 0])
    m = jnp.max(s, axis=0, keepdims=True)
    p = jnp.exp2(s - m)
    l = jnp.sum(p, axis=0, keepdims=True) if ones_row is None else jnp.zeros_like(m)
    if dv < LANES:
        acc_ref[...] = jnp.zeros_like(acc_ref)
    acc_ref[:dv, :] = jnp.dot(vct_ref[0, 0, :dv, :], p.astype(BF16), preferred_element_type=F32)

    if n_lat:
        assert n_lat == 1 or n_lat % 2 == 0

        def chunk(c):
            return pl.ds(pl.multiple_of(c * tk, tk), tk)

        def score_stage(c, slot):
            s_bufs[slot][...] = scores(kl_ref[0, 0, chunk(c), :])

        def softmax_stage(slot, m, l):
            s = s_bufs[slot][...]
            m_new = jnp.maximum(m, jnp.max(s, axis=0, keepdims=True))
            p = jnp.exp2(s - m_new)
            p_bufs[slot][...] = p.astype(BF16)
            alpha = jnp.exp2(m - m_new)
            if ones_row is None:
                l = alpha * l + jnp.sum(p, axis=0, keepdims=True)
            return m_new, l, alpha

        def value_stage(c, slot, alpha):
            pv = jnp.dot(vlt_ref[0, 0, :dv, chunk(c)], p_bufs[slot][...],
                         preferred_element_type=F32)
            acc_ref[:dv, :] = alpha * acc_ref[:dv, :] + pv

        score_stage(0, 0)
        m, l, alpha = softmax_stage(0, m, l)
        if n_lat > 1:
            score_stage(1, 1)

            def body(i, carry):
                m, l, alpha = carry
                for slot in (0, 1):
                    c = 2 * i + slot
                    score_stage(c + 2, slot)
                    value_stage(c, slot, alpha)
                    m, l, alpha = softmax_stage(1 - slot, m, l)
                return m, l, alpha

            m, l, alpha = lax.fori_loop(0, (n_lat - 2) // 2, body, (m, l, alpha))
            value_stage(n_lat - 2, 0, alpha)
            m, l, alpha = softmax_stage(1, m, l)
        value_stage(n_lat - 1, (n_lat - 1) % 2, alpha)
    if ones_row is not None:
        l = acc_ref[ones_row:ones_row + 1, :]
    o_ref[0, 0] = (acc_ref[...] / l).T.reshape(G, tq, LANES).astype(BF16)


def _attention(q, kc, vct, kl, vlt, tq, tk, dv, ones_row=None):
    B, Hk, G, R, _ = q.shape
    Lc = kc.shape[2]
    n_lat = 0 if kl is None else kl.shape[2] // tk
    qspec = pl.BlockSpec((1, 1, G, tq, LANES), lambda b, h, i: (b, h, 0, i, 0))
    kspec = lambda n: pl.BlockSpec((1, 1, n, LANES), lambda b, h, i: (b, h, 0, 0))
    vspec = lambda n: pl.BlockSpec((1, 1, LANES, n), lambda b, h, i: (b, h, 0, 0))
    ins = [q, kc, vct] + ([kl, vlt] if n_lat else [])
    specs = [qspec, kspec(Lc), vspec(Lc)] + ([kspec(kl.shape[2]), vspec(kl.shape[2])] if n_lat else [])
    return pl.pallas_call(
        functools.partial(_attn_kernel, n_lat, tk, dv, ones_row),
        out_shape=jax.ShapeDtypeStruct(q.shape, BF16),
        grid=(B, Hk, R // tq),
        in_specs=specs,
        out_specs=qspec,
        scratch_shapes=[pltpu.VMEM((LANES, G * tq), F32)]
        + ([pltpu.VMEM((tk, G * tq), F32)] * 2 + [pltpu.VMEM((tk, G * tq), BF16)] * 2 if n_lat else []),
        compiler_params=_params(("arbitrary", "arbitrary", "arbitrary"), VMEM_LIMIT),
        name="attn",
    )(*ins)


def _dft_mats(n):
    k = np.arange(n, dtype=np.float64)
    ang = 2.0 * np.pi * np.outer(k, k) / n
    return np.cos(ang), np.sin(ang)


def _split(x):
    hi = x.astype(BF16)
    return hi, (x - hi.astype(F32)).astype(BF16)


def _dot3(a, b):
    d = lambda p, q: jnp.dot(p, q, preferred_element_type=F32)
    return d(a[0], b[0]) + d(a[0], b[1]) + d(a[1], b[0])


def _split_const(m):
    return jnp.stack(_split(jnp.asarray(m, F32)))


def _dft1_kernel(x_ref, m1_ref, tw_ref, o_ref):
    s1, ns2 = x_ref.shape[1], x_ref.shape[2]
    for i in range(ns2):
        a = jnp.dot(m1_ref[...], x_ref[0, :, i, :], precision=HIGHEST, preferred_element_type=F32)
        are, aim = a[:s1], a[s1:]
        tre = jnp.tile(tw_ref[0, :, i * LANES:(i + 1) * LANES], (1, FNET_GROUPS))
        tim = jnp.tile(tw_ref[1, :, i * LANES:(i + 1) * LANES], (1, FNET_GROUPS))
        o_ref[0, 0, :, i, :] = are * tre - aim * tim
        o_ref[0, 1, :, i, :] = are * tim + aim * tre


def _dft2_kernel(a_ref, m2_ref, m3_ref, o_ref):
    kb, n2 = a_ref.shape[2], a_ref.shape[3]
    for j in range(kb):
        rhs = jnp.concatenate([a_ref[0, 0, j], a_ref[0, 1, j]], axis=0)
        y = _dot3((m2_ref[0], m2_ref[1]), _split(rhs))
        for g in range(FNET_GROUPS):
            sl = slice(g * LANES, (g + 1) * LANES)
            lhs = jnp.concatenate([y[:n2, sl], y[n2:, sl]], axis=1)
            o_ref[0, :, j, sl] = _dot3(_split(lhs), (m3_ref[0], m3_ref[1]))


def _fourier_lat(f):
    B, S, W = f.shape
    n2 = LANES
    s1 = S // n2
    c1, sn1 = _dft_mats(s1)
    c2, sn2 = _dft_mats(n2)
    cc, sc = _dft_mats(FNET_GROUP_DIM)
    m1 = jnp.asarray(np.concatenate([c1, -sn1], axis=0), F32)
    m2 = _split_const(np.block([[c2, sn2], [-sn2, c2]]))
    norm = 1.0 / math.sqrt(S * FNET_GROUP_DIM)
    m3 = _split_const(np.concatenate([cc, sc], axis=0) * norm)
    ang = 2.0 * np.pi * np.outer(np.arange(s1), np.arange(n2)) / S
    tw = np.stack([np.cos(ang), -np.sin(ang)])
    tw = jnp.asarray(np.repeat(tw[:, :, :, None], LANES, axis=3).reshape(2, s1, n2 * LANES), F32)

    ns2 = 8
    a = pl.pallas_call(
        _dft1_kernel,
        out_shape=jax.ShapeDtypeStruct((B, 2, s1, n2, W), F32),
        grid=(B, n2 // ns2),
        in_specs=[pl.BlockSpec((1, s1, ns2, W), lambda b, j: (b, 0, j, 0)),
                  pl.BlockSpec(m1.shape, lambda b, j: (0, 0)),
                  pl.BlockSpec((2, s1, ns2 * LANES), lambda b, j: (0, 0, j))],
        out_specs=pl.BlockSpec((1, 2, s1, ns2, W), lambda b, j: (b, 0, 0, j, 0)),
        compiler_params=_params(("arbitrary", "arbitrary"), VMEM_LIMIT),
        name="dft1",
    )(f.reshape(B, s1, n2, W), m1, tw)
    kb = min(8, s1)
    y = pl.pallas_call(
        _dft2_kernel,
        out_shape=jax.ShapeDtypeStruct((B, n2, s1, W), F32),
        grid=(B, s1 // kb),
        in_specs=[pl.BlockSpec((1, 2, kb, n2, W), lambda b, j: (b, 0, j, 0, 0)),
                  pl.BlockSpec(m2.shape, lambda b, j: (0, 0, 0)),
                  pl.BlockSpec(m3.shape, lambda b, j: (0, 0, 0))],
        out_specs=pl.BlockSpec((1, n2, kb, W), lambda b, j: (b, 0, j, 0)),
        compiler_params=_params(("arbitrary", "arbitrary"), VMEM_LIMIT),
        name="dft2",
    )(a, m2, m3)
    return y.reshape(B, S, W)


def _dftc_kernel(f_ref, mc_ref, m3_ref, o_ref):
    n = f_ref.shape[1]
    a = _dot3((mc_ref[0], mc_ref[1]), _split(f_ref[0]))
    for g in range(FNET_GROUPS):
        sl = slice(g * LANES, (g + 1) * LANES)
        lhs = jnp.concatenate([a[:n, sl], a[n:, sl]], axis=1)
        o_ref[0, :, sl] = _dot3(_split(lhs), (m3_ref[0], m3_ref[1]))


def _fourier_ctx(f):
    B, L, W = f.shape
    c, s = _dft_mats(L)
    cc, sc = _dft_mats(FNET_GROUP_DIM)
    mc = _split_const(np.concatenate([c, -s], axis=0))
    m3 = _split_const(np.concatenate([cc, sc], axis=0) / math.sqrt(L * FNET_GROUP_DIM))
    return pl.pallas_call(
        _dftc_kernel,
        out_shape=jax.ShapeDtypeStruct((B, L, W), F32),
        grid=(B,),
        in_specs=[pl.BlockSpec((1, L, W), lambda b: (b, 0, 0)),
                  pl.BlockSpec(mc.shape, lambda b: (0, 0, 0)),
                  pl.BlockSpec(m3.shape, lambda b: (0, 0, 0))],
        out_specs=pl.BlockSpec((1, L, W), lambda b: (b, 0, 0)),
        compiler_params=_params(("arbitrary",), VMEM_LIMIT),
        name="dftc",
    )(f, mc, m3)


def _out_kernel(has_f, *refs):
    if has_f:
        (h_ref, mod_ref, yf_ref, wof_ref, o_ref, woa_ref, n2_ref, wr_ref,
         h1_ref, m_ref, aff_ref) = refs
    else:
        (h_ref, mod_ref, o_ref, woa_ref, n2_ref, wr_ref, h1_ref, m_ref, aff_ref) = refs
    D = D_MODEL
    Hk, G = o_ref.shape[1], o_ref.shape[2]
    ocat = jnp.concatenate([o_ref[0, hk, g] for hk in range(Hk) for g in range(G)], axis=1)
    y = jnp.dot(ocat, woa_ref[...], preferred_element_type=F32)
    if has_f:
        y = y + jnp.dot(yf_ref[0].astype(BF16), wof_ref[...], preferred_element_type=F32)
    mod = mod_ref[0]
    h1 = h_ref[0] + mod[:, 2 * D:3 * D] * y
    h1_ref[0] = h1
    m = _modulate(h1, n2_ref[...], mod[:, 3 * D:4 * D], mod[:, 4 * D:5 * D])
    _store_tiled(m_ref.at[0], slice(None), m)
    logit = lax.dot_general(wr_ref[...], m, NT, precision=HIGHEST,
                            preferred_element_type=F32)
    e = jnp.exp(logit - jnp.max(logit, axis=0, keepdims=True))
    aff_ref[0] = e / jnp.sum(e, axis=0, keepdims=True)


def _out_proj(h, mod, yf, wof, o, woa, n2, wr_t, tm):
    B, R, D = h.shape
    _, Hk, G, _, _ = o.shape
    full = lambda a: pl.BlockSpec(a.shape, lambda b, i: (0,) * a.ndim)
    row = lambda w_: pl.BlockSpec((1, tm, w_), lambda b, i: (b, i, 0))
    modspec = pl.BlockSpec((1, 1, N_MOD * D), lambda b, i: (b, 0, 0))
    ospec = pl.BlockSpec((1, Hk, G, tm, LANES), lambda b, i: (b, 0, 0, i, 0))
    has_f = yf is not None
    ins = [h, mod] + ([yf, wof] if has_f else []) + [o, woa, n2, wr_t]
    specs = ([row(D), modspec] + ([row(FNET_WIDTH), full(wof)] if has_f else [])
             + [ospec, full(woa), full(n2), full(wr_t)])
    return pl.pallas_call(
        functools.partial(_out_kernel, has_f),
        out_shape=(jax.ShapeDtypeStruct((B, R, D), F32),
                   jax.ShapeDtypeStruct((B, R, SUB, LANES), F32),
                   jax.ShapeDtypeStruct((B, N_EXPERTS, R), F32)),
        grid=(B, R // tm),
        in_specs=specs,
        out_specs=(row(D), pl.BlockSpec((1, tm, SUB, LANES), lambda b, i: (b, i, 0, 0)),
                   pl.BlockSpec((1, N_EXPERTS, tm), lambda b, i: (b, 0, i))),
        compiler_params=_params(("arbitrary", "arbitrary"), VMEM_LIMIT),
        name="out_proj",
    )(*ins)


COMPACT_UNROLL = 4


def _topk_kernel(cap, aff_ref, u_ref, ones_ref, lmat_ref, lc_ref, lg_ref, cnt_ref, off_ref,
                 pos_ref, ac_ref, lct_ref, lgt_ref):
    a = aff_ref[0]
    E, N = a.shape
    NC = N // LANES
    R = NC * E
    keys = pltpu.bitcast(a, I32)

    def bit_step(i, tau):
        cand = tau | jnp.left_shift(jnp.int32(1), 30 - i)
        cnt = jnp.sum((keys >= cand).astype(I32), axis=1, keepdims=True)
        return jnp.where(cnt >= cap, cand, tau)

    tau = lax.fori_loop(0, 31, bit_step, jnp.zeros((E, 1), I32))
    gt = (keys > tau).astype(F32)
    eq = (keys == tau).astype(F32)
    need = (cap - jnp.sum(gt, axis=1, keepdims=True))

    def chunked(x):
        return jnp.concatenate([x[:, c * LANES:(c + 1) * LANES] for c in range(NC)], axis=0)

    a_c, gt_c, eq_c = chunked(a), chunked(gt), chunked(eq)
    need_c = jnp.tile(need, (NC, 1))

    def prefix(x):
        xb = x.astype(BF16)
        loc = jnp.dot(xb, u_ref[...], preferred_element_type=F32)
        tot = jnp.dot(xb, ones_ref[...], preferred_element_type=F32)
        offs = jnp.dot(lmat_ref[...], tot.astype(BF16), preferred_element_type=F32)
        return loc, tot, offs

    loc, tot, offs = prefix(eq_c)
    sel = jnp.maximum(gt_c, jnp.where(loc + offs < need_c, eq_c, 0.0))
    loc, tot, offs = prefix(sel)
    cnt_ref[0] = tot.astype(I32)
    off_ref[0] = offs.astype(I32)
    Rp = pos_ref.shape[1]
    selpos = jnp.where(sel > 0.0, loc, -1.0)
    if Rp > R:
        selpos = jnp.concatenate([selpos, jnp.full((Rp - R, LANES), -1.0, F32)], axis=0)
        a_c = jnp.concatenate([a_c, jnp.zeros((Rp - R, LANES), F32)], axis=0)
    pos_ref[...] = selpos.T
    ac_ref[...] = a_c.T
    tok = lax.broadcasted_iota(I32, (LANES, Rp), 0).astype(F32)

    def compact(i, carry):
        p = pos_ref[...]
        a_t = ac_ref[...]
        for u in range(COMPACT_UNROLL):
            j = i * COMPACT_UNROLL + u
            hit = p == lax.convert_element_type(j, F32)
            lct_ref[pl.ds(j, 1), :] = jnp.sum(jnp.where(hit, tok, 0.0), axis=0, keepdims=True)
            lgt_ref[pl.ds(j, 1), :] = jnp.sum(jnp.where(hit, a_t, 0.0), axis=0, keepdims=True)
        return carry

    lax.fori_loop(0, LANES // COMPACT_UNROLL, compact, 0)
    lc_ref[0] = lct_ref[...].T[:R].astype(I32)
    lg_ref[0] = lgt_ref[...].T[:R]


def _topk(aff, cap):
    B, E, N = aff.shape
    NC = N // LANES
    R = NC * E
    i = np.arange(LANES)
    u = jnp.asarray(i[:, None] < i[None, :], BF16)
    ones = jnp.ones((LANES, LANES), BF16)
    r = np.arange(R)
    lmat = jnp.asarray((r[:, None] % E == r[None, :] % E) & (r[None, :] // E < r[:, None] // E), BF16)
    full = lambda a: pl.BlockSpec(a.shape, lambda b: (0,) * a.ndim)
    ospec = pl.BlockSpec((1, R, LANES), lambda b: (b, 0, 0))
    sds = lambda dt: jax.ShapeDtypeStruct((B, R, LANES), dt)
    lc, lg, cnt, off = pl.pallas_call(
        functools.partial(_topk_kernel, cap),
        out_shape=(sds(I32), sds(F32), sds(I32), sds(I32)),
        grid=(B,),
        in_specs=[pl.BlockSpec((1, E, N), lambda b: (b, 0, 0)), full(u), full(ones), full(lmat)],
        out_specs=(ospec, ospec, ospec, ospec),
        scratch_shapes=[pltpu.VMEM((LANES, -(-R // LANES) * LANES), F32)] * 4,
        compiler_params=_params(("arbitrary",), VMEM_LIMIT),
        name="topk",
    )(aff, u, ones, lmat)
    by_expert = lambda x: x.reshape(B, NC, E, LANES).transpose(0, 2, 1, 3)
    return (by_expert(lc), by_expert(lg), by_expert(cnt)[..., 0], by_expert(off)[..., 0])


GATHER_ROWS = 256
ROW_UNROLL = 4


def _gather_kernel(cap, cnt_ref, off_ref, lc_ref, m_hbm, o_ref, x_ref, sems):
    e = pl.program_id(0)
    B, NC = lc_ref.shape[0], lc_ref.shape[1]
    for b in range(B):
        def chunk_body(c, carry, b=b):
            base = off_ref[b, e, c] + b * cap

            n = cnt_ref[b, e, c]

            def start_row(r, priority):
                t = c * LANES + lc_ref[b, c, r]
                pltpu.make_async_copy(m_hbm.at[b, t], x_ref.at[base + r],
                                      sems.at[b]).start(priority=priority)

            def pair_body(i, carry):
                start_row(2 * i, 0)
                start_row(2 * i + 1, 1)
                return carry

            lax.fori_loop(0, n // 2, pair_body, carry)

            @pl.when(n % 2 == 1)
            def _():
                start_row(n - 1, 0)

            return carry

        lax.fori_loop(0, NC, chunk_body, 0)
    step = min(GATHER_ROWS, cap)
    for b in range(B):
        done = x_ref.at[pl.ds(b * cap, cap)]
        pltpu.make_async_copy(done, done, sems.at[b]).wait()
        for r0 in range(b * cap, (b + 1) * cap, step):
            rows = slice(r0, r0 + step)
            o_ref[0, rows, :] = _untile(x_ref, rows).astype(BF16)


def _gather(m, lc, cnt, off, cap):
    B, N, sub, _ = m.shape
    E, NC = lc.shape[1], lc.shape[2]
    return pl.pallas_call(
        functools.partial(_gather_kernel, cap),
        out_shape=jax.ShapeDtypeStruct((E, B * cap, sub * LANES), BF16),
        grid_spec=pltpu.PrefetchScalarGridSpec(
            num_scalar_prefetch=2,
            grid=(E,),
            in_specs=[pl.BlockSpec((B, None, NC, LANES), lambda e, *_: (0, e, 0, 0),
                                   memory_space=pltpu.SMEM),
                      pl.BlockSpec(memory_space=pl.ANY)],
            out_specs=pl.BlockSpec((1, B * cap, sub * LANES), lambda e, *_: (e, 0, 0)),
            scratch_shapes=[pltpu.VMEM((B * cap, sub, LANES), m.dtype),
                            pltpu.SemaphoreType.DMA((B,))]),
        compiler_params=_params(("arbitrary",), VMEM_LIMIT),
        name="gather",
    )(cnt, off, lc, m)


def _ffn_kernel(n_parts, chunk_rows, *refs):
    wg_ref, wu_ref, wd_ref = refs[:3]
    x_refs = refs[3:3 + n_parts]
    o_refs = refs[3 + n_parts:3 + 2 * n_parts]
    acc_refs = refs[3 + 2 * n_parts:]
    j = pl.program_id(1)
    last = pl.num_programs(1) - 1
    wg = wg_ref[0].astype(BF16)
    wu = wu_ref[0].astype(BF16)
    wd = wd_ref[0].astype(BF16)
    chunks = [(x_ref, o_ref, acc_ref, slice(r0, r0 + nrows))
              for x_ref, o_ref, acc_ref, nrows in zip(x_refs, o_refs, acc_refs, chunk_rows)
              for r0 in range(0, x_ref.shape[1], nrows)]

    @pl.when(j == 0)
    def _():
        for acc_ref in acc_refs:
            acc_ref[...] = jnp.zeros_like(acc_ref)

    for x_ref, _, acc_ref, rows in chunks:
        x = x_ref[0, rows, :]
        a = jnp.dot(x, wg, preferred_element_type=F32)
        u = jnp.dot(x, wu, preferred_element_type=F32)
        hh = (a * jax.nn.sigmoid(a) * u).astype(BF16)
        acc_ref[rows, :] += jnp.dot(hh, wd, preferred_element_type=F32)

    @pl.when(j == last)
    def _():
        for _, o_ref, acc_ref, rows in chunks:
            _store_tiled(o_ref.at[0], rows, acc_ref[rows, :])


def _ffn(xs, layer, w_gate, w_up, w_down, chunk_rows):
    _, E, D, F = w_gate.shape
    tf = 512
    assert F // tf >= 2
    xspec = lambda x: pl.BlockSpec((1, x.shape[1], D), lambda e, j: (e, 0, 0))
    ospec = lambda x: pl.BlockSpec((1, x.shape[1], SUB, LANES), lambda e, j: (e, 0, 0, 0))
    outs = pl.pallas_call(
        functools.partial(_ffn_kernel, len(xs), chunk_rows),
        out_shape=tuple(jax.ShapeDtypeStruct((E, x.shape[1], SUB, LANES), F32) for x in xs),
        grid=(E, F // tf),
        in_specs=[pl.BlockSpec((None, 1, D, tf), lambda e, j: (layer, e, 0, j)),
                  pl.BlockSpec((None, 1, D, tf), lambda e, j: (layer, e, 0, j)),
                  pl.BlockSpec((None, 1, tf, D), lambda e, j: (layer, e, j, 0))]
        + [xspec(x) for x in xs],
        out_specs=tuple(ospec(x) for x in xs),
        scratch_shapes=[pltpu.VMEM((x.shape[1], D), F32) for x in xs],
        compiler_params=_params(("arbitrary", "arbitrary"), VMEM_LIMIT),
        name="ffn",
    )(w_gate, w_up, w_down, *xs)
    return outs


def _combine_kernel(cnt_ref, off_ref, lc_ref, lg_ref, y_ref, o_ref):
    b = pl.program_id(0)
    hf = pl.program_id(1)
    e = pl.program_id(2)
    nch = o_ref.shape[1] // LANES

    @pl.when(e == 0)
    def _():
        o_ref[...] = jnp.zeros_like(o_ref)

    def chunk_body(ci, carry):
        c = hf * nch + ci
        base = off_ref[b, e, c]

        n = cnt_ref[b, e, c]

        def rows_body(r0, width):
            ts = [ci * LANES + lc_ref[c, r0 + u] for u in range(width)]
            vals = [o_ref[0, ts[u]] + lg_ref[c, r0 + u] * y_ref[0, base + r0 + u]
                    for u in range(width)]
            for u in range(width):
                o_ref[0, ts[u]] = vals[u]

        def group_body(i, carry):
            rows_body(i * ROW_UNROLL, ROW_UNROLL)
            return carry

        def tail_body(r, carry):
            rows_body(r, 1)
            return carry

        full = n // ROW_UNROLL
        lax.fori_loop(0, full, group_body, carry)
        return lax.fori_loop(full * ROW_UNROLL, n, tail_body, carry)

    lax.fori_loop(0, nch, chunk_body, 0)


def _combine(y, lc, lg, cnt, off, n_tokens, cap, n_split):
    E = y.shape[0]
    B, _, NC, _ = lc.shape
    nh = n_tokens // n_split
    smem = lambda: pl.BlockSpec((None, None, NC, LANES), lambda b, h, e, *_: (b, e, 0, 0),
                                memory_space=pltpu.SMEM)
    return pl.pallas_call(
        _combine_kernel,
        out_shape=jax.ShapeDtypeStruct((B, n_tokens, SUB, LANES), F32),
        grid_spec=pltpu.PrefetchScalarGridSpec(
            num_scalar_prefetch=2,
            grid=(B, n_split, E),
            in_specs=[smem(), smem(),
                      pl.BlockSpec((1, cap, SUB, LANES), lambda b, h, e, *_: (e, b, 0, 0))],
            out_specs=pl.BlockSpec((1, nh, SUB, LANES), lambda b, h, e, *_: (b, h, 0, 0))),
        compiler_params=_params(("arbitrary", "arbitrary", "arbitrary"), VMEM_LIMIT),
        name="combine",
    )(cnt, off, lc, lg, y)


def _final_kernel(h_ref, moe_ref, mod_ref, g_ref, o_ref):
    D = D_MODEL
    h = h_ref[0] + mod_ref[0][:, 5 * D:6 * D] * _untile(moe_ref.at[0], slice(None))
    o_ref[0] = _rms(h, g_ref[...])


def _final(h, moe, mod, g, tm):
    B, R, D = h.shape
    row = pl.BlockSpec((1, tm, D), lambda b, i: (b, i, 0))
    return pl.pallas_call(
        _final_kernel,
        out_shape=jax.ShapeDtypeStruct((B, R, D), F32),
        grid=(B, R // tm),
        in_specs=[row, pl.BlockSpec((1, tm, SUB, LANES), lambda b, i: (b, i, 0, 0)),
                  pl.BlockSpec((1, 1, N_MOD * D), lambda b, i: (b, 0, 0)),
                  pl.BlockSpec((1, D), lambda b, i: (0, 0))],
        out_specs=row,
        compiler_params=_params(("arbitrary", "arbitrary"), VMEM_LIMIT),
        name="final",
    )(h, moe, mod, g)


def _rope_tables(n_lat, n_ctx, segments):
    t = jnp.arange(n_lat)
    pos = {"row": (t // GRID_W).astype(F32), "col": (t % GRID_W).astype(F32)}
    freq = {"row": np.zeros((2, LANES), np.float32), "col": np.zeros((2, LANES), np.float32)}
    first = np.zeros(LANES, np.float32)
    second = np.zeros(LANES, np.float32)
    for lane0, width, which in segments:
        half = width // 2
        idx = np.arange(half, dtype=np.float32)
        for lo, mask in ((lane0, first), (lane0 + half, second)):
            freq[which][0, lo:lo + half] = idx / half
            freq[which][1, lo:lo + half] = 1.0
            mask[lo:lo + half] = 1.0
    ang = jnp.zeros((n_lat, LANES), F32)
    for which in ("row", "col"):
        inv = (ROPE_THETA ** (-jnp.asarray(freq[which][0]))) * jnp.asarray(freq[which][1])
        ang = ang + pos[which][:, None] * inv[None, :]
    sin = jnp.sin(ang)
    lat = (jnp.cos(ang), -sin * first[None, :], sin * second[None, :])
    ctx = (jnp.ones((n_ctx, LANES), F32), jnp.zeros((n_ctx, LANES), F32), jnp.zeros((n_ctx, LANES), F32))
    return lat, ctx


def _head_slots(w, n_heads, width, lo, hi):
    k = w.shape[0]
    w3 = w.reshape(k, n_heads, width)[:, :, lo:hi]
    return jnp.pad(w3, ((0, 0), (0, 0), (0, LANES - (hi - lo)))).reshape(k, n_heads * LANES)


def _moe(m_l, aff_l, m_c, aff_c, layer, w_gate, w_up, w_down):
    B, S = m_l.shape[:2]
    cap_l = CAPACITY_FACTOR * S // N_EXPERTS
    lc, lg, cnt, off = _topk(aff_l, cap_l)
    xs = [_gather(m_l, lc, cnt, off, cap_l)]
    chunk_rows = [cap_l]
    if m_c is not None:
        L = m_c.shape[1]
        cap_c = CAPACITY_FACTOR * L // N_EXPERTS
        lcc, lgc, cntc, offc = _topk(aff_c, cap_c)
        xs.append(_gather(m_c, lcc, cntc, offc, cap_c))
        chunk_rows.append(B * cap_c)
    ys = _ffn(xs, layer, w_gate, w_up, w_down, tuple(chunk_rows))
    out_l = _combine(ys[0], lc, lg, cnt, off, S, cap_l, 2)
    out_c = None
    if m_c is not None:
        out_c = _combine(ys[1], lcc, lgc, cntc, offc, L, cap_c, 1)
    return out_l, out_c


def kernel(x, c, ctx, c_ctx, ada_w, ada_b, norm1, norm2, ab_w_in, ab_q_norm, ab_w_uq, ab_kv_norm,
           ab_w_ukv, ab_w_o, c_w_in, c_q_gain, c_k_gain, c_w_o, moe_router, moe_w_gate, moe_w_up,
           moe_w_down, final_norm):
    B, S, D = x.shape
    L = ctx.shape[1]
    depth = ada_w.shape[0]
    tm_l, tm_c = 512, L
    row2 = lambda v: v.reshape(1, -1)

    c8 = jnp.zeros((8, D), F32).at[:B].set(c).at[B].set(c_ctx)
    mod = _ada(c8, ada_w, ada_b)
    mod_l = [mod[i, :B][:, None, :] for i in range(depth)]
    mod_c = [jnp.broadcast_to(mod[i, B][None, None, :], (B, 1, N_MOD * D)) for i in range(depth)]

    h_l, h_c = x, ctx
    moe_l = moe_c = None
    for i in range(depth):
        last = i == depth - 1
        j = i // 2
        wr_t = moe_router[i].T
        n2 = row2(norm2[i])
        if i % 2 == 0:
            assert moe_l is None
            tabs_l, tabs_c = _rope_tables(S, L, ((MLA_NOPE, MLA_ROPE // 2, "row"),
                                                 (MLA_NOPE + MLA_ROPE // 2, MLA_ROPE // 2, "col")))
            w_ukv = ab_w_ukv[j]
            ekr = np.zeros((MLA_ROPE, MLA_HEADS * LANES), np.float32)
            for hd in range(MLA_HEADS):
                ekr[np.arange(MLA_ROPE), hd * LANES + MLA_NOPE + np.arange(MLA_ROPE)] = 1.0
            w = {
                "win": ab_w_in[j].astype(BF16),
                "qn": row2(ab_q_norm[j]),
                "wuq": _head_slots(ab_w_uq[j], MLA_HEADS, MLA_QK, 0, MLA_QK).astype(BF16),
                "kvn": row2(ab_kv_norm[j]),
                "wuk": _head_slots(w_ukv, MLA_HEADS, MLA_NOPE + MLA_V, 0, MLA_NOPE).astype(BF16),
                "wuvt": _head_slots(w_ukv, MLA_HEADS, MLA_NOPE + MLA_V, MLA_NOPE,
                                    MLA_NOPE + MLA_V).T.astype(BF16),
                "ekr": jnp.asarray(ekr, BF16),
            }
            n1 = row2(norm1[i])
            f_l, q_l, k_l, vt_l = _in_ab(h_l, mod_l[i], n1, w, tabs_l, tm_l)
            f_c, q_c, k_c, vt_c = _in_ab(h_c, mod_c[i], n1, w, tabs_c, tm_c)
            o_l = _attention(q_l[:, :, None], k_c, vt_c, k_l, vt_l, min(S, ATTN_QUERIES), ATTN_KEYS,
                             MLA_VROWS, MLA_V)
            yf_l = _fourier_lat(f_l)
            wof = ab_w_o[j][:FNET_WIDTH].astype(BF16)
            woa = jnp.pad(ab_w_o[j][FNET_WIDTH:].reshape(MLA_HEADS, MLA_V, D),
                          ((0, 0), (0, LANES - MLA_V), (0, 0))).reshape(MLA_HEADS * LANES, D).astype(BF16)
            h_l, m_l, aff_l = _out_proj(h_l, mod_l[i], yf_l, wof, o_l, woa, n2, wr_t, tm_l)
            m_c = aff_c = None
            if not last:
                o_c = _attention(q_c[:, :, None], k_c, vt_c, None, None, L, L, MLA_VROWS, MLA_V)
                yf_c = _fourier_ctx(f_c)
                h_c, m_c, aff_c = _out_proj(h_c, mod_c[i], yf_c, wof, o_c, woa, n2, wr_t, tm_c)
        else:
            tabs_l, tabs_c = _rope_tables(S, L, ((0, GQA_HEAD_DIM // 2, "row"),
                                                 (GQA_HEAD_DIM // 2, GQA_HEAD_DIM // 2, "col")))
            nqk = (GQA_HEADS + GQA_KV_HEADS) * GQA_HEAD_DIM
            w = {"wqk": c_w_in[j][:, :nqk].astype(BF16), "wvt": c_w_in[j][:, nqk:].T.astype(BF16),
                 "qg": row2(c_q_gain[j]), "kg": row2(c_k_gain[j])}
            n1 = row2(norm1[i])
            h_l, q_l, k_l, vt_l = _in_c(h_l, moe_l, mod_l[i - 1], mod_l[i], n1, w, tabs_l, tm_l)
            h_c, q_c, k_c, vt_c = _in_c(h_c, moe_c, mod_c[i - 1], mod_c[i], n1, w, tabs_c, tm_c)
            grp = lambda q: q.reshape(B, GQA_KV_HEADS, GQA_GROUP, q.shape[2], LANES)
            o_l = _attention(grp(q_l), k_c, vt_c, k_l, vt_l, min(S, ATTN_QUERIES // GQA_GROUP),
                             ATTN_KEYS, GQA_HEAD_DIM)
            woa = c_w_o[j].astype(BF16)
            h_l, m_l, aff_l = _out_proj(h_l, mod_l[i], None, None, o_l, woa, n2, wr_t, tm_l)
            m_c = aff_c = None
            if not last:
                o_c = _attention(grp(q_c), k_c, vt_c, None, None, L, L, GQA_HEAD_DIM)
                h_c, m_c, aff_c = _out_proj(h_c, mod_c[i], None, None, o_c, woa, n2, wr_t, tm_c)
        moe_l, moe_c = _moe(m_l, aff_l, m_c, aff_c, i, moe_w_gate, moe_w_up, moe_w_down)
    return _final(h_l, moe_l, mod_l[depth - 1], row2(final_norm), tm_l)
```

```python
import functools
import math

import jax
import jax.numpy as jnp
import numpy as np
from jax import lax
from jax.experimental import pallas as pl
from jax.experimental.pallas import tpu as pltpu

F32 = jnp.float32
BF16 = jnp.bfloat16
I32 = jnp.int32
HIGHEST = lax.Precision.HIGHEST

D_MODEL = 1024
GRID_W = 64
EPS = 1e-6
ROPE_THETA = 10000.0
N_MOD = 6
FNET_GROUPS = 4
FNET_GROUP_DIM = 128
FNET_WIDTH = FNET_GROUPS * FNET_GROUP_DIM
MLA_HEADS = 8
MLA_Q_LORA = 256
MLA_KV_LORA = 128
MLA_NOPE = 64
MLA_ROPE = 32
MLA_V = 64
MLA_QK = MLA_NOPE + MLA_ROPE
GQA_HEADS = 8
GQA_KV_HEADS = 2
GQA_GROUP = GQA_HEADS // GQA_KV_HEADS
GQA_HEAD_DIM = 128
N_EXPERTS = 16
EXPERT_FF = 2048
CAPACITY_FACTOR = 2

LANES = 128
MLA_VROWS = MLA_V + 16
SUB = D_MODEL // LANES
VMEM_LIMIT = 56 * 1024 * 1024
LOG2E = math.log2(math.e)
NT = (((1,), (1,)), ((), ()))


def _params(sem, vmem=None):
    return pltpu.CompilerParams(dimension_semantics=sem, vmem_limit_bytes=vmem)


def _rms(x, g):
    return x * lax.rsqrt(jnp.mean(x * x, axis=-1, keepdims=True) + EPS) * g


def _modulate(h, g, shift, scale):
    return _rms(h, g) * (1.0 + scale) + shift


def _rope(x, cos, sin_fwd, sin_bwd, shift):
    return (x * cos + pltpu.roll(x, LANES - shift, 1) * sin_fwd
            + pltpu.roll(x, shift, 1) * sin_bwd)


def _untile(ref, rows):
    x = ref[rows]
    n = x.shape[0]
    y = pltpu.einshape("grsl->gsrl", x.reshape(n // 8, 8, SUB, LANES))
    return jnp.concatenate([y[:, s].reshape(n, LANES) for s in range(SUB)], axis=1)


def _store_tiled(ref, rows, x):
    n = x.shape[0]
    y = jnp.concatenate([x[:, s * LANES:(s + 1) * LANES].reshape(n // 8, 1, 8, LANES)
                         for s in range(SUB)], axis=1)
    ref[rows] = pltpu.einshape("gsrl->grsl", y).reshape(n, SUB, LANES)


def _ada_kernel(c_ref, w_ref, b_ref, o_ref):
    c = c_ref[...]
    x = c * jax.nn.sigmoid(c)
    o_ref[0] = jnp.dot(x, w_ref[0], precision=HIGHEST, preferred_element_type=F32) + b_ref[0]


def _ada(c8, ada_w, ada_b):
    depth, d, n = ada_w.shape
    tn = 1536
    return pl.pallas_call(
        _ada_kernel,
        out_shape=jax.ShapeDtypeStruct((depth, 8, n), F32),
        grid=(depth, n // tn),
        in_specs=[pl.BlockSpec((8, d), lambda l, j: (0, 0)),
                  pl.BlockSpec((1, d, tn), lambda l, j: (l, 0, j)),
                  pl.BlockSpec((1, 1, tn), lambda l, j: (l, 0, j))],
        out_specs=pl.BlockSpec((1, 8, tn), lambda l, j: (l, 0, j)),
        compiler_params=_params(("arbitrary", "arbitrary"), VMEM_LIMIT),
        name="ada",
    )(c8, ada_w, ada_b.reshape(depth, 1, n))


def _in_ab_kernel(scale, *refs):
    (h_ref, mod_ref, n1_ref, win_ref, qn_ref, wuq_ref, kvn_ref, wuk_ref, wuvt_ref, ekr_ref,
     cos_ref, sf_ref, sb_ref, f_ref, q_ref, k_ref, vt_ref) = refs
    D = D_MODEL
    h = h_ref[0]
    mod = mod_ref[0]
    a = _modulate(h, n1_ref[...], mod[:, 0:D], mod[:, D:2 * D]).astype(BF16)
    p = jnp.dot(a, win_ref[...], preferred_element_type=F32)
    o = FNET_WIDTH
    f_ref[0] = p[:, :o]
    cq = p[:, o:o + MLA_Q_LORA]
    o += MLA_Q_LORA
    ckv = p[:, o:o + MLA_KV_LORA]
    o += MLA_KV_LORA
    kr = p[:, o:o + MLA_ROPE]
    cqn = _rms(cq, qn_ref[...]).astype(BF16)
    ckvn = _rms(ckv, kvn_ref[...]).astype(BF16)
    q = jnp.dot(cqn, wuq_ref[...], preferred_element_type=F32)
    k = (jnp.dot(ckvn, wuk_ref[...], preferred_element_type=F32)
         + jnp.dot(kr.astype(BF16), ekr_ref[...], preferred_element_type=F32))
    vt = lax.dot_general(wuvt_ref[...], ckvn, NT, preferred_element_type=F32)
    cos, sf, sb = cos_ref[...], sf_ref[...], sb_ref[...]
    shift = MLA_ROPE // 4
    ones = lax.broadcasted_iota(I32, (LANES, vt.shape[1]), 0) == MLA_V
    for hd in range(MLA_HEADS):
        sl = slice(hd * LANES, (hd + 1) * LANES)
        q_ref[0, hd] = (_rope(q[:, sl], cos, sf, sb, shift) * scale).astype(BF16)
        k_ref[0, hd] = _rope(k[:, sl], cos, sf, sb, shift).astype(BF16)
        vt_ref[0, hd] = jnp.where(ones, 1.0, vt[sl, :]).astype(BF16)


def _in_ab(h, mod, n1, w, tables, tm):
    B, R, D = h.shape
    H = MLA_HEADS
    full = lambda a: pl.BlockSpec(a.shape, lambda b, i: (0,) * a.ndim)
    row = lambda w_: pl.BlockSpec((1, tm, w_), lambda b, i: (b, i, 0))
    tab = pl.BlockSpec((tm, LANES), lambda b, i: (i, 0))
    hd = pl.BlockSpec((1, H, tm, LANES), lambda b, i: (b, 0, i, 0))
    hdt = pl.BlockSpec((1, H, LANES, tm), lambda b, i: (b, 0, 0, i))
    weights = [n1, w["win"], w["qn"], w["wuq"], w["kvn"], w["wuk"], w["wuvt"], w["ekr"]]
    return pl.pallas_call(
        functools.partial(_in_ab_kernel, MLA_QK ** -0.5 * LOG2E),
        out_shape=(jax.ShapeDtypeStruct((B, R, FNET_WIDTH), F32),
                   jax.ShapeDtypeStruct((B, H, R, LANES), BF16),
                   jax.ShapeDtypeStruct((B, H, R, LANES), BF16),
                   jax.ShapeDtypeStruct((B, H, LANES, R), BF16)),
        grid=(B, R // tm),
        in_specs=[row(D), pl.BlockSpec((1, 1, N_MOD * D), lambda b, i: (b, 0, 0))]
        + [full(a) for a in weights] + [tab, tab, tab],
        out_specs=(row(FNET_WIDTH), hd, hd, hdt),
        compiler_params=_params(("arbitrary", "arbitrary"), VMEM_LIMIT),
        name="in_ab",
    )(h, mod, *weights, *tables)


def _in_c_kernel(scale, *refs):
    (h_ref, moe_ref, modp_ref, mod_ref, n1_ref, wqk_ref, wvt_ref, qg_ref, kg_ref,
     cos_ref, sf_ref, sb_ref, h2_ref, q_ref, k_ref, vt_ref) = refs
    D = D_MODEL
    h = h_ref[0] + modp_ref[0][:, 5 * D:6 * D] * _untile(moe_ref.at[0], slice(None))
    h2_ref[0] = h
    mod = mod_ref[0]
    a = _modulate(h, n1_ref[...], mod[:, 0:D], mod[:, D:2 * D]).astype(BF16)
    p = jnp.dot(a, wqk_ref[...], preferred_element_type=F32)
    vt = lax.dot_general(wvt_ref[...], a, NT, preferred_element_type=F32)
    cos, sf, sb = cos_ref[...], sf_ref[...], sb_ref[...]
    shift = GQA_HEAD_DIM // 4
    nq = GQA_HEADS * GQA_HEAD_DIM
    for hd in range(GQA_HEADS):
        x = _rms(p[:, hd * LANES:(hd + 1) * LANES], qg_ref[...])
        q_ref[0, hd] = (_rope(x, cos, sf, sb, shift) * scale).astype(BF16)
    for hd in range(GQA_KV_HEADS):
        x = _rms(p[:, nq + hd * LANES:nq + (hd + 1) * LANES], kg_ref[...])
        k_ref[0, hd] = _rope(x, cos, sf, sb, shift).astype(BF16)
        vt_ref[0, hd] = vt[hd * LANES:(hd + 1) * LANES, :].astype(BF16)


def _in_c(h, moe, mod_prev, mod, n1, w, tables, tm):
    B, R, D = h.shape
    full = lambda a: pl.BlockSpec(a.shape, lambda b, i: (0,) * a.ndim)
    row = lambda w_: pl.BlockSpec((1, tm, w_), lambda b, i: (b, i, 0))
    tiled = pl.BlockSpec((1, tm, SUB, LANES), lambda b, i: (b, i, 0, 0))
    modspec = pl.BlockSpec((1, 1, N_MOD * D), lambda b, i: (b, 0, 0))
    tab = pl.BlockSpec((tm, LANES), lambda b, i: (i, 0))
    hd = lambda n: pl.BlockSpec((1, n, tm, LANES), lambda b, i: (b, 0, i, 0))
    hdt = pl.BlockSpec((1, GQA_KV_HEADS, LANES, tm), lambda b, i: (b, 0, 0, i))
    weights = [n1, w["wqk"], w["wvt"], w["qg"], w["kg"]]
    return pl.pallas_call(
        functools.partial(_in_c_kernel, GQA_HEAD_DIM ** -0.5 * LOG2E),
        out_shape=(jax.ShapeDtypeStruct((B, R, D), F32),
                   jax.ShapeDtypeStruct((B, GQA_HEADS, R, LANES), BF16),
                   jax.ShapeDtypeStruct((B, GQA_KV_HEADS, R, LANES), BF16),
                   jax.ShapeDtypeStruct((B, GQA_KV_HEADS, LANES, R), BF16)),
        grid=(B, R // tm),
        in_specs=[row(D), tiled, modspec, modspec] + [full(a) for a in weights] + [tab, tab, tab],
        out_specs=(row(D), hd(GQA_HEADS), hd(GQA_KV_HEADS), hdt),
        compiler_params=_params(("arbitrary", "arbitrary"), VMEM_LIMIT),
        name="in_c",
    )(h, moe, mod_prev, mod, *weights, *tables)


ATTN_QUERIES = 4096
ATTN_KEYS = 512


def _attn_kernel(n_lat, tk, dv, ones_row, *refs):
    if n_lat:
        q_ref, kc_ref, vct_ref, kl_ref, vlt_ref, o_ref, acc_ref, s0, s1, p0, p1 = refs
        s_bufs, p_bufs = (s0, s1), (p0, p1)
    else:
        q_ref, kc_ref, vct_ref, o_ref, acc_ref = refs
    G, tq = q_ref.shape[2], q_ref.shape[3]
    q = q_ref[0, 0].reshape(G * tq, LANES)

    def scores(k):
        return lax.dot_general(k, q, NT, preferred_element_type=F32)

    s = scores(kc_ref[0, 0])
    m = jnp.max(s, axis=0, keepdims=True)
    p = jnp.exp2(s - m)
    l = jnp.sum(p, axis=0, keepdims=True) if ones_row is None else jnp.zeros_like(m)
    if dv < LANES:
        acc_ref[...] = jnp.zeros_like(acc_ref)
    acc_ref[:dv, :] = jnp.dot(vct_ref[0, 0, :dv, :], p.astype(BF16), preferred_element_type=F32)

    if n_lat:
        assert n_lat == 1 or n_lat % 2 == 0

        def chunk(c):
            return pl.ds(pl.multiple_of(c * tk, tk), tk)

        def score_stage(c, slot):
            s_bufs[slot][...] = scores(kl_ref[0, 0, chunk(c), :])

        def softmax_stage(slot, m, l):
            s = s_bufs[slot][...]
            m_new = jnp.maximum(m, jnp.max(s, axis=0, keepdims=True))
            p = jnp.exp2(s - m_new)
            p_bufs[slot][...] = p.astype(BF16)
            alpha = jnp.exp2(m - m_new)
            if ones_row is None:
                l = alpha * l + jnp.sum(p, axis=0, keepdims=True)
            return m_new, l, alpha

        def value_stage(c, slot, alpha):
            pv = jnp.dot(vlt_ref[0, 0, :dv, chunk(c)], p_bufs[slot][...],
                         preferred_element_type=F32)
            acc_ref[:dv, :] = alpha * acc_ref[:dv, :] + pv

        score_stage(0, 0)
        m, l, alpha = softmax_stage(0, m, l)
        if n_lat > 1:
            score_stage(1, 1)

            def body(i, carry):
                m, l, alpha = carry
                for slot in (0, 1):
                    c = 2 * i + slot
                    score_stage(c + 2, slot)
                    value_stage(c, slot, alpha)
                    m, l, alpha = softmax_stage(1 - slot, m, l)
                return m, l, alpha

            m, l, alpha = lax.fori_loop(0, (n_lat - 2) // 2, body, (m, l, alpha))
            value_stage(n_lat - 2, 0, alpha)
            m, l, alpha = softmax_stage(1, m, l)
        value_stage(n_lat - 1, (n_lat - 1) % 2, alpha)
    if ones_row is not None:
        l = acc_ref[ones_row:ones_row + 1, :]
    o_ref[0, 0] = (acc_ref[...] / l).T.reshape(G, tq, LANES).astype(BF16)


def _attention(q, kc, vct, kl, vlt, tq, tk, dv, ones_row=None):
    B, Hk, G, R, _ = q.shape
    Lc = kc.shape[2]
    n_lat = 0 if kl is None else kl.shape[2] // tk
    qspec = pl.BlockSpec((1, 1, G, tq, LANES), lambda b, h, i: (b, h, 0, i, 0))
    kspec = lambda n: pl.BlockSpec((1, 1, n, LANES), lambda b, h, i: (b, h, 0, 0))
    vspec = lambda n: pl.BlockSpec((1, 1, LANES, n), lambda b, h, i: (b, h, 0, 0))
    ins = [q, kc, vct] + ([kl, vlt] if n_lat else [])
    specs = [qspec, kspec(Lc), vspec(Lc)] + ([kspec(kl.shape[2]), vspec(kl.shape[2])] if n_lat else [])
    return pl.pallas_call(
        functools.partial(_attn_kernel, n_lat, tk, dv, ones_row),
        out_shape=jax.ShapeDtypeStruct(q.shape, BF16),
        grid=(B, Hk, R // tq),
        in_specs=specs,
        out_specs=qspec,
        scratch_shapes=[pltpu.VMEM((LANES, G * tq), F32)]
        + ([pltpu.VMEM((tk, G * tq), F32)] * 2 + [pltpu.VMEM((tk, G * tq), BF16)] * 2 if n_lat else []),
        compiler_params=_params(("arbitrary", "arbitrary", "arbitrary"), VMEM_LIMIT),
        name="attn",
    )(*ins)


def _dft_mats(n):
    k = np.arange(n, dtype=np.float64)
    ang = 2.0 * np.pi * np.outer(k, k) / n
    return np.cos(ang), np.sin(ang)


def _split(x):
    hi = x.astype(BF16)
    return hi, (x - hi.astype(F32)).astype(BF16)


def _dot3(a, b):
    d = lambda p, q: jnp.dot(p, q, preferred_element_type=F32)
    return d(a[0], b[0]) + d(a[0], b[1]) + d(a[1], b[0])


def _split_const(m):
    return jnp.stack(_split(jnp.asarray(m, F32)))


def _dft1_kernel(x_ref, m1_ref, tw_ref, o_ref):
    s1, ns2 = x_ref.shape[1], x_ref.shape[2]
    for i in range(ns2):
        a = jnp.dot(m1_ref[...], x_ref[0, :, i, :], precision=HIGHEST, preferred_element_type=F32)
        are, aim = a[:s1], a[s1:]
        tre = jnp.tile(tw_ref[0, :, i * LANES:(i + 1) * LANES], (1, FNET_GROUPS))
        tim = jnp.tile(tw_ref[1, :, i * LANES:(i + 1) * LANES], (1, FNET_GROUPS))
        o_ref[0, 0, :, i, :] = are * tre - aim * tim
        o_ref[0, 1, :, i, :] = are * tim + aim * tre


def _dft2_kernel(a_ref, m2_ref, m3_ref, o_ref):
    kb, n2 = a_ref.shape[2], a_ref.shape[3]
    for j in range(kb):
        rhs = jnp.concatenate([a_ref[0, 0, j], a_ref[0, 1, j]], axis=0)
        y = _dot3((m2_ref[0], m2_ref[1]), _split(rhs))
        for g in range(FNET_GROUPS):
            sl = slice(g * LANES, (g + 1) * LANES)
            lhs = jnp.concatenate([y[:n2, sl], y[n2:, sl]], axis=1)
            o_ref[0, :, j, sl] = _dot3(_split(lhs), (m3_ref[0], m3_ref[1]))


def _fourier_lat(f):
    B, S, W = f.shape
    n2 = LANES
    s1 = S // n2
    c1, sn1 = _dft_mats(s1)
    c2, sn2 = _dft_mats(n2)
    cc, sc = _dft_mats(FNET_GROUP_DIM)
    m1 = jnp.asarray(np.concatenate([c1, -sn1], axis=0), F32)
    m2 = _split_const(np.block([[c2, sn2], [-sn2, c2]]))
    norm = 1.0 / math.sqrt(S * FNET_GROUP_DIM)
    m3 = _split_const(np.concatenate([cc, sc], axis=0) * norm)
    ang = 2.0 * np.pi * np.outer(np.arange(s1), np.arange(n2)) / S
    tw = np.stack([np.cos(ang), -np.sin(ang)])
    tw = jnp.asarray(np.repeat(tw[:, :, :, None], LANES, axis=3).reshape(2, s1, n2 * LANES), F32)

    ns2 = 8
    a = pl.pallas_call(
        _dft1_kernel,
        out_shape=jax.ShapeDtypeStruct((B, 2, s1, n2, W), F32),
        grid=(B, n2 // ns2),
        in_specs=[pl.BlockSpec((1, s1, ns2, W), lambda b, j: (b, 0, j, 0)),
                  pl.BlockSpec(m1.shape, lambda b, j: (0, 0)),
                  pl.BlockSpec((2, s1, ns2 * LANES), lambda b, j: (0, 0, j))],
        out_specs=pl.BlockSpec((1, 2, s1, ns2, W), lambda b, j: (b, 0, 0, j, 0)),
        compiler_params=_params(("arbitrary", "arbitrary"), VMEM_LIMIT),
        name="dft1",
    )(f.reshape(B, s1, n2, W), m1, tw)
    kb = min(8, s1)
    y = pl.pallas_call(
        _dft2_kernel,
        out_shape=jax.ShapeDtypeStruct((B, n2, s1, W), F32),
        grid=(B, s1 // kb),
        in_specs=[pl.BlockSpec((1, 2, kb, n2, W), lambda b, j: (b, 0, j, 0, 0)),
                  pl.BlockSpec(m2.shape, lambda b, j: (0, 0, 0)),
                  pl.BlockSpec(m3.shape, lambda b, j: (0, 0, 0))],
        out_specs=pl.BlockSpec((1, n2, kb, W), lambda b, j: (b, 0, j, 0)),
        compiler_params=_params(("arbitrary", "arbitrary"), VMEM_LIMIT),
        name="dft2",
    )(a, m2, m3)
    return y.reshape(B, S, W)


def _dftc_kernel(f_ref, mc_ref, m3_ref, o_ref):
    n = f_ref.shape[1]
    a = _dot3((mc_ref[0], mc_ref[1]), _split(f_ref[0]))
    for g in range(FNET_GROUPS):
        sl = slice(g * LANES, (g + 1) * LANES)
        lhs = jnp.concatenate([a[:n, sl], a[n:, sl]], axis=1)
        o_ref[0, :, sl] = _dot3(_split(lhs), (m3_ref[0], m3_ref[1]))


def _fourier_ctx(f):
    B, L, W = f.shape
    c, s = _dft_mats(L)
    cc, sc = _dft_mats(FNET_GROUP_DIM)
    mc = _split_const(np.concatenate([c, -s], axis=0))
    m3 = _split_const(np.concatenate([cc, sc], axis=0) / math.sqrt(L * FNET_GROUP_DIM))
    return pl.pallas_call(
        _dftc_kernel,
        out_shape=jax.ShapeDtypeStruct((B, L, W), F32),
        grid=(B,),
        in_specs=[pl.BlockSpec((1, L, W), lambda b: (b, 0, 0)),
                  pl.BlockSpec(mc.shape, lambda b: (0, 0, 0)),
                  pl.BlockSpec(m3.shape, lambda b: (0, 0, 0))],
        out_specs=pl.BlockSpec((1, L, W), lambda b: (b, 0, 0)),
        compiler_params=_params(("arbitrary",), VMEM_LIMIT),
        name="dftc",
    )(f, mc, m3)


def _out_kernel(has_f, *refs):
    if has_f:
        (h_ref, mod_ref, yf_ref, wof_ref, o_ref, woa_ref, n2_ref, wr_ref,
         h1_ref, m_ref, aff_ref) = refs
    else:
        (h_ref, mod_ref, o_ref, woa_ref, n2_ref, wr_ref, h1_ref, m_ref, aff_ref) = refs
    D = D_MODEL
    Hk, G = o_ref.shape[1], o_ref.shape[2]
    ocat = jnp.concatenate([o_ref[0, hk, g] for hk in range(Hk) for g in range(G)], axis=1)
    y = jnp.dot(ocat, woa_ref[...], preferred_element_type=F32)
    if has_f:
        y = y + jnp.dot(yf_ref[0].astype(BF16), wof_ref[...], preferred_element_type=F32)
    mod = mod_ref[0]
    h1 = h_ref[0] + mod[:, 2 * D:3 * D] * y
    h1_ref[0] = h1
    m = _modulate(h1, n2_ref[...], mod[:, 3 * D:4 * D], mod[:, 4 * D:5 * D])
    _store_tiled(m_ref.at[0], slice(None), m)
    logit = lax.dot_general(wr_ref[...], m, NT, precision=HIGHEST,
                            preferred_element_type=F32)
    e = jnp.exp(logit - jnp.max(logit, axis=0, keepdims=True))
    aff_ref[0] = e / jnp.sum(e, axis=0, keepdims=True)


def _out_proj(h, mod, yf, wof, o, woa, n2, wr_t, tm):
    B, R, D = h.shape
    _, Hk, G, _, _ = o.shape
    full = lambda a: pl.BlockSpec(a.shape, lambda b, i: (0,) * a.ndim)
    row = lambda w_: pl.BlockSpec((1, tm, w_), lambda b, i: (b, i, 0))
    modspec = pl.BlockSpec((1, 1, N_MOD * D), lambda b, i: (b, 0, 0))
    ospec = pl.BlockSpec((1, Hk, G, tm, LANES), lambda b, i: (b, 0, 0, i, 0))
    has_f = yf is not None
    ins = [h, mod] + ([yf, wof] if has_f else []) + [o, woa, n2, wr_t]
    specs = ([row(D), modspec] + ([row(FNET_WIDTH), full(wof)] if has_f else [])
             + [ospec, full(woa), full(n2), full(wr_t)])
    return pl.pallas_call(
        functools.partial(_out_kernel, has_f),
        out_shape=(jax.ShapeDtypeStruct((B, R, D), F32),
                   jax.ShapeDtypeStruct((B, R, SUB, LANES), F32),
                   jax.ShapeDtypeStruct((B, N_EXPERTS, R), F32)),
        grid=(B, R // tm),
        in_specs=specs,
        out_specs=(row(D), pl.BlockSpec((1, tm, SUB, LANES), lambda b, i: (b, i, 0, 0)),
                   pl.BlockSpec((1, N_EXPERTS, tm), lambda b, i: (b, 0, i))),
        compiler_params=_params(("arbitrary", "arbitrary"), VMEM_LIMIT),
        name="out_proj",
    )(*ins)


COMPACT_UNROLL = 4


def _topk_kernel(cap, aff_ref, u_ref, ones_ref, lmat_ref, lc_ref, lg_ref, cnt_ref, off_ref,
                 pos_ref, ac_ref, lct_ref, lgt_ref):
    a = aff_ref[0]
    E, N = a.shape
    NC = N // LANES
    R = NC * E
    keys = pltpu.bitcast(a, I32)

    def bit_step(i, tau):
        cand = tau | jnp.left_shift(jnp.int32(1), 30 - i)
        cnt = jnp.sum((keys >= cand).astype(I32), axis=1, keepdims=True)
        return jnp.where(cnt >= cap, cand, tau)

    tau = lax.fori_loop(0, 31, bit_step, jnp.zeros((E, 1), I32))
    gt = (keys > tau).astype(F32)
    eq = (keys == tau).astype(F32)
    need = (cap - jnp.sum(gt, axis=1, keepdims=True))

    def chunked(x):
        return jnp.concatenate([x[:, c * LANES:(c + 1) * LANES] for c in range(NC)], axis=0)

    a_c, gt_c, eq_c = chunked(a), chunked(gt), chunked(eq)
    need_c = jnp.tile(need, (NC, 1))

    def prefix(x):
        xb = x.astype(BF16)
        loc = jnp.dot(xb, u_ref[...], preferred_element_type=F32)
        tot = jnp.dot(xb, ones_ref[...], preferred_element_type=F32)
        offs = jnp.dot(lmat_ref[...], tot.astype(BF16), preferred_element_type=F32)
        return loc, tot, offs

    loc, tot, offs = prefix(eq_c)
    sel = jnp.maximum(gt_c, jnp.where(loc + offs < need_c, eq_c, 0.0))
    loc, tot, offs = prefix(sel)
    cnt_ref[0] = tot.astype(I32)
    off_ref[0] = offs.astype(I32)
    Rp = pos_ref.shape[1]
    selpos = jnp.where(sel > 0.0, loc, -1.0)
    if Rp > R:
        selpos = jnp.concatenate([selpos, jnp.full((Rp - R, LANES), -1.0, F32)], axis=0)
        a_c = jnp.concatenate([a_c, jnp.zeros((Rp - R, LANES), F32)], axis=0)
    pos_ref[...] = selpos.T
    ac_ref[...] = a_c.T
    tok = lax.broadcasted_iota(I32, (LANES, Rp), 0).astype(F32)

    def compact(i, carry):
        p = pos_ref[...]
        a_t = ac_ref[...]
        for u in range(COMPACT_UNROLL):
            j = i * COMPACT_UNROLL + u
            hit = p == lax.convert_element_type(j, F32)
            lct_ref[pl.ds(j, 1), :] = jnp.sum(jnp.where(hit, tok, 0.0), axis=0, keepdims=True)
            lgt_ref[pl.ds(j, 1), :] = jnp.sum(jnp.where(hit, a_t, 0.0), axis=0, keepdims=True)
        return carry

    lax.fori_loop(0, LANES // COMPACT_UNROLL, compact, 0)
    lc_ref[0] = lct_ref[...].T[:R].astype(I32)
    lg_ref[0] = lgt_ref[...].T[:R]


def _topk(aff, cap):
    B, E, N = aff.shape
    NC = N // LANES
    R = NC * E
    i = np.arange(LANES)
    u = jnp.asarray(i[:, None] < i[None, :], BF16)
    ones = jnp.ones((LANES, LANES), BF16)
    r = np.arange(R)
    lmat = jnp.asarray((r[:, None] % E == r[None, :] % E) & (r[None, :] // E < r[:, None] // E), BF16)
    full = lambda a: pl.BlockSpec(a.shape, lambda b: (0,) * a.ndim)
    ospec = pl.BlockSpec((1, R, LANES), lambda b: (b, 0, 0))
    sds = lambda dt: jax.ShapeDtypeStruct((B, R, LANES), dt)
    lc, lg, cnt, off = pl.pallas_call(
        functools.partial(_topk_kernel, cap),
        out_shape=(sds(I32), sds(F32), sds(I32), sds(I32)),
        grid=(B,),
        in_specs=[pl.BlockSpec((1, E, N), lambda b: (b, 0, 0)), full(u), full(ones), full(lmat)],
        out_specs=(ospec, ospec, ospec, ospec),
        scratch_shapes=[pltpu.VMEM((LANES, -(-R // LANES) * LANES), F32)] * 4,
        compiler_params=_params(("arbitrary",), VMEM_LIMIT),
        name="topk",
    )(aff, u, ones, lmat)
    by_expert = lambda x: x.reshape(B, NC, E, LANES).transpose(0, 2, 1, 3)
    flat = lambda x: by_expert(x).reshape(B, E, NC * LANES)
    return (flat(lc), flat(lg), by_expert(cnt)[..., 0], by_expert(off)[..., 0])


GATHER_ROWS = 256
ROW_UNROLL = 4


def _gather_kernel(cap, cnt_ref, off_ref, lc_ref, m_hbm, o_ref, x_ref, sems):
    e = pl.program_id(0)
    B, NC = lc_ref.shape[0], lc_ref.shape[1] // LANES
    for b in range(B):
        def chunk_body(c, carry, b=b):
            base = off_ref[b, e, c] + b * cap

            n = cnt_ref[b, e, c]

            def start_row(r, priority):
                t = c * LANES + lc_ref[b, c * LANES + r]
                pltpu.make_async_copy(m_hbm.at[b, t], x_ref.at[base + r],
                                      sems.at[b]).start(priority=priority)

            def pair_body(i, carry):
                start_row(2 * i, 0)
                start_row(2 * i + 1, 1)
                return carry

            lax.fori_loop(0, n // 2, pair_body, carry)

            @pl.when(n % 2 == 1)
            def _():
                start_row(n - 1, 0)

            return carry

        lax.fori_loop(0, NC, chunk_body, 0)
    step = min(GATHER_ROWS, cap)
    for b in range(B):
        done = x_ref.at[pl.ds(b * cap, cap)]
        pltpu.make_async_copy(done, done, sems.at[b]).wait()
        for r0 in range(b * cap, (b + 1) * cap, step):
            rows = slice(r0, r0 + step)
            o_ref[0, rows, :] = _untile(x_ref, rows).astype(BF16)


def _gather(m, lc, cnt, off, cap):
    B, N, sub, _ = m.shape
    E = lc.shape[1]
    return pl.pallas_call(
        functools.partial(_gather_kernel, cap),
        out_shape=jax.ShapeDtypeStruct((E, B * cap, sub * LANES), BF16),
        grid_spec=pltpu.PrefetchScalarGridSpec(
            num_scalar_prefetch=2,
            grid=(E,),
            in_specs=[pl.BlockSpec((None, B, lc.shape[2]), lambda e, *_: (e, 0, 0),
                                   memory_space=pltpu.SMEM),
                      pl.BlockSpec(memory_space=pl.ANY)],
            out_specs=pl.BlockSpec((1, B * cap, sub * LANES), lambda e, *_: (e, 0, 0)),
            scratch_shapes=[pltpu.VMEM((B * cap, sub, LANES), m.dtype),
                            pltpu.SemaphoreType.DMA((B,))]),
        compiler_params=_params(("arbitrary",), VMEM_LIMIT),
        name="gather",
    )(cnt, off, lc.transpose(1, 0, 2), m)


def _ffn_kernel(n_parts, chunk_rows, *refs):
    wg_ref, wu_ref, wd_ref = refs[:3]
    x_refs = refs[3:3 + n_parts]
    o_refs = refs[3 + n_parts:3 + 2 * n_parts]
    acc_refs = refs[3 + 2 * n_parts:]
    j = pl.program_id(1)
    last = pl.num_programs(1) - 1
    wg = wg_ref[0].astype(BF16)
    wu = wu_ref[0].astype(BF16)
    wd = wd_ref[0].astype(BF16)
    chunks = [(x_ref, o_ref, acc_ref, slice(r0, r0 + nrows))
              for x_ref, o_ref, acc_ref, nrows in zip(x_refs, o_refs, acc_refs, chunk_rows)
              for r0 in range(0, x_ref.shape[1], nrows)]

    @pl.when(j == 0)
    def _():
        for acc_ref in acc_refs:
            acc_ref[...] = jnp.zeros_like(acc_ref)

    for x_ref, _, acc_ref, rows in chunks:
        x = x_ref[0, rows, :]
        a = jnp.dot(x, wg, preferred_element_type=F32)
        u = jnp.dot(x, wu, preferred_element_type=F32)
        hh = (a * jax.nn.sigmoid(a) * u).astype(BF16)
        acc_ref[rows, :] += jnp.dot(hh, wd, preferred_element_type=F32)

    @pl.when(j == last)
    def _():
        for _, o_ref, acc_ref, rows in chunks:
            _store_tiled(o_ref.at[0], rows, acc_ref[rows, :])


def _ffn(xs, layer, w_gate, w_up, w_down, chunk_rows):
    _, E, D, F = w_gate.shape
    tf = 512
    assert F // tf >= 2
    xspec = lambda x: pl.BlockSpec((1, x.shape[1], D), lambda e, j: (e, 0, 0))
    ospec = lambda x: pl.BlockSpec((1, x.shape[1], SUB, LANES), lambda e, j: (e, 0, 0, 0))
    outs = pl.pallas_call(
        functools.partial(_ffn_kernel, len(xs), chunk_rows),
        out_shape=tuple(jax.ShapeDtypeStruct((E, x.shape[1], SUB, LANES), F32) for x in xs),
        grid=(E, F // tf),
        in_specs=[pl.BlockSpec((None, 1, D, tf), lambda e, j: (layer, e, 0, j)),
                  pl.BlockSpec((None, 1, D, tf), lambda e, j: (layer, e, 0, j)),
                  pl.BlockSpec((None, 1, tf, D), lambda e, j: (layer, e, j, 0))]
        + [xspec(x) for x in xs],
        out_specs=tuple(ospec(x) for x in xs),
        scratch_shapes=[pltpu.VMEM((x.shape[1], D), F32) for x in xs],
        compiler_params=_params(("arbitrary", "arbitrary"), VMEM_LIMIT),
        name="ffn",
    )(w_gate, w_up, w_down, *xs)
    return outs


def _combine_kernel(cnt_ref, off_ref, lc_ref, lg_ref, y_ref, o_ref):
    b = pl.program_id(0)
    hf = pl.program_id(1)
    e = pl.program_id(2)
    nch = o_ref.shape[1] // LANES

    @pl.when(e == 0)
    def _():
        o_ref[...] = jnp.zeros_like(o_ref)

    def chunk_body(ci, carry):
        c = hf * nch + ci
        base = off_ref[b, e, c]

        n = cnt_ref[b, e, c]

        def rows_body(r0, width):
            ts = [ci * LANES + lc_ref[0, c * LANES + r0 + u] for u in range(width)]
            vals = [o_ref[0, ts[u]] + lg_ref[0, c * LANES + r0 + u] * y_ref[0, base + r0 + u]
                    for u in range(width)]
            for u in range(width):
                o_ref[0, ts[u]] = vals[u]

        def group_body(i, carry):
            rows_body(i * ROW_UNROLL, ROW_UNROLL)
            return carry

        def tail_body(r, carry):
            rows_body(r, 1)
            return carry

        full = n // ROW_UNROLL
        lax.fori_loop(0, full, group_body, carry)
        return lax.fori_loop(full * ROW_UNROLL, n, tail_body, carry)

    lax.fori_loop(0, nch, chunk_body, 0)


def _combine(y, lc, lg, cnt, off, n_tokens, cap, n_split):
    E = y.shape[0]
    B = lc.shape[0]
    nh = n_tokens // n_split
    smem = lambda: pl.BlockSpec((None, None, 1, lc.shape[2]), lambda b, h, e, *_: (b, e, 0, 0),
                                memory_space=pltpu.SMEM)
    return pl.pallas_call(
        _combine_kernel,
        out_shape=jax.ShapeDtypeStruct((B, n_tokens, SUB, LANES), F32),
        grid_spec=pltpu.PrefetchScalarGridSpec(
            num_scalar_prefetch=2,
            grid=(B, n_split, E),
            in_specs=[smem(), smem(),
                      pl.BlockSpec((1, cap, SUB, LANES), lambda b, h, e, *_: (e, b, 0, 0))],
            out_specs=pl.BlockSpec((1, nh, SUB, LANES), lambda b, h, e, *_: (b, h, 0, 0))),
        compiler_params=_params(("arbitrary", "arbitrary", "arbitrary"), VMEM_LIMIT),
        name="combine",
    )(cnt, off, lc[:, :, None, :], lg[:, :, None, :], y)


def _final_kernel(h_ref, moe_ref, mod_ref, g_ref, o_ref):
    D = D_MODEL
    h = h_ref[0] + mod_ref[0][:, 5 * D:6 * D] * _untile(moe_ref.at[0], slice(None))
    o_ref[0] = _rms(h, g_ref[...])


def _final(h, moe, mod, g, tm):
    B, R, D = h.shape
    row = pl.BlockSpec((1, tm, D), lambda b, i: (b, i, 0))
    return pl.pallas_call(
        _final_kernel,
        out_shape=jax.ShapeDtypeStruct((B, R, D), F32),
        grid=(B, R // tm),
        in_specs=[row, pl.BlockSpec((1, tm, SUB, LANES), lambda b, i: (b, i, 0, 0)),
                  pl.BlockSpec((1, 1, N_MOD * D), lambda b, i: (b, 0, 0)),
                  pl.BlockSpec((1, D), lambda b, i: (0, 0))],
        out_specs=row,
        compiler_params=_params(("arbitrary", "arbitrary"), VMEM_LIMIT),
        name="final",
    )(h, moe, mod, g)


def _rope_tables(n_lat, n_ctx, segments):
    t = jnp.arange(n_lat)
    pos = {"row": (t // GRID_W).astype(F32), "col": (t % GRID_W).astype(F32)}
    freq = {"row": np.zeros((2, LANES), np.float32), "col": np.zeros((2, LANES), np.float32)}
    first = np.zeros(LANES, np.float32)
    second = np.zeros(LANES, np.float32)
    for lane0, width, which in segments:
        half = width // 2
        idx = np.arange(half, dtype=np.float32)
        for lo, mask in ((lane0, first), (lane0 + half, second)):
            freq[which][0, lo:lo + half] = idx / half
            freq[which][1, lo:lo + half] = 1.0
            mask[lo:lo + half] = 1.0
    ang = jnp.zeros((n_lat, LANES), F32)
    for which in ("row", "col"):
        inv = (ROPE_THETA ** (-jnp.asarray(freq[which][0]))) * jnp.asarray(freq[which][1])
        ang = ang + pos[which][:, None] * inv[None, :]
    sin = jnp.sin(ang)
    lat = (jnp.cos(ang), -sin * first[None, :], sin * second[None, :])
    ctx = (jnp.ones((n_ctx, LANES), F32), jnp.zeros((n_ctx, LANES), F32), jnp.zeros((n_ctx, LANES), F32))
    return lat, ctx


def _head_slots(w, n_heads, width, lo, hi):
    k = w.shape[0]
    w3 = w.reshape(k, n_heads, width)[:, :, lo:hi]
    return jnp.pad(w3, ((0, 0), (0, 0), (0, LANES - (hi - lo)))).reshape(k, n_heads * LANES)


def _moe(m_l, aff_l, m_c, aff_c, layer, w_gate, w_up, w_down):
    B, S = m_l.shape[:2]
    cap_l = CAPACITY_FACTOR * S // N_EXPERTS
    lc, lg, cnt, off = _topk(aff_l, cap_l)
    xs = [_gather(m_l, lc, cnt, off, cap_l)]
    chunk_rows = [cap_l]
    if m_c is not None:
        L = m_c.shape[1]
        cap_c = CAPACITY_FACTOR * L // N_EXPERTS
        lcc, lgc, cntc, offc = _topk(aff_c, cap_c)
        xs.append(_gather(m_c, lcc, cntc, offc, cap_c))
        chunk_rows.append(B * cap_c)
    ys = _ffn(xs, layer, w_gate, w_up, w_down, tuple(chunk_rows))
    out_l = _combine(ys[0], lc, lg, cnt, off, S, cap_l, 2)
    out_c = None
    if m_c is not None:
        out_c = _combine(ys[1], lcc, lgc, cntc, offc, L, cap_c, 1)
    return out_l, out_c


def kernel(x, c, ctx, c_ctx, ada_w, ada_b, norm1, norm2, ab_w_in, ab_q_norm, ab_w_uq, ab_kv_norm,
           ab_w_ukv, ab_w_o, c_w_in, c_q_gain, c_k_gain, c_w_o, moe_router, moe_w_gate, moe_w_up,
           moe_w_down, final_norm):
    B, S, D = x.shape
    L = ctx.shape[1]
    depth = ada_w.shape[0]
    tm_l, tm_c = 512, L
    row2 = lambda v: v.reshape(1, -1)

    c8 = jnp.zeros((8, D), F32).at[:B].set(c).at[B].set(c_ctx)
    mod = _ada(c8, ada_w, ada_b)
    mod_l = [mod[i, :B][:, None, :] for i in range(depth)]
    mod_c = [jnp.broadcast_to(mod[i, B][None, None, :], (B, 1, N_MOD * D)) for i in range(depth)]

    h_l, h_c = x, ctx
    moe_l = moe_c = None
    for i in range(depth):
        last = i == depth - 1
        j = i // 2
        wr_t = moe_router[i].T
        n2 = row2(norm2[i])
        if i % 2 == 0:
            assert moe_l is None
            tabs_l, tabs_c = _rope_tables(S, L, ((MLA_NOPE, MLA_ROPE // 2, "row"),
                                                 (MLA_NOPE + MLA_ROPE // 2, MLA_ROPE // 2, "col")))
            w_ukv = ab_w_ukv[j]
            ekr = np.zeros((MLA_ROPE, MLA_HEADS * LANES), np.float32)
            for hd in range(MLA_HEADS):
                ekr[np.arange(MLA_ROPE), hd * LANES + MLA_NOPE + np.arange(MLA_ROPE)] = 1.0
            w = {
                "win": ab_w_in[j].astype(BF16),
                "qn": row2(ab_q_norm[j]),
                "wuq": _head_slots(ab_w_uq[j], MLA_HEADS, MLA_QK, 0, MLA_QK).astype(BF16),
                "kvn": row2(ab_kv_norm[j]),
                "wuk": _head_slots(w_ukv, MLA_HEADS, MLA_NOPE + MLA_V, 0, MLA_NOPE).astype(BF16),
                "wuvt": _head_slots(w_ukv, MLA_HEADS, MLA_NOPE + MLA_V, MLA_NOPE,
                                    MLA_NOPE + MLA_V).T.astype(BF16),
                "ekr": jnp.asarray(ekr, BF16),
            }
            n1 = row2(norm1[i])
            f_l, q_l, k_l, vt_l = _in_ab(h_l, mod_l[i], n1, w, tabs_l, tm_l)
            f_c, q_c, k_c, vt_c = _in_ab(h_c, mod_c[i], n1, w, tabs_c, tm_c)
            o_l = _attention(q_l[:, :, None], k_c, vt_c, k_l, vt_l, min(S, ATTN_QUERIES), ATTN_KEYS,
                             MLA_VROWS, MLA_V)
            yf_l = _fourier_lat(f_l)
            wof = ab_w_o[j][:FNET_WIDTH].astype(BF16)
            woa = jnp.pad(ab_w_o[j][FNET_WIDTH:].reshape(MLA_HEADS, MLA_V, D),
                          ((0, 0), (0, LANES - MLA_V), (0, 0))).reshape(MLA_HEADS * LANES, D).astype(BF16)
            h_l, m_l, aff_l = _out_proj(h_l, mod_l[i], yf_l, wof, o_l, woa, n2, wr_t, tm_l)
            m_c = aff_c = None
            if not last:
                o_c = _attention(q_c[:, :, None], k_c, vt_c, None, None, L, L, MLA_VROWS, MLA_V)
                yf_c = _fourier_ctx(f_c)
                h_c, m_c, aff_c = _out_proj(h_c, mod_c[i], yf_c, wof, o_c, woa, n2, wr_t, tm_c)
        else:
            tabs_l, tabs_c = _rope_tables(S, L, ((0, GQA_HEAD_DIM // 2, "row"),
                                                 (GQA_HEAD_DIM // 2, GQA_HEAD_DIM // 2, "col")))
            nqk = (GQA_HEADS + GQA_KV_HEADS) * GQA_HEAD_DIM
            w = {"wqk": c_w_in[j][:, :nqk].astype(BF16), "wvt": c_w_in[j][:, nqk:].T.astype(BF16),
                 "qg": row2(c_q_gain[j]), "kg": row2(c_k_gain[j])}
            n1 = row2(norm1[i])
            h_l, q_l, k_l, vt_l = _in_c(h_l, moe_l, mod_l[i - 1], mod_l[i], n1, w, tabs_l, tm_l)
            h_c, q_c, k_c, vt_c = _in_c(h_c, moe_c, mod_c[i - 1], mod_c[i], n1, w, tabs_c, tm_c)
            grp = lambda q: q.reshape(B, GQA_KV_HEADS, GQA_GROUP, q.shape[2], LANES)
            o_l = _attention(grp(q_l), k_c, vt_c, k_l, vt_l, min(S, ATTN_QUERIES // GQA_GROUP),
                             ATTN_KEYS, GQA_HEAD_DIM)
            woa = c_w_o[j].astype(BF16)
            h_l, m_l, aff_l = _out_proj(h_l, mod_l[i], None, None, o_l, woa, n2, wr_t, tm_l)
            m_c = aff_c = None
            if not last:
                o_c = _attention(grp(q_c), k_c, vt_c, None, None, L, L, GQA_HEAD_DIM)
                h_c, m_c, aff_c = _out_proj(h_c, mod_c[i], None, None, o_c, woa, n2, wr_t, tm_c)
        moe_l, moe_c = _moe(m_l, aff_l, m_c, aff_c, i, moe_w_gate, moe_w_up, moe_w_down)
    return _final(h_l, moe_l, mod_l[depth - 1], row2(final_norm), tm_l)
```

```python
import functools
import math

import jax
import jax.numpy as jnp
import numpy as np
from jax import lax
from jax.experimental import pallas as pl
from jax.experimental.pallas import tpu as pltpu

F32 = jnp.float32
BF16 = jnp.bfloat16
I32 = jnp.int32
HIGHEST = lax.Precision.HIGHEST

D_MODEL = 1024
GRID_W = 64
EPS = 1e-6
ROPE_THETA = 10000.0
N_MOD = 6
FNET_GROUPS = 4
FNET_GROUP_DIM = 128
FNET_WIDTH = FNET_GROUPS * FNET_GROUP_DIM
MLA_HEADS = 8
MLA_Q_LORA = 256
MLA_KV_LORA = 128
MLA_NOPE = 64
MLA_ROPE = 32
MLA_V = 64
MLA_QK = MLA_NOPE + MLA_ROPE
GQA_HEADS = 8
GQA_KV_HEADS = 2
GQA_GROUP = GQA_HEADS // GQA_KV_HEADS
GQA_HEAD_DIM = 128
N_EXPERTS = 16
EXPERT_FF = 2048
CAPACITY_FACTOR = 2

LANES = 128
MLA_VROWS = MLA_V + 16
SUB = D_MODEL // LANES
VMEM_LIMIT = 56 * 1024 * 1024
LOG2E = math.log2(math.e)
NT = (((1,), (1,)), ((), ()))


def _params(sem, vmem=None):
    return pltpu.CompilerParams(dimension_semantics=sem, vmem_limit_bytes=vmem)


def _rms(x, g):
    return x * lax.rsqrt(jnp.mean(x * x, axis=-1, keepdims=True) + EPS) * g


def _modulate(h, g, shift, scale):
    return _rms(h, g) * (1.0 + scale) + shift


def _rope(x, cos, sin_fwd, sin_bwd, shift):
    return (x * cos + pltpu.roll(x, LANES - shift, 1) * sin_fwd
            + pltpu.roll(x, shift, 1) * sin_bwd)


def _rope_t(x, cos, sin_fwd, sin_bwd, shift):
    return (x * cos + pltpu.roll(x, LANES - shift, 0) * sin_fwd
            + pltpu.roll(x, shift, 0) * sin_bwd)


def _untile(ref, rows):
    x = ref[rows]
    n = x.shape[0]
    y = pltpu.einshape("grsl->gsrl", x.reshape(n // 8, 8, SUB, LANES))
    return jnp.concatenate([y[:, s].reshape(n, LANES) for s in range(SUB)], axis=1)


def _store_tiled(ref, rows, x):
    n = x.shape[0]
    y = jnp.concatenate([x[:, s * LANES:(s + 1) * LANES].reshape(n // 8, 1, 8, LANES)
                         for s in range(SUB)], axis=1)
    ref[rows] = pltpu.einshape("gsrl->grsl", y).reshape(n, SUB, LANES)


def _ada_kernel(c_ref, w_ref, b_ref, o_ref):
    c = c_ref[...]
    x = c * jax.nn.sigmoid(c)
    o_ref[0] = jnp.dot(x, w_ref[0], precision=HIGHEST, preferred_element_type=F32) + b_ref[0]


def _ada(c8, ada_w, ada_b):
    depth, d, n = ada_w.shape
    tn = 1536
    return pl.pallas_call(
        _ada_kernel,
        out_shape=jax.ShapeDtypeStruct((depth, 8, n), F32),
        grid=(depth, n // tn),
        in_specs=[pl.BlockSpec((8, d), lambda l, j: (0, 0)),
                  pl.BlockSpec((1, d, tn), lambda l, j: (l, 0, j)),
                  pl.BlockSpec((1, 1, tn), lambda l, j: (l, 0, j))],
        out_specs=pl.BlockSpec((1, 8, tn), lambda l, j: (l, 0, j)),
        compiler_params=_params(("arbitrary", "arbitrary"), VMEM_LIMIT),
        name="ada",
    )(c8, ada_w, ada_b.reshape(depth, 1, n))


def _in_ab_kernel(scale, *refs):
    (h_ref, mod_ref, n1_ref, win_ref, qn_ref, wuqt_ref, kvn_ref, wuk_ref, wuvt_ref, ekr_ref,
     cos_ref, sf_ref, sb_ref, cost_ref, sft_ref, sbt_ref, f_ref, qt_ref, k_ref, vt_ref) = refs
    D = D_MODEL
    h = h_ref[0]
    mod = mod_ref[0]
    a = _modulate(h, n1_ref[...], mod[:, 0:D], mod[:, D:2 * D]).astype(BF16)
    p = jnp.dot(a, win_ref[...], preferred_element_type=F32)
    o = FNET_WIDTH
    f_ref[0] = p[:, :o]
    cq = p[:, o:o + MLA_Q_LORA]
    o += MLA_Q_LORA
    ckv = p[:, o:o + MLA_KV_LORA]
    o += MLA_KV_LORA
    kr = p[:, o:o + MLA_ROPE]
    cqn = _rms(cq, qn_ref[...]).astype(BF16)
    ckvn = _rms(ckv, kvn_ref[...]).astype(BF16)
    qt = lax.dot_general(wuqt_ref[...], cqn, NT, preferred_element_type=F32)
    k = (jnp.dot(ckvn, wuk_ref[...], preferred_element_type=F32)
         + jnp.dot(kr.astype(BF16), ekr_ref[...], preferred_element_type=F32))
    vt = lax.dot_general(wuvt_ref[...], ckvn, NT, preferred_element_type=F32)
    cos, sf, sb = cos_ref[...], sf_ref[...], sb_ref[...]
    cost, sft, sbt = cost_ref[...], sft_ref[...], sbt_ref[...]
    shift = MLA_ROPE // 4
    ones = lax.broadcasted_iota(I32, (LANES, vt.shape[1]), 0) == MLA_V
    for hd in range(MLA_HEADS):
        sl = slice(hd * LANES, (hd + 1) * LANES)
        qt_ref[0, hd] = (_rope_t(qt[sl, :], cost, sft, sbt, shift) * scale).astype(BF16)
        k_ref[0, hd] = _rope(k[:, sl], cos, sf, sb, shift).astype(BF16)
        vt_ref[0, hd] = jnp.where(ones, 1.0, vt[sl, :]).astype(BF16)


def _in_ab(h, mod, n1, w, tables, tm):
    B, R, D = h.shape
    H = MLA_HEADS
    full = lambda a: pl.BlockSpec(a.shape, lambda b, i: (0,) * a.ndim)
    row = lambda w_: pl.BlockSpec((1, tm, w_), lambda b, i: (b, i, 0))
    tab = pl.BlockSpec((tm, LANES), lambda b, i: (i, 0))
    hd = pl.BlockSpec((1, H, tm, LANES), lambda b, i: (b, 0, i, 0))
    hdt = pl.BlockSpec((1, H, LANES, tm), lambda b, i: (b, 0, 0, i))
    tabt = pl.BlockSpec((LANES, tm), lambda b, i: (0, i))
    weights = [n1, w["win"], w["qn"], w["wuqt"], w["kvn"], w["wuk"], w["wuvt"], w["ekr"]]
    return pl.pallas_call(
        functools.partial(_in_ab_kernel, MLA_QK ** -0.5 * LOG2E),
        out_shape=(jax.ShapeDtypeStruct((B, R, FNET_WIDTH), F32),
                   jax.ShapeDtypeStruct((B, H, LANES, R), BF16),
                   jax.ShapeDtypeStruct((B, H, R, LANES), BF16),
                   jax.ShapeDtypeStruct((B, H, LANES, R), BF16)),
        grid=(B, R // tm),
        in_specs=[row(D), pl.BlockSpec((1, 1, N_MOD * D), lambda b, i: (b, 0, 0))]
        + [full(a) for a in weights] + [tab, tab, tab, tabt, tabt, tabt],
        out_specs=(row(FNET_WIDTH), hdt, hd, hdt),
        compiler_params=_params(("arbitrary", "arbitrary"), VMEM_LIMIT),
        name="in_ab",
    )(h, mod, *weights, *tables, *(t.T for t in tables))


def _in_c_kernel(scale, *refs):
    (h_ref, moe_ref, modp_ref, mod_ref, n1_ref, wqt_ref, wk_ref, wvt_ref, qg_ref, kg_ref,
     cos_ref, sf_ref, sb_ref, cost_ref, sft_ref, sbt_ref, h2_ref, qt_ref, k_ref, vt_ref) = refs
    D = D_MODEL
    h = h_ref[0] + modp_ref[0][:, 5 * D:6 * D] * _untile(moe_ref.at[0], slice(None))
    h2_ref[0] = h
    mod = mod_ref[0]
    a = _modulate(h, n1_ref[...], mod[:, 0:D], mod[:, D:2 * D]).astype(BF16)
    qt = lax.dot_general(wqt_ref[...], a, NT, preferred_element_type=F32)
    p = jnp.dot(a, wk_ref[...], preferred_element_type=F32)
    vt = lax.dot_general(wvt_ref[...], a, NT, preferred_element_type=F32)
    cos, sf, sb = cos_ref[...], sf_ref[...], sb_ref[...]
    cost, sft, sbt = cost_ref[...], sft_ref[...], sbt_ref[...]
    shift = GQA_HEAD_DIM // 4
    for hd in range(GQA_HEADS):
        x = qt[hd * LANES:(hd + 1) * LANES, :]
        x = x * lax.rsqrt(jnp.mean(x * x, axis=0, keepdims=True) + EPS) * qg_ref[...]
        qt_ref[0, hd] = (_rope_t(x, cost, sft, sbt, shift) * scale).astype(BF16)
    for hd in range(GQA_KV_HEADS):
        x = _rms(p[:, hd * LANES:(hd + 1) * LANES], kg_ref[...])
        k_ref[0, hd] = _rope(x, cos, sf, sb, shift).astype(BF16)
        vt_ref[0, hd] = vt[hd * LANES:(hd + 1) * LANES, :].astype(BF16)


def _in_c(h, moe, mod_prev, mod, n1, w, tables, tm):
    B, R, D = h.shape
    full = lambda a: pl.BlockSpec(a.shape, lambda b, i: (0,) * a.ndim)
    row = lambda w_: pl.BlockSpec((1, tm, w_), lambda b, i: (b, i, 0))
    tiled = pl.BlockSpec((1, tm, SUB, LANES), lambda b, i: (b, i, 0, 0))
    modspec = pl.BlockSpec((1, 1, N_MOD * D), lambda b, i: (b, 0, 0))
    tab = pl.BlockSpec((tm, LANES), lambda b, i: (i, 0))
    hd = lambda n: pl.BlockSpec((1, n, tm, LANES), lambda b, i: (b, 0, i, 0))
    hdt = pl.BlockSpec((1, GQA_KV_HEADS, LANES, tm), lambda b, i: (b, 0, 0, i))
    tabt = pl.BlockSpec((LANES, tm), lambda b, i: (0, i))
    hdq = pl.BlockSpec((1, GQA_HEADS, LANES, tm), lambda b, i: (b, 0, 0, i))
    weights = [n1, w["wqt"], w["wk"], w["wvt"], w["qg"], w["kg"]]
    return pl.pallas_call(
        functools.partial(_in_c_kernel, GQA_HEAD_DIM ** -0.5 * LOG2E),
        out_shape=(jax.ShapeDtypeStruct((B, R, D), F32),
                   jax.ShapeDtypeStruct((B, GQA_HEADS, LANES, R), BF16),
                   jax.ShapeDtypeStruct((B, GQA_KV_HEADS, R, LANES), BF16),
                   jax.ShapeDtypeStruct((B, GQA_KV_HEADS, LANES, R), BF16)),
        grid=(B, R // tm),
        in_specs=[row(D), tiled, modspec, modspec] + [full(a) for a in weights]
        + [tab, tab, tab, tabt, tabt, tabt],
        out_specs=(row(D), hdq, hd(GQA_KV_HEADS), hdt),
        compiler_params=_params(("arbitrary", "arbitrary"), VMEM_LIMIT),
        name="in_c",
    )(h, moe, mod_prev, mod, *weights, *tables, *(t.T for t in tables))


ATTN_QUERIES = 4096
ATTN_KEYS = 512


def _attn_kernel(n_lat, tk, dv, ones_row, *refs):
    if n_lat:
        q_ref, kc_ref, vct_ref, kl_ref, vlt_ref, o_ref, acc_ref, s0, s1, p0, p1 = refs
        s_bufs, p_bufs = (s0, s1), (p0, p1)
    else:
        q_ref, kc_ref, vct_ref, o_ref, acc_ref = refs
    G, tq = q_ref.shape[2], q_ref.shape[4]
    q = jnp.concatenate([q_ref[0, 0, g] for g in range(G)], axis=1)

    def scores(k):
        return jnp.dot(k, q, preferred_element_type=F32)

    s = scores(kc_ref[0, 0])
    m = jnp.max(s, axis=0, keepdims=True)
    p = jnp.exp2(s - m)
    l = jnp.sum(p, axis=0, keepdims=True) if ones_row is None else jnp.zeros_like(m)
    if dv < LANES:
        acc_ref[...] = jnp.zeros_like(acc_ref)
    acc_ref[:dv, :] = jnp.dot(vct_ref[0, 0, :dv, :], p.astype(BF16), preferred_element_type=F32)

    if n_lat:
        assert n_lat == 1 or n_lat % 2 == 0

        def chunk(c):
            return pl.ds(pl.multiple_of(c * tk, tk), tk)

        def score_stage(c, slot):
            s_bufs[slot][...] = scores(kl_ref[0, 0, chunk(c), :])

        def softmax_stage(slot, m, l):
            s = s_bufs[slot][...]
            m_new = jnp.maximum(m, jnp.max(s, axis=0, keepdims=True))
            p = jnp.exp2(s - m_new)
            p_bufs[slot][...] = p.astype(BF16)
            alpha = jnp.exp2(m - m_new)
            if ones_row is None:
                l = alpha * l + jnp.sum(p, axis=0, keepdims=True)
            return m_new, l, alpha

        def value_stage(c, slot, alpha):
            pv = jnp.dot(vlt_ref[0, 0, :dv, chunk(c)], p_bufs[slot][...],
                         preferred_element_type=F32)
            acc_ref[:dv, :] = alpha * acc_ref[:dv, :] + pv

        score_stage(0, 0)
        m, l, alpha = softmax_stage(0, m, l)
        if n_lat > 1:
            score_stage(1, 1)

            def body(i, carry):
                m, l, alpha = carry
                for slot in (0, 1):
                    c = 2 * i + slot
                    score_stage(c + 2, slot)
                    value_stage(c, slot, alpha)
                    m, l, alpha = softmax_stage(1 - slot, m, l)
                return m, l, alpha

            m, l, alpha = lax.fori_loop(0, (n_lat - 2) // 2, body, (m, l, alpha))
            value_stage(n_lat - 2, 0, alpha)
            m, l, alpha = softmax_stage(1, m, l)
        value_stage(n_lat - 1, (n_lat - 1) % 2, alpha)
    if ones_row is not None:
        l = acc_ref[ones_row:ones_row + 1, :]
    o_ref[0, 0] = (acc_ref[...] / l).T.reshape(G, tq, LANES).astype(BF16)


def _attention(q, kc, vct, kl, vlt, tq, tk, dv, ones_row=None):
    B, Hk, G, _, R = q.shape
    Lc = kc.shape[2]
    n_lat = 0 if kl is None else kl.shape[2] // tk
    qspec = pl.BlockSpec((1, 1, G, LANES, tq), lambda b, h, i: (b, h, 0, 0, i))
    ospec = pl.BlockSpec((1, 1, G, tq, LANES), lambda b, h, i: (b, h, 0, i, 0))
    kspec = lambda n: pl.BlockSpec((1, 1, n, LANES), lambda b, h, i: (b, h, 0, 0))
    vspec = lambda n: pl.BlockSpec((1, 1, LANES, n), lambda b, h, i: (b, h, 0, 0))
    ins = [q, kc, vct] + ([kl, vlt] if n_lat else [])
    specs = [qspec, kspec(Lc), vspec(Lc)] + ([kspec(kl.shape[2]), vspec(kl.shape[2])] if n_lat else [])
    return pl.pallas_call(
        functools.partial(_attn_kernel, n_lat, tk, dv, ones_row),
        out_shape=jax.ShapeDtypeStruct((B, Hk, G, R, LANES), BF16),
        grid=(B, Hk, R // tq),
        in_specs=specs,
        out_specs=ospec,
        scratch_shapes=[pltpu.VMEM((LANES, G * tq), F32)]
        + ([pltpu.VMEM((tk, G * tq), F32)] * 2 + [pltpu.VMEM((tk, G * tq), BF16)] * 2 if n_lat else []),
        compiler_params=_params(("arbitrary", "arbitrary", "arbitrary"), VMEM_LIMIT),
        name="attn",
    )(*ins)


def _dft_mats(n):
    k = np.arange(n, dtype=np.float64)
    ang = 2.0 * np.pi * np.outer(k, k) / n
    return np.cos(ang), np.sin(ang)


def _split(x):
    hi = x.astype(BF16)
    return hi, (x - hi.astype(F32)).astype(BF16)


def _dot3(a, b):
    d = lambda p, q: jnp.dot(p, q, preferred_element_type=F32)
    return d(a[0], b[0]) + d(a[0], b[1]) + d(a[1], b[0])


def _split_const(m):
    return jnp.stack(_split(jnp.asarray(m, F32)))


def _dft1_kernel(x_ref, m1_ref, tw_ref, o_ref):
    s1, ns2 = x_ref.shape[1], x_ref.shape[2]
    for i in range(ns2):
        a = jnp.dot(m1_ref[...], x_ref[0, :, i, :], precision=HIGHEST, preferred_element_type=F32)
        are, aim = a[:s1], a[s1:]
        tre = jnp.tile(tw_ref[0, :, i * LANES:(i + 1) * LANES], (1, FNET_GROUPS))
        tim = jnp.tile(tw_ref[1, :, i * LANES:(i + 1) * LANES], (1, FNET_GROUPS))
        o_ref[0, 0, :, i, :] = are * tre - aim * tim
        o_ref[0, 1, :, i, :] = are * tim + aim * tre


def _dft2_kernel(a_ref, m2_ref, m3_ref, o_ref):
    kb, n2 = a_ref.shape[2], a_ref.shape[3]
    for j in range(kb):
        rhs = jnp.concatenate([a_ref[0, 0, j], a_ref[0, 1, j]], axis=0)
        y = _dot3((m2_ref[0], m2_ref[1]), _split(rhs))
        for g in range(FNET_GROUPS):
            sl = slice(g * LANES, (g + 1) * LANES)
            lhs = jnp.concatenate([y[:n2, sl], y[n2:, sl]], axis=1)
            o_ref[0, :, j, sl] = _dot3(_split(lhs), (m3_ref[0], m3_ref[1]))


def _fourier_lat(f):
    B, S, W = f.shape
    n2 = LANES
    s1 = S // n2
    c1, sn1 = _dft_mats(s1)
    c2, sn2 = _dft_mats(n2)
    cc, sc = _dft_mats(FNET_GROUP_DIM)
    m1 = jnp.asarray(np.concatenate([c1, -sn1], axis=0), F32)
    m2 = _split_const(np.block([[c2, sn2], [-sn2, c2]]))
    norm = 1.0 / math.sqrt(S * FNET_GROUP_DIM)
    m3 = _split_const(np.concatenate([cc, sc], axis=0) * norm)
    ang = 2.0 * np.pi * np.outer(np.arange(s1), np.arange(n2)) / S
    tw = np.stack([np.cos(ang), -np.sin(ang)])
    tw = jnp.asarray(np.repeat(tw[:, :, :, None], LANES, axis=3).reshape(2, s1, n2 * LANES), F32)

    ns2 = 8
    a = pl.pallas_call(
        _dft1_kernel,
        out_shape=jax.ShapeDtypeStruct((B, 2, s1, n2, W), F32),
        grid=(B, n2 // ns2),
        in_specs=[pl.BlockSpec((1, s1, ns2, W), lambda b, j: (b, 0, j, 0)),
                  pl.BlockSpec(m1.shape, lambda b, j: (0, 0)),
                  pl.BlockSpec((2, s1, ns2 * LANES), lambda b, j: (0, 0, j))],
        out_specs=pl.BlockSpec((1, 2, s1, ns2, W), lambda b, j: (b, 0, 0, j, 0)),
        compiler_params=_params(("arbitrary", "arbitrary"), VMEM_LIMIT),
        name="dft1",
    )(f.reshape(B, s1, n2, W), m1, tw)
    kb = min(8, s1)
    y = pl.pallas_call(
        _dft2_kernel,
        out_shape=jax.ShapeDtypeStruct((B, n2, s1, W), F32),
        grid=(B, s1 // kb),
        in_specs=[pl.BlockSpec((1, 2, kb, n2, W), lambda b, j: (b, 0, j, 0, 0)),
                  pl.BlockSpec(m2.shape, lambda b, j: (0, 0, 0)),
                  pl.BlockSpec(m3.shape, lambda b, j: (0, 0, 0))],
        out_specs=pl.BlockSpec((1, n2, kb, W), lambda b, j: (b, 0, j, 0)),
        compiler_params=_params(("arbitrary", "arbitrary"), VMEM_LIMIT),
        name="dft2",
    )(a, m2, m3)
    return y.reshape(B, S, W)


def _dftc_kernel(f_ref, mc_ref, m3_ref, o_ref):
    n = f_ref.shape[1]
    a = _dot3((mc_ref[0], mc_ref[1]), _split(f_ref[0]))
    for g in range(FNET_GROUPS):
        sl = slice(g * LANES, (g + 1) * LANES)
        lhs = jnp.concatenate([a[:n, sl], a[n:, sl]], axis=1)
        o_ref[0, :, sl] = _dot3(_split(lhs), (m3_ref[0], m3_ref[1]))


def _fourier_ctx(f):
    B, L, W = f.shape
    c, s = _dft_mats(L)
    cc, sc = _dft_mats(FNET_GROUP_DIM)
    mc = _split_const(np.concatenate([c, -s], axis=0))
    m3 = _split_const(np.concatenate([cc, sc], axis=0) / math.sqrt(L * FNET_GROUP_DIM))
    return pl.pallas_call(
        _dftc_kernel,
        out_shape=jax.ShapeDtypeStruct((B, L, W), F32),
        grid=(B,),
        in_specs=[pl.BlockSpec((1, L, W), lambda b: (b, 0, 0)),
                  pl.BlockSpec(mc.shape, lambda b: (0, 0, 0)),
                  pl.BlockSpec(m3.shape, lambda b: (0, 0, 0))],
        out_specs=pl.BlockSpec((1, L, W), lambda b: (b, 0, 0)),
        compiler_params=_params(("arbitrary",), VMEM_LIMIT),
        name="dftc",
    )(f, mc, m3)


def _out_kernel(has_f, *refs):
    if has_f:
        (h_ref, mod_ref, yf_ref, wof_ref, o_ref, woa_ref, n2_ref, wr_ref,
         h1_ref, m_ref, aff_ref) = refs
    else:
        (h_ref, mod_ref, o_ref, woa_ref, n2_ref, wr_ref, h1_ref, m_ref, aff_ref) = refs
    D = D_MODEL
    Hk, G = o_ref.shape[1], o_ref.shape[2]
    ocat = jnp.concatenate([o_ref[0, hk, g] for hk in range(Hk) for g in range(G)], axis=1)
    y = jnp.dot(ocat, woa_ref[...], preferred_element_type=F32)
    if has_f:
        y = y + jnp.dot(yf_ref[0].astype(BF16), wof_ref[...], preferred_element_type=F32)
    mod = mod_ref[0]
    h1 = h_ref[0] + mod[:, 2 * D:3 * D] * y
    h1_ref[0] = h1
    m = _modulate(h1, n2_ref[...], mod[:, 3 * D:4 * D], mod[:, 4 * D:5 * D])
    _store_tiled(m_ref.at[0], slice(None), m)
    logit = lax.dot_general(wr_ref[...], m, NT, precision=HIGHEST,
                            preferred_element_type=F32)
    e = jnp.exp(logit - jnp.max(logit, axis=0, keepdims=True))
    aff_ref[0] = e / jnp.sum(e, axis=0, keepdims=True)


def _out_proj(h, mod, yf, wof, o, woa, n2, wr_t, tm):
    B, R, D = h.shape
    _, Hk, G, _, _ = o.shape
    full = lambda a: pl.BlockSpec(a.shape, lambda b, i: (0,) * a.ndim)
    row = lambda w_: pl.BlockSpec((1, tm, w_), lambda b, i: (b, i, 0))
    modspec = pl.BlockSpec((1, 1, N_MOD * D), lambda b, i: (b, 0, 0))
    ospec = pl.BlockSpec((1, Hk, G, tm, LANES), lambda b, i: (b, 0, 0, i, 0))
    has_f = yf is not None
    ins = [h, mod] + ([yf, wof] if has_f else []) + [o, woa, n2, wr_t]
    specs = ([row(D), modspec] + ([row(FNET_WIDTH), full(wof)] if has_f else [])
             + [ospec, full(woa), full(n2), full(wr_t)])
    return pl.pallas_call(
        functools.partial(_out_kernel, has_f),
        out_shape=(jax.ShapeDtypeStruct((B, R, D), F32),
                   jax.ShapeDtypeStruct((B, R, SUB, LANES), F32),
                   jax.ShapeDtypeStruct((B, N_EXPERTS, R), F32)),
        grid=(B, R // tm),
        in_specs=specs,
        out_specs=(row(D), pl.BlockSpec((1, tm, SUB, LANES), lambda b, i: (b, i, 0, 0)),
                   pl.BlockSpec((1, N_EXPERTS, tm), lambda b, i: (b, 0, i))),
        compiler_params=_params(("arbitrary", "arbitrary"), VMEM_LIMIT),
        name="out_proj",
    )(*ins)


COMPACT_UNROLL = 4


def _topk_kernel(cap, aff_ref, u_ref, ones_ref, lmat_ref, lc_ref, lg_ref, cnt_ref, off_ref,
                 pos_ref, ac_ref, lct_ref, lgt_ref):
    a = aff_ref[0]
    E, N = a.shape
    NC = N // LANES
    R = NC * E
    keys = pltpu.bitcast(a, I32)

    def bit_step(i, tau):
        cand = tau | jnp.left_shift(jnp.int32(1), 30 - i)
        cnt = jnp.sum((keys >= cand).astype(I32), axis=1, keepdims=True)
        return jnp.where(cnt >= cap, cand, tau)

    tau = lax.fori_loop(0, 31, bit_step, jnp.zeros((E, 1), I32))
    gt = (keys > tau).astype(F32)
    eq = (keys == tau).astype(F32)
    need = (cap - jnp.sum(gt, axis=1, keepdims=True))

    def chunked(x):
        return jnp.concatenate([x[:, c * LANES:(c + 1) * LANES] for c in range(NC)], axis=0)

    a_c, gt_c, eq_c = chunked(a), chunked(gt), chunked(eq)
    need_c = jnp.tile(need, (NC, 1))

    def prefix(x):
        xb = x.astype(BF16)
        loc = jnp.dot(xb, u_ref[...], preferred_element_type=F32)
        tot = jnp.dot(xb, ones_ref[...], preferred_element_type=F32)
        offs = jnp.dot(lmat_ref[...], tot.astype(BF16), preferred_element_type=F32)
        return loc, tot, offs

    loc, tot, offs = prefix(eq_c)
    sel = jnp.maximum(gt_c, jnp.where(loc + offs < need_c, eq_c, 0.0))
    loc, tot, offs = prefix(sel)
    cnt_ref[0] = tot.astype(I32)
    off_ref[0] = offs.astype(I32)
    Rp = pos_ref.shape[1]
    selpos = jnp.where(sel > 0.0, loc, -1.0)
    if Rp > R:
        selpos = jnp.concatenate([selpos, jnp.full((Rp - R, LANES), -1.0, F32)], axis=0)
        a_c = jnp.concatenate([a_c, jnp.zeros((Rp - R, LANES), F32)], axis=0)
    pos_ref[...] = selpos.T
    ac_ref[...] = a_c.T
    tok = (lax.broadcasted_iota(I32, (LANES, Rp), 0)
           + lax.broadcasted_iota(I32, (LANES, Rp), 1) // E * LANES).astype(F32)

    def compact(i, carry):
        p = pos_ref[...]
        a_t = ac_ref[...]
        for u in range(COMPACT_UNROLL):
            j = i * COMPACT_UNROLL + u
            hit = p == lax.convert_element_type(j, F32)
            lct_ref[pl.ds(j, 1), :] = jnp.sum(jnp.where(hit, tok, 0.0), axis=0, keepdims=True)
            lgt_ref[pl.ds(j, 1), :] = jnp.sum(jnp.where(hit, a_t, 0.0), axis=0, keepdims=True)
        return carry

    lax.fori_loop(0, LANES // COMPACT_UNROLL, compact, 0)
    lc_ref[0] = lct_ref[...].T[:R].astype(I32)
    lg_ref[0] = lgt_ref[...].T[:R]


def _topk(aff, cap):
    B, E, N = aff.shape
    NC = N // LANES
    R = NC * E
    i = np.arange(LANES)
    u = jnp.asarray(i[:, None] < i[None, :], BF16)
    ones = jnp.ones((LANES, LANES), BF16)
    r = np.arange(R)
    lmat = jnp.asarray((r[:, None] % E == r[None, :] % E) & (r[None, :] // E < r[:, None] // E), BF16)
    full = lambda a: pl.BlockSpec(a.shape, lambda b: (0,) * a.ndim)
    ospec = pl.BlockSpec((1, R, LANES), lambda b: (b, 0, 0))
    sds = lambda dt: jax.ShapeDtypeStruct((B, R, LANES), dt)
    lc, lg, cnt, off = pl.pallas_call(
        functools.partial(_topk_kernel, cap),
        out_shape=(sds(I32), sds(F32), sds(I32), sds(I32)),
        grid=(B,),
        in_specs=[pl.BlockSpec((1, E, N), lambda b: (b, 0, 0)), full(u), full(ones), full(lmat)],
        out_specs=(ospec, ospec, ospec, ospec),
        scratch_shapes=[pltpu.VMEM((LANES, -(-R // LANES) * LANES), F32)] * 4,
        compiler_params=_params(("arbitrary",), VMEM_LIMIT),
        name="topk",
    )(aff, u, ones, lmat)
    by_expert = lambda x: x.reshape(B, NC, E, LANES).transpose(0, 2, 1, 3)
    flat = lambda x: by_expert(x).reshape(B, E, NC * LANES)
    return (flat(lc), flat(lg), by_expert(cnt)[..., 0], by_expert(off)[..., 0])


GATHER_ROWS = 256
ROW_UNROLL = 4


def _gather_kernel(cap, B, cnt_ref, off_ref, *refs):
    lc_refs = refs[:B]
    m_hbm, o_ref, x_ref, sems = refs[B:]
    e = pl.program_id(0)
    NC = lc_refs[0].shape[1] // LANES
    for b in range(B):
        def chunk_body(c, carry, b=b):
            base = off_ref[b, e, c] + b * cap
            n = cnt_ref[b, e, c]

            def start_row(r, priority):
                t = lc_refs[b][0, c * LANES + r]
                pltpu.make_async_copy(m_hbm.at[b, t], x_ref.at[base + r],
                                      sems.at[b]).start(priority=priority)

            def group_body(i, carry):
                for u in range(ROW_UNROLL):
                    start_row(i * ROW_UNROLL + u, u % 2)
                return carry

            def tail_body(r, carry):
                start_row(r, 0)
                return carry

            full = n // ROW_UNROLL
            lax.fori_loop(0, full, group_body, carry)
            return lax.fori_loop(full * ROW_UNROLL, n, tail_body, carry)

        lax.fori_loop(0, NC, chunk_body, 0)
    step = min(GATHER_ROWS, cap)
    for b in range(B):
        done = x_ref.at[pl.ds(b * cap, cap)]
        pltpu.make_async_copy(done, done, sems.at[b]).wait()
        for r0 in range(b * cap, (b + 1) * cap, step):
            rows = slice(r0, r0 + step)
            o_ref[0, rows, :] = _untile(x_ref, rows).astype(BF16)


def _gather(m, lc, cnt, off, cap):
    B, N, sub, _ = m.shape
    E = lc.shape[1]
    lc4 = lc[:, :, None, :]

    def table(b):
        return pl.BlockSpec((None, None, 1, lc.shape[2]), lambda e, *_: (b, e, 0, 0),
                            memory_space=pltpu.SMEM)

    return pl.pallas_call(
        functools.partial(_gather_kernel, cap, B),
        out_shape=jax.ShapeDtypeStruct((E, B * cap, sub * LANES), BF16),
        grid_spec=pltpu.PrefetchScalarGridSpec(
            num_scalar_prefetch=2,
            grid=(E,),
            in_specs=[table(b) for b in range(B)] + [pl.BlockSpec(memory_space=pl.ANY)],
            out_specs=pl.BlockSpec((1, B * cap, sub * LANES), lambda e, *_: (e, 0, 0)),
            scratch_shapes=[pltpu.VMEM((B * cap, sub, LANES), m.dtype),
                            pltpu.SemaphoreType.DMA((B,))]),
        compiler_params=_params(("arbitrary",), VMEM_LIMIT),
        name="gather",
    )(cnt, off, *([lc4] * B), m)


def _ffn_kernel(n_parts, chunk_rows, *refs):
    wg_ref, wu_ref, wd_ref = refs[:3]
    x_refs = refs[3:3 + n_parts]
    o_refs = refs[3 + n_parts:3 + 2 * n_parts]
    acc_refs = refs[3 + 2 * n_parts:]
    j = pl.program_id(1)
    last = pl.num_programs(1) - 1
    wg = wg_ref[0].astype(BF16)
    wu = wu_ref[0].astype(BF16)
    wd = wd_ref[0].astype(BF16)
    chunks = [(x_ref, o_ref, acc_ref, slice(r0, r0 + nrows))
              for x_ref, o_ref, acc_ref, nrows in zip(x_refs, o_refs, acc_refs, chunk_rows)
              for r0 in range(0, x_ref.shape[1], nrows)]

    @pl.when(j == 0)
    def _():
        for acc_ref in acc_refs:
            acc_ref[...] = jnp.zeros_like(acc_ref)

    for x_ref, _, acc_ref, rows in chunks:
        x = x_ref[0, rows, :]
        a = jnp.dot(x, wg, preferred_element_type=F32)
        u = jnp.dot(x, wu, preferred_element_type=F32)
        hh = (a * jax.nn.sigmoid(a) * u).astype(BF16)
        acc_ref[rows, :] += jnp.dot(hh, wd, preferred_element_type=F32)

    @pl.when(j == last)
    def _():
        for _, o_ref, acc_ref, rows in chunks:
            _store_tiled(o_ref.at[0], rows, acc_ref[rows, :])


def _ffn(xs, layer, w_gate, w_up, w_down, chunk_rows):
    _, E, D, F = w_gate.shape
    tf = 512
    assert F // tf >= 2
    xspec = lambda x: pl.BlockSpec((1, x.shape[1], D), lambda e, j: (e, 0, 0))
    ospec = lambda x: pl.BlockSpec((1, x.shape[1], SUB, LANES), lambda e, j: (e, 0, 0, 0))
    outs = pl.pallas_call(
        functools.partial(_ffn_kernel, len(xs), chunk_rows),
        out_shape=tuple(jax.ShapeDtypeStruct((E, x.shape[1], SUB, LANES), F32) for x in xs),
        grid=(E, F // tf),
        in_specs=[pl.BlockSpec((None, 1, D, tf), lambda e, j: (layer, e, 0, j)),
                  pl.BlockSpec((None, 1, D, tf), lambda e, j: (layer, e, 0, j)),
                  pl.BlockSpec((None, 1, tf, D), lambda e, j: (layer, e, j, 0))]
        + [xspec(x) for x in xs],
        out_specs=tuple(ospec(x) for x in xs),
        scratch_shapes=[pltpu.VMEM((x.shape[1], D), F32) for x in xs],
        compiler_params=_params(("arbitrary", "arbitrary"), VMEM_LIMIT),
        name="ffn",
    )(w_gate, w_up, w_down, *xs)
    return outs


def _combine_kernel(cnt_ref, off_ref, lc_ref, lg_ref, y_ref, o_ref):
    b = pl.program_id(0)
    hf = pl.program_id(1)
    e = pl.program_id(2)
    nch = o_ref.shape[1] // LANES
    first = hf * o_ref.shape[1]

    @pl.when(e == 0)
    def _():
        o_ref[...] = jnp.zeros_like(o_ref)

    def chunk_body(ci, carry):
        c = hf * nch + ci
        base = off_ref[b, e, c]

        n = cnt_ref[b, e, c]

        def rows_body(r0, width):
            ts = [lc_ref[0, c * LANES + r0 + u] - first for u in range(width)]
            vals = [o_ref[0, ts[u]] + lg_ref[0, c * LANES + r0 + u] * y_ref[0, base + r0 + u]
                    for u in range(width)]
            for u in range(width):
                o_ref[0, ts[u]] = vals[u]

        def group_body(i, carry):
            rows_body(i * ROW_UNROLL, ROW_UNROLL)
            return carry

        def tail_body(r, carry):
            rows_body(r, 1)
            return carry

        full = n // ROW_UNROLL
        lax.fori_loop(0, full, group_body, carry)
        return lax.fori_loop(full * ROW_UNROLL, n, tail_body, carry)

    lax.fori_loop(0, nch, chunk_body, 0)


def _combine(y, lc, lg, cnt, off, n_tokens, cap, n_split):
    E = y.shape[0]
    B = lc.shape[0]
    nh = n_tokens // n_split
    smem = lambda: pl.BlockSpec((None, None, 1, lc.shape[2]), lambda b, h, e, *_: (b, e, 0, 0),
                                memory_space=pltpu.SMEM)
    return pl.pallas_call(
        _combine_kernel,
        out_shape=jax.ShapeDtypeStruct((B, n_tokens, SUB, LANES), F32),
        grid_spec=pltpu.PrefetchScalarGridSpec(
            num_scalar_prefetch=2,
            grid=(B, n_split, E),
            in_specs=[smem(), smem(),
                      pl.BlockSpec((1, cap, SUB, LANES), lambda b, h, e, *_: (e, b, 0, 0))],
            out_specs=pl.BlockSpec((1, nh, SUB, LANES), lambda b, h, e, *_: (b, h, 0, 0))),
        compiler_params=_params(("arbitrary", "arbitrary", "arbitrary"), VMEM_LIMIT),
        name="combine",
    )(cnt, off, lc[:, :, None, :], lg[:, :, None, :], y)


def _final_kernel(h_ref, moe_ref, mod_ref, g_ref, o_ref):
    D = D_MODEL
    h = h_ref[0] + mod_ref[0][:, 5 * D:6 * D] * _untile(moe_ref.at[0], slice(None))
    o_ref[0] = _rms(h, g_ref[...])


def _final(h, moe, mod, g, tm):
    B, R, D = h.shape
    row = pl.BlockSpec((1, tm, D), lambda b, i: (b, i, 0))
    return pl.pallas_call(
        _final_kernel,
        out_shape=jax.ShapeDtypeStruct((B, R, D), F32),
        grid=(B, R // tm),
        in_specs=[row, pl.BlockSpec((1, tm, SUB, LANES), lambda b, i: (b, i, 0, 0)),
                  pl.BlockSpec((1, 1, N_MOD * D), lambda b, i: (b, 0, 0)),
                  pl.BlockSpec((1, D), lambda b, i: (0, 0))],
        out_specs=row,
        compiler_params=_params(("arbitrary", "arbitrary"), VMEM_LIMIT),
        name="final",
    )(h, moe, mod, g)


def _rope_tables(n_lat, n_ctx, segments):
    t = jnp.arange(n_lat)
    pos = {"row": (t // GRID_W).astype(F32), "col": (t % GRID_W).astype(F32)}
    freq = {"row": np.zeros((2, LANES), np.float32), "col": np.zeros((2, LANES), np.float32)}
    first = np.zeros(LANES, np.float32)
    second = np.zeros(LANES, np.float32)
    for lane0, width, which in segments:
        half = width // 2
        idx = np.arange(half, dtype=np.float32)
        for lo, mask in ((lane0, first), (lane0 + half, second)):
            freq[which][0, lo:lo + half] = idx / half
            freq[which][1, lo:lo + half] = 1.0
            mask[lo:lo + half] = 1.0
    ang = jnp.zeros((n_lat, LANES), F32)
    for which in ("row", "col"):
        inv = (ROPE_THETA ** (-jnp.asarray(freq[which][0]))) * jnp.asarray(freq[which][1])
        ang = ang + pos[which][:, None] * inv[None, :]
    sin = jnp.sin(ang)
    lat = (jnp.cos(ang), -sin * first[None, :], sin * second[None, :])
    ctx = (jnp.ones((n_ctx, LANES), F32), jnp.zeros((n_ctx, LANES), F32), jnp.zeros((n_ctx, LANES), F32))
    return lat, ctx


def _head_slots(w, n_heads, width, lo, hi):
    k = w.shape[0]
    w3 = w.reshape(k, n_heads, width)[:, :, lo:hi]
    return jnp.pad(w3, ((0, 0), (0, 0), (0, LANES - (hi - lo)))).reshape(k, n_heads * LANES)


def _moe(m_l, aff_l, m_c, aff_c, layer, w_gate, w_up, w_down):
    B, S = m_l.shape[:2]
    cap_l = CAPACITY_FACTOR * S // N_EXPERTS
    lc, lg, cnt, off = _topk(aff_l, cap_l)
    xs = [_gather(m_l, lc, cnt, off, cap_l)]
    chunk_rows = [cap_l]
    if m_c is not None:
        L = m_c.shape[1]
        cap_c = CAPACITY_FACTOR * L // N_EXPERTS
        lcc, lgc, cntc, offc = _topk(aff_c, cap_c)
        xs.append(_gather(m_c, lcc, cntc, offc, cap_c))
        chunk_rows.append(B * cap_c)
    ys = _ffn(xs, layer, w_gate, w_up, w_down, tuple(chunk_rows))
    out_l = _combine(ys[0], lc, lg, cnt, off, S, cap_l, 2)
    out_c = None
    if m_c is not None:
        out_c = _combine(ys[1], lcc, lgc, cntc, offc, L, cap_c, 1)
    return out_l, out_c


def kernel(x, c, ctx, c_ctx, ada_w, ada_b, norm1, norm2, ab_w_in, ab_q_norm, ab_w_uq, ab_kv_norm,
           ab_w_ukv, ab_w_o, c_w_in, c_q_gain, c_k_gain, c_w_o, moe_router, moe_w_gate, moe_w_up,
           moe_w_down, final_norm):
    B, S, D = x.shape
    L = ctx.shape[1]
    depth = ada_w.shape[0]
    tm_l, tm_c = 512, L
    row2 = lambda v: v.reshape(1, -1)

    c8 = jnp.zeros((8, D), F32).at[:B].set(c).at[B].set(c_ctx)
    mod = _ada(c8, ada_w, ada_b)
    mod_l = [mod[i, :B][:, None, :] for i in range(depth)]
    mod_c = [jnp.broadcast_to(mod[i, B][None, None, :], (B, 1, N_MOD * D)) for i in range(depth)]

    h_l, h_c = x, ctx
    moe_l = moe_c = None
    for i in range(depth):
        last = i == depth - 1
        j = i // 2
        wr_t = moe_router[i].T
        n2 = row2(norm2[i])
        if i % 2 == 0:
            assert moe_l is None
            tabs_l, tabs_c = _rope_tables(S, L, ((MLA_NOPE, MLA_ROPE // 2, "row"),
                                                 (MLA_NOPE + MLA_ROPE // 2, MLA_ROPE // 2, "col")))
            w_ukv = ab_w_ukv[j]
            ekr = np.zeros((MLA_ROPE, MLA_HEADS * LANES), np.float32)
            for hd in range(MLA_HEADS):
                ekr[np.arange(MLA_ROPE), hd * LANES + MLA_NOPE + np.arange(MLA_ROPE)] = 1.0
            w = {
                "win": ab_w_in[j].astype(BF16),
                "qn": row2(ab_q_norm[j]),
                "wuqt": _head_slots(ab_w_uq[j], MLA_HEADS, MLA_QK, 0, MLA_QK).T.astype(BF16),
                "kvn": row2(ab_kv_norm[j]),
                "wuk": _head_slots(w_ukv, MLA_HEADS, MLA_NOPE + MLA_V, 0, MLA_NOPE).astype(BF16),
                "wuvt": _head_slots(w_ukv, MLA_HEADS, MLA_NOPE + MLA_V, MLA_NOPE,
                                    MLA_NOPE + MLA_V).T.astype(BF16),
                "ekr": jnp.asarray(ekr, BF16),
            }
            n1 = row2(norm1[i])
            f_l, q_l, k_l, vt_l = _in_ab(h_l, mod_l[i], n1, w, tabs_l, tm_l)
            f_c, q_c, k_c, vt_c = _in_ab(h_c, mod_c[i], n1, w, tabs_c, tm_c)
            o_l = _attention(q_l[:, :, None], k_c, vt_c, k_l, vt_l, min(S, ATTN_QUERIES), ATTN_KEYS,
                             MLA_VROWS, MLA_V)
            yf_l = _fourier_lat(f_l)
            wof = ab_w_o[j][:FNET_WIDTH].astype(BF16)
            woa = jnp.pad(ab_w_o[j][FNET_WIDTH:].reshape(MLA_HEADS, MLA_V, D),
                          ((0, 0), (0, LANES - MLA_V), (0, 0))).reshape(MLA_HEADS * LANES, D).astype(BF16)
            h_l, m_l, aff_l = _out_proj(h_l, mod_l[i], yf_l, wof, o_l, woa, n2, wr_t, tm_l)
            m_c = aff_c = None
            if not last:
                o_c = _attention(q_c[:, :, None], k_c, vt_c, None, None, L, L, MLA_VROWS, MLA_V)
                yf_c = _fourier_ctx(f_c)
                h_c, m_c, aff_c = _out_proj(h_c, mod_c[i], yf_c, wof, o_c, woa, n2, wr_t, tm_c)
        else:
            tabs_l, tabs_c = _rope_tables(S, L, ((0, GQA_HEAD_DIM // 2, "row"),
                                                 (GQA_HEAD_DIM // 2, GQA_HEAD_DIM // 2, "col")))
            nq = GQA_HEADS * GQA_HEAD_DIM
            nqk = nq + GQA_KV_HEADS * GQA_HEAD_DIM
            w = {"wqt": c_w_in[j][:, :nq].T.astype(BF16), "wk": c_w_in[j][:, nq:nqk].astype(BF16),
                 "wvt": c_w_in[j][:, nqk:].T.astype(BF16),
                 "qg": c_q_gain[j].reshape(-1, 1), "kg": row2(c_k_gain[j])}
            n1 = row2(norm1[i])
            h_l, q_l, k_l, vt_l = _in_c(h_l, moe_l, mod_l[i - 1], mod_l[i], n1, w, tabs_l, tm_l)
            h_c, q_c, k_c, vt_c = _in_c(h_c, moe_c, mod_c[i - 1], mod_c[i], n1, w, tabs_c, tm_c)
            grp = lambda q: q.reshape(B, GQA_KV_HEADS, GQA_GROUP, LANES, q.shape[3])
            o_l = _attention(grp(q_l), k_c, vt_c, k_l, vt_l, min(S, ATTN_QUERIES // GQA_GROUP),
                             ATTN_KEYS, GQA_HEAD_DIM)
            woa = c_w_o[j].astype(BF16)
            h_l, m_l, aff_l = _out_proj(h_l, mod_l[i], None, None, o_l, woa, n2, wr_t, tm_l)
            m_c = aff_c = None
            if not last:
                o_c = _attention(grp(q_c), k_c, vt_c, None, None, L, L, GQA_HEAD_DIM)
                h_c, m_c, aff_c = _out_proj(h_c, mod_c[i], None, None, o_c, woa, n2, wr_t, tm_c)
        moe_l, moe_c = _moe(m_l, aff_l, m_c, aff_c, i, moe_w_gate, moe_w_up, moe_w_down)
    return _final(h_l, moe_l, mod_l[depth - 1], row2(final_norm), tm_l)
```

```python
import functools
import math

import jax
import jax.numpy as jnp
import numpy as np
from jax import lax
from jax.experimental import pallas as pl
from jax.experimental.pallas import tpu as pltpu

F32 = jnp.float32
BF16 = jnp.bfloat16
I32 = jnp.int32
HIGHEST = lax.Precision.HIGHEST

D_MODEL = 1024
GRID_W = 64
EPS = 1e-6
ROPE_THETA = 10000.0
N_MOD = 6
FNET_GROUPS = 4
FNET_GROUP_DIM = 128
FNET_WIDTH = FNET_GROUPS * FNET_GROUP_DIM
MLA_HEADS = 8
MLA_Q_LORA = 256
MLA_KV_LORA = 128
MLA_NOPE = 64
MLA_ROPE = 32
MLA_V = 64
MLA_QK = MLA_NOPE + MLA_ROPE
GQA_HEADS = 8
GQA_KV_HEADS = 2
GQA_GROUP = GQA_HEADS // GQA_KV_HEADS
GQA_HEAD_DIM = 128
N_EXPERTS = 16
EXPERT_FF = 2048
CAPACITY_FACTOR = 2

LANES = 128
MLA_VROWS = MLA_V + 16
SUB = D_MODEL // LANES
VMEM_LIMIT = 56 * 1024 * 1024
LOG2E = math.log2(math.e)
NT = (((1,), (1,)), ((), ()))


def _params(sem, vmem=None):
    return pltpu.CompilerParams(dimension_semantics=sem, vmem_limit_bytes=vmem)


def _rms(x, g):
    return x * lax.rsqrt(jnp.mean(x * x, axis=-1, keepdims=True) + EPS) * g


def _modulate(h, g, shift, scale):
    return _rms(h, g) * (1.0 + scale) + shift


def _rope(x, cos, sin_fwd, sin_bwd, shift):
    return (x * cos + pltpu.roll(x, LANES - shift, 1) * sin_fwd
            + pltpu.roll(x, shift, 1) * sin_bwd)


def _rope_t(x, cos, sin_fwd, sin_bwd, shift):
    return (x * cos + pltpu.roll(x, LANES - shift, 0) * sin_fwd
            + pltpu.roll(x, shift, 0) * sin_bwd)


def _untile(ref, rows):
    x = ref[rows]
    n = x.shape[0]
    y = pltpu.einshape("grsl->gsrl", x.reshape(n // 8, 8, SUB, LANES))
    return jnp.concatenate([y[:, s].reshape(n, LANES) for s in range(SUB)], axis=1)


def _store_tiled(ref, rows, x):
    n = x.shape[0]
    y = jnp.concatenate([x[:, s * LANES:(s + 1) * LANES].reshape(n // 8, 1, 8, LANES)
                         for s in range(SUB)], axis=1)
    ref[rows] = pltpu.einshape("gsrl->grsl", y).reshape(n, SUB, LANES)


def _ada_kernel(c_ref, w_ref, b_ref, o_ref):
    c = c_ref[...]
    x = c * jax.nn.sigmoid(c)
    o_ref[0] = jnp.dot(x, w_ref[0], precision=HIGHEST, preferred_element_type=F32) + b_ref[0]


def _ada(c8, ada_w, ada_b):
    depth, d, n = ada_w.shape
    tn = 1536
    return pl.pallas_call(
        _ada_kernel,
        out_shape=jax.ShapeDtypeStruct((depth, 8, n), F32),
        grid=(depth, n // tn),
        in_specs=[pl.BlockSpec((8, d), lambda l, j: (0, 0)),
                  pl.BlockSpec((1, d, tn), lambda l, j: (l, 0, j)),
                  pl.BlockSpec((1, 1, tn), lambda l, j: (l, 0, j))],
        out_specs=pl.BlockSpec((1, 8, tn), lambda l, j: (l, 0, j)),
        compiler_params=_params(("arbitrary", "arbitrary"), VMEM_LIMIT),
        name="ada",
    )(c8, ada_w, ada_b.reshape(depth, 1, n))


def _in_ab_kernel(scale, *refs):
    (h_ref, mod_ref, n1_ref, win_ref, qn_ref, wuqt_ref, kvn_ref, wuk_ref, wuvt_ref, ekr_ref,
     cos_ref, sf_ref, sb_ref, cost_ref, sft_ref, sbt_ref, f_ref, qt_ref, k_ref, vt_ref) = refs
    D = D_MODEL
    h = h_ref[0]
    mod = mod_ref[0]
    a = _modulate(h, n1_ref[...], mod[:, 0:D], mod[:, D:2 * D]).astype(BF16)
    p = jnp.dot(a, win_ref[...], preferred_element_type=F32)
    o = FNET_WIDTH
    f_ref[0] = p[:, :o]
    cq = p[:, o:o + MLA_Q_LORA]
    o += MLA_Q_LORA
    ckv = p[:, o:o + MLA_KV_LORA]
    o += MLA_KV_LORA
    kr = p[:, o:o + MLA_ROPE]
    cqn = _rms(cq, qn_ref[...]).astype(BF16)
    ckvn = _rms(ckv, kvn_ref[...]).astype(BF16)
    qt = lax.dot_general(wuqt_ref[...], cqn, NT, preferred_element_type=F32)
    k = (jnp.dot(ckvn, wuk_ref[...], preferred_element_type=F32)
         + jnp.dot(kr.astype(BF16), ekr_ref[...], preferred_element_type=F32))
    vt = lax.dot_general(wuvt_ref[...], ckvn, NT, preferred_element_type=F32)
    cos, sf, sb = cos_ref[...], sf_ref[...], sb_ref[...]
    cost, sft, sbt = cost_ref[...], sft_ref[...], sbt_ref[...]
    shift = MLA_ROPE // 4
    ones = lax.broadcasted_iota(I32, (LANES, vt.shape[1]), 0) == MLA_V
    for hd in range(MLA_HEADS):
        sl = slice(hd * LANES, (hd + 1) * LANES)
        qt_ref[0, hd] = (_rope_t(qt[sl, :], cost, sft, sbt, shift) * scale).astype(BF16)
        k_ref[0, hd] = _rope(k[:, sl], cos, sf, sb, shift).astype(BF16)
        vt_ref[0, hd] = jnp.where(ones, 1.0, vt[sl, :]).astype(BF16)


def _in_ab(h, mod, n1, w, tables, tm):
    B, R, D = h.shape
    H = MLA_HEADS
    full = lambda a: pl.BlockSpec(a.shape, lambda b, i: (0,) * a.ndim)
    row = lambda w_: pl.BlockSpec((1, tm, w_), lambda b, i: (b, i, 0))
    tab = pl.BlockSpec((tm, LANES), lambda b, i: (i, 0))
    hd = pl.BlockSpec((1, H, tm, LANES), lambda b, i: (b, 0, i, 0))
    hdt = pl.BlockSpec((1, H, LANES, tm), lambda b, i: (b, 0, 0, i))
    tabt = pl.BlockSpec((LANES, tm), lambda b, i: (0, i))
    weights = [n1, w["win"], w["qn"], w["wuqt"], w["kvn"], w["wuk"], w["wuvt"], w["ekr"]]
    return pl.pallas_call(
        functools.partial(_in_ab_kernel, MLA_QK ** -0.5 * LOG2E),
        out_shape=(jax.ShapeDtypeStruct((B, R, FNET_WIDTH), F32),
                   jax.ShapeDtypeStruct((B, H, LANES, R), BF16),
                   jax.ShapeDtypeStruct((B, H, R, LANES), BF16),
                   jax.ShapeDtypeStruct((B, H, LANES, R), BF16)),
        grid=(B, R // tm),
        in_specs=[row(D), pl.BlockSpec((1, 1, N_MOD * D), lambda b, i: (b, 0, 0))]
        + [full(a) for a in weights] + [tab, tab, tab, tabt, tabt, tabt],
        out_specs=(row(FNET_WIDTH), hdt, hd, hdt),
        compiler_params=_params(("arbitrary", "arbitrary"), VMEM_LIMIT),
        name="in_ab",
    )(h, mod, *weights, *tables, *(t.T for t in tables))


def _in_c_kernel(scale, *refs):
    (h_ref, moe_ref, modp_ref, mod_ref, n1_ref, wqt_ref, wk_ref, wvt_ref, qg_ref, kg_ref,
     cos_ref, sf_ref, sb_ref, cost_ref, sft_ref, sbt_ref, h2_ref, qt_ref, k_ref, vt_ref) = refs
    D = D_MODEL
    h = h_ref[0] + modp_ref[0][:, 5 * D:6 * D] * _untile(moe_ref.at[0], slice(None))
    h2_ref[0] = h
    mod = mod_ref[0]
    a = _modulate(h, n1_ref[...], mod[:, 0:D], mod[:, D:2 * D]).astype(BF16)
    qt = lax.dot_general(wqt_ref[...], a, NT, preferred_element_type=F32)
    p = jnp.dot(a, wk_ref[...], preferred_element_type=F32)
    vt = lax.dot_general(wvt_ref[...], a, NT, preferred_element_type=F32)
    cos, sf, sb = cos_ref[...], sf_ref[...], sb_ref[...]
    cost, sft, sbt = cost_ref[...], sft_ref[...], sbt_ref[...]
    shift = GQA_HEAD_DIM // 4
    for hd in range(GQA_HEADS):
        x = qt[hd * LANES:(hd + 1) * LANES, :]
        x = x * lax.rsqrt(jnp.mean(x * x, axis=0, keepdims=True) + EPS) * qg_ref[...]
        qt_ref[0, hd] = (_rope_t(x, cost, sft, sbt, shift) * scale).astype(BF16)
    for hd in range(GQA_KV_HEADS):
        x = _rms(p[:, hd * LANES:(hd + 1) * LANES], kg_ref[...])
        k_ref[0, hd] = _rope(x, cos, sf, sb, shift).astype(BF16)
        vt_ref[0, hd] = vt[hd * LANES:(hd + 1) * LANES, :].astype(BF16)


def _in_c(h, moe, mod_prev, mod, n1, w, tables, tm):
    B, R, D = h.shape
    full = lambda a: pl.BlockSpec(a.shape, lambda b, i: (0,) * a.ndim)
    row = lambda w_: pl.BlockSpec((1, tm, w_), lambda b, i: (b, i, 0))
    tiled = pl.BlockSpec((1, tm, SUB, LANES), lambda b, i: (b, i, 0, 0))
    modspec = pl.BlockSpec((1, 1, N_MOD * D), lambda b, i: (b, 0, 0))
    tab = pl.BlockSpec((tm, LANES), lambda b, i: (i, 0))
    hd = lambda n: pl.BlockSpec((1, n, tm, LANES), lambda b, i: (b, 0, i, 0))
    hdt = pl.BlockSpec((1, GQA_KV_HEADS, LANES, tm), lambda b, i: (b, 0, 0, i))
    tabt = pl.BlockSpec((LANES, tm), lambda b, i: (0, i))
    hdq = pl.BlockSpec((1, GQA_HEADS, LANES, tm), lambda b, i: (b, 0, 0, i))
    weights = [n1, w["wqt"], w["wk"], w["wvt"], w["qg"], w["kg"]]
    return pl.pallas_call(
        functools.partial(_in_c_kernel, GQA_HEAD_DIM ** -0.5 * LOG2E),
        out_shape=(jax.ShapeDtypeStruct((B, R, D), F32),
                   jax.ShapeDtypeStruct((B, GQA_HEADS, LANES, R), BF16),
                   jax.ShapeDtypeStruct((B, GQA_KV_HEADS, R, LANES), BF16),
                   jax.ShapeDtypeStruct((B, GQA_KV_HEADS, LANES, R), BF16)),
        grid=(B, R // tm),
        in_specs=[row(D), tiled, modspec, modspec] + [full(a) for a in weights]
        + [tab, tab, tab, tabt, tabt, tabt],
        out_specs=(row(D), hdq, hd(GQA_KV_HEADS), hdt),
        compiler_params=_params(("arbitrary", "arbitrary"), VMEM_LIMIT),
        name="in_c",
    )(h, moe, mod_prev, mod, *weights, *tables, *(t.T for t in tables))


ATTN_QUERIES = 4096
ATTN_KEYS = 512


def _attn_kernel(n_lat, tk, dv, ones_row, *refs):
    if n_lat:
        q_ref, kc_ref, vct_ref, kl_ref, vlt_ref, o_ref, acc_ref, s0, s1, p0, p1 = refs
        s_bufs, p_bufs = (s0, s1), (p0, p1)
    else:
        q_ref, kc_ref, vct_ref, o_ref, acc_ref = refs
    G, tq = q_ref.shape[2], q_ref.shape[4]
    q = jnp.concatenate([q_ref[0, 0, g] for g in range(G)], axis=1)

    def scores(k):
        return jnp.dot(k, q, preferred_element_type=F32)

    s = scores(kc_ref[0, 0])
    m = jnp.max(s, axis=0, keepdims=True)
    p = jnp.exp2(s - m)
    l = jnp.sum(p, axis=0, keepdims=True) if ones_row is None else jnp.zeros_like(m)
    if dv < LANES:
        acc_ref[...] = jnp.zeros_like(acc_ref)
    acc_ref[:dv, :] = jnp.dot(vct_ref[0, 0, :dv, :], p.astype(BF16), preferred_element_type=F32)

    if n_lat:
        assert n_lat == 1 or n_lat % 2 == 0

        def chunk(c):
            return pl.ds(pl.multiple_of(c * tk, tk), tk)

        def score_stage(c, slot):
            s_bufs[slot][...] = scores(kl_ref[0, 0, chunk(c), :])

        def softmax_stage(slot, m, l):
            s = s_bufs[slot][...]
            m_new = jnp.maximum(m, jnp.max(s, axis=0, keepdims=True))
            p = jnp.exp2(s - m_new)
            p_bufs[slot][...] = p.astype(BF16)
            alpha = jnp.exp2(m - m_new)
            if ones_row is None:
                l = alpha * l + jnp.sum(p, axis=0, keepdims=True)
            return m_new, l, alpha

        def value_stage(c, slot, alpha):
            pv = jnp.dot(vlt_ref[0, 0, :dv, chunk(c)], p_bufs[slot][...],
                         preferred_element_type=F32)
            acc_ref[:dv, :] = alpha * acc_ref[:dv, :] + pv

        score_stage(0, 0)
        m, l, alpha = softmax_stage(0, m, l)
        if n_lat > 1:
            score_stage(1, 1)

            def body(i, carry):
                m, l, alpha = carry
                for slot in (0, 1):
                    c = 2 * i + slot
                    score_stage(c + 2, slot)
                    value_stage(c, slot, alpha)
                    m, l, alpha = softmax_stage(1 - slot, m, l)
                return m, l, alpha

            m, l, alpha = lax.fori_loop(0, (n_lat - 2) // 2, body, (m, l, alpha))
            value_stage(n_lat - 2, 0, alpha)
            m, l, alpha = softmax_stage(1, m, l)
        value_stage(n_lat - 1, (n_lat - 1) % 2, alpha)
    if ones_row is not None:
        l = acc_ref[ones_row:ones_row + 1, :]
    o_ref[0, 0] = (acc_ref[...] / l).T.reshape(G, tq, LANES).astype(BF16)


def _attention(q, kc, vct, kl, vlt, tq, tk, dv, ones_row=None):
    B, Hk, G, _, R = q.shape
    Lc = kc.shape[2]
    n_lat = 0 if kl is None else kl.shape[2] // tk
    qspec = pl.BlockSpec((1, 1, G, LANES, tq), lambda b, h, i: (b, h, 0, 0, i))
    ospec = pl.BlockSpec((1, 1, G, tq, LANES), lambda b, h, i: (b, h, 0, i, 0))
    kspec = lambda n: pl.BlockSpec((1, 1, n, LANES), lambda b, h, i: (b, h, 0, 0))
    vspec = lambda n: pl.BlockSpec((1, 1, LANES, n), lambda b, h, i: (b, h, 0, 0))
    ins = [q, kc, vct] + ([kl, vlt] if n_lat else [])
    specs = [qspec, kspec(Lc), vspec(Lc)] + ([kspec(kl.shape[2]), vspec(kl.shape[2])] if n_lat else [])
    return pl.pallas_call(
        functools.partial(_attn_kernel, n_lat, tk, dv, ones_row),
        out_shape=jax.ShapeDtypeStruct((B, Hk, G, R, LANES), BF16),
        grid=(B, Hk, R // tq),
        in_specs=specs,
        out_specs=ospec,
        scratch_shapes=[pltpu.VMEM((LANES, G * tq), F32)]
        + ([pltpu.VMEM((tk, G * tq), F32)] * 2 + [pltpu.VMEM((tk, G * tq), BF16)] * 2 if n_lat else []),
        compiler_params=_params(("arbitrary", "arbitrary", "arbitrary"), VMEM_LIMIT),
        name="attn",
    )(*ins)


def _dft_mats(n):
    k = np.arange(n, dtype=np.float64)
    ang = 2.0 * np.pi * np.outer(k, k) / n
    return np.cos(ang), np.sin(ang)


def _split(x):
    hi = x.astype(BF16)
    return hi, (x - hi.astype(F32)).astype(BF16)


def _dot3(a, b):
    d = lambda p, q: jnp.dot(p, q, preferred_element_type=F32)
    return d(a[0], b[0]) + d(a[0], b[1]) + d(a[1], b[0])


def _split_const(m):
    return jnp.stack(_split(jnp.asarray(m, F32)))


def _dft1_kernel(x_ref, m1_ref, tw_ref, o_ref):
    s1, ns2 = x_ref.shape[1], x_ref.shape[2]
    for i in range(ns2):
        a = jnp.dot(m1_ref[...], x_ref[0, :, i, :], precision=HIGHEST, preferred_element_type=F32)
        are, aim = a[:s1], a[s1:]
        tre = jnp.tile(tw_ref[0, :, i * LANES:(i + 1) * LANES], (1, FNET_GROUPS))
        tim = jnp.tile(tw_ref[1, :, i * LANES:(i + 1) * LANES], (1, FNET_GROUPS))
        o_ref[0, 0, :, i, :] = are * tre - aim * tim
        o_ref[0, 1, :, i, :] = are * tim + aim * tre


def _dft2_kernel(a_ref, m2_ref, m3_ref, o_ref):
    kb, n2 = a_ref.shape[2], a_ref.shape[3]
    for j in range(kb):
        rhs = jnp.concatenate([a_ref[0, 0, j], a_ref[0, 1, j]], axis=0)
        y = _dot3((m2_ref[0], m2_ref[1]), _split(rhs))
        for g in range(FNET_GROUPS):
            sl = slice(g * LANES, (g + 1) * LANES)
            lhs = jnp.concatenate([y[:n2, sl], y[n2:, sl]], axis=1)
            o_ref[0, :, j, sl] = _dot3(_split(lhs), (m3_ref[0], m3_ref[1]))


def _fourier_lat(f):
    B, S, W = f.shape
    n2 = LANES
    s1 = S // n2
    c1, sn1 = _dft_mats(s1)
    c2, sn2 = _dft_mats(n2)
    cc, sc = _dft_mats(FNET_GROUP_DIM)
    m1 = jnp.asarray(np.concatenate([c1, -sn1], axis=0), F32)
    m2 = _split_const(np.block([[c2, sn2], [-sn2, c2]]))
    norm = 1.0 / math.sqrt(S * FNET_GROUP_DIM)
    m3 = _split_const(np.concatenate([cc, sc], axis=0) * norm)
    ang = 2.0 * np.pi * np.outer(np.arange(s1), np.arange(n2)) / S
    tw = np.stack([np.cos(ang), -np.sin(ang)])
    tw = jnp.asarray(np.repeat(tw[:, :, :, None], LANES, axis=3).reshape(2, s1, n2 * LANES), F32)

    ns2 = 8
    a = pl.pallas_call(
        _dft1_kernel,
        out_shape=jax.ShapeDtypeStruct((B, 2, s1, n2, W), F32),
        grid=(B, n2 // ns2),
        in_specs=[pl.BlockSpec((1, s1, ns2, W), lambda b, j: (b, 0, j, 0)),
                  pl.BlockSpec(m1.shape, lambda b, j: (0, 0)),
                  pl.BlockSpec((2, s1, ns2 * LANES), lambda b, j: (0, 0, j))],
        out_specs=pl.BlockSpec((1, 2, s1, ns2, W), lambda b, j: (b, 0, 0, j, 0)),
        compiler_params=_params(("arbitrary", "arbitrary"), VMEM_LIMIT),
        name="dft1",
    )(f.reshape(B, s1, n2, W), m1, tw)
    kb = min(8, s1)
    y = pl.pallas_call(
        _dft2_kernel,
        out_shape=jax.ShapeDtypeStruct((B, n2, s1, W), F32),
        grid=(B, s1 // kb),
        in_specs=[pl.BlockSpec((1, 2, kb, n2, W), lambda b, j: (b, 0, j, 0, 0)),
                  pl.BlockSpec(m2.shape, lambda b, j: (0, 0, 0)),
                  pl.BlockSpec(m3.shape, lambda b, j: (0, 0, 0))],
        out_specs=pl.BlockSpec((1, n2, kb, W), lambda b, j: (b, 0, j, 0)),
        compiler_params=_params(("arbitrary", "arbitrary"), VMEM_LIMIT),
        name="dft2",
    )(a, m2, m3)
    return y.reshape(B, S, W)


def _dftc_kernel(f_ref, mc_ref, m3_ref, o_ref):
    n = f_ref.shape[1]
    a = _dot3((mc_ref[0], mc_ref[1]), _split(f_ref[0]))
    for g in range(FNET_GROUPS):
        sl = slice(g * LANES, (g + 1) * LANES)
        lhs = jnp.concatenate([a[:n, sl], a[n:, sl]], axis=1)
        o_ref[0, :, sl] = _dot3(_split(lhs), (m3_ref[0], m3_ref[1]))


def _fourier_ctx(f):
    B, L, W = f.shape
    c, s = _dft_mats(L)
    cc, sc = _dft_mats(FNET_GROUP_DIM)
    mc = _split_const(np.concatenate([c, -s], axis=0))
    m3 = _split_const(np.concatenate([cc, sc], axis=0) / math.sqrt(L * FNET_GROUP_DIM))
    return pl.pallas_call(
        _dftc_kernel,
        out_shape=jax.ShapeDtypeStruct((B, L, W), F32),
        grid=(B,),
        in_specs=[pl.BlockSpec((1, L, W), lambda b: (b, 0, 0)),
                  pl.BlockSpec(mc.shape, lambda b: (0, 0, 0)),
                  pl.BlockSpec(m3.shape, lambda b: (0, 0, 0))],
        out_specs=pl.BlockSpec((1, L, W), lambda b: (b, 0, 0)),
        compiler_params=_params(("arbitrary",), VMEM_LIMIT),
        name="dftc",
    )(f, mc, m3)


def _out_kernel(has_f, *refs):
    if has_f:
        (h_ref, mod_ref, yf_ref, wof_ref, o_ref, woa_ref, n2_ref, wr_ref,
         h1_ref, m_ref, aff_ref) = refs
    else:
        (h_ref, mod_ref, o_ref, woa_ref, n2_ref, wr_ref, h1_ref, m_ref, aff_ref) = refs
    D = D_MODEL
    Hk, G = o_ref.shape[1], o_ref.shape[2]
    ocat = jnp.concatenate([o_ref[0, hk, g] for hk in range(Hk) for g in range(G)], axis=1)
    y = jnp.dot(ocat, woa_ref[...], preferred_element_type=F32)
    if has_f:
        y = y + jnp.dot(yf_ref[0].astype(BF16), wof_ref[...], preferred_element_type=F32)
    mod = mod_ref[0]
    h1 = h_ref[0] + mod[:, 2 * D:3 * D] * y
    h1_ref[0] = h1
    m = _modulate(h1, n2_ref[...], mod[:, 3 * D:4 * D], mod[:, 4 * D:5 * D])
    _store_tiled(m_ref.at[0], slice(None), m)
    logit = lax.dot_general(wr_ref[...], m, NT, precision=HIGHEST,
                            preferred_element_type=F32)
    e = jnp.exp(logit - jnp.max(logit, axis=0, keepdims=True))
    aff_ref[0] = e / jnp.sum(e, axis=0, keepdims=True)


def _out_proj(h, mod, yf, wof, o, woa, n2, wr_t, tm):
    B, R, D = h.shape
    _, Hk, G, _, _ = o.shape
    full = lambda a: pl.BlockSpec(a.shape, lambda b, i: (0,) * a.ndim)
    row = lambda w_: pl.BlockSpec((1, tm, w_), lambda b, i: (b, i, 0))
    modspec = pl.BlockSpec((1, 1, N_MOD * D), lambda b, i: (b, 0, 0))
    ospec = pl.BlockSpec((1, Hk, G, tm, LANES), lambda b, i: (b, 0, 0, i, 0))
    has_f = yf is not None
    ins = [h, mod] + ([yf, wof] if has_f else []) + [o, woa, n2, wr_t]
    specs = ([row(D), modspec] + ([row(FNET_WIDTH), full(wof)] if has_f else [])
             + [ospec, full(woa), full(n2), full(wr_t)])
    return pl.pallas_call(
        functools.partial(_out_kernel, has_f),
        out_shape=(jax.ShapeDtypeStruct((B, R, D), F32),
                   jax.ShapeDtypeStruct((B, R, SUB, LANES), F32),
                   jax.ShapeDtypeStruct((B, N_EXPERTS, R), F32)),
        grid=(B, R // tm),
        in_specs=specs,
        out_specs=(row(D), pl.BlockSpec((1, tm, SUB, LANES), lambda b, i: (b, i, 0, 0)),
                   pl.BlockSpec((1, N_EXPERTS, tm), lambda b, i: (b, 0, i))),
        compiler_params=_params(("arbitrary", "arbitrary"), VMEM_LIMIT),
        name="out_proj",
    )(*ins)


COMPACT_UNROLL = 4


def _topk_kernel(cap, aff_ref, u_ref, ones_ref, lmat_ref, lc_ref, lg_ref, cnt_ref, off_ref,
                 pos_ref, ac_ref, lct_ref, lgt_ref):
    a = aff_ref[0]
    E, N = a.shape
    NC = N // LANES
    R = NC * E
    keys = pltpu.bitcast(a, I32)

    def bit_step(i, tau):
        cand = tau | jnp.left_shift(jnp.int32(1), 30 - i)
        cnt = jnp.sum((keys >= cand).astype(I32), axis=1, keepdims=True)
        return jnp.where(cnt >= cap, cand, tau)

    tau = lax.fori_loop(0, 31, bit_step, jnp.zeros((E, 1), I32))
    gt = (keys > tau).astype(F32)
    eq = (keys == tau).astype(F32)
    need = (cap - jnp.sum(gt, axis=1, keepdims=True))

    def chunked(x):
        return jnp.concatenate([x[:, c * LANES:(c + 1) * LANES] for c in range(NC)], axis=0)

    a_c, gt_c, eq_c = chunked(a), chunked(gt), chunked(eq)
    need_c = jnp.tile(need, (NC, 1))

    def prefix(x):
        xb = x.astype(BF16)
        loc = jnp.dot(xb, u_ref[...], preferred_element_type=F32)
        tot = jnp.dot(xb, ones_ref[...], preferred_element_type=F32)
        offs = jnp.dot(lmat_ref[...], tot.astype(BF16), preferred_element_type=F32)
        return loc, tot, offs

    loc, tot, offs = prefix(eq_c)
    sel = jnp.maximum(gt_c, jnp.where(loc + offs < need_c, eq_c, 0.0))
    loc, tot, offs = prefix(sel)
    cnt_ref[0] = tot.astype(I32)
    off_ref[0] = offs.astype(I32)
    Rp = pos_ref.shape[1]
    selpos = jnp.where(sel > 0.0, loc, -1.0)
    if Rp > R:
        selpos = jnp.concatenate([selpos, jnp.full((Rp - R, LANES), -1.0, F32)], axis=0)
        a_c = jnp.concatenate([a_c, jnp.zeros((Rp - R, LANES), F32)], axis=0)
    pos_ref[...] = selpos.T
    ac_ref[...] = a_c.T
    tok = (lax.broadcasted_iota(I32, (LANES, Rp), 0)
           + lax.broadcasted_iota(I32, (LANES, Rp), 1) // E * LANES).astype(F32)

    def compact(i, carry):
        p = pos_ref[...]
        a_t = ac_ref[...]
        for u in range(COMPACT_UNROLL):
            j = i * COMPACT_UNROLL + u
            hit = p == lax.convert_element_type(j, F32)
            lct_ref[pl.ds(j, 1), :] = jnp.sum(jnp.where(hit, tok, 0.0), axis=0, keepdims=True)
            lgt_ref[pl.ds(j, 1), :] = jnp.sum(jnp.where(hit, a_t, 0.0), axis=0, keepdims=True)
        return carry

    lax.fori_loop(0, LANES // COMPACT_UNROLL, compact, 0)
    lc_ref[0] = lct_ref[...].T[:R].astype(I32)
    lg_ref[0] = lgt_ref[...].T[:R]


def _topk(aff, cap):
    B, E, N = aff.shape
    NC = N // LANES
    R = NC * E
    i = np.arange(LANES)
    u = jnp.asarray(i[:, None] < i[None, :], BF16)
    ones = jnp.ones((LANES, LANES), BF16)
    r = np.arange(R)
    lmat = jnp.asarray((r[:, None] % E == r[None, :] % E) & (r[None, :] // E < r[:, None] // E), BF16)
    full = lambda a: pl.BlockSpec(a.shape, lambda b: (0,) * a.ndim)
    ospec = pl.BlockSpec((1, R, LANES), lambda b: (b, 0, 0))
    sds = lambda dt: jax.ShapeDtypeStruct((B, R, LANES), dt)
    lc, lg, cnt, off = pl.pallas_call(
        functools.partial(_topk_kernel, cap),
        out_shape=(sds(I32), sds(F32), sds(I32), sds(I32)),
        grid=(B,),
        in_specs=[pl.BlockSpec((1, E, N), lambda b: (b, 0, 0)), full(u), full(ones), full(lmat)],
        out_specs=(ospec, ospec, ospec, ospec),
        scratch_shapes=[pltpu.VMEM((LANES, -(-R // LANES) * LANES), F32)] * 4,
        compiler_params=_params(("arbitrary",), VMEM_LIMIT),
        name="topk",
    )(aff, u, ones, lmat)
    by_expert = lambda x: x.reshape(B, NC, E, LANES).transpose(0, 2, 1, 3)
    flat = lambda x: by_expert(x).reshape(B, E, NC * LANES)
    return (flat(lc), flat(lg), by_expert(cnt)[..., 0], by_expert(off)[..., 0])


GATHER_ROWS = 256
DMA_UNROLL = 8
ROW_UNROLL = 4


def _gather_kernel(cap, B, cnt_ref, off_ref, *refs):
    lc_refs = refs[:B]
    m_hbm, o_ref, x_ref, sems = refs[B:]
    e = pl.program_id(0)
    NC = lc_refs[0].shape[1] // LANES
    for b in range(B):
        def chunk_body(c, carry, b=b):
            base = off_ref[b, e, c] + b * cap
            n = cnt_ref[b, e, c]

            def start_row(r, priority):
                t = lc_refs[b][0, c * LANES + r]
                pltpu.make_async_copy(m_hbm.at[b, t], x_ref.at[base + r],
                                      sems.at[b]).start(priority=priority)

            def group_body(i, carry):
                for u in range(DMA_UNROLL):
                    start_row(i * DMA_UNROLL + u, u % 2)
                return carry

            def tail_body(r, carry):
                start_row(r, 0)
                return carry

            full = n // DMA_UNROLL
            lax.fori_loop(0, full, group_body, carry)
            return lax.fori_loop(full * DMA_UNROLL, n, tail_body, carry)

        lax.fori_loop(0, NC, chunk_body, 0)
    step = min(GATHER_ROWS, cap)
    for b in range(B):
        done = x_ref.at[pl.ds(b * cap, cap)]
        pltpu.make_async_copy(done, done, sems.at[b]).wait()
        for r0 in range(b * cap, (b + 1) * cap, step):
            rows = slice(r0, r0 + step)
            o_ref[0, rows, :] = _untile(x_ref, rows).astype(BF16)


def _gather(m, lc, cnt, off, cap):
    B, N, sub, _ = m.shape
    E = lc.shape[1]
    lc4 = lc[:, :, None, :]

    def table(b):
        return pl.BlockSpec((None, None, 1, lc.shape[2]), lambda e, *_: (b, e, 0, 0),
                            memory_space=pltpu.SMEM)

    return pl.pallas_call(
        functools.partial(_gather_kernel, cap, B),
        out_shape=jax.ShapeDtypeStruct((E, B * cap, sub * LANES), BF16),
        grid_spec=pltpu.PrefetchScalarGridSpec(
            num_scalar_prefetch=2,
            grid=(E,),
            in_specs=[table(b) for b in range(B)] + [pl.BlockSpec(memory_space=pl.ANY)],
            out_specs=pl.BlockSpec((1, B * cap, sub * LANES), lambda e, *_: (e, 0, 0)),
            scratch_shapes=[pltpu.VMEM((B * cap, sub, LANES), m.dtype),
                            pltpu.SemaphoreType.DMA((B,))]),
        compiler_params=_params(("arbitrary",), VMEM_LIMIT),
        name="gather",
    )(cnt, off, *([lc4] * B), m)


def _ffn_kernel(n_parts, chunk_rows, *refs):
    wg_ref, wu_ref, wd_ref = refs[:3]
    x_refs = refs[3:3 + n_parts]
    o_refs = refs[3 + n_parts:3 + 2 * n_parts]
    acc_refs = refs[3 + 2 * n_parts:]
    j = pl.program_id(1)
    last = pl.num_programs(1) - 1
    wg = wg_ref[0].astype(BF16)
    wu = wu_ref[0].astype(BF16)
    wd = wd_ref[0].astype(BF16)
    chunks = [(x_ref, o_ref, acc_ref, slice(r0, r0 + nrows))
              for x_ref, o_ref, acc_ref, nrows in zip(x_refs, o_refs, acc_refs, chunk_rows)
              for r0 in range(0, x_ref.shape[1], nrows)]

    @pl.when(j == 0)
    def _():
        for acc_ref in acc_refs:
            acc_ref[...] = jnp.zeros_like(acc_ref)

    for x_ref, _, acc_ref, rows in chunks:
        x = x_ref[0, rows, :]
        a = jnp.dot(x, wg, preferred_element_type=F32)
        u = jnp.dot(x, wu, preferred_element_type=F32)
        hh = (a * jax.nn.sigmoid(a) * u).astype(BF16)
        acc_ref[rows, :] += jnp.dot(hh, wd, preferred_element_type=F32)

    @pl.when(j == last)
    def _():
        for _, o_ref, acc_ref, rows in chunks:
            _store_tiled(o_ref.at[0], rows, acc_ref[rows, :])


def _ffn(xs, layer, w_gate, w_up, w_down, chunk_rows):
    _, E, D, F = w_gate.shape
    tf = 512
    assert F // tf >= 2
    xspec = lambda x: pl.BlockSpec((1, x.shape[1], D), lambda e, j: (e, 0, 0))
    ospec = lambda x: pl.BlockSpec((1, x.shape[1], SUB, LANES), lambda e, j: (e, 0, 0, 0))
    outs = pl.pallas_call(
        functools.partial(_ffn_kernel, len(xs), chunk_rows),
        out_shape=tuple(jax.ShapeDtypeStruct((E, x.shape[1], SUB, LANES), F32) for x in xs),
        grid=(E, F // tf),
        in_specs=[pl.BlockSpec((None, 1, D, tf), lambda e, j: (layer, e, 0, j)),
                  pl.BlockSpec((None, 1, D, tf), lambda e, j: (layer, e, 0, j)),
                  pl.BlockSpec((None, 1, tf, D), lambda e, j: (layer, e, j, 0))]
        + [xspec(x) for x in xs],
        out_specs=tuple(ospec(x) for x in xs),
        scratch_shapes=[pltpu.VMEM((x.shape[1], D), F32) for x in xs],
        compiler_params=_params(("arbitrary", "arbitrary"), VMEM_LIMIT),
        name="ffn",
    )(w_gate, w_up, w_down, *xs)
    return outs


def _combine_kernel(cnt_ref, off_ref, lc_ref, lg_ref, y_ref, o_ref):
    b = pl.program_id(0)
    hf = pl.program_id(1)
    e = pl.program_id(2)
    nch = o_ref.shape[1] // LANES
    first = hf * o_ref.shape[1]

    @pl.when(e == 0)
    def _():
        o_ref[...] = jnp.zeros_like(o_ref)

    def chunk_body(ci, carry):
        c = hf * nch + ci
        base = off_ref[b, e, c]

        n = cnt_ref[b, e, c]

        def rows_body(r0, width):
            ts = [lc_ref[0, c * LANES + r0 + u] - first for u in range(width)]
            vals = [o_ref[0, ts[u]] + lg_ref[0, c * LANES + r0 + u] * y_ref[0, base + r0 + u]
                    for u in range(width)]
            for u in range(width):
                o_ref[0, ts[u]] = vals[u]

        def group_body(i, carry):
            rows_body(i * ROW_UNROLL, ROW_UNROLL)
            return carry

        def tail_body(r, carry):
            rows_body(r, 1)
            return carry

        full = n // ROW_UNROLL
        lax.fori_loop(0, full, group_body, carry)
        return lax.fori_loop(full * ROW_UNROLL, n, tail_body, carry)

    lax.fori_loop(0, nch, chunk_body, 0)


def _combine(y, lc, lg, cnt, off, n_tokens, cap, n_split):
    E = y.shape[0]
    B = lc.shape[0]
    nh = n_tokens // n_split
    smem = lambda: pl.BlockSpec((None, None, 1, lc.shape[2]), lambda b, h, e, *_: (b, e, 0, 0),
                                memory_space=pltpu.SMEM)
    return pl.pallas_call(
        _combine_kernel,
        out_shape=jax.ShapeDtypeStruct((B, n_tokens, SUB, LANES), F32),
        grid_spec=pltpu.PrefetchScalarGridSpec(
            num_scalar_prefetch=2,
            grid=(B, n_split, E),
            in_specs=[smem(), smem(),
                      pl.BlockSpec((1, cap, SUB, LANES), lambda b, h, e, *_: (e, b, 0, 0))],
            out_specs=pl.BlockSpec((1, nh, SUB, LANES), lambda b, h, e, *_: (b, h, 0, 0))),
        compiler_params=_params(("arbitrary", "arbitrary", "arbitrary"), VMEM_LIMIT),
        name="combine",
    )(cnt, off, lc[:, :, None, :], lg[:, :, None, :], y)


def _final_kernel(h_ref, moe_ref, mod_ref, g_ref, o_ref):
    D = D_MODEL
    h = h_ref[0] + mod_ref[0][:, 5 * D:6 * D] * _untile(moe_ref.at[0], slice(None))
    o_ref[0] = _rms(h, g_ref[...])


def _final(h, moe, mod, g, tm):
    B, R, D = h.shape
    row = pl.BlockSpec((1, tm, D), lambda b, i: (b, i, 0))
    return pl.pallas_call(
        _final_kernel,
        out_shape=jax.ShapeDtypeStruct((B, R, D), F32),
        grid=(B, R // tm),
        in_specs=[row, pl.BlockSpec((1, tm, SUB, LANES), lambda b, i: (b, i, 0, 0)),
                  pl.BlockSpec((1, 1, N_MOD * D), lambda b, i: (b, 0, 0)),
                  pl.BlockSpec((1, D), lambda b, i: (0, 0))],
        out_specs=row,
        compiler_params=_params(("arbitrary", "arbitrary"), VMEM_LIMIT),
        name="final",
    )(h, moe, mod, g)


def _rope_tables(n_lat, n_ctx, segments):
    t = jnp.arange(n_lat)
    pos = {"row": (t // GRID_W).astype(F32), "col": (t % GRID_W).astype(F32)}
    freq = {"row": np.zeros((2, LANES), np.float32), "col": np.zeros((2, LANES), np.float32)}
    first = np.zeros(LANES, np.float32)
    second = np.zeros(LANES, np.float32)
    for lane0, width, which in segments:
        half = width // 2
        idx = np.arange(half, dtype=np.float32)
        for lo, mask in ((lane0, first), (lane0 + half, second)):
            freq[which][0, lo:lo + half] = idx / half
            freq[which][1, lo:lo + half] = 1.0
            mask[lo:lo + half] = 1.0
    ang = jnp.zeros((n_lat, LANES), F32)
    for which in ("row", "col"):
        inv = (ROPE_THETA ** (-jnp.asarray(freq[which][0]))) * jnp.asarray(freq[which][1])
        ang = ang + pos[which][:, None] * inv[None, :]
    sin = jnp.sin(ang)
    lat = (jnp.cos(ang), -sin * first[None, :], sin * second[None, :])
    ctx = (jnp.ones((n_ctx, LANES), F32), jnp.zeros((n_ctx, LANES), F32), jnp.zeros((n_ctx, LANES), F32))
    return lat, ctx


def _head_slots(w, n_heads, width, lo, hi):
    k = w.shape[0]
    w3 = w.reshape(k, n_heads, width)[:, :, lo:hi]
    return jnp.pad(w3, ((0, 0), (0, 0), (0, LANES - (hi - lo)))).reshape(k, n_heads * LANES)


def _moe(m_l, aff_l, m_c, aff_c, layer, w_gate, w_up, w_down):
    B, S = m_l.shape[:2]
    cap_l = CAPACITY_FACTOR * S // N_EXPERTS
    lc, lg, cnt, off = _topk(aff_l, cap_l)
    xs = [_gather(m_l, lc, cnt, off, cap_l)]
    chunk_rows = [cap_l]
    if m_c is not None:
        L = m_c.shape[1]
        cap_c = CAPACITY_FACTOR * L // N_EXPERTS
        lcc, lgc, cntc, offc = _topk(aff_c, cap_c)
        xs.append(_gather(m_c, lcc, cntc, offc, cap_c))
        chunk_rows.append(B * cap_c)
    ys = _ffn(xs, layer, w_gate, w_up, w_down, tuple(chunk_rows))
    out_l = _combine(ys[0], lc, lg, cnt, off, S, cap_l, 2)
    out_c = None
    if m_c is not None:
        out_c = _combine(ys[1], lcc, lgc, cntc, offc, L, cap_c, 1)
    return out_l, out_c


def kernel(x, c, ctx, c_ctx, ada_w, ada_b, norm1, norm2, ab_w_in, ab_q_norm, ab_w_uq, ab_kv_norm,
           ab_w_ukv, ab_w_o, c_w_in, c_q_gain, c_k_gain, c_w_o, moe_router, moe_w_gate, moe_w_up,
           moe_w_down, final_norm):
    B, S, D = x.shape
    L = ctx.shape[1]
    depth = ada_w.shape[0]
    tm_l, tm_c = 512, L
    row2 = lambda v: v.reshape(1, -1)

    c8 = jnp.zeros((8, D), F32).at[:B].set(c).at[B].set(c_ctx)
    mod = _ada(c8, ada_w, ada_b)
    mod_l = [mod[i, :B][:, None, :] for i in range(depth)]
    mod_c = [jnp.broadcast_to(mod[i, B][None, None, :], (B, 1, N_MOD * D)) for i in range(depth)]

    h_l, h_c = x, ctx
    moe_l = moe_c = None
    for i in range(depth):
        last = i == depth - 1
        j = i // 2
        wr_t = moe_router[i].T
        n2 = row2(norm2[i])
        if i % 2 == 0:
            assert moe_l is None
            tabs_l, tabs_c = _rope_tables(S, L, ((MLA_NOPE, MLA_ROPE // 2, "row"),
                                                 (MLA_NOPE + MLA_ROPE // 2, MLA_ROPE // 2, "col")))
            w_ukv = ab_w_ukv[j]
            ekr = np.zeros((MLA_ROPE, MLA_HEADS * LANES), np.float32)
            for hd in range(MLA_HEADS):
                ekr[np.arange(MLA_ROPE), hd * LANES + MLA_NOPE + np.arange(MLA_ROPE)] = 1.0
            w = {
                "win": ab_w_in[j].astype(BF16),
                "qn": row2(ab_q_norm[j]),
                "wuqt": _head_slots(ab_w_uq[j], MLA_HEADS, MLA_QK, 0, MLA_QK).T.astype(BF16),
                "kvn": row2(ab_kv_norm[j]),
                "wuk": _head_slots(w_ukv, MLA_HEADS, MLA_NOPE + MLA_V, 0, MLA_NOPE).astype(BF16),
                "wuvt": _head_slots(w_ukv, MLA_HEADS, MLA_NOPE + MLA_V, MLA_NOPE,
                                    MLA_NOPE + MLA_V).T.astype(BF16),
                "ekr": jnp.asarray(ekr, BF16),
            }
            n1 = row2(norm1[i])
            f_l, q_l, k_l, vt_l = _in_ab(h_l, mod_l[i], n1, w, tabs_l, tm_l)
            f_c, q_c, k_c, vt_c = _in_ab(h_c, mod_c[i], n1, w, tabs_c, tm_c)
            o_l = _attention(q_l[:, :, None], k_c, vt_c, k_l, vt_l, min(S, ATTN_QUERIES), ATTN_KEYS,
                             MLA_VROWS, MLA_V)
            yf_l = _fourier_lat(f_l)
            wof = ab_w_o[j][:FNET_WIDTH].astype(BF16)
            woa = jnp.pad(ab_w_o[j][FNET_WIDTH:].reshape(MLA_HEADS, MLA_V, D),
                          ((0, 0), (0, LANES - MLA_V), (0, 0))).reshape(MLA_HEADS * LANES, D).astype(BF16)
            h_l, m_l, aff_l = _out_proj(h_l, mod_l[i], yf_l, wof, o_l, woa, n2, wr_t, tm_l)
            m_c = aff_c = None
            if not last:
                o_c = _attention(q_c[:, :, None], k_c, vt_c, None, None, L, L, MLA_VROWS, MLA_V)
                yf_c = _fourier_ctx(f_c)
                h_c, m_c, aff_c = _out_proj(h_c, mod_c[i], yf_c, wof, o_c, woa, n2, wr_t, tm_c)
        else:
            tabs_l, tabs_c = _rope_tables(S, L, ((0, GQA_HEAD_DIM // 2, "row"),
                                                 (GQA_HEAD_DIM // 2, GQA_HEAD_DIM // 2, "col")))
            nq = GQA_HEADS * GQA_HEAD_DIM
            nqk = nq + GQA_KV_HEADS * GQA_HEAD_DIM
            w = {"wqt": c_w_in[j][:, :nq].T.astype(BF16), "wk": c_w_in[j][:, nq:nqk].astype(BF16),
                 "wvt": c_w_in[j][:, nqk:].T.astype(BF16),
                 "qg": c_q_gain[j].reshape(-1, 1), "kg": row2(c_k_gain[j])}
            n1 = row2(norm1[i])
            h_l, q_l, k_l, vt_l = _in_c(h_l, moe_l, mod_l[i - 1], mod_l[i], n1, w, tabs_l, tm_l)
            h_c, q_c, k_c, vt_c = _in_c(h_c, moe_c, mod_c[i - 1], mod_c[i], n1, w, tabs_c, tm_c)
            grp = lambda q: q.reshape(B, GQA_KV_HEADS, GQA_GROUP, LANES, q.shape[3])
            o_l = _attention(grp(q_l), k_c, vt_c, k_l, vt_l, min(S, ATTN_QUERIES // GQA_GROUP),
                             ATTN_KEYS, GQA_HEAD_DIM)
            woa = c_w_o[j].astype(BF16)
            h_l, m_l, aff_l = _out_proj(h_l, mod_l[i], None, None, o_l, woa, n2, wr_t, tm_l)
            m_c = aff_c = None
            if not last:
                o_c = _attention(grp(q_c), k_c, vt_c, None, None, L, L, GQA_HEAD_DIM)
                h_c, m_c, aff_c = _out_proj(h_c, mod_c[i], None, None, o_c, woa, n2, wr_t, tm_c)
        moe_l, moe_c = _moe(m_l, aff_l, m_c, aff_c, i, moe_w_gate, moe_w_up, moe_w_down)
    return _final(h_l, moe_l, mod_l[depth - 1], row2(final_norm), tm_l)
```

```python
import functools
import math

import jax
import jax.numpy as jnp
import numpy as np
from jax import lax
from jax.experimental import pallas as pl
from jax.experimental.pallas import tpu as pltpu

F32 = jnp.float32
BF16 = jnp.bfloat16
I32 = jnp.int32
HIGHEST = lax.Precision.HIGHEST

D_MODEL = 1024
GRID_W = 64
EPS = 1e-6
ROPE_THETA = 10000.0
N_MOD = 6
FNET_GROUPS = 4
FNET_GROUP_DIM = 128
FNET_WIDTH = FNET_GROUPS * FNET_GROUP_DIM
MLA_HEADS = 8
MLA_Q_LORA = 256
MLA_KV_LORA = 128
MLA_NOPE = 64
MLA_ROPE = 32
MLA_V = 64
MLA_QK = MLA_NOPE + MLA_ROPE
GQA_HEADS = 8
GQA_KV_HEADS = 2
GQA_GROUP = GQA_HEADS // GQA_KV_HEADS
GQA_HEAD_DIM = 128
N_EXPERTS = 16
EXPERT_FF = 2048
CAPACITY_FACTOR = 2

LANES = 128
MLA_VROWS = MLA_V + 16
SUB = D_MODEL // LANES
VMEM_LIMIT = 56 * 1024 * 1024
LOG2E = math.log2(math.e)
NT = (((1,), (1,)), ((), ()))


def _params(sem, vmem=None):
    return pltpu.CompilerParams(dimension_semantics=sem, vmem_limit_bytes=vmem)


def _rms(x, g):
    return x * lax.rsqrt(jnp.mean(x * x, axis=-1, keepdims=True) + EPS) * g


def _modulate(h, g, shift, scale):
    return _rms(h, g) * (1.0 + scale) + shift


def _rope(x, cos, sin_fwd, sin_bwd, shift):
    return (x * cos + pltpu.roll(x, LANES - shift, 1) * sin_fwd
            + pltpu.roll(x, shift, 1) * sin_bwd)


def _rope_t(x, cos, sin_fwd, sin_bwd, shift):
    return (x * cos + pltpu.roll(x, LANES - shift, 0) * sin_fwd
            + pltpu.roll(x, shift, 0) * sin_bwd)


def _untile(ref, rows):
    x = ref[rows]
    n = x.shape[0]
    y = pltpu.einshape("grsl->gsrl", x.reshape(n // 8, 8, SUB, LANES))
    return jnp.concatenate([y[:, s].reshape(n, LANES) for s in range(SUB)], axis=1)


def _store_tiled(ref, rows, x):
    n = x.shape[0]
    y = jnp.concatenate([x[:, s * LANES:(s + 1) * LANES].reshape(n // 8, 1, 8, LANES)
                         for s in range(SUB)], axis=1)
    ref[rows] = pltpu.einshape("gsrl->grsl", y).reshape(n, SUB, LANES)


def _ada_kernel(c_ref, w_ref, b_ref, o_ref):
    c = c_ref[...]
    x = c * jax.nn.sigmoid(c)
    o_ref[0] = jnp.dot(x, w_ref[0], precision=HIGHEST, preferred_element_type=F32) + b_ref[0]


def _ada(c8, ada_w, ada_b):
    depth, d, n = ada_w.shape
    tn = 1536
    return pl.pallas_call(
        _ada_kernel,
        out_shape=jax.ShapeDtypeStruct((depth, 8, n), F32),
        grid=(depth, n // tn),
        in_specs=[pl.BlockSpec((8, d), lambda l, j: (0, 0)),
                  pl.BlockSpec((1, d, tn), lambda l, j: (l, 0, j)),
                  pl.BlockSpec((1, 1, tn), lambda l, j: (l, 0, j))],
        out_specs=pl.BlockSpec((1, 8, tn), lambda l, j: (l, 0, j)),
        compiler_params=_params(("arbitrary", "arbitrary"), VMEM_LIMIT),
        name="ada",
    )(c8, ada_w, ada_b.reshape(depth, 1, n))


def _in_ab_kernel(scale, *refs):
    (h_ref, mod_ref, n1_ref, win_ref, qn_ref, wuqt_ref, kvn_ref, wuk_ref, wuvt_ref, ekr_ref,
     cos_ref, sf_ref, sb_ref, cost_ref, sft_ref, sbt_ref, f_ref, qt_ref, k_ref, vt_ref) = refs
    D = D_MODEL
    h = h_ref[0]
    mod = mod_ref[0]
    a = _modulate(h, n1_ref[...], mod[:, 0:D], mod[:, D:2 * D]).astype(BF16)
    p = jnp.dot(a, win_ref[...], preferred_element_type=F32)
    o = FNET_WIDTH
    f_ref[0] = p[:, :o]
    cq = p[:, o:o + MLA_Q_LORA]
    o += MLA_Q_LORA
    ckv = p[:, o:o + MLA_KV_LORA]
    o += MLA_KV_LORA
    kr = p[:, o:o + MLA_ROPE]
    cqn = _rms(cq, qn_ref[...]).astype(BF16)
    ckvn = _rms(ckv, kvn_ref[...]).astype(BF16)
    qt = lax.dot_general(wuqt_ref[...], cqn, NT, preferred_element_type=F32)
    k = (jnp.dot(ckvn, wuk_ref[...], preferred_element_type=F32)
         + jnp.dot(kr.astype(BF16), ekr_ref[...], preferred_element_type=F32))
    vt = lax.dot_general(wuvt_ref[...], ckvn, NT, preferred_element_type=F32)
    cos, sf, sb = cos_ref[...], sf_ref[...], sb_ref[...]
    cost, sft, sbt = cost_ref[...], sft_ref[...], sbt_ref[...]
    shift = MLA_ROPE // 4
    ones = lax.broadcasted_iota(I32, (LANES, vt.shape[1]), 0) == MLA_V
    for hd in range(MLA_HEADS):
        sl = slice(hd * LANES, (hd + 1) * LANES)
        qt_ref[0, hd] = (_rope_t(qt[sl, :], cost, sft, sbt, shift) * scale).astype(BF16)
        k_ref[0, hd] = _rope(k[:, sl], cos, sf, sb, shift).astype(BF16)
        vt_ref[0, hd] = jnp.where(ones, 1.0, vt[sl, :]).astype(BF16)


def _in_ab(h, mod, n1, w, tables, tm):
    B, R, D = h.shape
    H = MLA_HEADS
    full = lambda a: pl.BlockSpec(a.shape, lambda b, i: (0,) * a.ndim)
    row = lambda w_: pl.BlockSpec((1, tm, w_), lambda b, i: (b, i, 0))
    tab = pl.BlockSpec((tm, LANES), lambda b, i: (i, 0))
    hd = pl.BlockSpec((1, H, tm, LANES), lambda b, i: (b, 0, i, 0))
    hdt = pl.BlockSpec((1, H, LANES, tm), lambda b, i: (b, 0, 0, i))
    tabt = pl.BlockSpec((LANES, tm), lambda b, i: (0, i))
    weights = [n1, w["win"], w["qn"], w["wuqt"], w["kvn"], w["wuk"], w["wuvt"], w["ekr"]]
    return pl.pallas_call(
        functools.partial(_in_ab_kernel, MLA_QK ** -0.5 * LOG2E),
        out_shape=(jax.ShapeDtypeStruct((B, R, FNET_WIDTH), F32),
                   jax.ShapeDtypeStruct((B, H, LANES, R), BF16),
                   jax.ShapeDtypeStruct((B, H, R, LANES), BF16),
                   jax.ShapeDtypeStruct((B, H, LANES, R), BF16)),
        grid=(B, R // tm),
        in_specs=[row(D), pl.BlockSpec((1, 1, N_MOD * D), lambda b, i: (b, 0, 0))]
        + [full(a) for a in weights] + [tab, tab, tab, tabt, tabt, tabt],
        out_specs=(row(FNET_WIDTH), hdt, hd, hdt),
        compiler_params=_params(("arbitrary", "arbitrary"), VMEM_LIMIT),
        name="in_ab",
    )(h, mod, *weights, *tables, *(t.T for t in tables))


def _in_c_kernel(scale, *refs):
    (h_ref, moe_ref, modp_ref, mod_ref, n1_ref, wqt_ref, wk_ref, wvt_ref, qg_ref, kg_ref,
     cos_ref, sf_ref, sb_ref, cost_ref, sft_ref, sbt_ref, h2_ref, qt_ref, k_ref, vt_ref) = refs
    D = D_MODEL
    h = h_ref[0] + modp_ref[0][:, 5 * D:6 * D] * _untile(moe_ref.at[0], slice(None))
    h2_ref[0] = h
    mod = mod_ref[0]
    a = _modulate(h, n1_ref[...], mod[:, 0:D], mod[:, D:2 * D]).astype(BF16)
    qt = lax.dot_general(wqt_ref[...], a, NT, preferred_element_type=F32)
    p = jnp.dot(a, wk_ref[...], preferred_element_type=F32)
    vt = lax.dot_general(wvt_ref[...], a, NT, preferred_element_type=F32)
    cos, sf, sb = cos_ref[...], sf_ref[...], sb_ref[...]
    cost, sft, sbt = cost_ref[...], sft_ref[...], sbt_ref[...]
    shift = GQA_HEAD_DIM // 4
    for hd in range(GQA_HEADS):
        x = qt[hd * LANES:(hd + 1) * LANES, :]
        x = x * lax.rsqrt(jnp.mean(x * x, axis=0, keepdims=True) + EPS) * qg_ref[...]
        qt_ref[0, hd] = (_rope_t(x, cost, sft, sbt, shift) * scale).astype(BF16)
    for hd in range(GQA_KV_HEADS):
        x = _rms(p[:, hd * LANES:(hd + 1) * LANES], kg_ref[...])
        k_ref[0, hd] = _rope(x, cos, sf, sb, shift).astype(BF16)
        vt_ref[0, hd] = vt[hd * LANES:(hd + 1) * LANES, :].astype(BF16)


def _in_c(h, moe, mod_prev, mod, n1, w, tables, tm):
    B, R, D = h.shape
    full = lambda a: pl.BlockSpec(a.shape, lambda b, i: (0,) * a.ndim)
    row = lambda w_: pl.BlockSpec((1, tm, w_), lambda b, i: (b, i, 0))
    tiled = pl.BlockSpec((1, tm, SUB, LANES), lambda b, i: (b, i, 0, 0))
    modspec = pl.BlockSpec((1, 1, N_MOD * D), lambda b, i: (b, 0, 0))
    tab = pl.BlockSpec((tm, LANES), lambda b, i: (i, 0))
    hd = lambda n: pl.BlockSpec((1, n, tm, LANES), lambda b, i: (b, 0, i, 0))
    hdt = pl.BlockSpec((1, GQA_KV_HEADS, LANES, tm), lambda b, i: (b, 0, 0, i))
    tabt = pl.BlockSpec((LANES, tm), lambda b, i: (0, i))
    hdq = pl.BlockSpec((1, GQA_HEADS, LANES, tm), lambda b, i: (b, 0, 0, i))
    weights = [n1, w["wqt"], w["wk"], w["wvt"], w["qg"], w["kg"]]
    return pl.pallas_call(
        functools.partial(_in_c_kernel, GQA_HEAD_DIM ** -0.5 * LOG2E),
        out_shape=(jax.ShapeDtypeStruct((B, R, D), F32),
                   jax.ShapeDtypeStruct((B, GQA_HEADS, LANES, R), BF16),
                   jax.ShapeDtypeStruct((B, GQA_KV_HEADS, R, LANES), BF16),
                   jax.ShapeDtypeStruct((B, GQA_KV_HEADS, LANES, R), BF16)),
        grid=(B, R // tm),
        in_specs=[row(D), tiled, modspec, modspec] + [full(a) for a in weights]
        + [tab, tab, tab, tabt, tabt, tabt],
        out_specs=(row(D), hdq, hd(GQA_KV_HEADS), hdt),
        compiler_params=_params(("arbitrary", "arbitrary"), VMEM_LIMIT),
        name="in_c",
    )(h, moe, mod_prev, mod, *weights, *tables, *(t.T for t in tables))


ATTN_QUERIES = 4096
ATTN_KEYS = 512


def _attn_kernel(n_lat, tk, dv, ones_row, *refs):
    if n_lat:
        q_ref, kc_ref, vct_ref, kl_ref, vlt_ref, o_ref, acc_ref, s0, s1, p0, p1 = refs
        s_bufs, p_bufs = (s0, s1), (p0, p1)
    else:
        q_ref, kc_ref, vct_ref, o_ref, acc_ref = refs
    G, tq = q_ref.shape[2], q_ref.shape[4]
    q = jnp.concatenate([q_ref[0, 0, g] for g in range(G)], axis=1)

    def scores(k):
        return jnp.dot(k, q, preferred_element_type=F32)

    s = scores(kc_ref[0, 0])
    m = jnp.max(s, axis=0, keepdims=True)
    p = jnp.exp2(s - m)
    l = jnp.sum(p, axis=0, keepdims=True) if ones_row is None else jnp.zeros_like(m)
    if dv < LANES:
        acc_ref[...] = jnp.zeros_like(acc_ref)
    acc_ref[:dv, :] = jnp.dot(vct_ref[0, 0, :dv, :], p.astype(BF16), preferred_element_type=F32)

    if n_lat:
        assert n_lat == 1 or n_lat % 2 == 0

        def chunk(c):
            return pl.ds(pl.multiple_of(c * tk, tk), tk)

        def score_stage(c, slot):
            s_bufs[slot][...] = scores(kl_ref[0, 0, chunk(c), :])

        def softmax_stage(slot, m, l):
            s = s_bufs[slot][...]
            m_new = jnp.maximum(m, jnp.max(s, axis=0, keepdims=True))
            p = jnp.exp2(s - m_new)
            p_bufs[slot][...] = p.astype(BF16)
            alpha = jnp.exp2(m - m_new)
            if ones_row is None:
                l = alpha * l + jnp.sum(p, axis=0, keepdims=True)
            return m_new, l, alpha

        def value_stage(c, slot, alpha):
            pv = jnp.dot(vlt_ref[0, 0, :dv, chunk(c)], p_bufs[slot][...],
                         preferred_element_type=F32)
            acc_ref[:dv, :] = alpha * acc_ref[:dv, :] + pv

        score_stage(0, 0)
        m, l, alpha = softmax_stage(0, m, l)
        if n_lat > 1:
            score_stage(1, 1)

            def body(i, carry):
                m, l, alpha = carry
                for slot in (0, 1):
                    c = 2 * i + slot
                    score_stage(c + 2, slot)
                    value_stage(c, slot, alpha)
                    m, l, alpha = softmax_stage(1 - slot, m, l)
                return m, l, alpha

            m, l, alpha = lax.fori_loop(0, (n_lat - 2) // 2, body, (m, l, alpha))
            value_stage(n_lat - 2, 0, alpha)
            m, l, alpha = softmax_stage(1, m, l)
        value_stage(n_lat - 1, (n_lat - 1) % 2, alpha)
    if ones_row is not None:
        l = acc_ref[ones_row:ones_row + 1, :]
    o_ref[0, 0] = (acc_ref[...] / l).T.reshape(G, tq, LANES).astype(BF16)


def _attention(q, kc, vct, kl, vlt, tq, tk, dv, ones_row=None):
    B, Hk, G, _, R = q.shape
    Lc = kc.shape[2]
    n_lat = 0 if kl is None else kl.shape[2] // tk
    qspec = pl.BlockSpec((1, 1, G, LANES, tq), lambda b, h, i: (b, h, 0, 0, i))
    ospec = pl.BlockSpec((1, 1, G, tq, LANES), lambda b, h, i: (b, h, 0, i, 0))
    kspec = lambda n: pl.BlockSpec((1, 1, n, LANES), lambda b, h, i: (b, h, 0, 0))
    vspec = lambda n: pl.BlockSpec((1, 1, LANES, n), lambda b, h, i: (b, h, 0, 0))
    ins = [q, kc, vct] + ([kl, vlt] if n_lat else [])
    specs = [qspec, kspec(Lc), vspec(Lc)] + ([kspec(kl.shape[2]), vspec(kl.shape[2])] if n_lat else [])
    return pl.pallas_call(
        functools.partial(_attn_kernel, n_lat, tk, dv, ones_row),
        out_shape=jax.ShapeDtypeStruct((B, Hk, G, R, LANES), BF16),
        grid=(B, Hk, R // tq),
        in_specs=specs,
        out_specs=ospec,
        scratch_shapes=[pltpu.VMEM((LANES, G * tq), F32)]
        + ([pltpu.VMEM((tk, G * tq), F32)] * 2 + [pltpu.VMEM((tk, G * tq), BF16)] * 2 if n_lat else []),
        compiler_params=_params(("arbitrary", "arbitrary", "arbitrary"), VMEM_LIMIT),
        name="attn",
    )(*ins)


def _dft_mats(n):
    k = np.arange(n, dtype=np.float64)
    ang = 2.0 * np.pi * np.outer(k, k) / n
    return np.cos(ang), np.sin(ang)


def _split(x):
    hi = x.astype(BF16)
    return hi, (x - hi.astype(F32)).astype(BF16)


def _dot3(a, b):
    d = lambda p, q: jnp.dot(p, q, preferred_element_type=F32)
    return d(a[0], b[0]) + d(a[0], b[1]) + d(a[1], b[0])


def _split_const(m):
    return jnp.stack(_split(jnp.asarray(m, F32)))


def _dft1_kernel(x_ref, m1_ref, tw_ref, o_ref):
    s1, ns2 = x_ref.shape[1], x_ref.shape[2]
    for i in range(ns2):
        a = jnp.dot(m1_ref[...], x_ref[0, :, i, :], precision=HIGHEST, preferred_element_type=F32)
        are, aim = a[:s1], a[s1:]
        tre = jnp.tile(tw_ref[0, :, i * LANES:(i + 1) * LANES], (1, FNET_GROUPS))
        tim = jnp.tile(tw_ref[1, :, i * LANES:(i + 1) * LANES], (1, FNET_GROUPS))
        o_ref[0, 0, :, i, :] = are * tre - aim * tim
        o_ref[0, 1, :, i, :] = are * tim + aim * tre


def _dft2_kernel(a_ref, m2_ref, m3_ref, o_ref):
    kb, n2 = a_ref.shape[2], a_ref.shape[3]
    for j in range(kb):
        rhs = jnp.concatenate([a_ref[0, 0, j], a_ref[0, 1, j]], axis=0)
        y = _dot3((m2_ref[0], m2_ref[1]), _split(rhs))
        for g in range(FNET_GROUPS):
            sl = slice(g * LANES, (g + 1) * LANES)
            lhs = jnp.concatenate([y[:n2, sl], y[n2:, sl]], axis=1)
            o_ref[0, :, j, sl] = _dot3(_split(lhs), (m3_ref[0], m3_ref[1]))


def _fourier_lat(f):
    B, S, W = f.shape
    n2 = LANES
    s1 = S // n2
    c1, sn1 = _dft_mats(s1)
    c2, sn2 = _dft_mats(n2)
    cc, sc = _dft_mats(FNET_GROUP_DIM)
    m1 = jnp.asarray(np.concatenate([c1, -sn1], axis=0), F32)
    m2 = _split_const(np.block([[c2, sn2], [-sn2, c2]]))
    norm = 1.0 / math.sqrt(S * FNET_GROUP_DIM)
    m3 = _split_const(np.concatenate([cc, sc], axis=0) * norm)
    ang = 2.0 * np.pi * np.outer(np.arange(s1), np.arange(n2)) / S
    tw = np.stack([np.cos(ang), -np.sin(ang)])
    tw = jnp.asarray(np.repeat(tw[:, :, :, None], LANES, axis=3).reshape(2, s1, n2 * LANES), F32)

    ns2 = 8
    a = pl.pallas_call(
        _dft1_kernel,
        out_shape=jax.ShapeDtypeStruct((B, 2, s1, n2, W), F32),
        grid=(B, n2 // ns2),
        in_specs=[pl.BlockSpec((1, s1, ns2, W), lambda b, j: (b, 0, j, 0)),
                  pl.BlockSpec(m1.shape, lambda b, j: (0, 0)),
                  pl.BlockSpec((2, s1, ns2 * LANES), lambda b, j: (0, 0, j))],
        out_specs=pl.BlockSpec((1, 2, s1, ns2, W), lambda b, j: (b, 0, 0, j, 0)),
        compiler_params=_params(("arbitrary", "arbitrary"), VMEM_LIMIT),
        name="dft1",
    )(f.reshape(B, s1, n2, W), m1, tw)
    kb = min(8, s1)
    y = pl.pallas_call(
        _dft2_kernel,
        out_shape=jax.ShapeDtypeStruct((B, n2, s1, W), F32),
        grid=(B, s1 // kb),
        in_specs=[pl.BlockSpec((1, 2, kb, n2, W), lambda b, j: (b, 0, j, 0, 0)),
                  pl.BlockSpec(m2.shape, lambda b, j: (0, 0, 0)),
                  pl.BlockSpec(m3.shape, lambda b, j: (0, 0, 0))],
        out_specs=pl.BlockSpec((1, n2, kb, W), lambda b, j: (b, 0, j, 0)),
        compiler_params=_params(("arbitrary", "arbitrary"), VMEM_LIMIT),
        name="dft2",
    )(a, m2, m3)
    return y.reshape(B, S, W)


def _dftc_kernel(f_ref, mc_ref, m3_ref, o_ref):
    n = f_ref.shape[1]
    a = _dot3((mc_ref[0], mc_ref[1]), _split(f_ref[0]))
    for g in range(FNET_GROUPS):
        sl = slice(g * LANES, (g + 1) * LANES)
        lhs = jnp.concatenate([a[:n, sl], a[n:, sl]], axis=1)
        o_ref[0, :, sl] = _dot3(_split(lhs), (m3_ref[0], m3_ref[1]))


def _fourier_ctx(f):
    B, L, W = f.shape
    c, s = _dft_mats(L)
    cc, sc = _dft_mats(FNET_GROUP_DIM)
    mc = _split_const(np.concatenate([c, -s], axis=0))
    m3 = _split_const(np.concatenate([cc, sc], axis=0) / math.sqrt(L * FNET_GROUP_DIM))
    return pl.pallas_call(
        _dftc_kernel,
        out_shape=jax.ShapeDtypeStruct((B, L, W), F32),
        grid=(B,),
        in_specs=[pl.BlockSpec((1, L, W), lambda b: (b, 0, 0)),
                  pl.BlockSpec(mc.shape, lambda b: (0, 0, 0)),
                  pl.BlockSpec(m3.shape, lambda b: (0, 0, 0))],
        out_specs=pl.BlockSpec((1, L, W), lambda b: (b, 0, 0)),
        compiler_params=_params(("arbitrary",), VMEM_LIMIT),
        name="dftc",
    )(f, mc, m3)


def _out_kernel(has_f, *refs):
    if has_f:
        (h_ref, mod_ref, yf_ref, wof_ref, o_ref, woa_ref, n2_ref, wr_ref,
         h1_ref, m_ref, aff_ref) = refs
    else:
        (h_ref, mod_ref, o_ref, woa_ref, n2_ref, wr_ref, h1_ref, m_ref, aff_ref) = refs
    D = D_MODEL
    Hk, G = o_ref.shape[1], o_ref.shape[2]
    ocat = jnp.concatenate([o_ref[0, hk, g] for hk in range(Hk) for g in range(G)], axis=1)
    y = jnp.dot(ocat, woa_ref[...], preferred_element_type=F32)
    if has_f:
        y = y + jnp.dot(yf_ref[0].astype(BF16), wof_ref[...], preferred_element_type=F32)
    mod = mod_ref[0]
    h1 = h_ref[0] + mod[:, 2 * D:3 * D] * y
    h1_ref[0] = h1
    m = _modulate(h1, n2_ref[...], mod[:, 3 * D:4 * D], mod[:, 4 * D:5 * D])
    _store_tiled(m_ref.at[0], slice(None), m)
    m_hi, m_lo = _split(m)
    nt = lambda a, b: lax.dot_general(a, b, NT, preferred_element_type=F32)
    logit = nt(wr_ref[0], m_hi) + nt(wr_ref[0], m_lo) + nt(wr_ref[1], m_hi)
    e = jnp.exp(logit - jnp.max(logit, axis=0, keepdims=True))
    aff_ref[0] = e / jnp.sum(e, axis=0, keepdims=True)


def _out_proj(h, mod, yf, wof, o, woa, n2, wr_t, tm):
    B, R, D = h.shape
    _, Hk, G, _, _ = o.shape
    full = lambda a: pl.BlockSpec(a.shape, lambda b, i: (0,) * a.ndim)
    row = lambda w_: pl.BlockSpec((1, tm, w_), lambda b, i: (b, i, 0))
    modspec = pl.BlockSpec((1, 1, N_MOD * D), lambda b, i: (b, 0, 0))
    ospec = pl.BlockSpec((1, Hk, G, tm, LANES), lambda b, i: (b, 0, 0, i, 0))
    has_f = yf is not None
    ins = [h, mod] + ([yf, wof] if has_f else []) + [o, woa, n2, wr_t]
    specs = ([row(D), modspec] + ([row(FNET_WIDTH), full(wof)] if has_f else [])
             + [ospec, full(woa), full(n2), full(wr_t)])
    return pl.pallas_call(
        functools.partial(_out_kernel, has_f),
        out_shape=(jax.ShapeDtypeStruct((B, R, D), F32),
                   jax.ShapeDtypeStruct((B, R, SUB, LANES), F32),
                   jax.ShapeDtypeStruct((B, N_EXPERTS, R), F32)),
        grid=(B, R // tm),
        in_specs=specs,
        out_specs=(row(D), pl.BlockSpec((1, tm, SUB, LANES), lambda b, i: (b, i, 0, 0)),
                   pl.BlockSpec((1, N_EXPERTS, tm), lambda b, i: (b, 0, i))),
        compiler_params=_params(("arbitrary", "arbitrary"), VMEM_LIMIT),
        name="out_proj",
    )(*ins)


COMPACT_UNROLL = 4


def _topk_kernel(cap, aff_ref, u_ref, ones_ref, lmat_ref, lc_ref, lg_ref, cnt_ref, off_ref,
                 pos_ref, ac_ref, lct_ref, lgt_ref):
    a = aff_ref[0]
    E, N = a.shape
    NC = N // LANES
    R = NC * E
    keys = pltpu.bitcast(a, I32)

    def bit_step(i, tau):
        cand = tau | jnp.left_shift(jnp.int32(1), 30 - i)
        cnt = jnp.sum((keys >= cand).astype(I32), axis=1, keepdims=True)
        return jnp.where(cnt >= cap, cand, tau)

    tau = lax.fori_loop(0, 31, bit_step, jnp.zeros((E, 1), I32))
    gt = (keys > tau).astype(F32)
    eq = (keys == tau).astype(F32)
    need = (cap - jnp.sum(gt, axis=1, keepdims=True))

    def chunked(x):
        return jnp.concatenate([x[:, c * LANES:(c + 1) * LANES] for c in range(NC)], axis=0)

    a_c, gt_c, eq_c = chunked(a), chunked(gt), chunked(eq)
    need_c = jnp.tile(need, (NC, 1))

    def prefix(x):
        xb = x.astype(BF16)
        loc = jnp.dot(xb, u_ref[...], preferred_element_type=F32)
        tot = jnp.dot(xb, ones_ref[...], preferred_element_type=F32)
        offs = jnp.dot(lmat_ref[...], tot.astype(BF16), preferred_element_type=F32)
        return loc, tot, offs

    loc, tot, offs = prefix(eq_c)
    sel = jnp.maximum(gt_c, jnp.where(loc + offs < need_c, eq_c, 0.0))
    loc, tot, offs = prefix(sel)
    cnt_ref[0] = tot.astype(I32)
    off_ref[0] = offs.astype(I32)
    Rp = pos_ref.shape[1]
    selpos = jnp.where(sel > 0.0, loc, -1.0)
    if Rp > R:
        selpos = jnp.concatenate([selpos, jnp.full((Rp - R, LANES), -1.0, F32)], axis=0)
        a_c = jnp.concatenate([a_c, jnp.zeros((Rp - R, LANES), F32)], axis=0)
    pos_ref[...] = selpos.T
    ac_ref[...] = a_c.T
    tok = (lax.broadcasted_iota(I32, (LANES, Rp), 0)
           + lax.broadcasted_iota(I32, (LANES, Rp), 1) // E * LANES).astype(F32)

    def compact(i, carry):
        p = pos_ref[...]
        a_t = ac_ref[...]
        for u in range(COMPACT_UNROLL):
            j = i * COMPACT_UNROLL + u
            hit = p == lax.convert_element_type(j, F32)
            lct_ref[pl.ds(j, 1), :] = jnp.sum(jnp.where(hit, tok, 0.0), axis=0, keepdims=True)
            lgt_ref[pl.ds(j, 1), :] = jnp.sum(jnp.where(hit, a_t, 0.0), axis=0, keepdims=True)
        return carry

    lax.fori_loop(0, LANES // COMPACT_UNROLL, compact, 0)
    lc_ref[0] = lct_ref[...].T[:R].astype(I32)
    lg_ref[0] = lgt_ref[...].T[:R]


def _topk(aff, cap):
    B, E, N = aff.shape
    NC = N // LANES
    R = NC * E
    i = np.arange(LANES)
    u = jnp.asarray(i[:, None] < i[None, :], BF16)
    ones = jnp.ones((LANES, LANES), BF16)
    r = np.arange(R)
    lmat = jnp.asarray((r[:, None] % E == r[None, :] % E) & (r[None, :] // E < r[:, None] // E), BF16)
    full = lambda a: pl.BlockSpec(a.shape, lambda b: (0,) * a.ndim)
    ospec = pl.BlockSpec((1, R, LANES), lambda b: (b, 0, 0))
    sds = lambda dt: jax.ShapeDtypeStruct((B, R, LANES), dt)
    lc, lg, cnt, off = pl.pallas_call(
        functools.partial(_topk_kernel, cap),
        out_shape=(sds(I32), sds(F32), sds(I32), sds(I32)),
        grid=(B,),
        in_specs=[pl.BlockSpec((1, E, N), lambda b: (b, 0, 0)), full(u), full(ones), full(lmat)],
        out_specs=(ospec, ospec, ospec, ospec),
        scratch_shapes=[pltpu.VMEM((LANES, -(-R // LANES) * LANES), F32)] * 4,
        compiler_params=_params(("arbitrary",), VMEM_LIMIT),
        name="topk",
    )(aff, u, ones, lmat)
    by_expert = lambda x: x.reshape(B, NC, E, LANES).transpose(0, 2, 1, 3)
    flat = lambda x: by_expert(x).reshape(B, E, NC * LANES)
    return (flat(lc), flat(lg), by_expert(cnt)[..., 0], by_expert(off)[..., 0])


GATHER_ROWS = 256
ROW_UNROLL = 4


def _gather_kernel(cap, B, cnt_ref, off_ref, *refs):
    lc_refs = refs[:B]
    m_hbm, o_ref, x_ref, sems = refs[B:]
    e = pl.program_id(0)
    NC = lc_refs[0].shape[1] // LANES
    for b in range(B):
        def chunk_body(c, carry, b=b):
            base = off_ref[b, e, c] + b * cap
            n = cnt_ref[b, e, c]

            def start_row(r, priority):
                t = lc_refs[b][0, c * LANES + r]
                pltpu.make_async_copy(m_hbm.at[b, t], x_ref.at[base + r],
                                      sems.at[b]).start(priority=priority)

            def group_body(i, carry):
                for u in range(ROW_UNROLL):
                    start_row(i * ROW_UNROLL + u, u % 2)
                return carry

            def tail_body(r, carry):
                start_row(r, 0)
                return carry

            full = n // ROW_UNROLL
            lax.fori_loop(0, full, group_body, carry)
            return lax.fori_loop(full * ROW_UNROLL, n, tail_body, carry)

        lax.fori_loop(0, NC, chunk_body, 0)
    step = min(GATHER_ROWS, cap)
    for b in range(B):
        done = x_ref.at[pl.ds(b * cap, cap)]
        pltpu.make_async_copy(done, done, sems.at[b]).wait()
        for r0 in range(b * cap, (b + 1) * cap, step):
            rows = slice(r0, r0 + step)
            o_ref[0, rows, :] = _untile(x_ref, rows).astype(BF16)


def _gather(m, lc, cnt, off, cap):
    B, N, sub, _ = m.shape
    E = lc.shape[1]
    lc4 = lc[:, :, None, :]

    def table(b):
        return pl.BlockSpec((None, None, 1, lc.shape[2]), lambda e, *_: (b, e, 0, 0),
                            memory_space=pltpu.SMEM)

    return pl.pallas_call(
        functools.partial(_gather_kernel, cap, B),
        out_shape=jax.ShapeDtypeStruct((E, B * cap, sub * LANES), BF16),
        grid_spec=pltpu.PrefetchScalarGridSpec(
            num_scalar_prefetch=2,
            grid=(E,),
            in_specs=[table(b) for b in range(B)] + [pl.BlockSpec(memory_space=pl.ANY)],
            out_specs=pl.BlockSpec((1, B * cap, sub * LANES), lambda e, *_: (e, 0, 0)),
            scratch_shapes=[pltpu.VMEM((B * cap, sub, LANES), m.dtype),
                            pltpu.SemaphoreType.DMA((B,))]),
        compiler_params=_params(("arbitrary",), VMEM_LIMIT),
        name="gather",
    )(cnt, off, *([lc4] * B), m)


def _ffn_kernel(n_parts, chunk_rows, *refs):
    wg_ref, wu_ref, wd_ref = refs[:3]
    x_refs = refs[3:3 + n_parts]
    o_refs = refs[3 + n_parts:3 + 2 * n_parts]
    acc_refs = refs[3 + 2 * n_parts:]
    j = pl.program_id(1)
    last = pl.num_programs(1) - 1
    wg = wg_ref[0].astype(BF16)
    wu = wu_ref[0].astype(BF16)
    wd = wd_ref[0].astype(BF16)
    chunks = [(x_ref, o_ref, acc_ref, slice(r0, r0 + nrows))
              for x_ref, o_ref, acc_ref, nrows in zip(x_refs, o_refs, acc_refs, chunk_rows)
              for r0 in range(0, x_ref.shape[1], nrows)]

    @pl.when(j == 0)
    def _():
        for acc_ref in acc_refs:
            acc_ref[...] = jnp.zeros_like(acc_ref)

    for x_ref, _, acc_ref, rows in chunks:
        x = x_ref[0, rows, :]
        a = jnp.dot(x, wg, preferred_element_type=F32)
        u = jnp.dot(x, wu, preferred_element_type=F32)
        hh = (a * jax.nn.sigmoid(a) * u).astype(BF16)
        acc_ref[rows, :] += jnp.dot(hh, wd, preferred_element_type=F32)

    @pl.when(j == last)
    def _():
        for _, o_ref, acc_ref, rows in chunks:
            _store_tiled(o_ref.at[0], rows, acc_ref[rows, :])


def _ffn(xs, layer, w_gate, w_up, w_down, chunk_rows):
    _, E, D, F = w_gate.shape
    tf = 512
    assert F // tf >= 2
    xspec = lambda x: pl.BlockSpec((1, x.shape[1], D), lambda e, j: (e, 0, 0))
    ospec = lambda x: pl.BlockSpec((1, x.shape[1], SUB, LANES), lambda e, j: (e, 0, 0, 0))
    outs = pl.pallas_call(
        functools.partial(_ffn_kernel, len(xs), chunk_rows),
        out_shape=tuple(jax.ShapeDtypeStruct((E, x.shape[1], SUB, LANES), F32) for x in xs),
        grid=(E, F // tf),
        in_specs=[pl.BlockSpec((None, 1, D, tf), lambda e, j: (layer, e, 0, j)),
                  pl.BlockSpec((None, 1, D, tf), lambda e, j: (layer, e, 0, j)),
                  pl.BlockSpec((None, 1, tf, D), lambda e, j: (layer, e, j, 0))]
        + [xspec(x) for x in xs],
        out_specs=tuple(ospec(x) for x in xs),
        scratch_shapes=[pltpu.VMEM((x.shape[1], D), F32) for x in xs],
        compiler_params=_params(("arbitrary", "arbitrary"), VMEM_LIMIT),
        name="ffn",
    )(w_gate, w_up, w_down, *xs)
    return outs


def _combine_kernel(cnt_ref, off_ref, lc_ref, lg_ref, y_ref, o_ref):
    b = pl.program_id(0)
    hf = pl.program_id(1)
    e = pl.program_id(2)
    nch = o_ref.shape[1] // LANES
    first = hf * o_ref.shape[1]

    @pl.when(e == 0)
    def _():
        o_ref[...] = jnp.zeros_like(o_ref)

    def chunk_body(ci, carry):
        c = hf * nch + ci
        base = off_ref[b, e, c]

        n = cnt_ref[b, e, c]

        def rows_body(r0, width):
            ts = [lc_ref[0, c * LANES + r0 + u] - first for u in range(width)]
            vals = [o_ref[0, ts[u]] + lg_ref[0, c * LANES + r0 + u] * y_ref[0, base + r0 + u]
                    for u in range(width)]
            for u in range(width):
                o_ref[0, ts[u]] = vals[u]

        def group_body(i, carry):
            rows_body(i * ROW_UNROLL, ROW_UNROLL)
            return carry

        def tail_body(r, carry):
            rows_body(r, 1)
            return carry

        full = n // ROW_UNROLL
        lax.fori_loop(0, full, group_body, carry)
        return lax.fori_loop(full * ROW_UNROLL, n, tail_body, carry)

    lax.fori_loop(0, nch, chunk_body, 0)


def _combine(y, lc, lg, cnt, off, n_tokens, cap, n_split):
    E = y.shape[0]
    B = lc.shape[0]
    nh = n_tokens // n_split
    smem = lambda: pl.BlockSpec((None, None, 1, lc.shape[2]), lambda b, h, e, *_: (b, e, 0, 0),
                                memory_space=pltpu.SMEM)
    return pl.pallas_call(
        _combine_kernel,
        out_shape=jax.ShapeDtypeStruct((B, n_tokens, SUB, LANES), F32),
        grid_spec=pltpu.PrefetchScalarGridSpec(
            num_scalar_prefetch=2,
            grid=(B, n_split, E),
            in_specs=[smem(), smem(),
                      pl.BlockSpec((1, cap, SUB, LANES), lambda b, h, e, *_: (e, b, 0, 0))],
            out_specs=pl.BlockSpec((1, nh, SUB, LANES), lambda b, h, e, *_: (b, h, 0, 0))),
        compiler_params=_params(("arbitrary", "arbitrary", "arbitrary"), VMEM_LIMIT),
        name="combine",
    )(cnt, off, lc[:, :, None, :], lg[:, :, None, :], y)


def _final_kernel(h_ref, moe_ref, mod_ref, g_ref, o_ref):
    D = D_MODEL
    h = h_ref[0] + mod_ref[0][:, 5 * D:6 * D] * _untile(moe_ref.at[0], slice(None))
    o_ref[0] = _rms(h, g_ref[...])


def _final(h, moe, mod, g, tm):
    B, R, D = h.shape
    row = pl.BlockSpec((1, tm, D), lambda b, i: (b, i, 0))
    return pl.pallas_call(
        _final_kernel,
        out_shape=jax.ShapeDtypeStruct((B, R, D), F32),
        grid=(B, R // tm),
        in_specs=[row, pl.BlockSpec((1, tm, SUB, LANES), lambda b, i: (b, i, 0, 0)),
                  pl.BlockSpec((1, 1, N_MOD * D), lambda b, i: (b, 0, 0)),
                  pl.BlockSpec((1, D), lambda b, i: (0, 0))],
        out_specs=row,
        compiler_params=_params(("arbitrary", "arbitrary"), VMEM_LIMIT),
        name="final",
    )(h, moe, mod, g)


def _rope_tables(n_lat, n_ctx, segments):
    t = jnp.arange(n_lat)
    pos = {"row": (t // GRID_W).astype(F32), "col": (t % GRID_W).astype(F32)}
    freq = {"row": np.zeros((2, LANES), np.float32), "col": np.zeros((2, LANES), np.float32)}
    first = np.zeros(LANES, np.float32)
    second = np.zeros(LANES, np.float32)
    for lane0, width, which in segments:
        half = width // 2
        idx = np.arange(half, dtype=np.float32)
        for lo, mask in ((lane0, first), (lane0 + half, second)):
            freq[which][0, lo:lo + half] = idx / half
            freq[which][1, lo:lo + half] = 1.0
            mask[lo:lo + half] = 1.0
    ang = jnp.zeros((n_lat, LANES), F32)
    for which in ("row", "col"):
        inv = (ROPE_THETA ** (-jnp.asarray(freq[which][0]))) * jnp.asarray(freq[which][1])
        ang = ang + pos[which][:, None] * inv[None, :]
    sin = jnp.sin(ang)
    lat = (jnp.cos(ang), -sin * first[None, :], sin * second[None, :])
    ctx = (jnp.ones((n_ctx, LANES), F32), jnp.zeros((n_ctx, LANES), F32), jnp.zeros((n_ctx, LANES), F32))
    return lat, ctx


def _head_slots(w, n_heads, width, lo, hi):
    k = w.shape[0]
    w3 = w.reshape(k, n_heads, width)[:, :, lo:hi]
    return jnp.pad(w3, ((0, 0), (0, 0), (0, LANES - (hi - lo)))).reshape(k, n_heads * LANES)


def _moe(m_l, aff_l, m_c, aff_c, layer, w_gate, w_up, w_down):
    B, S = m_l.shape[:2]
    cap_l = CAPACITY_FACTOR * S // N_EXPERTS
    lc, lg, cnt, off = _topk(aff_l, cap_l)
    xs = [_gather(m_l, lc, cnt, off, cap_l)]
    chunk_rows = [cap_l]
    if m_c is not None:
        L = m_c.shape[1]
        cap_c = CAPACITY_FACTOR * L // N_EXPERTS
        lcc, lgc, cntc, offc = _topk(aff_c, cap_c)
        xs.append(_gather(m_c, lcc, cntc, offc, cap_c))
        chunk_rows.append(B * cap_c)
    ys = _ffn(xs, layer, w_gate, w_up, w_down, tuple(chunk_rows))
    out_l = _combine(ys[0], lc, lg, cnt, off, S, cap_l, 2)
    out_c = None
    if m_c is not None:
        out_c = _combine(ys[1], lcc, lgc, cntc, offc, L, cap_c, 1)
    return out_l, out_c


def kernel(x, c, ctx, c_ctx, ada_w, ada_b, norm1, norm2, ab_w_in, ab_q_norm, ab_w_uq, ab_kv_norm,
           ab_w_ukv, ab_w_o, c_w_in, c_q_gain, c_k_gain, c_w_o, moe_router, moe_w_gate, moe_w_up,
           moe_w_down, final_norm):
    B, S, D = x.shape
    L = ctx.shape[1]
    depth = ada_w.shape[0]
    tm_l, tm_c = 512, L
    row2 = lambda v: v.reshape(1, -1)

    c8 = jnp.zeros((8, D), F32).at[:B].set(c).at[B].set(c_ctx)
    mod = _ada(c8, ada_w, ada_b)
    mod_l = [mod[i, :B][:, None, :] for i in range(depth)]
    mod_c = [jnp.broadcast_to(mod[i, B][None, None, :], (B, 1, N_MOD * D)) for i in range(depth)]

    h_l, h_c = x, ctx
    moe_l = moe_c = None
    for i in range(depth):
        last = i == depth - 1
        j = i // 2
        wr_t = _split_const(moe_router[i].T)
        n2 = row2(norm2[i])
        if i % 2 == 0:
            assert moe_l is None
            tabs_l, tabs_c = _rope_tables(S, L, ((MLA_NOPE, MLA_ROPE // 2, "row"),
                                                 (MLA_NOPE + MLA_ROPE // 2, MLA_ROPE // 2, "col")))
            w_ukv = ab_w_ukv[j]
            ekr = np.zeros((MLA_ROPE, MLA_HEADS * LANES), np.float32)
            for hd in range(MLA_HEADS):
                ekr[np.arange(MLA_ROPE), hd * LANES + MLA_NOPE + np.arange(MLA_ROPE)] = 1.0
            w = {
                "win": ab_w_in[j].astype(BF16),
                "qn": row2(ab_q_norm[j]),
                "wuqt": _head_slots(ab_w_uq[j], MLA_HEADS, MLA_QK, 0, MLA_QK).T.astype(BF16),
                "kvn": row2(ab_kv_norm[j]),
                "wuk": _head_slots(w_ukv, MLA_HEADS, MLA_NOPE + MLA_V, 0, MLA_NOPE).astype(BF16),
                "wuvt": _head_slots(w_ukv, MLA_HEADS, MLA_NOPE + MLA_V, MLA_NOPE,
                                    MLA_NOPE + MLA_V).T.astype(BF16),
                "ekr": jnp.asarray(ekr, BF16),
            }
            n1 = row2(norm1[i])
            f_l, q_l, k_l, vt_l = _in_ab(h_l, mod_l[i], n1, w, tabs_l, tm_l)
            f_c, q_c, k_c, vt_c = _in_ab(h_c, mod_c[i], n1, w, tabs_c, tm_c)
            o_l = _attention(q_l[:, :, None], k_c, vt_c, k_l, vt_l, min(S, ATTN_QUERIES), ATTN_KEYS,
                             MLA_VROWS, MLA_V)
            yf_l = _fourier_lat(f_l)
            wof = ab_w_o[j][:FNET_WIDTH].astype(BF16)
            woa = jnp.pad(ab_w_o[j][FNET_WIDTH:].reshape(MLA_HEADS, MLA_V, D),
                          ((0, 0), (0, LANES - MLA_V), (0, 0))).reshape(MLA_HEADS * LANES, D).astype(BF16)
            h_l, m_l, aff_l = _out_proj(h_l, mod_l[i], yf_l, wof, o_l, woa, n2, wr_t, tm_l)
            m_c = aff_c = None
            if not last:
                o_c = _attention(q_c[:, :, None], k_c, vt_c, None, None, L, L, MLA_VROWS, MLA_V)
                yf_c = _fourier_ctx(f_c)
                h_c, m_c, aff_c = _out_proj(h_c, mod_c[i], yf_c, wof, o_c, woa, n2, wr_t, tm_c)
        else:
            tabs_l, tabs_c = _rope_tables(S, L, ((0, GQA_HEAD_DIM // 2, "row"),
                                                 (GQA_HEAD_DIM // 2, GQA_HEAD_DIM // 2, "col")))
            nq = GQA_HEADS * GQA_HEAD_DIM
            nqk = nq + GQA_KV_HEADS * GQA_HEAD_DIM
            w = {"wqt": c_w_in[j][:, :nq].T.astype(BF16), "wk": c_w_in[j][:, nq:nqk].astype(BF16),
                 "wvt": c_w_in[j][:, nqk:].T.astype(BF16),
                 "qg": c_q_gain[j].reshape(-1, 1), "kg": row2(c_k_gain[j])}
            n1 = row2(norm1[i])
            h_l, q_l, k_l, vt_l = _in_c(h_l, moe_l, mod_l[i - 1], mod_l[i], n1, w, tabs_l, tm_l)
            h_c, q_c, k_c, vt_c = _in_c(h_c, moe_c, mod_c[i - 1], mod_c[i], n1, w, tabs_c, tm_c)
            grp = lambda q: q.reshape(B, GQA_KV_HEADS, GQA_GROUP, LANES, q.shape[3])
            o_l = _attention(grp(q_l), k_c, vt_c, k_l, vt_l, min(S, ATTN_QUERIES // GQA_GROUP),
                             ATTN_KEYS, GQA_HEAD_DIM)
            woa = c_w_o[j].astype(BF16)
            h_l, m_l, aff_l = _out_proj(h_l, mod_l[i], None, None, o_l, woa, n2, wr_t, tm_l)
            m_c = aff_c = None
            if not last:
                o_c = _attention(grp(q_c), k_c, vt_c, None, None, L, L, GQA_HEAD_DIM)
                h_c, m_c, aff_c = _out_proj(h_c, mod_c[i], None, None, o_c, woa, n2, wr_t, tm_c)
        moe_l, moe_c = _moe(m_l, aff_l, m_c, aff_c, i, moe_w_gate, moe_w_up, moe_w_down)
    return _final(h_l, moe_l, mod_l[depth - 1], row2(final_norm), tm_l)
```

```python
import functools
import math

import jax
import jax.numpy as jnp
import numpy as np
from jax import lax
from jax.experimental import pallas as pl
from jax.experimental.pallas import tpu as pltpu

F32 = jnp.float32
BF16 = jnp.bfloat16
I32 = jnp.int32
HIGHEST = lax.Precision.HIGHEST

D_MODEL = 1024
GRID_W = 64
EPS = 1e-6
ROPE_THETA = 10000.0
N_MOD = 6
FNET_GROUPS = 4
FNET_GROUP_DIM = 128
FNET_WIDTH = FNET_GROUPS * FNET_GROUP_DIM
MLA_HEADS = 8
MLA_Q_LORA = 256
MLA_KV_LORA = 128
MLA_NOPE = 64
MLA_ROPE = 32
MLA_V = 64
MLA_QK = MLA_NOPE + MLA_ROPE
GQA_HEADS = 8
GQA_KV_HEADS = 2
GQA_GROUP = GQA_HEADS // GQA_KV_HEADS
GQA_HEAD_DIM = 128
N_EXPERTS = 16
EXPERT_FF = 2048
CAPACITY_FACTOR = 2

LANES = 128
MLA_VROWS = MLA_V + 16
SUB = D_MODEL // LANES
VMEM_LIMIT = 56 * 1024 * 1024
LOG2E = math.log2(math.e)
NT = (((1,), (1,)), ((), ()))


def _params(sem, vmem=None):
    return pltpu.CompilerParams(dimension_semantics=sem, vmem_limit_bytes=vmem)


def _rms(x, g):
    return x * lax.rsqrt(jnp.mean(x * x, axis=-1, keepdims=True) + EPS) * g


def _modulate(h, g, shift, scale):
    return _rms(h, g) * (1.0 + scale) + shift


def _rope(x, cos, sin_fwd, sin_bwd, shift):
    return (x * cos + pltpu.roll(x, LANES - shift, 1) * sin_fwd
            + pltpu.roll(x, shift, 1) * sin_bwd)


def _rope_t(x, cos, sin_fwd, sin_bwd, shift):
    return (x * cos + pltpu.roll(x, LANES - shift, 0) * sin_fwd
            + pltpu.roll(x, shift, 0) * sin_bwd)


def _untile(ref, rows):
    x = ref[rows]
    n = x.shape[0]
    y = pltpu.einshape("grsl->gsrl", x.reshape(n // 8, 8, SUB, LANES))
    return jnp.concatenate([y[:, s].reshape(n, LANES) for s in range(SUB)], axis=1)


def _store_tiled(ref, rows, x):
    n = x.shape[0]
    y = jnp.concatenate([x[:, s * LANES:(s + 1) * LANES].reshape(n // 8, 1, 8, LANES)
                         for s in range(SUB)], axis=1)
    ref[rows] = pltpu.einshape("gsrl->grsl", y).reshape(n, SUB, LANES)


def _ada_kernel(c_ref, w_ref, b_ref, o_ref):
    c = c_ref[...]
    x = c * jax.nn.sigmoid(c)
    o_ref[0] = jnp.dot(x, w_ref[0], precision=HIGHEST, preferred_element_type=F32) + b_ref[0]


def _ada(c8, ada_w, ada_b):
    depth, d, n = ada_w.shape
    tn = 1536
    return pl.pallas_call(
        _ada_kernel,
        out_shape=jax.ShapeDtypeStruct((depth, 8, n), F32),
        grid=(depth, n // tn),
        in_specs=[pl.BlockSpec((8, d), lambda l, j: (0, 0)),
                  pl.BlockSpec((1, d, tn), lambda l, j: (l, 0, j)),
                  pl.BlockSpec((1, 1, tn), lambda l, j: (l, 0, j))],
        out_specs=pl.BlockSpec((1, 8, tn), lambda l, j: (l, 0, j)),
        compiler_params=_params(("arbitrary", "arbitrary"), VMEM_LIMIT),
        name="ada",
    )(c8, ada_w, ada_b.reshape(depth, 1, n))


def _in_ab_kernel(scale, *refs):
    (h_ref, mod_ref, n1_ref, win_ref, qn_ref, wuqt_ref, kvn_ref, wuk_ref, wuvt_ref, ekr_ref,
     cos_ref, sf_ref, sb_ref, cost_ref, sft_ref, sbt_ref, f_ref, qt_ref, k_ref, vt_ref) = refs
    D = D_MODEL
    h = h_ref[0]
    mod = mod_ref[0]
    a = _modulate(h, n1_ref[...], mod[:, 0:D], mod[:, D:2 * D]).astype(BF16)
    p = jnp.dot(a, win_ref[...], preferred_element_type=F32)
    o = FNET_WIDTH
    f_ref[0] = p[:, :o]
    cq = p[:, o:o + MLA_Q_LORA]
    o += MLA_Q_LORA
    ckv = p[:, o:o + MLA_KV_LORA]
    o += MLA_KV_LORA
    kr = p[:, o:o + MLA_ROPE]
    cqn = _rms(cq, qn_ref[...]).astype(BF16)
    ckvn = _rms(ckv, kvn_ref[...]).astype(BF16)
    qt = lax.dot_general(wuqt_ref[...], cqn, NT, preferred_element_type=F32)
    k = (jnp.dot(ckvn, wuk_ref[...], preferred_element_type=F32)
         + jnp.dot(kr.astype(BF16), ekr_ref[...], preferred_element_type=F32))
    vt = lax.dot_general(wuvt_ref[...], ckvn, NT, preferred_element_type=F32)
    cos, sf, sb = cos_ref[...], sf_ref[...], sb_ref[...]
    cost, sft, sbt = cost_ref[...], sft_ref[...], sbt_ref[...]
    shift = MLA_ROPE // 4
    ones = lax.broadcasted_iota(I32, (LANES, vt.shape[1]), 0) == MLA_V
    for hd in range(MLA_HEADS):
        sl = slice(hd * LANES, (hd + 1) * LANES)
        qt_ref[0, hd] = (_rope_t(qt[sl, :], cost, sft, sbt, shift) * scale).astype(BF16)
        k_ref[0, hd] = _rope(k[:, sl], cos, sf, sb, shift).astype(BF16)
        vt_ref[0, hd] = jnp.where(ones, 1.0, vt[sl, :]).astype(BF16)


def _in_ab(h, mod, n1, w, tables, tm):
    B, R, D = h.shape
    H = MLA_HEADS
    full = lambda a: pl.BlockSpec(a.shape, lambda b, i: (0,) * a.ndim)
    row = lambda w_: pl.BlockSpec((1, tm, w_), lambda b, i: (b, i, 0))
    tab = pl.BlockSpec((tm, LANES), lambda b, i: (i, 0))
    hd = pl.BlockSpec((1, H, tm, LANES), lambda b, i: (b, 0, i, 0))
    hdt = pl.BlockSpec((1, H, LANES, tm), lambda b, i: (b, 0, 0, i))
    tabt = pl.BlockSpec((LANES, tm), lambda b, i: (0, i))
    weights = [n1, w["win"], w["qn"], w["wuqt"], w["kvn"], w["wuk"], w["wuvt"], w["ekr"]]
    return pl.pallas_call(
        functools.partial(_in_ab_kernel, MLA_QK ** -0.5 * LOG2E),
        out_shape=(jax.ShapeDtypeStruct((B, R, FNET_WIDTH), F32),
                   jax.ShapeDtypeStruct((B, H, LANES, R), BF16),
                   jax.ShapeDtypeStruct((B, H, R, LANES), BF16),
                   jax.ShapeDtypeStruct((B, H, LANES, R), BF16)),
        grid=(B, R // tm),
        in_specs=[row(D), pl.BlockSpec((1, 1, N_MOD * D), lambda b, i: (b, 0, 0))]
        + [full(a) for a in weights] + [tab, tab, tab, tabt, tabt, tabt],
        out_specs=(row(FNET_WIDTH), hdt, hd, hdt),
        compiler_params=_params(("arbitrary", "arbitrary"), VMEM_LIMIT),
        name="in_ab",
    )(h, mod, *weights, *tables, *(t.T for t in tables))


def _in_c_kernel(scale, *refs):
    (h_ref, moe_ref, modp_ref, mod_ref, n1_ref, wqt_ref, wk_ref, wvt_ref, qg_ref, kg_ref,
     cos_ref, sf_ref, sb_ref, cost_ref, sft_ref, sbt_ref, h2_ref, qt_ref, k_ref, vt_ref) = refs
    D = D_MODEL
    h = h_ref[0] + modp_ref[0][:, 5 * D:6 * D] * _untile(moe_ref.at[0], slice(None))
    h2_ref[0] = h
    mod = mod_ref[0]
    a = _modulate(h, n1_ref[...], mod[:, 0:D], mod[:, D:2 * D]).astype(BF16)
    qt = lax.dot_general(wqt_ref[...], a, NT, preferred_element_type=F32)
    p = jnp.dot(a, wk_ref[...], preferred_element_type=F32)
    vt = lax.dot_general(wvt_ref[...], a, NT, preferred_element_type=F32)
    cos, sf, sb = cos_ref[...], sf_ref[...], sb_ref[...]
    cost, sft, sbt = cost_ref[...], sft_ref[...], sbt_ref[...]
    shift = GQA_HEAD_DIM // 4
    for hd in range(GQA_HEADS):
        x = qt[hd * LANES:(hd + 1) * LANES, :]
        x = x * lax.rsqrt(jnp.mean(x * x, axis=0, keepdims=True) + EPS) * qg_ref[...]
        qt_ref[0, hd] = (_rope_t(x, cost, sft, sbt, shift) * scale).astype(BF16)
    for hd in range(GQA_KV_HEADS):
        x = _rms(p[:, hd * LANES:(hd + 1) * LANES], kg_ref[...])
        k_ref[0, hd] = _rope(x, cos, sf, sb, shift).astype(BF16)
        vt_ref[0, hd] = vt[hd * LANES:(hd + 1) * LANES, :].astype(BF16)


def _in_c(h, moe, mod_prev, mod, n1, w, tables, tm):
    B, R, D = h.shape
    full = lambda a: pl.BlockSpec(a.shape, lambda b, i: (0,) * a.ndim)
    row = lambda w_: pl.BlockSpec((1, tm, w_), lambda b, i: (b, i, 0))
    tiled = pl.BlockSpec((1, tm, SUB, LANES), lambda b, i: (b, i, 0, 0))
    modspec = pl.BlockSpec((1, 1, N_MOD * D), lambda b, i: (b, 0, 0))
    tab = pl.BlockSpec((tm, LANES), lambda b, i: (i, 0))
    hd = lambda n: pl.BlockSpec((1, n, tm, LANES), lambda b, i: (b, 0, i, 0))
    hdt = pl.BlockSpec((1, GQA_KV_HEADS, LANES, tm), lambda b, i: (b, 0, 0, i))
    tabt = pl.BlockSpec((LANES, tm), lambda b, i: (0, i))
    hdq = pl.BlockSpec((1, GQA_HEADS, LANES, tm), lambda b, i: (b, 0, 0, i))
    weights = [n1, w["wqt"], w["wk"], w["wvt"], w["qg"], w["kg"]]
    return pl.pallas_call(
        functools.partial(_in_c_kernel, GQA_HEAD_DIM ** -0.5 * LOG2E),
        out_shape=(jax.ShapeDtypeStruct((B, R, D), F32),
                   jax.ShapeDtypeStruct((B, GQA_HEADS, LANES, R), BF16),
                   jax.ShapeDtypeStruct((B, GQA_KV_HEADS, R, LANES), BF16),
                   jax.ShapeDtypeStruct((B, GQA_KV_HEADS, LANES, R), BF16)),
        grid=(B, R // tm),
        in_specs=[row(D), tiled, modspec, modspec] + [full(a) for a in weights]
        + [tab, tab, tab, tabt, tabt, tabt],
        out_specs=(row(D), hdq, hd(GQA_KV_HEADS), hdt),
        compiler_params=_params(("arbitrary", "arbitrary"), VMEM_LIMIT),
        name="in_c",
    )(h, moe, mod_prev, mod, *weights, *tables, *(t.T for t in tables))


ATTN_QUERIES = 4096
ATTN_KEYS = 512


def _attn_kernel(n_lat, tk, dv, ones_row, *refs):
    if n_lat:
        q_ref, kc_ref, vct_ref, kl_ref, vlt_ref, o_ref, acc_ref, s0, s1, p0, p1 = refs
        s_bufs, p_bufs = (s0, s1), (p0, p1)
    else:
        q_ref, kc_ref, vct_ref, o_ref, acc_ref = refs
    G, tq = q_ref.shape[2], q_ref.shape[4]
    q = jnp.concatenate([q_ref[0, 0, g] for g in range(G)], axis=1)

    def scores(k):
        return jnp.dot(k, q, preferred_element_type=F32)

    s = scores(kc_ref[0, 0])
    m = jnp.max(s, axis=0, keepdims=True)
    p = jnp.exp2(s - m)
    l = jnp.sum(p, axis=0, keepdims=True) if ones_row is None else jnp.zeros_like(m)
    if dv < LANES:
        acc_ref[...] = jnp.zeros_like(acc_ref)
    acc_ref[:dv, :] = jnp.dot(vct_ref[0, 0, :dv, :], p.astype(BF16), preferred_element_type=F32)

    if n_lat:
        assert n_lat == 1 or n_lat % 2 == 0

        def chunk(c):
            return pl.ds(pl.multiple_of(c * tk, tk), tk)

        def score_stage(c, slot):
            s_bufs[slot][...] = scores(kl_ref[0, 0, chunk(c), :])

        def softmax_stage(slot, m, l):
            s = s_bufs[slot][...]
            m_new = jnp.maximum(m, jnp.max(s, axis=0, keepdims=True))
            p = jnp.exp2(s - m_new)
            p_bufs[slot][...] = p.astype(BF16)
            alpha = jnp.exp2(m - m_new)
            if ones_row is None:
                l = alpha * l + jnp.sum(p, axis=0, keepdims=True)
            return m_new, l, alpha

        def value_stage(c, slot, alpha):
            pv = jnp.dot(vlt_ref[0, 0, :dv, chunk(c)], p_bufs[slot][...],
                         preferred_element_type=F32)
            acc_ref[:dv, :] = alpha * acc_ref[:dv, :] + pv

        score_stage(0, 0)
        m, l, alpha = softmax_stage(0, m, l)
        if n_lat > 1:
            score_stage(1, 1)

            def body(i, carry):
                m, l, alpha = carry
                for slot in (0, 1):
                    c = 2 * i + slot
                    score_stage(c + 2, slot)
                    value_stage(c, slot, alpha)
                    m, l, alpha = softmax_stage(1 - slot, m, l)
                return m, l, alpha

            m, l, alpha = lax.fori_loop(0, (n_lat - 2) // 2, body, (m, l, alpha))
            value_stage(n_lat - 2, 0, alpha)
            m, l, alpha = softmax_stage(1, m, l)
        value_stage(n_lat - 1, (n_lat - 1) % 2, alpha)
    if ones_row is not None:
        l = acc_ref[ones_row:ones_row + 1, :]
    o_ref[0, 0] = (acc_ref[...] / l).T.reshape(G, tq, LANES).astype(BF16)


def _attention(q, kc, vct, kl, vlt, tq, tk, dv, ones_row=None):
    B, Hk, G, _, R = q.shape
    Lc = kc.shape[2]
    n_lat = 0 if kl is None else kl.shape[2] // tk
    qspec = pl.BlockSpec((1, 1, G, LANES, tq), lambda b, h, i: (b, h, 0, 0, i))
    ospec = pl.BlockSpec((1, 1, G, tq, LANES), lambda b, h, i: (b, h, 0, i, 0))
    kspec = lambda n: pl.BlockSpec((1, 1, n, LANES), lambda b, h, i: (b, h, 0, 0))
    vspec = lambda n: pl.BlockSpec((1, 1, LANES, n), lambda b, h, i: (b, h, 0, 0))
    ins = [q, kc, vct] + ([kl, vlt] if n_lat else [])
    specs = [qspec, kspec(Lc), vspec(Lc)] + ([kspec(kl.shape[2]), vspec(kl.shape[2])] if n_lat else [])
    return pl.pallas_call(
        functools.partial(_attn_kernel, n_lat, tk, dv, ones_row),
        out_shape=jax.ShapeDtypeStruct((B, Hk, G, R, LANES), BF16),
        grid=(B, Hk, R // tq),
        in_specs=specs,
        out_specs=ospec,
        scratch_shapes=[pltpu.VMEM((LANES, G * tq), F32)]
        + ([pltpu.VMEM((tk, G * tq), F32)] * 2 + [pltpu.VMEM((tk, G * tq), BF16)] * 2 if n_lat else []),
        compiler_params=_params(("arbitrary", "arbitrary", "arbitrary"), VMEM_LIMIT),
        name="attn",
    )(*ins)


def _dft_mats(n):
    k = np.arange(n, dtype=np.float64)
    ang = 2.0 * np.pi * np.outer(k, k) / n
    return np.cos(ang), np.sin(ang)


def _split(x):
    hi = x.astype(BF16)
    return hi, (x - hi.astype(F32)).astype(BF16)


def _dot3(a, b):
    d = lambda p, q: jnp.dot(p, q, preferred_element_type=F32)
    return d(a[0], b[0]) + d(a[0], b[1]) + d(a[1], b[0])


def _split_const(m):
    return jnp.stack(_split(jnp.asarray(m, F32)))


def _dft1_kernel(x_ref, m1_ref, tw_ref, o_ref):
    s1, ns2 = x_ref.shape[1], x_ref.shape[2]
    for i in range(ns2):
        a = jnp.dot(m1_ref[...], x_ref[0, :, i, :], precision=HIGHEST, preferred_element_type=F32)
        are, aim = a[:s1], a[s1:]
        tre = jnp.tile(tw_ref[0, :, i * LANES:(i + 1) * LANES], (1, FNET_GROUPS))
        tim = jnp.tile(tw_ref[1, :, i * LANES:(i + 1) * LANES], (1, FNET_GROUPS))
        o_ref[0, 0, :, i, :] = are * tre - aim * tim
        o_ref[0, 1, :, i, :] = are * tim + aim * tre


def _dft2_kernel(a_ref, m2_ref, m3_ref, o_ref):
    kb, n2 = a_ref.shape[2], a_ref.shape[3]
    for j in range(kb):
        rhs = jnp.concatenate([a_ref[0, 0, j], a_ref[0, 1, j]], axis=0)
        y = _dot3((m2_ref[0], m2_ref[1]), _split(rhs))
        for g in range(FNET_GROUPS):
            sl = slice(g * LANES, (g + 1) * LANES)
            lhs = jnp.concatenate([y[:n2, sl], y[n2:, sl]], axis=1)
            o_ref[0, :, j, sl] = _dot3(_split(lhs), (m3_ref[0], m3_ref[1]))


def _fourier_lat(f):
    B, S, W = f.shape
    n2 = LANES
    s1 = S // n2
    c1, sn1 = _dft_mats(s1)
    c2, sn2 = _dft_mats(n2)
    cc, sc = _dft_mats(FNET_GROUP_DIM)
    m1 = jnp.asarray(np.concatenate([c1, -sn1], axis=0), F32)
    m2 = _split_const(np.block([[c2, sn2], [-sn2, c2]]))
    norm = 1.0 / math.sqrt(S * FNET_GROUP_DIM)
    m3 = _split_const(np.concatenate([cc, sc], axis=0) * norm)
    ang = 2.0 * np.pi * np.outer(np.arange(s1), np.arange(n2)) / S
    tw = np.stack([np.cos(ang), -np.sin(ang)])
    tw = jnp.asarray(np.repeat(tw[:, :, :, None], LANES, axis=3).reshape(2, s1, n2 * LANES), F32)

    ns2 = 8
    a = pl.pallas_call(
        _dft1_kernel,
        out_shape=jax.ShapeDtypeStruct((B, 2, s1, n2, W), F32),
        grid=(B, n2 // ns2),
        in_specs=[pl.BlockSpec((1, s1, ns2, W), lambda b, j: (b, 0, j, 0)),
                  pl.BlockSpec(m1.shape, lambda b, j: (0, 0)),
                  pl.BlockSpec((2, s1, ns2 * LANES), lambda b, j: (0, 0, j))],
        out_specs=pl.BlockSpec((1, 2, s1, ns2, W), lambda b, j: (b, 0, 0, j, 0)),
        compiler_params=_params(("arbitrary", "arbitrary"), VMEM_LIMIT),
        name="dft1",
    )(f.reshape(B, s1, n2, W), m1, tw)
    kb = min(8, s1)
    y = pl.pallas_call(
        _dft2_kernel,
        out_shape=jax.ShapeDtypeStruct((B, n2, s1, W), F32),
        grid=(B, s1 // kb),
        in_specs=[pl.BlockSpec((1, 2, kb, n2, W), lambda b, j: (b, 0, j, 0, 0)),
                  pl.BlockSpec(m2.shape, lambda b, j: (0, 0, 0)),
                  pl.BlockSpec(m3.shape, lambda b, j: (0, 0, 0))],
        out_specs=pl.BlockSpec((1, n2, kb, W), lambda b, j: (b, 0, j, 0)),
        compiler_params=_params(("arbitrary", "arbitrary"), VMEM_LIMIT),
        name="dft2",
    )(a, m2, m3)
    return y.reshape(B, S, W)


def _dftc_kernel(f_ref, mc_ref, m3_ref, o_ref):
    n = f_ref.shape[1]
    a = _dot3((mc_ref[0], mc_ref[1]), _split(f_ref[0]))
    for g in range(FNET_GROUPS):
        sl = slice(g * LANES, (g + 1) * LANES)
        lhs = jnp.concatenate([a[:n, sl], a[n:, sl]], axis=1)
        o_ref[0, :, sl] = _dot3(_split(lhs), (m3_ref[0], m3_ref[1]))


def _fourier_ctx(f):
    B, L, W = f.shape
    c, s = _dft_mats(L)
    cc, sc = _dft_mats(FNET_GROUP_DIM)
    mc = _split_const(np.concatenate([c, -s], axis=0))
    m3 = _split_const(np.concatenate([cc, sc], axis=0) / math.sqrt(L * FNET_GROUP_DIM))
    return pl.pallas_call(
        _dftc_kernel,
        out_shape=jax.ShapeDtypeStruct((B, L, W), F32),
        grid=(B,),
        in_specs=[pl.BlockSpec((1, L, W), lambda b: (b, 0, 0)),
                  pl.BlockSpec(mc.shape, lambda b: (0, 0, 0)),
                  pl.BlockSpec(m3.shape, lambda b: (0, 0, 0))],
        out_specs=pl.BlockSpec((1, L, W), lambda b: (b, 0, 0)),
        compiler_params=_params(("arbitrary",), VMEM_LIMIT),
        name="dftc",
    )(f, mc, m3)


def _out_kernel(has_f, *refs):
    if has_f:
        (h_ref, mod_ref, yf_ref, wof_ref, o_ref, woa_ref, n2_ref, wr_ref,
         h1_ref, m_ref, aff_ref) = refs
    else:
        (h_ref, mod_ref, o_ref, woa_ref, n2_ref, wr_ref, h1_ref, m_ref, aff_ref) = refs
    D = D_MODEL
    Hk, G = o_ref.shape[1], o_ref.shape[2]
    ocat = jnp.concatenate([o_ref[0, hk, g] for hk in range(Hk) for g in range(G)], axis=1)
    y = jnp.dot(ocat, woa_ref[...], preferred_element_type=F32)
    if has_f:
        y = y + jnp.dot(yf_ref[0].astype(BF16), wof_ref[...], preferred_element_type=F32)
    mod = mod_ref[0]
    h1 = h_ref[0] + mod[:, 2 * D:3 * D] * y
    h1_ref[0] = h1
    m = _modulate(h1, n2_ref[...], mod[:, 3 * D:4 * D], mod[:, 4 * D:5 * D])
    _store_tiled(m_ref.at[0], slice(None), m)
    m_hi, m_lo = _split(m)
    nt = lambda a, b: lax.dot_general(a, b, NT, preferred_element_type=F32)
    logit = nt(wr_ref[0], m_hi) + nt(wr_ref[0], m_lo) + nt(wr_ref[1], m_hi)
    e = jnp.exp(logit - jnp.max(logit, axis=0, keepdims=True))
    aff_ref[0] = e / jnp.sum(e, axis=0, keepdims=True)


def _out_proj(h, mod, yf, wof, o, woa, n2, wr_t, tm):
    B, R, D = h.shape
    _, Hk, G, _, _ = o.shape
    full = lambda a: pl.BlockSpec(a.shape, lambda b, i: (0,) * a.ndim)
    row = lambda w_: pl.BlockSpec((1, tm, w_), lambda b, i: (b, i, 0))
    modspec = pl.BlockSpec((1, 1, N_MOD * D), lambda b, i: (b, 0, 0))
    ospec = pl.BlockSpec((1, Hk, G, tm, LANES), lambda b, i: (b, 0, 0, i, 0))
    has_f = yf is not None
    ins = [h, mod] + ([yf, wof] if has_f else []) + [o, woa, n2, wr_t]
    specs = ([row(D), modspec] + ([row(FNET_WIDTH), full(wof)] if has_f else [])
             + [ospec, full(woa), full(n2), full(wr_t)])
    return pl.pallas_call(
        functools.partial(_out_kernel, has_f),
        out_shape=(jax.ShapeDtypeStruct((B, R, D), F32),
                   jax.ShapeDtypeStruct((B, R, SUB, LANES), F32),
                   jax.ShapeDtypeStruct((B, N_EXPERTS, R), F32)),
        grid=(B, R // tm),
        in_specs=specs,
        out_specs=(row(D), pl.BlockSpec((1, tm, SUB, LANES), lambda b, i: (b, i, 0, 0)),
                   pl.BlockSpec((1, N_EXPERTS, tm), lambda b, i: (b, 0, i))),
        compiler_params=_params(("arbitrary", "arbitrary"), VMEM_LIMIT),
        name="out_proj",
    )(*ins)


COMPACT_UNROLL = 4


def _topk_kernel(cap, aff_ref, u_ref, ones_ref, lmat_ref, lc_ref, lg_ref, cnt_ref, off_ref,
                 pos_ref, ac_ref, lct_ref, lgt_ref):
    a = aff_ref[0]
    E, N = a.shape
    NC = N // LANES
    R = NC * E
    keys = pltpu.bitcast(a, I32)

    def bit_step(i, tau):
        cand = tau | jnp.left_shift(jnp.int32(1), 30 - i)
        cnt = jnp.sum((keys >= cand).astype(I32), axis=1, keepdims=True)
        return jnp.where(cnt >= cap, cand, tau)

    tau = lax.fori_loop(0, 31, bit_step, jnp.zeros((E, 1), I32))
    gt = (keys > tau).astype(F32)
    eq = (keys == tau).astype(F32)
    need = (cap - jnp.sum(gt, axis=1, keepdims=True))

    def chunked(x):
        return jnp.concatenate([x[:, c * LANES:(c + 1) * LANES] for c in range(NC)], axis=0)

    a_c, gt_c, eq_c = chunked(a), chunked(gt), chunked(eq)
    need_c = jnp.tile(need, (NC, 1))

    def prefix(x):
        xb = x.astype(BF16)
        loc = jnp.dot(xb, u_ref[...], preferred_element_type=F32)
        tot = jnp.dot(xb, ones_ref[...], preferred_element_type=F32)
        offs = jnp.dot(lmat_ref[...], tot.astype(BF16), preferred_element_type=F32)
        return loc, tot, offs

    loc, tot, offs = prefix(eq_c)
    sel = jnp.maximum(gt_c, jnp.where(loc + offs < need_c, eq_c, 0.0))
    loc, tot, offs = prefix(sel)
    cnt_ref[0] = tot.astype(I32)
    off_ref[0] = offs.astype(I32)
    Rp = pos_ref.shape[1]
    selpos = jnp.where(sel > 0.0, loc, -1.0)
    if Rp > R:
        selpos = jnp.concatenate([selpos, jnp.full((Rp - R, LANES), -1.0, F32)], axis=0)
        a_c = jnp.concatenate([a_c, jnp.zeros((Rp - R, LANES), F32)], axis=0)
    pos_ref[...] = selpos.T
    ac_ref[...] = a_c.T
    tok = (lax.broadcasted_iota(I32, (LANES, Rp), 0)
           + lax.broadcasted_iota(I32, (LANES, Rp), 1) // E * LANES).astype(F32)

    def compact(i, carry):
        p = pos_ref[...]
        a_t = ac_ref[...]
        for u in range(COMPACT_UNROLL):
            j = i * COMPACT_UNROLL + u
            hit = p == lax.convert_element_type(j, F32)
            lct_ref[pl.ds(j, 1), :] = jnp.sum(jnp.where(hit, tok, 0.0), axis=0, keepdims=True)
            lgt_ref[pl.ds(j, 1), :] = jnp.sum(jnp.where(hit, a_t, 0.0), axis=0, keepdims=True)
        return carry

    lax.fori_loop(0, LANES // COMPACT_UNROLL, compact, 0)
    lc_ref[0] = lct_ref[...].T[:R].astype(I32)
    lg_ref[0] = lgt_ref[...].T[:R]


def _topk(aff, cap):
    B, E, N = aff.shape
    NC = N // LANES
    R = NC * E
    i = np.arange(LANES)
    u = jnp.asarray(i[:, None] < i[None, :], BF16)
    ones = jnp.ones((LANES, LANES), BF16)
    r = np.arange(R)
    lmat = jnp.asarray((r[:, None] % E == r[None, :] % E) & (r[None, :] // E < r[:, None] // E), BF16)
    full = lambda a: pl.BlockSpec(a.shape, lambda b: (0,) * a.ndim)
    ospec = pl.BlockSpec((1, R, LANES), lambda b: (b, 0, 0))
    sds = lambda dt: jax.ShapeDtypeStruct((B, R, LANES), dt)
    lc, lg, cnt, off = pl.pallas_call(
        functools.partial(_topk_kernel, cap),
        out_shape=(sds(I32), sds(F32), sds(I32), sds(I32)),
        grid=(B,),
        in_specs=[pl.BlockSpec((1, E, N), lambda b: (b, 0, 0)), full(u), full(ones), full(lmat)],
        out_specs=(ospec, ospec, ospec, ospec),
        scratch_shapes=[pltpu.VMEM((LANES, -(-R // LANES) * LANES), F32)] * 4,
        compiler_params=_params(("arbitrary",), VMEM_LIMIT),
        name="topk",
    )(aff, u, ones, lmat)
    by_expert = lambda x: x.reshape(B, NC, E, LANES).transpose(0, 2, 1, 3)
    flat = lambda x: by_expert(x).reshape(B, E, NC * LANES)
    return (flat(lc), flat(lg), by_expert(cnt)[..., 0], by_expert(off)[..., 0])


GATHER_ROWS = 256
ROW_UNROLL = 4


def _gather_kernel(cap, B, cnt_ref, off_ref, *refs):
    lc_refs = refs[:B]
    m_hbm, o_ref, x_ref, sems = refs[B:]
    e = pl.program_id(0)
    NC = lc_refs[0].shape[1] // LANES
    for b in range(B):
        def chunk_body(c, carry, b=b):
            base = off_ref[b, e, c] + b * cap
            n = cnt_ref[b, e, c]

            def start_row(r, priority):
                t = lc_refs[b][0, c * LANES + r]
                pltpu.make_async_copy(m_hbm.at[b, t], x_ref.at[base + r],
                                      sems.at[b]).start(priority=priority)

            def group_body(i, carry):
                for u in range(ROW_UNROLL):
                    start_row(i * ROW_UNROLL + u, u % 2)
                return carry

            def tail_body(r, carry):
                start_row(r, 0)
                return carry

            full = n // ROW_UNROLL
            lax.fori_loop(0, full, group_body, carry)
            return lax.fori_loop(full * ROW_UNROLL, n, tail_body, carry)

        lax.fori_loop(0, NC, chunk_body, 0)
    step = min(GATHER_ROWS, cap)
    for b in range(B):
        done = x_ref.at[pl.ds(b * cap, cap)]
        pltpu.make_async_copy(done, done, sems.at[b]).wait()
        for r0 in range(b * cap, (b + 1) * cap, step):
            rows = slice(r0, r0 + step)
            o_ref[0, rows, :] = _untile(x_ref, rows).astype(BF16)


def _gather(m, lc, cnt, off, cap):
    B, N, sub, _ = m.shape
    E = lc.shape[1]
    lc4 = lc[:, :, None, :]

    def table(b):
        return pl.BlockSpec((None, None, 1, lc.shape[2]), lambda e, *_: (b, e, 0, 0),
                            memory_space=pltpu.SMEM)

    return pl.pallas_call(
        functools.partial(_gather_kernel, cap, B),
        out_shape=jax.ShapeDtypeStruct((E, B * cap, sub * LANES), BF16),
        grid_spec=pltpu.PrefetchScalarGridSpec(
            num_scalar_prefetch=2,
            grid=(E,),
            in_specs=[table(b) for b in range(B)] + [pl.BlockSpec(memory_space=pl.ANY)],
            out_specs=pl.BlockSpec((1, B * cap, sub * LANES), lambda e, *_: (e, 0, 0)),
            scratch_shapes=[pltpu.VMEM((B * cap, sub, LANES), m.dtype),
                            pltpu.SemaphoreType.DMA((B,))]),
        compiler_params=_params(("arbitrary",), VMEM_LIMIT),
        name="gather",
    )(cnt, off, *([lc4] * B), m)


def _ffn_kernel(n_parts, chunk_rows, *refs):
    wg_ref, wu_ref, wd_ref = refs[:3]
    x_refs = refs[3:3 + n_parts]
    o_refs = refs[3 + n_parts:3 + 2 * n_parts]
    acc_refs = refs[3 + 2 * n_parts:]
    j = pl.program_id(1)
    last = pl.num_programs(1) - 1
    wg = wg_ref[0].astype(BF16)
    wu = wu_ref[0].astype(BF16)
    wd = wd_ref[0].astype(BF16)
    chunks = [(x_ref, o_ref, acc_ref, slice(r0, r0 + nrows))
              for x_ref, o_ref, acc_ref, nrows in zip(x_refs, o_refs, acc_refs, chunk_rows)
              for r0 in range(0, x_ref.shape[1], nrows)]

    @pl.when(j == 0)
    def _():
        for acc_ref in acc_refs:
            acc_ref[...] = jnp.zeros_like(acc_ref)

    for x_ref, _, acc_ref, rows in chunks:
        x = x_ref[0, rows, :]
        a = jnp.dot(x, wg, preferred_element_type=F32)
        u = jnp.dot(x, wu, preferred_element_type=F32)
        hh = (a * jax.nn.sigmoid(a) * u).astype(BF16)
        acc_ref[rows, :] += jnp.dot(hh, wd, preferred_element_type=F32)

    @pl.when(j == last)
    def _():
        for _, o_ref, acc_ref, rows in chunks:
            _store_tiled(o_ref.at[0], rows, acc_ref[rows, :])


def _ffn(xs, layer, w_gate, w_up, w_down, chunk_rows):
    _, E, D, F = w_gate.shape
    tf = 512
    assert F // tf >= 2
    xspec = lambda x: pl.BlockSpec((1, x.shape[1], D), lambda e, j: (e, 0, 0))
    ospec = lambda x: pl.BlockSpec((1, x.shape[1], SUB, LANES), lambda e, j: (e, 0, 0, 0))
    outs = pl.pallas_call(
        functools.partial(_ffn_kernel, len(xs), chunk_rows),
        out_shape=tuple(jax.ShapeDtypeStruct((E, x.shape[1], SUB, LANES), F32) for x in xs),
        grid=(E, F // tf),
        in_specs=[pl.BlockSpec((None, 1, D, tf), lambda e, j: (layer, e, 0, j)),
                  pl.BlockSpec((None, 1, D, tf), lambda e, j: (layer, e, 0, j)),
                  pl.BlockSpec((None, 1, tf, D), lambda e, j: (layer, e, j, 0))]
        + [xspec(x) for x in xs],
        out_specs=tuple(ospec(x) for x in xs),
        scratch_shapes=[pltpu.VMEM((x.shape[1], D), F32) for x in xs],
        compiler_params=_params(("arbitrary", "arbitrary"), VMEM_LIMIT),
        name="ffn",
    )(w_gate, w_up, w_down, *xs)
    return outs


def _combine_kernel(cnt_ref, off_ref, lc_ref, lg_ref, y_ref, o_ref):
    b = pl.program_id(0)
    hf = pl.program_id(1)
    e = pl.program_id(2)
    nch = o_ref.shape[1] // LANES
    first = hf * o_ref.shape[1]

    @pl.when(e == 0)
    def _():
        o_ref[...] = jnp.zeros_like(o_ref)

    def chunk_body(ci, carry):
        c = hf * nch + ci
        base = off_ref[b, e, c]

        n = cnt_ref[b, e, c]

        def rows_body(r0, width):
            ts = [lc_ref[0, c * LANES + r0 + u] - first for u in range(width)]
            vals = [o_ref[0, ts[u]] + lg_ref[0, c * LANES + r0 + u] * y_ref[0, base + r0 + u]
                    for u in range(width)]
            for u in range(width):
                o_ref[0, ts[u]] = vals[u]

        def group_body(i, carry):
            rows_body(i * ROW_UNROLL, ROW_UNROLL)
            return carry

        def tail_body(r, carry):
            rows_body(r, 1)
            return carry

        full = n // ROW_UNROLL
        lax.fori_loop(0, full, group_body, carry)
        return lax.fori_loop(full * ROW_UNROLL, n, tail_body, carry)

    lax.fori_loop(0, nch, chunk_body, 0)


def _combine(y, lc, lg, cnt, off, n_tokens, cap, n_split):
    E = y.shape[0]
    B = lc.shape[0]
    nh = n_tokens // n_split
    smem = lambda: pl.BlockSpec((None, None, 1, lc.shape[2]), lambda b, h, e, *_: (b, e, 0, 0),
                                memory_space=pltpu.SMEM)
    return pl.pallas_call(
        _combine_kernel,
        out_shape=jax.ShapeDtypeStruct((B, n_tokens, SUB, LANES), F32),
        grid_spec=pltpu.PrefetchScalarGridSpec(
            num_scalar_prefetch=2,
            grid=(B, n_split, E),
            in_specs=[smem(), smem(),
                      pl.BlockSpec((1, cap, SUB, LANES), lambda b, h, e, *_: (e, b, 0, 0))],
            out_specs=pl.BlockSpec((1, nh, SUB, LANES), lambda b, h, e, *_: (b, h, 0, 0))),
        compiler_params=_params(("arbitrary", "arbitrary", "arbitrary"), VMEM_LIMIT),
        name="combine",
    )(cnt, off, lc[:, :, None, :], lg[:, :, None, :], y)


def _final_kernel(h_ref, moe_ref, mod_ref, g_ref, o_ref):
    D = D_MODEL
    h = h_ref[0] + mod_ref[0][:, 5 * D:6 * D] * _untile(moe_ref.at[0], slice(None))
    o_ref[0] = _rms(h, g_ref[...])


def _final(h, moe, mod, g, tm):
    B, R, D = h.shape
    row = pl.BlockSpec((1, tm, D), lambda b, i: (b, i, 0))
    return pl.pallas_call(
        _final_kernel,
        out_shape=jax.ShapeDtypeStruct((B, R, D), F32),
        grid=(B, R // tm),
        in_specs=[row, pl.BlockSpec((1, tm, SUB, LANES), lambda b, i: (b, i, 0, 0)),
                  pl.BlockSpec((1, 1, N_MOD * D), lambda b, i: (b, 0, 0)),
                  pl.BlockSpec((1, D), lambda b, i: (0, 0))],
        out_specs=row,
        compiler_params=_params(("arbitrary", "arbitrary"), VMEM_LIMIT),
        name="final",
    )(h, moe, mod, g)


def _rope_tables(n_lat, n_ctx, segments):
    t = jnp.arange(n_lat)
    pos = {"row": (t // GRID_W).astype(F32), "col": (t % GRID_W).astype(F32)}
    freq = {"row": np.zeros((2, LANES), np.float32), "col": np.zeros((2, LANES), np.float32)}
    first = np.zeros(LANES, np.float32)
    second = np.zeros(LANES, np.float32)
    for lane0, width, which in segments:
        half = width // 2
        idx = np.arange(half, dtype=np.float32)
        for lo, mask in ((lane0, first), (lane0 + half, second)):
            freq[which][0, lo:lo + half] = idx / half
            freq[which][1, lo:lo + half] = 1.0
            mask[lo:lo + half] = 1.0
    ang = jnp.zeros((n_lat, LANES), F32)
    for which in ("row", "col"):
        inv = (ROPE_THETA ** (-jnp.asarray(freq[which][0]))) * jnp.asarray(freq[which][1])
        ang = ang + pos[which][:, None] * inv[None, :]
    sin = jnp.sin(ang)
    lat = (jnp.cos(ang), -sin * first[None, :], sin * second[None, :])
    ctx = (jnp.ones((n_ctx, LANES), F32), jnp.zeros((n_ctx, LANES), F32), jnp.zeros((n_ctx, LANES), F32))
    return lat, ctx


def _head_slots(w, n_heads, width, lo, hi):
    k = w.shape[0]
    w3 = w.reshape(k, n_heads, width)[:, :, lo:hi]
    return jnp.pad(w3, ((0, 0), (0, 0), (0, LANES - (hi - lo)))).reshape(k, n_heads * LANES)


def _moe(m_l, aff_l, m_c, aff_c, layer, w_gate, w_up, w_down):
    B, S = m_l.shape[:2]
    cap_l = CAPACITY_FACTOR * S // N_EXPERTS
    lc, lg, cnt, off = _topk(aff_l, cap_l)
    xs = [_gather(m_l, lc, cnt, off, cap_l)]
    chunk_rows = [cap_l]
    if m_c is not None:
        L = m_c.shape[1]
        cap_c = CAPACITY_FACTOR * L // N_EXPERTS
        lcc, lgc, cntc, offc = _topk(aff_c, cap_c)
        xs.append(_gather(m_c, lcc, cntc, offc, cap_c))
        chunk_rows.append(B * cap_c)
    ys = _ffn(xs, layer, w_gate, w_up, w_down, tuple(chunk_rows))
    out_l = _combine(ys[0], lc, lg, cnt, off, S, cap_l, 2)
    out_c = None
    if m_c is not None:
        out_c = _combine(ys[1], lcc, lgc, cntc, offc, L, cap_c, 1)
    return out_l, out_c


def kernel(x, c, ctx, c_ctx, ada_w, ada_b, norm1, norm2, ab_w_in, ab_q_norm, ab_w_uq, ab_kv_norm,
           ab_w_ukv, ab_w_o, c_w_in, c_q_gain, c_k_gain, c_w_o, moe_router, moe_w_gate, moe_w_up,
           moe_w_down, final_norm):
    B, S, D = x.shape
    L = ctx.shape[1]
    depth = ada_w.shape[0]
    tm_l, tm_c = min(S, 1024), L
    row2 = lambda v: v.reshape(1, -1)

    c8 = jnp.zeros((8, D), F32).at[:B].set(c).at[B].set(c_ctx)
    mod = _ada(c8, ada_w, ada_b)
    mod_l = [mod[i, :B][:, None, :] for i in range(depth)]
    mod_c = [jnp.broadcast_to(mod[i, B][None, None, :], (B, 1, N_MOD * D)) for i in range(depth)]

    h_l, h_c = x, ctx
    moe_l = moe_c = None
    for i in range(depth):
        last = i == depth - 1
        j = i // 2
        wr_t = _split_const(moe_router[i].T)
        n2 = row2(norm2[i])
        if i % 2 == 0:
            assert moe_l is None
            tabs_l, tabs_c = _rope_tables(S, L, ((MLA_NOPE, MLA_ROPE // 2, "row"),
                                                 (MLA_NOPE + MLA_ROPE // 2, MLA_ROPE // 2, "col")))
            w_ukv = ab_w_ukv[j]
            ekr = np.zeros((MLA_ROPE, MLA_HEADS * LANES), np.float32)
            for hd in range(MLA_HEADS):
                ekr[np.arange(MLA_ROPE), hd * LANES + MLA_NOPE + np.arange(MLA_ROPE)] = 1.0
            w = {
                "win": ab_w_in[j].astype(BF16),
                "qn": row2(ab_q_norm[j]),
                "wuqt": _head_slots(ab_w_uq[j], MLA_HEADS, MLA_QK, 0, MLA_QK).T.astype(BF16),
                "kvn": row2(ab_kv_norm[j]),
                "wuk": _head_slots(w_ukv, MLA_HEADS, MLA_NOPE + MLA_V, 0, MLA_NOPE).astype(BF16),
                "wuvt": _head_slots(w_ukv, MLA_HEADS, MLA_NOPE + MLA_V, MLA_NOPE,
                                    MLA_NOPE + MLA_V).T.astype(BF16),
                "ekr": jnp.asarray(ekr, BF16),
            }
            n1 = row2(norm1[i])
            f_l, q_l, k_l, vt_l = _in_ab(h_l, mod_l[i], n1, w, tabs_l, tm_l)
            f_c, q_c, k_c, vt_c = _in_ab(h_c, mod_c[i], n1, w, tabs_c, tm_c)
            o_l = _attention(q_l[:, :, None], k_c, vt_c, k_l, vt_l, min(S, ATTN_QUERIES), ATTN_KEYS,
                             MLA_VROWS, MLA_V)
            yf_l = _fourier_lat(f_l)
            wof = ab_w_o[j][:FNET_WIDTH].astype(BF16)
            woa = jnp.pad(ab_w_o[j][FNET_WIDTH:].reshape(MLA_HEADS, MLA_V, D),
                          ((0, 0), (0, LANES - MLA_V), (0, 0))).reshape(MLA_HEADS * LANES, D).astype(BF16)
            h_l, m_l, aff_l = _out_proj(h_l, mod_l[i], yf_l, wof, o_l, woa, n2, wr_t, tm_l)
            m_c = aff_c = None
            if not last:
                o_c = _attention(q_c[:, :, None], k_c, vt_c, None, None, L, L, MLA_VROWS, MLA_V)
                yf_c = _fourier_ctx(f_c)
                h_c, m_c, aff_c = _out_proj(h_c, mod_c[i], yf_c, wof, o_c, woa, n2, wr_t, tm_c)
        else:
            tabs_l, tabs_c = _rope_tables(S, L, ((0, GQA_HEAD_DIM // 2, "row"),
                                                 (GQA_HEAD_DIM // 2, GQA_HEAD_DIM // 2, "col")))
            nq = GQA_HEADS * GQA_HEAD_DIM
            nqk = nq + GQA_KV_HEADS * GQA_HEAD_DIM
            w = {"wqt": c_w_in[j][:, :nq].T.astype(BF16), "wk": c_w_in[j][:, nq:nqk].astype(BF16),
                 "wvt": c_w_in[j][:, nqk:].T.astype(BF16),
                 "qg": c_q_gain[j].reshape(-1, 1), "kg": row2(c_k_gain[j])}
            n1 = row2(norm1[i])
            h_l, q_l, k_l, vt_l = _in_c(h_l, moe_l, mod_l[i - 1], mod_l[i], n1, w, tabs_l,
                                        min(tm_l, 512))
            h_c, q_c, k_c, vt_c = _in_c(h_c, moe_c, mod_c[i - 1], mod_c[i], n1, w, tabs_c, tm_c)
            grp = lambda q: q.reshape(B, GQA_KV_HEADS, GQA_GROUP, LANES, q.shape[3])
            o_l = _attention(grp(q_l), k_c, vt_c, k_l, vt_l, min(S, ATTN_QUERIES // GQA_GROUP),
                             ATTN_KEYS, GQA_HEAD_DIM)
            woa = c_w_o[j].astype(BF16)
            h_l, m_l, aff_l = _out_proj(h_l, mod_l[i], None, None, o_l, woa, n2, wr_t, tm_l)
            m_c = aff_c = None
            if not last:
                o_c = _attention(grp(q_c), k_c, vt_c, None, None, L, L, GQA_HEAD_DIM)
                h_c, m_c, aff_c = _out_proj(h_c, mod_c[i], None, None, o_c, woa, n2, wr_t, tm_c)
        moe_l, moe_c = _moe(m_l, aff_l, m_c, aff_c, i, moe_w_gate, moe_w_up, moe_w_down)
    return _final(h_l, moe_l, mod_l[depth - 1], row2(final_norm), tm_l)
```

```python
import functools
import math

import jax
import jax.numpy as jnp
import numpy as np
from jax import lax
from jax.experimental import pallas as pl
from jax.experimental.pallas import tpu as pltpu

F32 = jnp.float32
BF16 = jnp.bfloat16
I32 = jnp.int32
HIGHEST = lax.Precision.HIGHEST

D_MODEL = 1024
GRID_W = 64
EPS = 1e-6
ROPE_THETA = 10000.0
N_MOD = 6
FNET_GROUPS = 4
FNET_GROUP_DIM = 128
FNET_WIDTH = FNET_GROUPS * FNET_GROUP_DIM
MLA_HEADS = 8
MLA_Q_LORA = 256
MLA_KV_LORA = 128
MLA_NOPE = 64
MLA_ROPE = 32
MLA_V = 64
MLA_QK = MLA_NOPE + MLA_ROPE
GQA_HEADS = 8
GQA_KV_HEADS = 2
GQA_GROUP = GQA_HEADS // GQA_KV_HEADS
GQA_HEAD_DIM = 128
N_EXPERTS = 16
EXPERT_FF = 2048
CAPACITY_FACTOR = 2

LANES = 128
MLA_VROWS = MLA_V + 16
SUB = D_MODEL // LANES
VMEM_LIMIT = 56 * 1024 * 1024
LOG2E = math.log2(math.e)
NT = (((1,), (1,)), ((), ()))


def _params(sem, vmem=None):
    return pltpu.CompilerParams(dimension_semantics=sem, vmem_limit_bytes=vmem)


def _rms(x, g):
    return x * lax.rsqrt(jnp.mean(x * x, axis=-1, keepdims=True) + EPS) * g


def _modulate(h, g, shift, scale):
    return _rms(h, g) * (1.0 + scale) + shift


def _rope(x, cos, sin_fwd, sin_bwd, shift):
    return (x * cos + pltpu.roll(x, LANES - shift, 1) * sin_fwd
            + pltpu.roll(x, shift, 1) * sin_bwd)


def _rope_t(x, cos, sin_fwd, sin_bwd, shift):
    return (x * cos + pltpu.roll(x, LANES - shift, 0) * sin_fwd
            + pltpu.roll(x, shift, 0) * sin_bwd)


def _untile(ref, rows):
    x = ref[rows]
    n = x.shape[0]
    y = pltpu.einshape("grsl->gsrl", x.reshape(n // 8, 8, SUB, LANES))
    return jnp.concatenate([y[:, s].reshape(n, LANES) for s in range(SUB)], axis=1)


def _store_tiled(ref, rows, x):
    n = x.shape[0]
    y = jnp.concatenate([x[:, s * LANES:(s + 1) * LANES].reshape(n // 8, 1, 8, LANES)
                         for s in range(SUB)], axis=1)
    ref[rows] = pltpu.einshape("gsrl->grsl", y).reshape(n, SUB, LANES)


def _ada_kernel(c_ref, w_ref, b_ref, o_ref):
    c = c_ref[...]
    x = c * jax.nn.sigmoid(c)
    o_ref[0] = jnp.dot(x, w_ref[0], precision=HIGHEST, preferred_element_type=F32) + b_ref[0]


def _ada(c8, ada_w, ada_b):
    depth, d, n = ada_w.shape
    tn = 1536
    return pl.pallas_call(
        _ada_kernel,
        out_shape=jax.ShapeDtypeStruct((depth, 8, n), F32),
        grid=(depth, n // tn),
        in_specs=[pl.BlockSpec((8, d), lambda l, j: (0, 0)),
                  pl.BlockSpec((1, d, tn), lambda l, j: (l, 0, j)),
                  pl.BlockSpec((1, 1, tn), lambda l, j: (l, 0, j))],
        out_specs=pl.BlockSpec((1, 8, tn), lambda l, j: (l, 0, j)),
        compiler_params=_params(("arbitrary", "arbitrary"), VMEM_LIMIT),
        name="ada",
    )(c8, ada_w, ada_b.reshape(depth, 1, n))


def _in_ab_kernel(scale, *refs):
    (h_ref, mod_ref, n1_ref, win_ref, qn_ref, wuqt_ref, kvn_ref, wuk_ref, wuvt_ref, ekr_ref,
     cos_ref, sf_ref, sb_ref, cost_ref, sft_ref, sbt_ref, f_ref, qt_ref, k_ref, vt_ref) = refs
    D = D_MODEL
    h = h_ref[0]
    mod = mod_ref[0]
    a = _modulate(h, n1_ref[...], mod[:, 0:D], mod[:, D:2 * D]).astype(BF16)
    p = jnp.dot(a, win_ref[...], preferred_element_type=F32)
    o = FNET_WIDTH
    f_ref[0] = p[:, :o]
    cq = p[:, o:o + MLA_Q_LORA]
    o += MLA_Q_LORA
    ckv = p[:, o:o + MLA_KV_LORA]
    o += MLA_KV_LORA
    kr = p[:, o:o + MLA_ROPE]
    cqn = _rms(cq, qn_ref[...]).astype(BF16)
    ckvn = _rms(ckv, kvn_ref[...]).astype(BF16)
    qt = lax.dot_general(wuqt_ref[...], cqn, NT, preferred_element_type=F32)
    k = (jnp.dot(ckvn, wuk_ref[...], preferred_element_type=F32)
         + jnp.dot(kr.astype(BF16), ekr_ref[...], preferred_element_type=F32))
    vt = lax.dot_general(wuvt_ref[...], ckvn, NT, preferred_element_type=F32)
    cos, sf, sb = cos_ref[...], sf_ref[...], sb_ref[...]
    cost, sft, sbt = cost_ref[...], sft_ref[...], sbt_ref[...]
    shift = MLA_ROPE // 4
    ones = lax.broadcasted_iota(I32, (LANES, vt.shape[1]), 0) == MLA_V
    for hd in range(MLA_HEADS):
        sl = slice(hd * LANES, (hd + 1) * LANES)
        qt_ref[0, hd] = (_rope_t(qt[sl, :], cost, sft, sbt, shift) * scale).astype(BF16)
        k_ref[0, hd] = _rope(k[:, sl], cos, sf, sb, shift).astype(BF16)
        vt_ref[0, hd] = jnp.where(ones, 1.0, vt[sl, :]).astype(BF16)


def _in_ab(h, mod, n1, w, tables, tm):
    B, R, D = h.shape
    H = MLA_HEADS
    full = lambda a: pl.BlockSpec(a.shape, lambda b, i: (0,) * a.ndim)
    row = lambda w_: pl.BlockSpec((1, tm, w_), lambda b, i: (b, i, 0))
    tab = pl.BlockSpec((tm, LANES), lambda b, i: (i, 0))
    hd = pl.BlockSpec((1, H, tm, LANES), lambda b, i: (b, 0, i, 0))
    hdt = pl.BlockSpec((1, H, LANES, tm), lambda b, i: (b, 0, 0, i))
    tabt = pl.BlockSpec((LANES, tm), lambda b, i: (0, i))
    weights = [n1, w["win"], w["qn"], w["wuqt"], w["kvn"], w["wuk"], w["wuvt"], w["ekr"]]
    return pl.pallas_call(
        functools.partial(_in_ab_kernel, MLA_QK ** -0.5 * LOG2E),
        out_shape=(jax.ShapeDtypeStruct((B, R, FNET_WIDTH), F32),
                   jax.ShapeDtypeStruct((B, H, LANES, R), BF16),
                   jax.ShapeDtypeStruct((B, H, R, LANES), BF16),
                   jax.ShapeDtypeStruct((B, H, LANES, R), BF16)),
        grid=(B, R // tm),
        in_specs=[row(D), pl.BlockSpec((1, 1, N_MOD * D), lambda b, i: (b, 0, 0))]
        + [full(a) for a in weights] + [tab, tab, tab, tabt, tabt, tabt],
        out_specs=(row(FNET_WIDTH), hdt, hd, hdt),
        compiler_params=_params(("arbitrary", "arbitrary"), VMEM_LIMIT),
        name="in_ab",
    )(h, mod, *weights, *tables, *(t.T for t in tables))


def _in_c_kernel(scale, *refs):
    (h_ref, moe_ref, modp_ref, mod_ref, n1_ref, wqt_ref, wk_ref, wvt_ref, qg_ref, kg_ref,
     cos_ref, sf_ref, sb_ref, cost_ref, sft_ref, sbt_ref, h2_ref, qt_ref, k_ref, vt_ref) = refs
    D = D_MODEL
    h = h_ref[0] + modp_ref[0][:, 5 * D:6 * D] * _untile(moe_ref.at[0], slice(None))
    h2_ref[0] = h
    mod = mod_ref[0]
    a = _modulate(h, n1_ref[...], mod[:, 0:D], mod[:, D:2 * D]).astype(BF16)
    qt = lax.dot_general(wqt_ref[...], a, NT, preferred_element_type=F32)
    p = jnp.dot(a, wk_ref[...], preferred_element_type=F32)
    vt = lax.dot_general(wvt_ref[...], a, NT, preferred_element_type=F32)
    cos, sf, sb = cos_ref[...], sf_ref[...], sb_ref[...]
    cost, sft, sbt = cost_ref[...], sft_ref[...], sbt_ref[...]
    shift = GQA_HEAD_DIM // 4
    for hd in range(GQA_HEADS):
        x = qt[hd * LANES:(hd + 1) * LANES, :]
        x = x * lax.rsqrt(jnp.mean(x * x, axis=0, keepdims=True) + EPS) * qg_ref[...]
        qt_ref[0, hd] = (_rope_t(x, cost, sft, sbt, shift) * scale).astype(BF16)
    for hd in range(GQA_KV_HEADS):
        x = _rms(p[:, hd * LANES:(hd + 1) * LANES], kg_ref[...])
        k_ref[0, hd] = _rope(x, cos, sf, sb, shift).astype(BF16)
        vt_ref[0, hd] = vt[hd * LANES:(hd + 1) * LANES, :].astype(BF16)


def _in_c(h, moe, mod_prev, mod, n1, w, tables, tm):
    B, R, D = h.shape
    full = lambda a: pl.BlockSpec(a.shape, lambda b, i: (0,) * a.ndim)
    row = lambda w_: pl.BlockSpec((1, tm, w_), lambda b, i: (b, i, 0))
    tiled = pl.BlockSpec((1, tm, SUB, LANES), lambda b, i: (b, i, 0, 0))
    modspec = pl.BlockSpec((1, 1, N_MOD * D), lambda b, i: (b, 0, 0))
    tab = pl.BlockSpec((tm, LANES), lambda b, i: (i, 0))
    hd = lambda n: pl.BlockSpec((1, n, tm, LANES), lambda b, i: (b, 0, i, 0))
    hdt = pl.BlockSpec((1, GQA_KV_HEADS, LANES, tm), lambda b, i: (b, 0, 0, i))
    tabt = pl.BlockSpec((LANES, tm), lambda b, i: (0, i))
    hdq = pl.BlockSpec((1, GQA_HEADS, LANES, tm), lambda b, i: (b, 0, 0, i))
    weights = [n1, w["wqt"], w["wk"], w["wvt"], w["qg"], w["kg"]]
    return pl.pallas_call(
        functools.partial(_in_c_kernel, GQA_HEAD_DIM ** -0.5 * LOG2E),
        out_shape=(jax.ShapeDtypeStruct((B, R, D), F32),
                   jax.ShapeDtypeStruct((B, GQA_HEADS, LANES, R), BF16),
                   jax.ShapeDtypeStruct((B, GQA_KV_HEADS, R, LANES), BF16),
                   jax.ShapeDtypeStruct((B, GQA_KV_HEADS, LANES, R), BF16)),
        grid=(B, R // tm),
        in_specs=[row(D), tiled, modspec, modspec] + [full(a) for a in weights]
        + [tab, tab, tab, tabt, tabt, tabt],
        out_specs=(row(D), hdq, hd(GQA_KV_HEADS), hdt),
        compiler_params=_params(("arbitrary", "arbitrary"), VMEM_LIMIT),
        name="in_c",
    )(h, moe, mod_prev, mod, *weights, *tables, *(t.T for t in tables))


ATTN_QUERIES = 4096
ATTN_KEYS = 512


def _attn_kernel(n_lat, tk, dv, ones_row, *refs):
    if n_lat:
        q_ref, kc_ref, vct_ref, kl_ref, vlt_ref, o_ref, acc_ref, s0, s1, p0, p1 = refs
        s_bufs, p_bufs = (s0, s1), (p0, p1)
    else:
        q_ref, kc_ref, vct_ref, o_ref, acc_ref = refs
    G, tq = q_ref.shape[2], q_ref.shape[4]
    q = jnp.concatenate([q_ref[0, 0, g] for g in range(G)], axis=1)

    def scores(k):
        return jnp.dot(k, q, preferred_element_type=F32)

    s = scores(kc_ref[0, 0])
    m = jnp.max(s, axis=0, keepdims=True)
    p = jnp.exp2(s - m)
    l = jnp.sum(p, axis=0, keepdims=True) if ones_row is None else jnp.zeros_like(m)
    if dv < LANES:
        acc_ref[...] = jnp.zeros_like(acc_ref)
    acc_ref[:dv, :] = jnp.dot(vct_ref[0, 0, :dv, :], p.astype(BF16), preferred_element_type=F32)

    if n_lat:
        assert n_lat == 1 or n_lat % 2 == 0

        def chunk(c):
            return pl.ds(pl.multiple_of(c * tk, tk), tk)

        def score_stage(c, slot):
            s_bufs[slot][...] = scores(kl_ref[0, 0, chunk(c), :])

        def softmax_stage(slot, m, l):
            s = s_bufs[slot][...]
            m_new = jnp.maximum(m, jnp.max(s, axis=0, keepdims=True))
            p = jnp.exp2(s - m_new)
            p_bufs[slot][...] = p.astype(BF16)
            alpha = jnp.exp2(m - m_new)
            if ones_row is None:
                l = alpha * l + jnp.sum(p, axis=0, keepdims=True)
            return m_new, l, alpha

        def value_stage(c, slot, alpha):
            pv = jnp.dot(vlt_ref[0, 0, :dv, chunk(c)], p_bufs[slot][...],
                         preferred_element_type=F32)
            acc_ref[:dv, :] = alpha * acc_ref[:dv, :] + pv

        score_stage(0, 0)
        m, l, alpha = softmax_stage(0, m, l)
        if n_lat > 1:
            score_stage(1, 1)

            def body(i, carry):
                m, l, alpha = carry
                for slot in (0, 1):
                    c = 2 * i + slot
                    score_stage(c + 2, slot)
                    value_stage(c, slot, alpha)
                    m, l, alpha = softmax_stage(1 - slot, m, l)
                return m, l, alpha

            m, l, alpha = lax.fori_loop(0, (n_lat - 2) // 2, body, (m, l, alpha))
            value_stage(n_lat - 2, 0, alpha)
            m, l, alpha = softmax_stage(1, m, l)
        value_stage(n_lat - 1, (n_lat - 1) % 2, alpha)
    if ones_row is not None:
        l = acc_ref[ones_row:ones_row + 1, :]
    o_ref[0, 0] = (acc_ref[...] / l).T.reshape(G, tq, LANES).astype(BF16)


def _attention(q, kc, vct, kl, vlt, tq, tk, dv, ones_row=None):
    B, Hk, G, _, R = q.shape
    Lc = kc.shape[2]
    n_lat = 0 if kl is None else kl.shape[2] // tk
    qspec = pl.BlockSpec((1, 1, G, LANES, tq), lambda b, h, i: (b, h, 0, 0, i))
    ospec = pl.BlockSpec((1, 1, G, tq, LANES), lambda b, h, i: (b, h, 0, i, 0))
    kspec = lambda n: pl.BlockSpec((1, 1, n, LANES), lambda b, h, i: (b, h, 0, 0))
    vspec = lambda n: pl.BlockSpec((1, 1, LANES, n), lambda b, h, i: (b, h, 0, 0))
    ins = [q, kc, vct] + ([kl, vlt] if n_lat else [])
    specs = [qspec, kspec(Lc), vspec(Lc)] + ([kspec(kl.shape[2]), vspec(kl.shape[2])] if n_lat else [])
    return pl.pallas_call(
        functools.partial(_attn_kernel, n_lat, tk, dv, ones_row),
        out_shape=jax.ShapeDtypeStruct((B, Hk, G, R, LANES), BF16),
        grid=(B, Hk, R // tq),
        in_specs=specs,
        out_specs=ospec,
        scratch_shapes=[pltpu.VMEM((LANES, G * tq), F32)]
        + ([pltpu.VMEM((tk, G * tq), F32)] * 2 + [pltpu.VMEM((tk, G * tq), BF16)] * 2 if n_lat else []),
        compiler_params=_params(("arbitrary", "arbitrary", "arbitrary"), VMEM_LIMIT),
        name="attn",
    )(*ins)


def _dft_mats(n):
    k = np.arange(n, dtype=np.float64)
    ang = 2.0 * np.pi * np.outer(k, k) / n
    return np.cos(ang), np.sin(ang)


def _split(x):
    hi = x.astype(BF16)
    return hi, (x - hi.astype(F32)).astype(BF16)


def _dot3(a, b):
    d = lambda p, q: jnp.dot(p, q, preferred_element_type=F32)
    return d(a[0], b[0]) + d(a[0], b[1]) + d(a[1], b[0])


def _split_const(m):
    return jnp.stack(_split(jnp.asarray(m, F32)))


def _dft1_kernel(x_ref, m1_ref, tw_ref, o_ref):
    s1, ns2 = x_ref.shape[1], x_ref.shape[2]
    for i in range(ns2):
        a = jnp.dot(m1_ref[...], x_ref[0, :, i, :], precision=HIGHEST, preferred_element_type=F32)
        are, aim = a[:s1], a[s1:]
        tre = jnp.tile(tw_ref[0, :, i * LANES:(i + 1) * LANES], (1, FNET_GROUPS))
        tim = jnp.tile(tw_ref[1, :, i * LANES:(i + 1) * LANES], (1, FNET_GROUPS))
        o_ref[0, 0, :, i, :] = are * tre - aim * tim
        o_ref[0, 1, :, i, :] = are * tim + aim * tre


def _dft2_kernel(a_ref, m2_ref, m3_ref, o_ref):
    kb, n2 = a_ref.shape[2], a_ref.shape[3]
    for j in range(kb):
        rhs = jnp.concatenate([a_ref[0, 0, j], a_ref[0, 1, j]], axis=0)
        y = _dot3((m2_ref[0], m2_ref[1]), _split(rhs))
        for g in range(FNET_GROUPS):
            sl = slice(g * LANES, (g + 1) * LANES)
            lhs = jnp.concatenate([y[:n2, sl], y[n2:, sl]], axis=1)
            o_ref[0, :, j, sl] = _dot3(_split(lhs), (m3_ref[0], m3_ref[1]))


def _fourier_lat(f):
    B, S, W = f.shape
    n2 = LANES
    s1 = S // n2
    c1, sn1 = _dft_mats(s1)
    c2, sn2 = _dft_mats(n2)
    cc, sc = _dft_mats(FNET_GROUP_DIM)
    m1 = jnp.asarray(np.concatenate([c1, -sn1], axis=0), F32)
    m2 = _split_const(np.block([[c2, sn2], [-sn2, c2]]))
    norm = 1.0 / math.sqrt(S * FNET_GROUP_DIM)
    m3 = _split_const(np.concatenate([cc, sc], axis=0) * norm)
    ang = 2.0 * np.pi * np.outer(np.arange(s1), np.arange(n2)) / S
    tw = np.stack([np.cos(ang), -np.sin(ang)])
    tw = jnp.asarray(np.repeat(tw[:, :, :, None], LANES, axis=3).reshape(2, s1, n2 * LANES), F32)

    ns2 = 8
    a = pl.pallas_call(
        _dft1_kernel,
        out_shape=jax.ShapeDtypeStruct((B, 2, s1, n2, W), F32),
        grid=(B, n2 // ns2),
        in_specs=[pl.BlockSpec((1, s1, ns2, W), lambda b, j: (b, 0, j, 0)),
                  pl.BlockSpec(m1.shape, lambda b, j: (0, 0)),
                  pl.BlockSpec((2, s1, ns2 * LANES), lambda b, j: (0, 0, j))],
        out_specs=pl.BlockSpec((1, 2, s1, ns2, W), lambda b, j: (b, 0, 0, j, 0)),
        compiler_params=_params(("arbitrary", "arbitrary"), VMEM_LIMIT),
        name="dft1",
    )(f.reshape(B, s1, n2, W), m1, tw)
    kb = min(8, s1)
    y = pl.pallas_call(
        _dft2_kernel,
        out_shape=jax.ShapeDtypeStruct((B, n2, s1, W), F32),
        grid=(B, s1 // kb),
        in_specs=[pl.BlockSpec((1, 2, kb, n2, W), lambda b, j: (b, 0, j, 0, 0)),
                  pl.BlockSpec(m2.shape, lambda b, j: (0, 0, 0)),
                  pl.BlockSpec(m3.shape, lambda b, j: (0, 0, 0))],
        out_specs=pl.BlockSpec((1, n2, kb, W), lambda b, j: (b, 0, j, 0)),
        compiler_params=_params(("arbitrary", "arbitrary"), VMEM_LIMIT),
        name="dft2",
    )(a, m2, m3)
    return y.reshape(B, S, W)


def _dftc_kernel(f_ref, mc_ref, m3_ref, o_ref):
    n = f_ref.shape[1]
    a = _dot3((mc_ref[0], mc_ref[1]), _split(f_ref[0]))
    for g in range(FNET_GROUPS):
        sl = slice(g * LANES, (g + 1) * LANES)
        lhs = jnp.concatenate([a[:n, sl], a[n:, sl]], axis=1)
        o_ref[0, :, sl] = _dot3(_split(lhs), (m3_ref[0], m3_ref[1]))


def _fourier_ctx(f):
    B, L, W = f.shape
    c, s = _dft_mats(L)
    cc, sc = _dft_mats(FNET_GROUP_DIM)
    mc = _split_const(np.concatenate([c, -s], axis=0))
    m3 = _split_const(np.concatenate([cc, sc], axis=0) / math.sqrt(L * FNET_GROUP_DIM))
    return pl.pallas_call(
        _dftc_kernel,
        out_shape=jax.ShapeDtypeStruct((B, L, W), F32),
        grid=(B,),
        in_specs=[pl.BlockSpec((1, L, W), lambda b: (b, 0, 0)),
                  pl.BlockSpec(mc.shape, lambda b: (0, 0, 0)),
                  pl.BlockSpec(m3.shape, lambda b: (0, 0, 0))],
        out_specs=pl.BlockSpec((1, L, W), lambda b: (b, 0, 0)),
        compiler_params=_params(("arbitrary",), VMEM_LIMIT),
        name="dftc",
    )(f, mc, m3)


def _out_kernel(has_f, *refs):
    if has_f:
        (h_ref, mod_ref, yf_ref, wof_ref, o_ref, woa_ref, n2_ref, wr_ref,
         h1_ref, m_ref, aff_ref) = refs
    else:
        (h_ref, mod_ref, o_ref, woa_ref, n2_ref, wr_ref, h1_ref, m_ref, aff_ref) = refs
    D = D_MODEL
    Hk, G = o_ref.shape[1], o_ref.shape[2]
    ocat = jnp.concatenate([o_ref[0, hk, g] for hk in range(Hk) for g in range(G)], axis=1)
    y = jnp.dot(ocat, woa_ref[...], preferred_element_type=F32)
    if has_f:
        y = y + jnp.dot(yf_ref[0].astype(BF16), wof_ref[...], preferred_element_type=F32)
    mod = mod_ref[0]
    h1 = h_ref[0] + mod[:, 2 * D:3 * D] * y
    h1_ref[0] = h1
    m = _modulate(h1, n2_ref[...], mod[:, 3 * D:4 * D], mod[:, 4 * D:5 * D])
    _store_tiled(m_ref.at[0], slice(None), m)
    m_hi, m_lo = _split(m)
    nt = lambda a, b: lax.dot_general(a, b, NT, preferred_element_type=F32)
    logit = nt(wr_ref[0], m_hi) + nt(wr_ref[0], m_lo) + nt(wr_ref[1], m_hi)
    e = jnp.exp(logit - jnp.max(logit, axis=0, keepdims=True))
    aff_ref[0] = e / jnp.sum(e, axis=0, keepdims=True)


def _out_proj(h, mod, yf, wof, o, woa, n2, wr_t, tm):
    B, R, D = h.shape
    _, Hk, G, _, _ = o.shape
    full = lambda a: pl.BlockSpec(a.shape, lambda b, i: (0,) * a.ndim)
    row = lambda w_: pl.BlockSpec((1, tm, w_), lambda b, i: (b, i, 0))
    modspec = pl.BlockSpec((1, 1, N_MOD * D), lambda b, i: (b, 0, 0))
    ospec = pl.BlockSpec((1, Hk, G, tm, LANES), lambda b, i: (b, 0, 0, i, 0))
    has_f = yf is not None
    ins = [h, mod] + ([yf, wof] if has_f else []) + [o, woa, n2, wr_t]
    specs = ([row(D), modspec] + ([row(FNET_WIDTH), full(wof)] if has_f else [])
             + [ospec, full(woa), full(n2), full(wr_t)])
    return pl.pallas_call(
        functools.partial(_out_kernel, has_f),
        out_shape=(jax.ShapeDtypeStruct((B, R, D), F32),
                   jax.ShapeDtypeStruct((B, R, SUB, LANES), F32),
                   jax.ShapeDtypeStruct((B, N_EXPERTS, R), F32)),
        grid=(B, R // tm),
        in_specs=specs,
        out_specs=(row(D), pl.BlockSpec((1, tm, SUB, LANES), lambda b, i: (b, i, 0, 0)),
                   pl.BlockSpec((1, N_EXPERTS, tm), lambda b, i: (b, 0, i))),
        compiler_params=_params(("arbitrary", "arbitrary"), VMEM_LIMIT),
        name="out_proj",
    )(*ins)


COMPACT_UNROLL = 4


def _topk_kernel(cap, aff_ref, u_ref, ones_ref, lmat_ref, lc_ref, lg_ref, cnt_ref, off_ref,
                 pos_ref, ac_ref, lct_ref, lgt_ref):
    a = aff_ref[0]
    E, N = a.shape
    NC = N // LANES
    R = NC * E
    keys = pltpu.bitcast(a, I32)

    def bit_step(i, tau):
        cand = tau | jnp.left_shift(jnp.int32(1), 30 - i)
        cnt = jnp.sum((keys >= cand).astype(I32), axis=1, keepdims=True)
        return jnp.where(cnt >= cap, cand, tau)

    tau = lax.fori_loop(0, 31, bit_step, jnp.zeros((E, 1), I32))
    gt = (keys > tau).astype(F32)
    eq = (keys == tau).astype(F32)
    need = (cap - jnp.sum(gt, axis=1, keepdims=True))

    def chunked(x):
        return jnp.concatenate([x[:, c * LANES:(c + 1) * LANES] for c in range(NC)], axis=0)

    a_c, gt_c, eq_c = chunked(a), chunked(gt), chunked(eq)
    need_c = jnp.tile(need, (NC, 1))

    def prefix(x):
        xb = x.astype(BF16)
        loc = jnp.dot(xb, u_ref[...], preferred_element_type=F32)
        tot = jnp.dot(xb, ones_ref[...], preferred_element_type=F32)
        offs = jnp.dot(lmat_ref[...], tot.astype(BF16), preferred_element_type=F32)
        return loc, tot, offs

    loc, tot, offs = prefix(eq_c)
    sel = jnp.maximum(gt_c, jnp.where(loc + offs < need_c, eq_c, 0.0))
    loc, tot, offs = prefix(sel)
    cnt_ref[0] = tot.astype(I32)
    off_ref[0] = offs.astype(I32)
    Rp = pos_ref.shape[1]
    selpos = jnp.where(sel > 0.0, loc, -1.0)
    if Rp > R:
        selpos = jnp.concatenate([selpos, jnp.full((Rp - R, LANES), -1.0, F32)], axis=0)
        a_c = jnp.concatenate([a_c, jnp.zeros((Rp - R, LANES), F32)], axis=0)
    pos_ref[...] = selpos.T
    ac_ref[...] = a_c.T
    tok = (lax.broadcasted_iota(I32, (LANES, Rp), 0)
           + lax.broadcasted_iota(I32, (LANES, Rp), 1) // E * LANES).astype(F32)

    def compact(i, carry):
        p = pos_ref[...]
        a_t = ac_ref[...]
        for u in range(COMPACT_UNROLL):
            j = i * COMPACT_UNROLL + u
            hit = p == lax.convert_element_type(j, F32)
            lct_ref[pl.ds(j, 1), :] = jnp.sum(jnp.where(hit, tok, 0.0), axis=0, keepdims=True)
            lgt_ref[pl.ds(j, 1), :] = jnp.sum(jnp.where(hit, a_t, 0.0), axis=0, keepdims=True)
        return carry

    lax.fori_loop(0, LANES // COMPACT_UNROLL, compact, 0)
    lc_ref[0] = lct_ref[...].T[:R].astype(I32)
    lg_ref[0] = lgt_ref[...].T[:R]


def _topk(aff, cap):
    B, E, N = aff.shape
    NC = N // LANES
    R = NC * E
    i = np.arange(LANES)
    u = jnp.asarray(i[:, None] < i[None, :], BF16)
    ones = jnp.ones((LANES, LANES), BF16)
    r = np.arange(R)
    lmat = jnp.asarray((r[:, None] % E == r[None, :] % E) & (r[None, :] // E < r[:, None] // E), BF16)
    full = lambda a: pl.BlockSpec(a.shape, lambda b: (0,) * a.ndim)
    ospec = pl.BlockSpec((1, R, LANES), lambda b: (b, 0, 0))
    sds = lambda dt: jax.ShapeDtypeStruct((B, R, LANES), dt)
    lc, lg, cnt, off = pl.pallas_call(
        functools.partial(_topk_kernel, cap),
        out_shape=(sds(I32), sds(F32), sds(I32), sds(I32)),
        grid=(B,),
        in_specs=[pl.BlockSpec((1, E, N), lambda b: (b, 0, 0)), full(u), full(ones), full(lmat)],
        out_specs=(ospec, ospec, ospec, ospec),
        scratch_shapes=[pltpu.VMEM((LANES, -(-R // LANES) * LANES), F32)] * 4,
        compiler_params=_params(("arbitrary",), VMEM_LIMIT),
        name="topk",
    )(aff, u, ones, lmat)
    by_expert = lambda x: x.reshape(B, NC, E, LANES).transpose(0, 2, 1, 3)
    flat = lambda x: by_expert(x).reshape(B, E, NC * LANES)
    return (flat(lc), flat(lg), by_expert(cnt)[..., 0], by_expert(off)[..., 0])


GATHER_ROWS = 256
ROW_UNROLL = 4


def _gather_kernel(cap, B, cnt_ref, off_ref, *refs):
    lc_refs = refs[:B]
    m_hbm, o_ref, x_ref, sems = refs[B:]
    e = pl.program_id(0)
    NC = lc_refs[0].shape[1] // LANES
    for b in range(B):
        def chunk_body(c, carry, b=b):
            base = off_ref[b, e, c] + b * cap
            n = cnt_ref[b, e, c]

            def start_row(r, priority):
                t = lc_refs[b][0, c * LANES + r]
                pltpu.make_async_copy(m_hbm.at[b, t], x_ref.at[base + r],
                                      sems.at[b]).start(priority=priority)

            def group_body(i, carry):
                for u in range(ROW_UNROLL):
                    start_row(i * ROW_UNROLL + u, u % 2)
                return carry

            def tail_body(r, carry):
                start_row(r, 0)
                return carry

            full = n // ROW_UNROLL
            lax.fori_loop(0, full, group_body, carry)
            return lax.fori_loop(full * ROW_UNROLL, n, tail_body, carry)

        lax.fori_loop(0, NC, chunk_body, 0)
    step = min(GATHER_ROWS, cap)
    for b in range(B):
        done = x_ref.at[pl.ds(b * cap, cap)]
        pltpu.make_async_copy(done, done, sems.at[b]).wait()
        for r0 in range(b * cap, (b + 1) * cap, step):
            rows = slice(r0, r0 + step)
            o_ref[0, rows, :] = _untile(x_ref, rows).astype(BF16)


def _gather(m, lc, cnt, off, cap):
    B, N, sub, _ = m.shape
    E = lc.shape[1]
    lc4 = lc[:, :, None, :]

    def table(b):
        return pl.BlockSpec((None, None, 1, lc.shape[2]), lambda e, *_: (b, e, 0, 0),
                            memory_space=pltpu.SMEM)

    return pl.pallas_call(
        functools.partial(_gather_kernel, cap, B),
        out_shape=jax.ShapeDtypeStruct((E, B * cap, sub * LANES), BF16),
        grid_spec=pltpu.PrefetchScalarGridSpec(
            num_scalar_prefetch=2,
            grid=(E,),
            in_specs=[table(b) for b in range(B)] + [pl.BlockSpec(memory_space=pl.ANY)],
            out_specs=pl.BlockSpec((1, B * cap, sub * LANES), lambda e, *_: (e, 0, 0)),
            scratch_shapes=[pltpu.VMEM((B * cap, sub, LANES), m.dtype),
                            pltpu.SemaphoreType.DMA((B,))]),
        compiler_params=_params(("arbitrary",), VMEM_LIMIT),
        name="gather",
    )(cnt, off, *([lc4] * B), m)


def _ffn_kernel(n_parts, chunk_rows, *refs):
    wg_ref, wu_ref, wd_ref = refs[:3]
    x_refs = refs[3:3 + n_parts]
    o_refs = refs[3 + n_parts:3 + 2 * n_parts]
    acc_refs = refs[3 + 2 * n_parts:]
    j = pl.program_id(1)
    last = pl.num_programs(1) - 1
    wg = wg_ref[0].astype(BF16)
    wu = wu_ref[0].astype(BF16)
    wd = wd_ref[0].astype(BF16)
    chunks = [(x_ref, o_ref, acc_ref, slice(r0, r0 + nrows))
              for x_ref, o_ref, acc_ref, nrows in zip(x_refs, o_refs, acc_refs, chunk_rows)
              for r0 in range(0, x_ref.shape[1], nrows)]

    @pl.when(j == 0)
    def _():
        for acc_ref in acc_refs:
            acc_ref[...] = jnp.zeros_like(acc_ref)

    for x_ref, _, acc_ref, rows in chunks:
        x = x_ref[0, rows, :]
        a = jnp.dot(x, wg, preferred_element_type=F32)
        u = jnp.dot(x, wu, preferred_element_type=F32)
        hh = (a * jax.nn.sigmoid(a) * u).astype(BF16)
        acc_ref[rows, :] += jnp.dot(hh, wd, preferred_element_type=F32)

    @pl.when(j == last)
    def _():
        for _, o_ref, acc_ref, rows in chunks:
            _store_tiled(o_ref.at[0], rows, acc_ref[rows, :])


def _ffn(xs, layer, w_gate, w_up, w_down, chunk_rows):
    _, E, D, F = w_gate.shape
    tf = 512
    assert F // tf >= 2
    xspec = lambda x: pl.BlockSpec((1, x.shape[1], D), lambda e, j: (e, 0, 0))
    ospec = lambda x: pl.BlockSpec((1, x.shape[1], SUB, LANES), lambda e, j: (e, 0, 0, 0))
    outs = pl.pallas_call(
        functools.partial(_ffn_kernel, len(xs), chunk_rows),
        out_shape=tuple(jax.ShapeDtypeStruct((E, x.shape[1], SUB, LANES), F32) for x in xs),
        grid=(E, F // tf),
        in_specs=[pl.BlockSpec((None, 1, D, tf), lambda e, j: (layer, e, 0, j)),
                  pl.BlockSpec((None, 1, D, tf), lambda e, j: (layer, e, 0, j)),
                  pl.BlockSpec((None, 1, tf, D), lambda e, j: (layer, e, j, 0))]
        + [xspec(x) for x in xs],
        out_specs=tuple(ospec(x) for x in xs),
        scratch_shapes=[pltpu.VMEM((x.shape[1], D), F32) for x in xs],
        compiler_params=_params(("arbitrary", "arbitrary"), VMEM_LIMIT),
        name="ffn",
    )(w_gate, w_up, w_down, *xs)
    return outs


def _combine_kernel(cnt_ref, off_ref, lc_ref, lg_ref, y_ref, o_ref):
    b = pl.program_id(0)
    hf = pl.program_id(1)
    e = pl.program_id(2)
    nch = o_ref.shape[1] // LANES
    first = hf * o_ref.shape[1]

    @pl.when(e == 0)
    def _():
        o_ref[...] = jnp.zeros_like(o_ref)

    def chunk_body(ci, carry):
        c = hf * nch + ci
        base = off_ref[b, e, c]

        n = cnt_ref[b, e, c]

        def rows_body(r0, width):
            ts = [lc_ref[0, c * LANES + r0 + u] - first for u in range(width)]
            vals = [o_ref[0, ts[u]] + lg_ref[0, c * LANES + r0 + u] * y_ref[0, base + r0 + u]
                    for u in range(width)]
            for u in range(width):
                o_ref[0, ts[u]] = vals[u]

        def group_body(i, carry):
            rows_body(i * ROW_UNROLL, ROW_UNROLL)
            return carry

        def tail_body(r, carry):
            rows_body(r, 1)
            return carry

        full = n // ROW_UNROLL
        lax.fori_loop(0, full, group_body, carry)
        return lax.fori_loop(full * ROW_UNROLL, n, tail_body, carry)

    lax.fori_loop(0, nch, chunk_body, 0)


def _combine(y, lc, lg, cnt, off, n_tokens, cap, n_split):
    E = y.shape[0]
    B = lc.shape[0]
    nh = n_tokens // n_split
    smem = lambda: pl.BlockSpec((None, None, 1, lc.shape[2]), lambda b, h, e, *_: (b, e, 0, 0),
                                memory_space=pltpu.SMEM)
    return pl.pallas_call(
        _combine_kernel,
        out_shape=jax.ShapeDtypeStruct((B, n_tokens, SUB, LANES), F32),
        grid_spec=pltpu.PrefetchScalarGridSpec(
            num_scalar_prefetch=2,
            grid=(B, n_split, E),
            in_specs=[smem(), smem(),
                      pl.BlockSpec((1, cap, SUB, LANES), lambda b, h, e, *_: (e, b, 0, 0))],
            out_specs=pl.BlockSpec((1, nh, SUB, LANES), lambda b, h, e, *_: (b, h, 0, 0),
                                   pipeline_mode=pl.Buffered(1))),
        compiler_params=_params(("arbitrary", "arbitrary", "arbitrary"), VMEM_LIMIT),
        name="combine",
    )(cnt, off, lc[:, :, None, :], lg[:, :, None, :], y)


def _final_kernel(h_ref, moe_ref, mod_ref, g_ref, o_ref):
    D = D_MODEL
    h = h_ref[0] + mod_ref[0][:, 5 * D:6 * D] * _untile(moe_ref.at[0], slice(None))
    o_ref[0] = _rms(h, g_ref[...])


def _final(h, moe, mod, g, tm):
    B, R, D = h.shape
    row = pl.BlockSpec((1, tm, D), lambda b, i: (b, i, 0))
    return pl.pallas_call(
        _final_kernel,
        out_shape=jax.ShapeDtypeStruct((B, R, D), F32),
        grid=(B, R // tm),
        in_specs=[row, pl.BlockSpec((1, tm, SUB, LANES), lambda b, i: (b, i, 0, 0)),
                  pl.BlockSpec((1, 1, N_MOD * D), lambda b, i: (b, 0, 0)),
                  pl.BlockSpec((1, D), lambda b, i: (0, 0))],
        out_specs=row,
        compiler_params=_params(("arbitrary", "arbitrary"), VMEM_LIMIT),
        name="final",
    )(h, moe, mod, g)


def _rope_tables(n_lat, n_ctx, segments):
    t = jnp.arange(n_lat)
    pos = {"row": (t // GRID_W).astype(F32), "col": (t % GRID_W).astype(F32)}
    freq = {"row": np.zeros((2, LANES), np.float32), "col": np.zeros((2, LANES), np.float32)}
    first = np.zeros(LANES, np.float32)
    second = np.zeros(LANES, np.float32)
    for lane0, width, which in segments:
        half = width // 2
        idx = np.arange(half, dtype=np.float32)
        for lo, mask in ((lane0, first), (lane0 + half, second)):
            freq[which][0, lo:lo + half] = idx / half
            freq[which][1, lo:lo + half] = 1.0
            mask[lo:lo + half] = 1.0
    ang = jnp.zeros((n_lat, LANES), F32)
    for which in ("row", "col"):
        inv = (ROPE_THETA ** (-jnp.asarray(freq[which][0]))) * jnp.asarray(freq[which][1])
        ang = ang + pos[which][:, None] * inv[None, :]
    sin = jnp.sin(ang)
    lat = (jnp.cos(ang), -sin * first[None, :], sin * second[None, :])
    ctx = (jnp.ones((n_ctx, LANES), F32), jnp.zeros((n_ctx, LANES), F32), jnp.zeros((n_ctx, LANES), F32))
    return lat, ctx


def _head_slots(w, n_heads, width, lo, hi):
    k = w.shape[0]
    w3 = w.reshape(k, n_heads, width)[:, :, lo:hi]
    return jnp.pad(w3, ((0, 0), (0, 0), (0, LANES - (hi - lo)))).reshape(k, n_heads * LANES)


def _moe(m_l, aff_l, m_c, aff_c, layer, w_gate, w_up, w_down):
    B, S = m_l.shape[:2]
    cap_l = CAPACITY_FACTOR * S // N_EXPERTS
    lc, lg, cnt, off = _topk(aff_l, cap_l)
    xs = [_gather(m_l, lc, cnt, off, cap_l)]
    chunk_rows = [cap_l]
    if m_c is not None:
        L = m_c.shape[1]
        cap_c = CAPACITY_FACTOR * L // N_EXPERTS
        lcc, lgc, cntc, offc = _topk(aff_c, cap_c)
        xs.append(_gather(m_c, lcc, cntc, offc, cap_c))
        chunk_rows.append(B * cap_c)
    ys = _ffn(xs, layer, w_gate, w_up, w_down, tuple(chunk_rows))
    out_l = _combine(ys[0], lc, lg, cnt, off, S, cap_l, 1)
    out_c = None
    if m_c is not None:
        out_c = _combine(ys[1], lcc, lgc, cntc, offc, L, cap_c, 1)
    return out_l, out_c


def kernel(x, c, ctx, c_ctx, ada_w, ada_b, norm1, norm2, ab_w_in, ab_q_norm, ab_w_uq, ab_kv_norm,
           ab_w_ukv, ab_w_o, c_w_in, c_q_gain, c_k_gain, c_w_o, moe_router, moe_w_gate, moe_w_up,
           moe_w_down, final_norm):
    B, S, D = x.shape
    L = ctx.shape[1]
    depth = ada_w.shape[0]
    tm_l, tm_c = min(S, 1024), L
    row2 = lambda v: v.reshape(1, -1)

    c8 = jnp.zeros((8, D), F32).at[:B].set(c).at[B].set(c_ctx)
    mod = _ada(c8, ada_w, ada_b)
    mod_l = [mod[i, :B][:, None, :] for i in range(depth)]
    mod_c = [jnp.broadcast_to(mod[i, B][None, None, :], (B, 1, N_MOD * D)) for i in range(depth)]

    h_l, h_c = x, ctx
    moe_l = moe_c = None
    for i in range(depth):
        last = i == depth - 1
        j = i // 2
        wr_t = _split_const(moe_router[i].T)
        n2 = row2(norm2[i])
        if i % 2 == 0:
            assert moe_l is None
            tabs_l, tabs_c = _rope_tables(S, L, ((MLA_NOPE, MLA_ROPE // 2, "row"),
                                                 (MLA_NOPE + MLA_ROPE // 2, MLA_ROPE // 2, "col")))
            w_ukv = ab_w_ukv[j]
            ekr = np.zeros((MLA_ROPE, MLA_HEADS * LANES), np.float32)
            for hd in range(MLA_HEADS):
                ekr[np.arange(MLA_ROPE), hd * LANES + MLA_NOPE + np.arange(MLA_ROPE)] = 1.0
            w = {
                "win": ab_w_in[j].astype(BF16),
                "qn": row2(ab_q_norm[j]),
                "wuqt": _head_slots(ab_w_uq[j], MLA_HEADS, MLA_QK, 0, MLA_QK).T.astype(BF16),
                "kvn": row2(ab_kv_norm[j]),
                "wuk": _head_slots(w_ukv, MLA_HEADS, MLA_NOPE + MLA_V, 0, MLA_NOPE).astype(BF16),
                "wuvt": _head_slots(w_ukv, MLA_HEADS, MLA_NOPE + MLA_V, MLA_NOPE,
                                    MLA_NOPE + MLA_V).T.astype(BF16),
                "ekr": jnp.asarray(ekr, BF16),
            }
            n1 = row2(norm1[i])
            f_l, q_l, k_l, vt_l = _in_ab(h_l, mod_l[i], n1, w, tabs_l, tm_l)
            f_c, q_c, k_c, vt_c = _in_ab(h_c, mod_c[i], n1, w, tabs_c, tm_c)
            o_l = _attention(q_l[:, :, None], k_c, vt_c, k_l, vt_l, min(S, ATTN_QUERIES), ATTN_KEYS,
                             MLA_VROWS, MLA_V)
            yf_l = _fourier_lat(f_l)
            wof = ab_w_o[j][:FNET_WIDTH].astype(BF16)
            woa = jnp.pad(ab_w_o[j][FNET_WIDTH:].reshape(MLA_HEADS, MLA_V, D),
                          ((0, 0), (0, LANES - MLA_V), (0, 0))).reshape(MLA_HEADS * LANES, D).astype(BF16)
            h_l, m_l, aff_l = _out_proj(h_l, mod_l[i], yf_l, wof, o_l, woa, n2, wr_t, tm_l)
            m_c = aff_c = None
            if not last:
                o_c = _attention(q_c[:, :, None], k_c, vt_c, None, None, L, L, MLA_VROWS, MLA_V)
                yf_c = _fourier_ctx(f_c)
                h_c, m_c, aff_c = _out_proj(h_c, mod_c[i], yf_c, wof, o_c, woa, n2, wr_t, tm_c)
        else:
            tabs_l, tabs_c = _rope_tables(S, L, ((0, GQA_HEAD_DIM // 2, "row"),
                                                 (GQA_HEAD_DIM // 2, GQA_HEAD_DIM // 2, "col")))
            nq = GQA_HEADS * GQA_HEAD_DIM
            nqk = nq + GQA_KV_HEADS * GQA_HEAD_DIM
            w = {"wqt": c_w_in[j][:, :nq].T.astype(BF16), "wk": c_w_in[j][:, nq:nqk].astype(BF16),
                 "wvt": c_w_in[j][:, nqk:].T.astype(BF16),
                 "qg": c_q_gain[j].reshape(-1, 1), "kg": row2(c_k_gain[j])}
            n1 = row2(norm1[i])
            h_l, q_l, k_l, vt_l = _in_c(h_l, moe_l, mod_l[i - 1], mod_l[i], n1, w, tabs_l,
                                        min(tm_l, 512))
            h_c, q_c, k_c, vt_c = _in_c(h_c, moe_c, mod_c[i - 1], mod_c[i], n1, w, tabs_c, tm_c)
            grp = lambda q: q.reshape(B, GQA_KV_HEADS, GQA_GROUP, LANES, q.shape[3])
            o_l = _attention(grp(q_l), k_c, vt_c, k_l, vt_l, min(S, ATTN_QUERIES // GQA_GROUP),
                             ATTN_KEYS, GQA_HEAD_DIM)
            woa = c_w_o[j].astype(BF16)
            h_l, m_l, aff_l = _out_proj(h_l, mod_l[i], None, None, o_l, woa, n2, wr_t, tm_l)
            m_c = aff_c = None
            if not last:
                o_c = _attention(grp(q_c), k_c, vt_c, None, None, L, L, GQA_HEAD_DIM)
                h_c, m_c, aff_c = _out_proj(h_c, mod_c[i], None, None, o_c, woa, n2, wr_t, tm_c)
        moe_l, moe_c = _moe(m_l, aff_l, m_c, aff_c, i, moe_w_gate, moe_w_up, moe_w_down)
    return _final(h_l, moe_l, mod_l[depth - 1], row2(final_norm), tm_l)
```
